```python
import math
import jax, jax.numpy as jnp
from jax import lax
import numpy as np

D_MODEL = 1024
BATCH = 4
SEQ = 4096
DEPTH = 4

CHUNK = 64
HEAD_DIM = 64
N_HEADS_ATTN = 8
N_IDX_HEADS = 4
IDX_DIM = 64
TOPK_MAX = 256
Q_BLOCK = 64
N_HEADS_RET = 4
N_HEADS_MLSTM = 4
CONV_WIDTH = 4
ROPE_THETA = 10000.0
N_EXPERTS = 16
N_GROUPS = 4
EXPERTS_PER_GROUP = N_EXPERTS // N_GROUPS
TOP_K_EXPERTS = 2
D_FF_EXPERT = 256
LN_EPS = 1e-5

MIX_ATTN = N_HEADS_ATTN * HEAD_DIM
MIX_RET = N_HEADS_RET * HEAD_DIM
MIX_MLSTM = N_HEADS_MLSTM * HEAD_DIM
D_MIX = MIX_ATTN + MIX_RET + MIX_MLSTM

DEEPNORM_ALPHA = (2.0 * DEPTH) ** 0.25
DEEPNORM_BETA = (8.0 * DEPTH) ** -0.25

SPLIT_SIZES = (
    MIX_ATTN, MIX_ATTN, MIX_ATTN,
    N_IDX_HEADS * IDX_DIM, IDX_DIM, N_IDX_HEADS,
    MIX_RET, MIX_RET, MIX_RET, MIX_RET,
    MIX_MLSTM, MIX_MLSTM, MIX_MLSTM, MIX_MLSTM,
    N_HEADS_MLSTM, N_HEADS_MLSTM,
)
IN_PROJ_WIDTH = sum(SPLIT_SIZES)
SPLIT_OFFSETS = tuple(int(v) for v in np.cumsum((0,) + SPLIT_SIZES)[:-1])
SPLIT_POINTS = SPLIT_OFFSETS[1:]

kernel_name = 'hybrid_dsa_retention_mlstm_moe_trunk'


def layer_norm(x, g, b):
    xf = x.astype(jnp.float32)
    mu = jnp.mean(xf, axis=-1, keepdims=True)
    var = jnp.mean(jnp.square(xf - mu), axis=-1, keepdims=True)
    return ((xf - mu) * lax.rsqrt(var + LN_EPS) * g + b).astype(x.dtype)


def head_norm(x):
    xf = x.astype(jnp.float32)
    mu = jnp.mean(xf, axis=-1, keepdims=True)
    var = jnp.mean(jnp.square(xf - mu), axis=-1, keepdims=True)
    return ((xf - mu) * lax.rsqrt(var + LN_EPS)).astype(x.dtype)


def rope_tables(positions):
    half = HEAD_DIM // 2
    inv_freq = ROPE_THETA ** (-jnp.arange(half, dtype=jnp.float32) / half)
    ang = positions.astype(jnp.float32)[..., None] * inv_freq
    return jnp.cos(ang)[:, :, None, :], jnp.sin(ang)[:, :, None, :]


def apply_rope(x, cos, sin):
    half = x.shape[-1] // 2
    x1 = x[..., :half].astype(jnp.float32)
    x2 = x[..., half:].astype(jnp.float32)
    return jnp.concatenate([x1 * cos - x2 * sin, x2 * cos + x1 * sin], axis=-1).astype(x.dtype)


def causal_conv(x, w, b):
    seq = x.shape[1]
    xp = jnp.pad(x, ((0, 0), (CONV_WIDTH - 1, 0), (0, 0)))
    out = b
    for j in range(CONV_WIDTH):
        out = out + xp[:, j:j + seq] * w[j]
    return out


def dsa_attention(q, k, v, iq, ik, iw):
    bsz, seq, n_heads, hd = q.shape
    topk = min(TOPK_MAX, seq // 4)
    n_blocks = seq // Q_BLOCK
    key_pos = jnp.arange(seq)
    iw = iw.astype(jnp.float32) * (N_IDX_HEADS ** -0.5) * (IDX_DIM ** -0.5)
    gather = jax.vmap(lambda arr, ids: arr[ids])

    def one_block(blk):
        start = blk * Q_BLOCK
        qb = lax.dynamic_slice_in_dim(q, start, Q_BLOCK, axis=1)
        iqb = lax.dynamic_slice_in_dim(iq, start, Q_BLOCK, axis=1)
        iwb = lax.dynamic_slice_in_dim(iw, start, Q_BLOCK, axis=1)
        q_pos = start + jnp.arange(Q_BLOCK)
        limit = (q_pos // CHUNK + 1) * CHUNK
        visible = key_pos[None, :] < limit[:, None]
        rel = jax.nn.relu(jnp.einsum('bqhd,bsd->bqhs', iqb, ik).astype(jnp.float32))
        score = jnp.einsum('bqhs,bqh->bqs', rel, iwb)
        score = jnp.where(visible[None], score, -jnp.inf)
        _, idx = lax.top_k(score, topk)
        valid = idx < limit[None, :, None]
        k_sel = gather(k, idx)
        v_sel = gather(v, idx)
        s = jnp.einsum('bqhd,bqkhd->bqhk', qb, k_sel).astype(jnp.float32) * (hd ** -0.5)
        s = jnp.where(valid[:, :, None, :], s, -jnp.inf)
        p = jax.nn.softmax(s, axis=-1).astype(v.dtype)
        return jnp.einsum('bqhk,bqkhd->bqhd', p, v_sel)

    out = lax.map(one_block, jnp.arange(n_blocks))
    return out.transpose(1, 0, 2, 3, 4).reshape(bsz, seq, n_heads, hd)


def retention_chunkwise(q, k, v):
    bsz, seq, nh, hd = q.shape
    dv = v.shape[-1]
    n_chunks = seq // CHUNK
    f32 = jnp.float32
    log_gamma = jnp.log1p(-(2.0 ** (-5.0 - jnp.arange(nh, dtype=f32))))

    def to_chunks(a):
        return a.astype(f32).reshape(bsz, n_chunks, CHUNK, nh, -1)

    qc = to_chunks(q)
    kc = to_chunks(k) * (hd ** -0.5)
    vc = to_chunks(v)
    pos = jnp.arange(CHUNK, dtype=f32)
    diff = pos[:, None] - pos[None, :]
    decay_in = jnp.where(diff >= 0, jnp.exp(diff[None] * log_gamma[:, None, None]), 0.0)
    scores = jnp.einsum('bnihd,bnjhd->bnhij', qc, kc) * decay_in
    inner = jnp.einsum('bnhij,bnjhe->bnihe', scores, vc)
    k_decay = jnp.exp((CHUNK - 1.0 - pos)[None, :] * log_gamma[:, None])
    chunk_kv = jnp.einsum('bnjhd,hj,bnjhe->nbhde', kc, k_decay, vc)
    chunk_decay = jnp.exp(CHUNK * log_gamma)[None, :, None, None]

    def step(state, kv):
        return state * chunk_decay + kv, state

    _, prev = lax.scan(step, jnp.zeros((bsz, nh, hd, dv), f32), chunk_kv)
    q_decay = jnp.exp((pos + 1.0)[None, :] * log_gamma[:, None])
    cross = jnp.einsum('bnihd,hi,nbhde->bnihe', qc, q_decay, prev)
    return (inner + cross).reshape(bsz, seq, nh, dv)


def mlstm_chunkwise(q, k, v, i_pre, f_pre):
    bsz, seq, nh, hd = q.shape
    n_chunks = seq // CHUNK
    f32 = jnp.float32

    def to_chunks(a):
        return a.astype(f32).reshape(bsz, n_chunks, CHUNK, nh, -1).transpose(1, 0, 3, 2, 4)

    qc = to_chunks(q)
    kc = to_chunks(k) * (hd ** -0.5)
    vc = to_chunks(v)
    ic = to_chunks(i_pre[..., None])[..., 0]
    lfc = to_chunks(jax.nn.log_sigmoid(f_pre.astype(f32))[..., None])[..., 0]
    causal = jnp.tril(jnp.ones((CHUNK, CHUNK), dtype=bool))

    def step(carry, xs):
        c_mem, n_mem, m_mem = carry
        q_, k_, v_, i_, lf_ = xs
        b = jnp.cumsum(lf_, axis=-1)
        log_w = jnp.where(causal, b[..., :, None] - b[..., None, :] + i_[..., None, :], -jnp.inf)
        log_inter = b + m_mem[..., None]
        m_q = jnp.maximum(log_inter, jnp.max(log_w, axis=-1))
        w = jnp.exp(log_w - m_q[..., None])
        inter = jnp.exp(log_inter - m_q)
        s = jnp.einsum('bhjd,bhld->bhjl', q_, k_) * w
        num = jnp.einsum('bhjl,bhle->bhje', s, v_) + inter[..., None] * jnp.einsum('bhjd,bhde->bhje', q_, c_mem)
        den = jnp.sum(s, axis=-1) + inter * jnp.einsum('bhjd,bhd->bhj', q_, n_mem)
        h = num / jnp.maximum(jnp.abs(den), jnp.exp(-m_q))[..., None]
        b_last = b[..., -1]
        log_k = b_last[..., None] - b + i_
        m_new = jnp.maximum(b_last + m_mem, jnp.max(log_k, axis=-1))
        kw = jnp.exp(log_k - m_new[..., None])
        decay = jnp.exp(b_last + m_mem - m_new)
        c_new = decay[..., None, None] * c_mem + jnp.einsum('bhl,bhld,bhle->bhde', kw, k_, v_)
        n_new = decay[..., None] * n_mem + jnp.einsum('bhl,bhld->bhd', kw, k_)
        return (c_new, n_new, m_new), h

    init = (jnp.zeros((bsz, nh, hd, hd), f32), jnp.zeros((bsz, nh, hd), f32), jnp.zeros((bsz, nh), f32))
    _, hs = lax.scan(step, init, (qc, kc, vc, ic, lfc))
    return hs.transpose(1, 0, 3, 2, 4).reshape(bsz, seq, nh, hd)


def token_mixers(h, w_in, w_out, i_bias, f_bias, conv_w, conv_b, cos, sin):
    bsz, seq, _ = h.shape
    (aq, ak, av, iq, ik, iw, rq, rk, rv, rg,
     mq, mk, mv, mo, mi, mf) = jnp.split(h @ w_in, SPLIT_POINTS, axis=-1)

    def heads(a, n):
        return a.reshape(bsz, seq, n, -1)

    o_a = dsa_attention(apply_rope(heads(aq, N_HEADS_ATTN), cos, sin),
                        apply_rope(heads(ak, N_HEADS_ATTN), cos, sin),
                        heads(av, N_HEADS_ATTN),
                        apply_rope(heads(iq, N_IDX_HEADS), cos, sin),
                        apply_rope(heads(ik, 1), cos, sin)[:, :, 0],
                        iw).reshape(bsz, seq, MIX_ATTN)
    y_r = retention_chunkwise(apply_rope(heads(rq, N_HEADS_RET), cos, sin),
                              apply_rope(heads(rk, N_HEADS_RET), cos, sin),
                              heads(rv, N_HEADS_RET))
    o_b = (head_norm(y_r).reshape(bsz, seq, MIX_RET) * jax.nn.silu(rg)).astype(h.dtype)
    qk = jax.nn.silu(causal_conv(jnp.concatenate([mq, mk], axis=-1), conv_w, conv_b))
    mq_c, mk_c = jnp.split(qk, 2, axis=-1)
    h_tilde = mlstm_chunkwise(heads(mq_c, N_HEADS_MLSTM), heads(mk_c, N_HEADS_MLSTM),
                              heads(mv, N_HEADS_MLSTM), mi + i_bias, mf + f_bias)
    h_t = jax.nn.sigmoid(heads(mo, N_HEADS_MLSTM).astype(jnp.float32)) * h_tilde
    o_c = head_norm(h_t).reshape(bsz, seq, MIX_MLSTM).astype(h.dtype)
    return jnp.concatenate([o_a, o_b, o_c], axis=-1) @ w_out


def moe_ffn(h, w_router, b_router, w_gate, w_up, w_down):
    n_tok = h.shape[0]
    scores = jax.nn.sigmoid((h @ w_router).astype(jnp.float32))
    biased = scores + b_router.astype(jnp.float32)
    grouped = biased.reshape(n_tok, N_GROUPS, EXPERTS_PER_GROUP)
    group_score = jnp.sum(lax.top_k(grouped, TOP_K_EXPERTS)[0], axis=-1)
    best_group = jnp.argmax(group_score, axis=-1)
    in_group = jnp.take_along_axis(grouped, best_group[:, None, None], axis=1)[:, 0]
    _, local = lax.top_k(in_group, TOP_K_EXPERTS)
    expert_idx = best_group[:, None] * EXPERTS_PER_GROUP + local
    wts = jnp.take_along_axis(scores, expert_idx, axis=1)
    wts = wts / jnp.sum(wts, axis=-1, keepdims=True)
    gate = jnp.einsum('tk,tke->te', wts, jax.nn.one_hot(expert_idx, N_EXPERTS, dtype=jnp.float32))
    hid = jax.nn.silu(jnp.einsum('td,edf->tef', h, w_gate)) * jnp.einsum('td,edf->tef', h, w_up)
    hid = hid * gate[:, :, None].astype(hid.dtype)
    return jnp.einsum('tef,efd->td', hid, w_down)


def setup_inputs(seed: int = 0) -> dict:
    key = jax.random.key(seed)
    ks = jax.random.split(key, 20)
    f32 = jnp.float32

    def nrm(k, shape, scale):
        return jax.random.normal(k, shape, f32) * scale

    x = nrm(ks[0], (BATCH, SEQ, D_MODEL), 1.0)
    c = nrm(ks[1], (BATCH, D_MODEL), 1.0)
    offset = jax.random.randint(ks[2], (BATCH, 1), 0, 1024, dtype=jnp.int32)
    positions = offset + jnp.arange(SEQ, dtype=jnp.int32)[None, :]
    w_ada = nrm(ks[3], (DEPTH, D_MODEL, 6 * D_MODEL), 0.1 * D_MODEL ** -0.5)
    b_ada = nrm(ks[4], (DEPTH, 6 * D_MODEL), 0.02)
    col_scale = np.ones(IN_PROJ_WIDTH, np.float32)
    for part in (2, 8, 12):
        col_scale[SPLIT_OFFSETS[part]:SPLIT_OFFSETS[part] + SPLIT_SIZES[part]] = DEEPNORM_BETA
    w_in = nrm(ks[5], (DEPTH, D_MODEL, IN_PROJ_WIDTH), D_MODEL ** -0.5) * jnp.asarray(col_scale)
    i_bias = nrm(ks[6], (DEPTH, N_HEADS_MLSTM), 0.1)
    f_bias = jnp.linspace(3.0, 6.0, N_HEADS_MLSTM, dtype=f32)[None, :] + nrm(ks[7], (DEPTH, N_HEADS_MLSTM), 0.01)
    conv_w = nrm(ks[8], (DEPTH, CONV_WIDTH, 2 * MIX_MLSTM), CONV_WIDTH ** -0.5)
    conv_b = nrm(ks[9], (DEPTH, 2 * MIX_MLSTM), 0.01)
    w_out = nrm(ks[10], (DEPTH, D_MIX, D_MODEL), DEEPNORM_BETA * D_MIX ** -0.5)
    ln_mix_g = 1.0 + nrm(ks[11], (DEPTH, D_MODEL), 0.01)
    ln_mix_b = nrm(ks[12], (DEPTH, D_MODEL), 0.01)
    w_router = nrm(ks[13], (D_MODEL, N_EXPERTS), D_MODEL ** -0.5)
    b_router = nrm(ks[14], (N_EXPERTS,), 0.01)
    w_gate = nrm(ks[15], (DEPTH, N_EXPERTS, D_MODEL, D_FF_EXPERT), D_MODEL ** -0.5)
    w_up = nrm(ks[16], (DEPTH, N_EXPERTS, D_MODEL, D_FF_EXPERT), DEEPNORM_BETA * D_MODEL ** -0.5)
    w_down = nrm(ks[17], (DEPTH, N_EXPERTS, D_FF_EXPERT, D_MODEL), DEEPNORM_BETA * D_FF_EXPERT ** -0.5)
    ln_ffn_g = 1.0 + nrm(ks[18], (DEPTH, D_MODEL), 0.01)
    ln_ffn_b = nrm(ks[19], (DEPTH, D_MODEL), 0.01)
    return {'x': x, 'c': c, 'positions': positions, 'w_ada': w_ada, 'b_ada': b_ada,
            'w_in': w_in, 'i_bias': i_bias, 'f_bias': f_bias, 'conv_w': conv_w, 'conv_b': conv_b,
            'w_out': w_out, 'ln_mix_g': ln_mix_g, 'ln_mix_b': ln_mix_b,
            'w_router': w_router, 'b_router': b_router, 'w_gate': w_gate, 'w_up': w_up,
            'w_down': w_down, 'ln_ffn_g': ln_ffn_g, 'ln_ffn_b': ln_ffn_b}


def reference(x, c, positions, w_ada, b_ada, w_in, i_bias, f_bias, conv_w, conv_b,
              w_out, ln_mix_g, ln_mix_b, w_router, b_router, w_gate, w_up, w_down,
              ln_ffn_g, ln_ffn_b):
    bsz, seq, d = x.shape
    cos, sin = rope_tables(positions)
    c_act = jax.nn.silu(c)
    for l in range(DEPTH):
        mod = (c_act @ w_ada[l] + b_ada[l])[:, None, :]
        sh_m, sc_m, g_m, sh_f, sc_f, g_f = jnp.split(mod, 6, axis=-1)
        h = x * (1.0 + sc_m) + sh_m
        mix = token_mixers(h, w_in[l], w_out[l], i_bias[l], f_bias[l], conv_w[l], conv_b[l], cos, sin)
        x = layer_norm(DEEPNORM_ALPHA * x + (1.0 + g_m) * mix, ln_mix_g[l], ln_mix_b[l])
        h = x * (1.0 + sc_f) + sh_f
        y = moe_ffn(h.reshape(bsz * seq, d), w_router, b_router, w_gate[l], w_up[l], w_down[l])
        x = layer_norm(DEEPNORM_ALPHA * x + (1.0 + g_f) * y.reshape(bsz, seq, d), ln_ffn_g[l], ln_ffn_b[l])
    return x
```

```python
import functools

import numpy as np
import jax
import jax.numpy as jnp
from jax import lax
from jax.experimental import pallas as pl
from jax.experimental.pallas import tpu as pltpu

F32 = jnp.float32
BF16 = jnp.bfloat16

HEAD_DIM = 64
CHUNK = 64
N_HEADS_ATTN = 8
N_IDX_HEADS = 4
IDX_DIM = 64
TOPK_MAX = 256
N_HEADS_RET = 4
N_HEADS_MLSTM = 4
CONV_WIDTH = 4
ROPE_THETA = 10000.0
N_EXPERTS = 16
N_GROUPS = 4
EXPERTS_PER_GROUP = N_EXPERTS // N_GROUPS
D_FF_EXPERT = 256
LN_EPS = 1e-5

MIX_ATTN = N_HEADS_ATTN * HEAD_DIM
MIX_RET = N_HEADS_RET * HEAD_DIM
MIX_MLSTM = N_HEADS_MLSTM * HEAD_DIM

LANES = 128
VMEM_LIMIT = 56 * 1024 * 1024

A_WIDTH = 3 * MIX_ATTN + N_IDX_HEADS * IDX_DIM
R_WIDTH = 4 * MIX_RET
M_WIDTH = 4 * MIX_MLSTM
S_WIDTH = LANES
S_IW = IDX_DIM
S_MI = S_IW + N_IDX_HEADS
S_MF = S_MI + N_HEADS_MLSTM
W_TOTAL = A_WIDTH + R_WIDTH + M_WIDTH + S_WIDTH

INT_MIN = -2 ** 31
NEG_BIG = -1e30


def _dot(a, b):
    return jnp.dot(a, b, preferred_element_type=F32)


def _dot_nt(a, b):
    return lax.dot_general(a, b, (((1,), (1,)), ((), ())), preferred_element_type=F32)


def _dot_tn(a, b):
    return lax.dot_general(a, b, (((0,), (0,)), ((), ())), preferred_element_type=F32)


def _params(sem):
    return pltpu.CompilerParams(dimension_semantics=sem, vmem_limit_bytes=VMEM_LIMIT)


def _ada_kernel(c_ref, w_ref, b_ref, o_ref):
    c = c_ref[...]
    c_act = c * jax.nn.sigmoid(c)
    o_ref[0] = _dot(c_act, w_ref[0]) + b_ref[0]


def _ada_mod(c_pad, w_ada, b_ada, tn):
    depth, d, n = w_ada.shape
    rows = c_pad.shape[0]
    return pl.pallas_call(
        _ada_kernel,
        grid=(depth, n // tn),
        in_specs=[pl.BlockSpec((rows, d), lambda l, j: (0, 0)),
                  pl.BlockSpec((1, d, tn), lambda l, j: (l, 0, j)),
                  pl.BlockSpec((1, 1, tn), lambda l, j: (l, 0, j))],
        out_specs=pl.BlockSpec((1, rows, tn), lambda l, j: (l, 0, j)),
        out_shape=jax.ShapeDtypeStruct((depth, rows, n), F32),
        compiler_params=_params(("parallel", "parallel")),
        name="ada_mod",
    )(c_pad, w_ada, b_ada.reshape(depth, 1, n))


def _rope(y, cos, sin):
    w = y.shape[1]
    reps = w // LANES
    cosw = jnp.concatenate([cos] * reps, axis=1) if reps > 1 else cos
    sinw = jnp.concatenate([sin] * reps, axis=1) if reps > 1 else sin
    lane = lax.broadcasted_iota(jnp.int32, y.shape, 1)
    first = (lane % HEAD_DIM) < (HEAD_DIM // 2)
    partner = jnp.where(first, pltpu.roll(y, w - HEAD_DIM // 2, 1), pltpu.roll(y, HEAD_DIM // 2, 1))
    return y * cosw + partner * sinw


def _inproj_kernel(x_ref, sc_ref, sh_ref, w_ref, cos_ref, sin_ref, a_ref, r_ref, m_ref, s_ref):
    h = (x_ref[0] * (1.0 + sc_ref[0]) + sh_ref[0]).astype(BF16)
    cos = cos_ref[0]
    sin = sin_ref[0]

    def proj(start, width):
        return _dot(h, w_ref[:, start:start + width])

    a_ref[0, :, 0:MIX_ATTN] = _rope(proj(0, MIX_ATTN), cos, sin).astype(BF16)
    a_ref[0, :, MIX_ATTN:2 * MIX_ATTN] = _rope(proj(MIX_ATTN, MIX_ATTN), cos, sin).astype(BF16)
    a_ref[0, :, 2 * MIX_ATTN:3 * MIX_ATTN] = proj(2 * MIX_ATTN, MIX_ATTN).astype(BF16)
    a_ref[0, :, 3 * MIX_ATTN:A_WIDTH] = _rope(proj(3 * MIX_ATTN, N_IDX_HEADS * IDX_DIM), cos, sin).astype(BF16)
    r0 = A_WIDTH
    r_ref[0, :, 0:2 * MIX_RET] = _rope(proj(r0, 2 * MIX_RET), cos, sin).astype(BF16)
    r_ref[0, :, 2 * MIX_RET:R_WIDTH] = proj(r0 + 2 * MIX_RET, 2 * MIX_RET).astype(BF16)
    m0 = A_WIDTH + R_WIDTH
    m_ref[0, :, 0:2 * MIX_MLSTM] = proj(m0, 2 * MIX_MLSTM).astype(BF16)
    m_ref[0, :, 2 * MIX_MLSTM:M_WIDTH] = proj(m0 + 2 * MIX_MLSTM, 2 * MIX_MLSTM).astype(BF16)
    y = proj(m0 + M_WIDTH, S_WIDTH)
    lane = lax.broadcasted_iota(jnp.int32, y.shape, 1)
    s_ref[0] = jnp.where(lane < IDX_DIM, _rope(y, cos, sin), y)


def _in_proj(x, sc, sh, w, cos_t, sin_t, tm):
    bsz, seq, d = x.shape
    row = lambda b, i: (b, i, 0)
    per_b = lambda b, i: (b, 0, 0)
    return pl.pallas_call(
        _inproj_kernel,
        grid=(bsz, seq // tm),
        in_specs=[pl.BlockSpec((1, tm, d), row),
                  pl.BlockSpec((1, 1, d), per_b),
                  pl.BlockSpec((1, 1, d), per_b),
                  pl.BlockSpec((d, W_TOTAL), lambda b, i: (0, 0)),
                  pl.BlockSpec((1, tm, LANES), row),
                  pl.BlockSpec((1, tm, LANES), row)],
        out_specs=[pl.BlockSpec((1, tm, A_WIDTH), row),
                   pl.BlockSpec((1, tm, R_WIDTH), row),
                   pl.BlockSpec((1, tm, M_WIDTH), row),
                   pl.BlockSpec((1, tm, S_WIDTH), row)],
        out_shape=[jax.ShapeDtypeStruct((bsz, seq, A_WIDTH), BF16),
                   jax.ShapeDtypeStruct((bsz, seq, R_WIDTH), BF16),
                   jax.ShapeDtypeStruct((bsz, seq, M_WIDTH), BF16),
                   jax.ShapeDtypeStruct((bsz, seq, S_WIDTH), F32)],
        compiler_params=_params(("parallel", "parallel")),
        name="in_proj",
    )(x, sc, sh, w, cos_t, sin_t)


def _dsa_kernel(q_ref, k_ref, v_ref, iq_ref, ik_ref, sm_ref, tri_ref, o_ref, key_scr, *, tq, kc, topk):
    q0 = pl.program_id(1) * tq
    n_chunks = (q0 + tq + kc - 1) // kc
    rows = q0 + lax.broadcasted_iota(jnp.int32, (tq, 1), 0)
    row_limit = (rows // CHUNK + 1) * CHUNK
    col = lax.broadcasted_iota(jnp.int32, (1, kc), 1)

    iw = sm_ref[0][:, S_IW:S_IW + N_IDX_HEADS] * (N_IDX_HEADS ** -0.5 * IDX_DIM ** -0.5)
    iq = iq_ref[0]

    def score_body(c, carry):
        k0 = pl.multiple_of(c * kc, kc)
        ikc = ik_ref[0, pl.ds(k0, kc), :][:, 0:IDX_DIM].astype(BF16)
        score = jnp.zeros((tq, kc), F32)
        for h in range(N_IDX_HEADS):
            rel = jnp.maximum(_dot_nt(iq[:, h * IDX_DIM:(h + 1) * IDX_DIM], ikc), 0.0)
            score = score + rel * iw[:, h:h + 1]
        bits = pltpu.bitcast(score, jnp.int32)
        key = jnp.where(bits >= 0, bits, bits ^ jnp.int32(0x7FFFFFFF))
        key_scr[c] = jnp.where(k0 + col < row_limit, key, jnp.int32(INT_MIN))
        return carry

    lax.fori_loop(0, n_chunks, score_body, 0)

    def count_ge(thr):
        def body(c, acc):
            key = key_scr[c]
            for j in range(kc // LANES):
                acc = acc + jnp.where(key[:, j * LANES:(j + 1) * LANES] >= thr, 1, 0)
            return acc
        acc = lax.fori_loop(0, n_chunks, body, jnp.zeros((tq, LANES), jnp.int32))
        return jnp.sum(acc, axis=1, keepdims=True)

    def bit_body(i, t):
        cand = t | lax.shift_left(jnp.int32(1), 31 - i)
        cnt = count_ge(cand ^ jnp.int32(INT_MIN))
        return jnp.where(cnt >= topk, cand, t)

    t = lax.fori_loop(0, 32, bit_body, jnp.zeros((tq, LANES), jnp.int32))
    thr = t ^ jnp.int32(INT_MIN)
    thr_next = jnp.where(thr == jnp.int32(2 ** 31 - 1), thr, thr + 1)
    cnt_gt = jnp.where(thr[:, 0:1] == jnp.int32(2 ** 31 - 1), 0, count_ge(thr_next))
    need = (topk - cnt_gt).astype(F32)
    thr_col = thr[:, 0:1]

    q = q_ref[0]
    tri = tri_ref[...]
    n_h = N_HEADS_ATTN

    def attn_body(c, carry):
        ties_before, ms, ls, accs = carry
        k0 = pl.multiple_of(c * kc, kc)
        key = key_scr[c]
        tie = key == thr_col
        rank = _dot(jnp.where(tie, 1.0, 0.0).astype(BF16), tri) + ties_before
        sel = (key > thr_col) | (tie & (rank <= need))
        sel = sel & (k0 + col < row_limit)
        kch = k_ref[0, pl.ds(k0, kc), :]
        vch = v_ref[0, pl.ds(k0, kc), :]
        new_ms, new_ls, new_accs = [], [], []
        for h in range(n_h):
            sl = slice(h * HEAD_DIM, (h + 1) * HEAD_DIM)
            s = _dot_nt(q[:, sl], kch[:, sl]) * (HEAD_DIM ** -0.5)
            s = jnp.where(sel, s, NEG_BIG)
            m_new = jnp.maximum(ms[h], jnp.max(s, axis=1, keepdims=True))
            p = jnp.where(sel, jnp.exp(s - m_new), 0.0)
            alpha = jnp.exp(ms[h] - m_new)
            new_ls.append(alpha * ls[h] + jnp.sum(p, axis=1, keepdims=True))
            new_accs.append(alpha * accs[h] + _dot(p.astype(BF16), vch[:, sl]))
            new_ms.append(m_new)
        return (rank[:, kc - 1:kc], tuple(new_ms), tuple(new_ls), tuple(new_accs))

    init = (jnp.zeros((tq, 1), F32),
            tuple(jnp.full((tq, 1), NEG_BIG, F32) for _ in range(n_h)),
            tuple(jnp.zeros((tq, 1), F32) for _ in range(n_h)),
            tuple(jnp.zeros((tq, HEAD_DIM), F32) for _ in range(n_h)))
    _, _, ls, accs = lax.fori_loop(0, n_chunks, attn_body, init)
    for h in range(n_h):
        o_ref[0, :, h * HEAD_DIM:(h + 1) * HEAD_DIM] = (accs[h] / ls[h]).astype(o_ref.dtype)


def _dsa(a_proj, small, tri, tq, kc):
    bsz, seq, _ = a_proj.shape
    topk = min(TOPK_MAX, seq // 4)
    kern = functools.partial(_dsa_kernel, tq=tq, kc=kc, topk=topk)
    iq_blk = 3 * MIX_ATTN // (N_IDX_HEADS * IDX_DIM)
    return pl.pallas_call(
        kern,
        grid=(bsz, seq // tq),
        in_specs=[pl.BlockSpec((1, tq, MIX_ATTN), lambda b, i: (b, i, 0)),
                  pl.BlockSpec((1, seq, MIX_ATTN), lambda b, i: (b, 0, 1)),
                  pl.BlockSpec((1, seq, MIX_ATTN), lambda b, i: (b, 0, 2)),
                  pl.BlockSpec((1, tq, N_IDX_HEADS * IDX_DIM), lambda b, i: (b, i, iq_blk)),
                  pl.BlockSpec((1, seq, S_WIDTH), lambda b, i: (b, 0, 0)),
                  pl.BlockSpec((1, tq, S_WIDTH), lambda b, i: (b, i, 0)),
                  pl.BlockSpec((kc, kc), lambda b, i: (0, 0))],
        out_specs=pl.BlockSpec((1, tq, MIX_ATTN), lambda b, i: (b, i, 0)),
        out_shape=jax.ShapeDtypeStruct((bsz, seq, MIX_ATTN), BF16),
        scratch_shapes=[pltpu.VMEM((seq // kc, tq, kc), jnp.int32)],
        compiler_params=_params(("parallel", "arbitrary")),
        name="dsa",
    )(a_proj, a_proj, a_proj, a_proj, small, small, tri)


def _head_norm(y):
    mu = jnp.mean(y, axis=-1, keepdims=True)
    var = jnp.mean(jnp.square(y - mu), axis=-1, keepdims=True)
    return (y - mu) * lax.rsqrt(var + LN_EPS)


def _ret_kernel(r_ref, o_ref, state_scr, *, cr):
    @pl.when(pl.program_id(1) == 0)
    def _():
        state_scr[...] = jnp.zeros_like(state_scr)

    r = r_ref[0]
    ri = lax.broadcasted_iota(jnp.int32, (cr, cr), 0)
    ci = lax.broadcasted_iota(jnp.int32, (cr, cr), 1)
    diff = (ri - ci).astype(F32)
    pos = lax.broadcasted_iota(jnp.int32, (cr, 1), 0).astype(F32)
    for h in range(N_HEADS_RET):
        log_gamma = jnp.log1p(jnp.full((1, 1), -(2.0 ** (-5.0 - h)), F32))
        sl = lambda part: slice(part * MIX_RET + h * HEAD_DIM, part * MIX_RET + (h + 1) * HEAD_DIM)
        q = r[:, sl(0)]
        k = r[:, sl(1)]
        v = r[:, sl(2)]
        g = r[:, sl(3)].astype(F32)
        decay_in = jnp.where(diff >= 0, jnp.exp(diff * log_gamma), 0.0)
        scores = _dot_nt(q, k) * (HEAD_DIM ** -0.5) * decay_in
        inner = _dot(scores.astype(BF16), v)
        state = state_scr[h]
        cross = jnp.exp((pos + 1.0) * log_gamma) * _dot(q, state.astype(BF16))
        k_dec = (k.astype(F32) * (HEAD_DIM ** -0.5) * jnp.exp((cr - 1.0 - pos) * log_gamma)).astype(BF16)
        state_scr[h] = state * jnp.exp(cr * log_gamma) + _dot_tn(k_dec, v)
        y = _head_norm(inner + cross)
        o_ref[0, :, h * HEAD_DIM:(h + 1) * HEAD_DIM] = (y * (g * jax.nn.sigmoid(g))).astype(o_ref.dtype)


def _retention(r_proj, cr):
    bsz, seq, _ = r_proj.shape
    return pl.pallas_call(
        functools.partial(_ret_kernel, cr=cr),
        grid=(bsz, seq // cr),
        in_specs=[pl.BlockSpec((1, cr, R_WIDTH), lambda b, i: (b, i, 0))],
        out_specs=pl.BlockSpec((1, cr, MIX_RET), lambda b, i: (b, i, 0)),
        out_shape=jax.ShapeDtypeStruct((bsz, seq, MIX_RET), BF16),
        scratch_shapes=[pltpu.VMEM((N_HEADS_RET, HEAD_DIM, HEAD_DIM), F32)],
        compiler_params=_params(("parallel", "arbitrary")),
        name="retention",
    )(r_proj)


def _mlstm_kernel(m_ref, sm_ref, bias_ref, cw_ref, cb_ref, tril_ref, o_ref,
                  xbuf, c_scr, n_scr, m_scr, *, cm):
    halo = 8

    @pl.when(pl.program_id(1) == 0)
    def _():
        xbuf[0:halo, :] = jnp.zeros((halo, 2 * MIX_MLSTM), F32)
        c_scr[...] = jnp.zeros_like(c_scr)
        n_scr[...] = jnp.zeros_like(n_scr)
        m_scr[...] = jnp.zeros_like(m_scr)

    mm = m_ref[0]
    xbuf[halo:halo + cm, :] = mm[:, 0:2 * MIX_MLSTM].astype(F32)
    conv = cb_ref[...]
    for j in range(CONV_WIDTH):
        off = halo - (CONV_WIDTH - 1) + j
        conv = conv + xbuf[off:off + cm, :] * cw_ref[j:j + 1, :]
    xbuf[0:halo, :] = xbuf[cm:cm + halo, :]
    qk = conv * jax.nn.sigmoid(conv)

    gates = sm_ref[0] + bias_ref[...]
    log_f = jax.nn.log_sigmoid(gates)
    b_all = jnp.dot(tril_ref[...], log_f, preferred_element_type=F32,
                    precision=lax.Precision.HIGHEST)
    lane = lax.broadcasted_iota(jnp.int32, gates.shape, 1)
    rows_t = jnp.where(lane >= S_MF, b_all, gates).T
    ri = lax.broadcasted_iota(jnp.int32, (cm, cm), 0)
    ci = lax.broadcasted_iota(jnp.int32, (cm, cm), 1)
    causal = ri >= ci
    scale = HEAD_DIM ** -0.5
    for h in range(N_HEADS_MLSTM):
        sl = slice(h * HEAD_DIM, (h + 1) * HEAD_DIM)
        q = qk[:, sl].astype(BF16)
        k = qk[:, MIX_MLSTM + h * HEAD_DIM:MIX_MLSTM + (h + 1) * HEAD_DIM]
        v = mm[:, 2 * MIX_MLSTM + h * HEAD_DIM:2 * MIX_MLSTM + (h + 1) * HEAD_DIM]
        og = mm[:, 3 * MIX_MLSTM + h * HEAD_DIM:3 * MIX_MLSTM + (h + 1) * HEAD_DIM].astype(F32)
        b_c = b_all[:, S_MF + h:S_MF + h + 1]
        i_c = gates[:, S_MI + h:S_MI + h + 1]
        b_r = rows_t[S_MF + h:S_MF + h + 1, :]
        i_r = rows_t[S_MI + h:S_MI + h + 1, :]
        m_prev = m_scr[h]
        c_mem = c_scr[h]
        n_mem = n_scr[h]

        log_w = jnp.where(causal, b_c - b_r + i_r, -jnp.inf)
        log_inter = b_c + m_prev
        m_q = jnp.maximum(log_inter, jnp.max(log_w, axis=1, keepdims=True))
        w = jnp.exp(log_w - m_q)
        inter = jnp.exp(log_inter - m_q)
        s = _dot_nt(q, k.astype(BF16)) * scale * w
        qf = q.astype(F32)
        num = _dot(s.astype(BF16), v) + inter * _dot(q, c_mem.astype(BF16))
        den = jnp.sum(s, axis=1, keepdims=True) + inter * jnp.sum(qf * n_mem, axis=1, keepdims=True)
        h_tilde = num / jnp.maximum(jnp.abs(den), jnp.exp(-m_q))

        b_last = b_c[cm - 1:cm, :]
        log_k = b_last - b_c + i_c
        m_new = jnp.maximum(b_last + m_prev, jnp.max(log_k, axis=0, keepdims=True))
        kw = k * (scale * jnp.exp(log_k - m_new))
        decay = jnp.exp(b_last + m_prev - m_new)
        c_scr[h] = decay * c_mem + _dot_tn(kw.astype(BF16), v)
        n_scr[h] = decay * n_mem + jnp.sum(kw, axis=0, keepdims=True)
        m_scr[h] = m_new

        o_ref[0, :, sl] = _head_norm(jax.nn.sigmoid(og) * h_tilde).astype(o_ref.dtype)


def _mlstm(m_proj, small, gate_bias, conv_w, conv_b, tril, cm):
    bsz, seq, _ = m_proj.shape
    const = lambda b, i: (0, 0)
    return pl.pallas_call(
        functools.partial(_mlstm_kernel, cm=cm),
        grid=(bsz, seq // cm),
        in_specs=[pl.BlockSpec((1, cm, M_WIDTH), lambda b, i: (b, i, 0)),
                  pl.BlockSpec((1, cm, S_WIDTH), lambda b, i: (b, i, 0)),
                  pl.BlockSpec((1, S_WIDTH), const),
                  pl.BlockSpec((CONV_WIDTH, 2 * MIX_MLSTM), const),
                  pl.BlockSpec((1, 2 * MIX_MLSTM), const),
                  pl.BlockSpec((cm, cm), const)],
        out_specs=pl.BlockSpec((1, cm, MIX_MLSTM), lambda b, i: (b, i, 0)),
        out_shape=jax.ShapeDtypeStruct((bsz, seq, MIX_MLSTM), BF16),
        scratch_shapes=[pltpu.VMEM((cm + 8, 2 * MIX_MLSTM), F32),
                        pltpu.VMEM((N_HEADS_MLSTM, HEAD_DIM, HEAD_DIM), F32),
                        pltpu.VMEM((N_HEADS_MLSTM, 1, HEAD_DIM), F32),
                        pltpu.VMEM((N_HEADS_MLSTM, 1, 1), F32)],
        compiler_params=_params(("parallel", "arbitrary")),
        name="mlstm",
    )(m_proj, small, gate_bias, conv_w, conv_b, tril)


def _layer_norm(z, g, b):
    mu = jnp.mean(z, axis=-1, keepdims=True)
    var = jnp.mean(jnp.square(z - mu), axis=-1, keepdims=True)
    return (z - mu) * lax.rsqrt(var + LN_EPS) * g + b


def _outproj_kernel(oa_ref, ob_ref, oc_ref, w_ref, x_ref, gm_ref, g_ref, b_ref, o_ref, *, alpha):
    mix = _dot(oa_ref[0], w_ref[0:MIX_ATTN, :])
    mix = mix + _dot(ob_ref[0], w_ref[MIX_ATTN:MIX_ATTN + MIX_RET, :])
    mix = mix + _dot(oc_ref[0], w_ref[MIX_ATTN + MIX_RET:, :])
    z = alpha * x_ref[0] + (1.0 + gm_ref[0]) * mix
    o_ref[0] = _layer_norm(z, g_ref[...], b_ref[...])


def _out_proj(o_a, o_b, o_c, w_out, x, g_m, ln_g, ln_b, tm, alpha):
    bsz, seq, d = x.shape
    row = lambda b, i: (b, i, 0)
    const = lambda b, i: (0, 0)
    return pl.pallas_call(
        functools.partial(_outproj_kernel, alpha=alpha),
        grid=(bsz, seq // tm),
        in_specs=[pl.BlockSpec((1, tm, MIX_ATTN), row),
                  pl.BlockSpec((1, tm, MIX_RET), row),
                  pl.BlockSpec((1, tm, MIX_MLSTM), row),
                  pl.BlockSpec(w_out.shape, const),
                  pl.BlockSpec((1, tm, d), row),
                  pl.BlockSpec((1, 1, d), lambda b, i: (b, 0, 0)),
                  pl.BlockSpec((1, d), const),
                  pl.BlockSpec((1, d), const)],
        out_specs=pl.BlockSpec((1, tm, d), row),
        out_shape=jax.ShapeDtypeStruct((bsz, seq, d), F32),
        compiler_params=_params(("parallel", "parallel")),
        name="out_proj",
    )(o_a, o_b, o_c, w_out, x, g_m, ln_g, ln_b)


def _route(scores, biased):
    col = lambda a, e: a[:, e:e + 1]
    epg = EXPERTS_PER_GROUP
    group_scores = []
    for g in range(N_GROUPS):
        vals = [col(biased, g * epg + j) for j in range(epg)]
        best = None
        for a in range(epg):
            for b in range(a + 1, epg):
                pair = vals[a] + vals[b]
                best = pair if best is None else jnp.maximum(best, pair)
        group_scores.append(best)
    best_g = jnp.zeros_like(group_scores[0], dtype=jnp.int32)
    best_v = group_scores[0]
    for g in range(1, N_GROUPS):
        better = group_scores[g] > best_v
        best_g = jnp.where(better, g, best_g)
        best_v = jnp.where(better, group_scores[g], best_v)
    cand_b = [sum(jnp.where(best_g == g, col(biased, g * epg + j), 0.0) for g in range(N_GROUPS))
              for j in range(epg)]
    cand_s = [sum(jnp.where(best_g == g, col(scores, g * epg + j), 0.0) for g in range(N_GROUPS))
              for j in range(epg)]

    def argmax_first(vals, skip=None):
        idx = None
        val = None
        for j, vj in enumerate(vals):
            if skip is not None:
                vj = jnp.where(skip == j, -jnp.inf, vj)
            if idx is None:
                idx, val = jnp.zeros_like(best_g), vj
            else:
                better = vj > val
                idx = jnp.where(better, j, idx)
                val = jnp.where(better, vj, val)
        return idx

    first = argmax_first(cand_b)
    second = argmax_first(cand_b, skip=first)
    w1 = sum(jnp.where(first == j, cand_s[j], 0.0) for j in range(epg))
    w2 = sum(jnp.where(second == j, cand_s[j], 0.0) for j in range(epg))
    total = w1 + w2
    e1 = best_g * epg + first
    e2 = best_g * epg + second
    lane = lax.broadcasted_iota(jnp.int32, scores.shape, 1)
    return jnp.where(lane == e1, w1 / total, 0.0) + jnp.where(lane == e2, w2 / total, 0.0)


def _moe_kernel(x_ref, sc_ref, sh_ref, gf_ref, wr_ref, br_ref, wgu_ref, wd_ref, g_ref, b_ref, o_ref,
                h_scr, gate_scr, acc_scr, *, alpha):
    e = pl.program_id(2)

    @pl.when(e == 0)
    def _():
        h = x_ref[0] * (1.0 + sc_ref[0]) + sh_ref[0]
        h_scr[...] = h.astype(BF16)
        scores = jax.nn.sigmoid(_dot(h, wr_ref[...]))
        gate_scr[...] = _route(scores, scores + br_ref[...])
        acc_scr[...] = jnp.zeros_like(acc_scr)

    h = h_scr[...]
    gu = _dot(h, wgu_ref[0])
    gate_pre = gu[:, 0:D_FF_EXPERT]
    up = gu[:, D_FF_EXPERT:]
    lane = lax.broadcasted_iota(jnp.int32, gate_scr.shape, 1)
    ge = jnp.sum(jnp.where(lane == e, gate_scr[...], 0.0), axis=1, keepdims=True)
    hid = gate_pre * jax.nn.sigmoid(gate_pre) * up * ge
    acc_scr[...] += _dot(hid.astype(BF16), wd_ref[0])

    @pl.when(e == pl.num_programs(2) - 1)
    def _():
        z = alpha * x_ref[0] + (1.0 + gf_ref[0]) * acc_scr[...]
        o_ref[0] = _layer_norm(z, g_ref[...], b_ref[...])


def _moe(x, sc, sh, g_f, w_router, b_router, w_gu, w_down, ln_g, ln_b, tm, alpha):
    bsz, seq, d = x.shape
    n_exp = w_gu.shape[0]
    row = lambda b, i, e: (b, i, 0)
    per_b = lambda b, i, e: (b, 0, 0)
    const = lambda b, i, e: (0, 0)
    return pl.pallas_call(
        functools.partial(_moe_kernel, alpha=alpha),
        grid=(bsz, seq // tm, n_exp),
        in_specs=[pl.BlockSpec((1, tm, d), row),
                  pl.BlockSpec((1, 1, d), per_b),
                  pl.BlockSpec((1, 1, d), per_b),
                  pl.BlockSpec((1, 1, d), per_b),
                  pl.BlockSpec((d, LANES), const),
                  pl.BlockSpec((1, LANES), const),
                  pl.BlockSpec((1, d, 2 * D_FF_EXPERT), lambda b, i, e: (e, 0, 0)),
                  pl.BlockSpec((1, D_FF_EXPERT, d), lambda b, i, e: (e, 0, 0)),
                  pl.BlockSpec((1, d), const),
                  pl.BlockSpec((1, d), const)],
        out_specs=pl.BlockSpec((1, tm, d), row),
        out_shape=jax.ShapeDtypeStruct((bsz, seq, d), F32),
        scratch_shapes=[pltpu.VMEM((tm, d), BF16),
                        pltpu.VMEM((tm, LANES), F32),
                        pltpu.VMEM((tm, d), F32)],
        compiler_params=_params(("parallel", "parallel", "arbitrary")),
        name="moe",
    )(x, sc, sh, g_f, w_router, b_router, w_gu, w_down, ln_g, ln_b)


def _pick(n, pref):
    t = min(pref, n)
    while n % t:
        t //= 2
    return t


def _rope_tables(positions):
    half = HEAD_DIM // 2
    inv_freq = ROPE_THETA ** (-jnp.arange(half, dtype=F32) / half)
    ang = positions.astype(F32)[..., None] * inv_freq
    cos, sin = jnp.cos(ang), jnp.sin(ang)
    reps = LANES // HEAD_DIM
    return (jnp.concatenate([cos, cos] * reps, axis=-1),
            jnp.concatenate([-sin, sin] * reps, axis=-1))


def _reorder_w_in(w_in):
    d = w_in.shape[0]
    mid0 = A_WIDTH + IDX_DIM + N_IDX_HEADS
    mid1 = mid0 + R_WIDTH + M_WIDTH
    pad = jnp.zeros((d, S_WIDTH - IDX_DIM - N_IDX_HEADS - 2 * N_HEADS_MLSTM), w_in.dtype)
    return jnp.concatenate([w_in[:, :A_WIDTH], w_in[:, mid0:mid1],
                            w_in[:, A_WIDTH:mid0], w_in[:, mid1:], pad], axis=1).astype(BF16)


def kernel(x, c, positions, w_ada, b_ada, w_in, i_bias, f_bias, conv_w, conv_b, w_out, ln_mix_g, ln_mix_b,
           w_router, b_router, w_gate, w_up, w_down, ln_ffn_g, ln_ffn_b):
    bsz, seq, d = x.shape
    depth = w_ada.shape[0]
    alpha = (2.0 * depth) ** 0.25

    tm = _pick(seq, 512)
    tq = _pick(seq, 128)
    kc = _pick(seq, 512)
    cr = _pick(seq, 256)
    cm = _pick(seq, 256)
    tmoe = _pick(seq, 1024)

    cos_t, sin_t = _rope_tables(positions)
    c_pad = jnp.zeros((8, d), F32).at[:bsz].set(c)
    mod = _ada_mod(c_pad, w_ada, b_ada, _pick(6 * d, 1536))

    tri_incl = (jnp.arange(kc)[:, None] <= jnp.arange(kc)[None, :]).astype(BF16)
    tril = (jnp.arange(cm)[:, None] >= jnp.arange(cm)[None, :]).astype(F32)
    w_router_p = jnp.zeros((d, LANES), F32).at[:, :N_EXPERTS].set(w_router)
    b_router_p = jnp.zeros((1, LANES), F32).at[0, :N_EXPERTS].set(b_router)

    for l in range(depth):
        parts = [mod[l, :bsz, j * d:(j + 1) * d].reshape(bsz, 1, d) for j in range(6)]
        sh_m, sc_m, g_m, sh_f, sc_f, g_f = parts
        a_proj, r_proj, m_proj, small = _in_proj(x, sc_m, sh_m, _reorder_w_in(w_in[l]), cos_t, sin_t, tm)
        o_a = _dsa(a_proj, small, tri_incl, tq, kc)
        o_b = _retention(r_proj, cr)
        gate_bias = (jnp.zeros((1, S_WIDTH), F32).at[0, S_MI:S_MI + N_HEADS_MLSTM].set(i_bias[l])
                     .at[0, S_MF:S_MF + N_HEADS_MLSTM].set(f_bias[l]))
        o_c = _mlstm(m_proj, small, gate_bias, conv_w[l], conv_b[l].reshape(1, -1), tril, cm)
        x = _out_proj(o_a, o_b, o_c, w_out[l].astype(BF16), x, g_m,
                      ln_mix_g[l].reshape(1, d), ln_mix_b[l].reshape(1, d), tm, alpha)
        w_gu = jnp.concatenate([w_gate[l], w_up[l]], axis=-1).astype(BF16)
        x = _moe(x, sc_f, sh_f, g_f, w_router_p, b_router_p, w_gu, w_down[l].astype(BF16),
                 ln_ffn_g[l].reshape(1, d), ln_ffn_b[l].reshape(1, d), tmoe, alpha)
    return x
```

```python
import functools

import numpy as np
import jax
import jax.numpy as jnp
from jax import lax
from jax.experimental import pallas as pl
from jax.experimental.pallas import tpu as pltpu

F32 = jnp.float32
BF16 = jnp.bfloat16

HEAD_DIM = 64
CHUNK = 64
N_HEADS_ATTN = 8
N_IDX_HEADS = 4
IDX_DIM = 64
TOPK_MAX = 256
N_HEADS_RET = 4
N_HEADS_MLSTM = 4
CONV_WIDTH = 4
ROPE_THETA = 10000.0
N_EXPERTS = 16
N_GROUPS = 4
EXPERTS_PER_GROUP = N_EXPERTS // N_GROUPS
D_FF_EXPERT = 256
LN_EPS = 1e-5

MIX_ATTN = N_HEADS_ATTN * HEAD_DIM
MIX_RET = N_HEADS_RET * HEAD_DIM
MIX_MLSTM = N_HEADS_MLSTM * HEAD_DIM

LANES = 128
VMEM_LIMIT = 56 * 1024 * 1024

IQ_WIDTH = N_IDX_HEADS * IDX_DIM
R_WIDTH = 4 * MIX_RET
M_WIDTH = 4 * MIX_MLSTM
S_WIDTH = LANES
S_IW = 0
S_MI = S_IW + N_IDX_HEADS
S_MF = S_MI + N_HEADS_MLSTM
S_ROWS = 16
V_PAD = N_HEADS_ATTN * LANES
V_ROWS = HEAD_DIM + 16
OFF_Q = 0
OFF_K = OFF_Q + MIX_ATTN
OFF_IQ = OFF_K + MIX_ATTN
OFF_R = OFF_IQ + IQ_WIDTH
OFF_M = OFF_R + R_WIDTH
OFF_IK = OFF_M + M_WIDTH
OFF_S = OFF_IK + LANES
OFF_V = OFF_S + S_WIDTH
W_TOTAL = OFF_V + V_PAD

INT_MIN = -2 ** 31
NEG_BIG = -1e30
LOG2_E = 1.4426950408889634


def _dot(a, b):
    return jnp.dot(a, b, preferred_element_type=F32)


def _dot_nt(a, b):
    return lax.dot_general(a, b, (((1,), (1,)), ((), ())), preferred_element_type=F32)


def _dot_tn(a, b):
    return lax.dot_general(a, b, (((0,), (0,)), ((), ())), preferred_element_type=F32)


def _params(sem):
    return pltpu.CompilerParams(dimension_semantics=sem, vmem_limit_bytes=VMEM_LIMIT)


def _ada_kernel(c_ref, w_ref, b_ref, o_ref):
    c = c_ref[...]
    c_act = c * jax.nn.sigmoid(c)
    o_ref[0] = _dot(c_act, w_ref[0]) + b_ref[0]


def _ada_mod(c_pad, w_ada, b_ada, tn):
    depth, d, n = w_ada.shape
    rows = c_pad.shape[0]
    return pl.pallas_call(
        _ada_kernel,
        grid=(depth, n // tn),
        in_specs=[pl.BlockSpec((rows, d), lambda l, j: (0, 0)),
                  pl.BlockSpec((1, d, tn), lambda l, j: (l, 0, j)),
                  pl.BlockSpec((1, 1, tn), lambda l, j: (l, 0, j))],
        out_specs=pl.BlockSpec((1, rows, tn), lambda l, j: (l, 0, j)),
        out_shape=jax.ShapeDtypeStruct((depth, rows, n), F32),
        compiler_params=_params(("parallel", "parallel")),
        name="ada_mod",
    )(c_pad, w_ada, b_ada.reshape(depth, 1, n))


def _rope(y, cos, sin):
    w = y.shape[1]
    reps = w // LANES
    cosw = jnp.concatenate([cos] * reps, axis=1) if reps > 1 else cos
    sinw = jnp.concatenate([sin] * reps, axis=1) if reps > 1 else sin
    lane = lax.broadcasted_iota(jnp.int32, y.shape, 1)
    first = (lane % HEAD_DIM) < (HEAD_DIM // 2)
    partner = jnp.where(first, pltpu.roll(y, w - HEAD_DIM // 2, 1), pltpu.roll(y, HEAD_DIM // 2, 1))
    return y * cosw + partner * sinw


def _inproj_kernel(x_ref, sc_ref, sh_ref, w_ref, cos_ref, sin_ref,
                   q_ref, k_ref, iq_ref, ik_ref, r_ref, m_ref, s_ref, st_ref, vt_ref, *, kb):
    h = (x_ref[0] * (1.0 + sc_ref[0]) + sh_ref[0]).astype(BF16)
    cos = cos_ref[0]
    sin = sin_ref[0]

    def proj(start, width):
        return _dot(h, w_ref[:, start:start + width])

    q_ref[0] = (_rope(proj(OFF_Q, MIX_ATTN), cos, sin) * (HEAD_DIM ** -0.5 * LOG2_E)).astype(BF16)
    k_ref[0] = _rope(proj(OFF_K, MIX_ATTN), cos, sin).astype(BF16)
    iq_ref[0] = _rope(proj(OFF_IQ, IQ_WIDTH), cos, sin).astype(BF16)
    ik_ref[0] = _rope(proj(OFF_IK, LANES), cos, sin).astype(BF16)
    r_ref[0, :, 0:2 * MIX_RET] = _rope(proj(OFF_R, 2 * MIX_RET), cos, sin).astype(BF16)
    r_ref[0, :, 2 * MIX_RET:R_WIDTH] = proj(OFF_R + 2 * MIX_RET, 2 * MIX_RET).astype(BF16)
    m_ref[0, :, 0:2 * MIX_MLSTM] = proj(OFF_M, 2 * MIX_MLSTM).astype(BF16)
    m_ref[0, :, 2 * MIX_MLSTM:M_WIDTH] = proj(OFF_M + 2 * MIX_MLSTM, 2 * MIX_MLSTM).astype(BF16)
    y = proj(OFF_S, S_WIDTH)
    s_ref[0] = y
    st_ref[0] = y.T[0:S_ROWS, :]
    tm = h.shape[0]
    for part in range(V_PAD // MIX_ATTN):
        yv = proj(OFF_V + part * MIX_ATTN, MIX_ATTN)
        lane = lax.broadcasted_iota(jnp.int32, yv.shape, 1)
        yvt = jnp.where(lane % LANES == HEAD_DIM, 1.0, yv).T.astype(BF16)
        for j in range(tm // kb):
            vt_ref[0, j, part * MIX_ATTN:(part + 1) * MIX_ATTN, :] = yvt[:, j * kb:(j + 1) * kb]


def _in_proj(x, sc, sh, w, cos_t, sin_t, tm, kb):
    bsz, seq, d = x.shape
    row = lambda b, i: (b, i, 0)
    per_b = lambda b, i: (b, 0, 0)
    widths = (MIX_ATTN, MIX_ATTN, IQ_WIDTH, LANES, R_WIDTH, M_WIDTH)
    return pl.pallas_call(
        functools.partial(_inproj_kernel, kb=kb),
        grid=(bsz, seq // tm),
        in_specs=[pl.BlockSpec((1, tm, d), row),
                  pl.BlockSpec((1, 1, d), per_b),
                  pl.BlockSpec((1, 1, d), per_b),
                  pl.BlockSpec((d, W_TOTAL), lambda b, i: (0, 0)),
                  pl.BlockSpec((1, tm, LANES), row),
                  pl.BlockSpec((1, tm, LANES), row)],
        out_specs=[pl.BlockSpec((1, tm, wd), row) for wd in widths]
                  + [pl.BlockSpec((1, tm, S_WIDTH), row),
                     pl.BlockSpec((1, S_ROWS, tm), lambda b, i: (b, 0, i)),
                     pl.BlockSpec((1, tm // kb, V_PAD, kb), lambda b, i: (b, i, 0, 0))],
        out_shape=[jax.ShapeDtypeStruct((bsz, seq, wd), BF16) for wd in widths]
                  + [jax.ShapeDtypeStruct((bsz, seq, S_WIDTH), F32),
                     jax.ShapeDtypeStruct((bsz, S_ROWS, seq), F32),
                     jax.ShapeDtypeStruct((bsz, seq // kb, V_PAD, kb), BF16)],
        compiler_params=_params(("parallel", "parallel")),
        name="in_proj",
    )(x, sc, sh, w, cos_t, sin_t)


def _dsa_kernel(q_ref, k_ref, iq_ref, ik_ref, st_ref, vt_ref, tril_ref, o_ref,
                key_scr, bias_scr, s_scr, p_scr, m_scr, alpha_scr, acc_scr, *, tq, kb, topk):
    q0 = pl.program_id(1) * tq
    n_blocks = (q0 + tq + kb - 1) // kb
    qpos = q0 + lax.broadcasted_iota(jnp.int32, (1, tq), 1)
    q_limit = (qpos // CHUNK + 1) * CHUNK
    krow = lax.broadcasted_iota(jnp.int32, (kb, 1), 0)

    def head_of_pair(x, h):
        pair = x[:, (h // 2) * LANES:(h // 2 + 1) * LANES]
        lane = lax.broadcasted_iota(jnp.int32, pair.shape, 1)
        keep = (lane < HEAD_DIM) if h % 2 == 0 else (lane >= HEAD_DIM)
        return jnp.where(keep, pair, jnp.zeros_like(pair))

    iw = st_ref[0][S_IW:S_IW + N_IDX_HEADS, :] * (N_IDX_HEADS ** -0.5 * IDX_DIM ** -0.5)
    iq = iq_ref[0]
    iq_heads = [head_of_pair(iq, h) for h in range(N_IDX_HEADS)]

    def score_body(c, carry):
        k0 = pl.multiple_of(c * kb, kb)
        ik2 = ik_ref[0, pl.ds(k0, kb), :]
        score = jnp.zeros((kb, tq), F32)
        for h in range(N_IDX_HEADS):
            score = score + jnp.maximum(_dot_nt(ik2, iq_heads[h]), 0.0) * iw[h:h + 1, :]
        bits = pltpu.bitcast(score, jnp.int32)
        key = jnp.where(bits >= 0, bits, bits ^ jnp.int32(0x7FFFFFFF))
        key_scr[c] = jnp.where(k0 + krow < q_limit, key, jnp.int32(INT_MIN))
        return carry

    lax.fori_loop(0, n_blocks, score_body, 0)

    def count_ge(thr):
        def body(c, acc):
            return acc + jnp.sum(jnp.where(key_scr[c] >= thr, 1, 0), axis=0, keepdims=True)
        return lax.fori_loop(0, n_blocks, body, jnp.zeros((1, tq), jnp.int32))

    def bit_body(i, t):
        cand = t | lax.shift_left(jnp.int32(1), 31 - i)
        cnt = count_ge(cand ^ jnp.int32(INT_MIN))
        return jnp.where(cnt >= topk, cand, t)

    t = lax.fori_loop(0, 32, bit_body, jnp.zeros((1, tq), jnp.int32))
    thr = jnp.maximum(t ^ jnp.int32(INT_MIN), jnp.int32(INT_MIN + 1))
    int_max = jnp.int32(2 ** 31 - 1)
    cnt_gt = jnp.where(thr == int_max, 0, count_ge(jnp.where(thr == int_max, thr, thr + 1)))
    need = (topk - cnt_gt).astype(F32)

    q = q_ref[0]
    q_heads = [head_of_pair(q, h) for h in range(N_HEADS_ATTN)]
    m_scr[...] = jnp.full(m_scr.shape, NEG_BIG, F32)
    alpha_scr[...] = jnp.ones(alpha_scr.shape, F32)
    acc_scr[...] = jnp.zeros(acc_scr.shape, F32)
    p_scr[...] = jnp.zeros(p_scr.shape, BF16)

    def stage_mask(c, ties_before):
        key = key_scr[c]
        tie = key == thr
        rank = _dot(tril_ref[...], jnp.where(tie, 1.0, 0.0).astype(BF16)) + ties_before
        sel = (key > thr) | (tie & (rank <= need))
        bias_scr[...] = jnp.where(sel, 0.0, NEG_BIG)
        return rank[kb - 1:kb, :]

    def stage_logits(c, h):
        k0 = pl.multiple_of(c * kb, kb)
        kp = k_ref[0, pl.ds(k0, kb), (h // 2) * LANES:(h // 2 + 1) * LANES]
        s_scr[h] = _dot_nt(kp, q_heads[h])

    def stage_softmax(h):
        for half in range(tq // LANES):
            ln = slice(half * LANES, (half + 1) * LANES)
            s = s_scr[h, :, ln] + bias_scr[:, ln]
            m_old = m_scr[h, :, ln]
            m_new = jnp.maximum(m_old, jnp.max(s, axis=0, keepdims=True))
            p_scr[h, :, ln] = jnp.exp2(s - m_new).astype(BF16)
            alpha_scr[h, :, ln] = jnp.exp2(m_old - m_new)
            m_scr[h, :, ln] = m_new

    def stage_values(c, h):
        vt = vt_ref[0, c, h * LANES:h * LANES + V_ROWS, :]
        acc_scr[h, 0:V_ROWS, :] = alpha_scr[h] * acc_scr[h, 0:V_ROWS, :] + _dot(vt, p_scr[h])

    ties0 = stage_mask(0, jnp.zeros((1, tq), F32))
    for h in range(N_HEADS_ATTN):
        stage_logits(0, h)

    def attn_body(j, ties_before):
        c_old = jnp.maximum(j - 2, 0)
        for h in range(N_HEADS_ATTN):
            stage_values(c_old, h)
            stage_softmax(h)
            stage_logits(j, h)
        return stage_mask(j, ties_before)

    lax.fori_loop(1, n_blocks, attn_body, ties0)
    for h in range(N_HEADS_ATTN):
        stage_values(jnp.maximum(n_blocks - 2, 0), h)
        stage_softmax(h)
    for h in range(N_HEADS_ATTN):
        stage_values(n_blocks - 1, h)
    for h in range(N_HEADS_ATTN):
        acc = acc_scr[h]
        out = acc * (1.0 / acc[HEAD_DIM:HEAD_DIM + 1, :])
        o_ref[0, :, h * LANES:(h + 1) * LANES] = out.T.astype(o_ref.dtype)


def _dsa(q, k, iq, ik2, small_t, v_t, tril, tq, kb):
    bsz, seq, _ = q.shape
    topk = min(TOPK_MAX, seq // 4)
    kern = functools.partial(_dsa_kernel, tq=tq, kb=kb, topk=topk)
    return pl.pallas_call(
        kern,
        grid=(bsz, seq // tq),
        in_specs=[pl.BlockSpec((1, tq, MIX_ATTN), lambda b, i: (b, i, 0)),
                  pl.BlockSpec((1, seq, MIX_ATTN), lambda b, i: (b, 0, 0)),
                  pl.BlockSpec((1, tq, IQ_WIDTH), lambda b, i: (b, i, 0)),
                  pl.BlockSpec((1, seq, LANES), lambda b, i: (b, 0, 0)),
                  pl.BlockSpec((1, S_ROWS, tq), lambda b, i: (b, 0, i)),
                  pl.BlockSpec((1, seq // kb, V_PAD, kb), lambda b, i: (b, 0, 0, 0)),
                  pl.BlockSpec((kb, kb), lambda b, i: (0, 0))],
        out_specs=pl.BlockSpec((1, tq, V_PAD), lambda b, i: (b, i, 0)),
        out_shape=jax.ShapeDtypeStruct((bsz, seq, V_PAD), BF16),
        scratch_shapes=[pltpu.VMEM((seq // kb, kb, tq), jnp.int32),
                        pltpu.VMEM((kb, tq), F32),
                        pltpu.VMEM((N_HEADS_ATTN, kb, tq), F32),
                        pltpu.VMEM((N_HEADS_ATTN, kb, tq), BF16),
                        pltpu.VMEM((N_HEADS_ATTN, 1, tq), F32),
                        pltpu.VMEM((N_HEADS_ATTN, 1, tq), F32),
                        pltpu.VMEM((N_HEADS_ATTN, LANES, tq), F32)],
        compiler_params=_params(("parallel", "arbitrary")),
        name="dsa",
    )(q, k, iq, ik2, small_t, v_t, tril)


def _head_norm(y):
    mu = jnp.mean(y, axis=-1, keepdims=True)
    var = jnp.mean(jnp.square(y - mu), axis=-1, keepdims=True)
    return (y - mu) * lax.rsqrt(var + LN_EPS)


def _ret_kernel(r_ref, o_ref, state_scr, *, cr):
    @pl.when(pl.program_id(1) == 0)
    def _():
        state_scr[...] = jnp.zeros_like(state_scr)

    r = r_ref[0]
    ri = lax.broadcasted_iota(jnp.int32, (cr, cr), 0)
    ci = lax.broadcasted_iota(jnp.int32, (cr, cr), 1)
    diff = (ri - ci).astype(F32)
    pos = lax.broadcasted_iota(jnp.int32, (cr, 1), 0).astype(F32)
    for h in range(N_HEADS_RET):
        log_gamma = jnp.log1p(jnp.full((1, 1), -(2.0 ** (-5.0 - h)), F32))
        sl = lambda part: slice(part * MIX_RET + h * HEAD_DIM, part * MIX_RET + (h + 1) * HEAD_DIM)
        q = r[:, sl(0)]
        k = r[:, sl(1)]
        v = r[:, sl(2)]
        g = r[:, sl(3)].astype(F32)
        decay_in = jnp.where(diff >= 0, jnp.exp(diff * log_gamma), 0.0)
        scores = _dot_nt(q, k) * (HEAD_DIM ** -0.5) * decay_in
        inner = _dot(scores.astype(BF16), v)
        state = state_scr[h]
        cross = jnp.exp((pos + 1.0) * log_gamma) * _dot(q, state.astype(BF16))
        k_dec = (k.astype(F32) * (HEAD_DIM ** -0.5) * jnp.exp((cr - 1.0 - pos) * log_gamma)).astype(BF16)
        state_scr[h] = state * jnp.exp(cr * log_gamma) + _dot_tn(k_dec, v)
        y = _head_norm(inner + cross)
        o_ref[0, :, h * HEAD_DIM:(h + 1) * HEAD_DIM] = (y * (g * jax.nn.sigmoid(g))).astype(o_ref.dtype)


def _retention(r_proj, cr):
    bsz, seq, _ = r_proj.shape
    return pl.pallas_call(
        functools.partial(_ret_kernel, cr=cr),
        grid=(bsz, seq // cr),
        in_specs=[pl.BlockSpec((1, cr, R_WIDTH), lambda b, i: (b, i, 0))],
        out_specs=pl.BlockSpec((1, cr, MIX_RET), lambda b, i: (b, i, 0)),
        out_shape=jax.ShapeDtypeStruct((bsz, seq, MIX_RET), BF16),
        scratch_shapes=[pltpu.VMEM((N_HEADS_RET, HEAD_DIM, HEAD_DIM), F32)],
        compiler_params=_params(("parallel", "arbitrary")),
        name="retention",
    )(r_proj)


def _mlstm_kernel(m_ref, sm_ref, bias_ref, cw_ref, cb_ref, tril_ref, o_ref,
                  xbuf, c_scr, n_scr, m_scr, *, cm):
    halo = 8

    @pl.when(pl.program_id(1) == 0)
    def _():
        xbuf[0:halo, :] = jnp.zeros((halo, 2 * MIX_MLSTM), F32)
        c_scr[...] = jnp.zeros_like(c_scr)
        n_scr[...] = jnp.zeros_like(n_scr)
        m_scr[...] = jnp.zeros_like(m_scr)

    mm = m_ref[0]
    xbuf[halo:halo + cm, :] = mm[:, 0:2 * MIX_MLSTM].astype(F32)
    conv = cb_ref[...]
    for j in range(CONV_WIDTH):
        off = halo - (CONV_WIDTH - 1) + j
        conv = conv + xbuf[off:off + cm, :] * cw_ref[j:j + 1, :]
    xbuf[0:halo, :] = xbuf[cm:cm + halo, :]
    qk = conv * jax.nn.sigmoid(conv)

    gates = sm_ref[0] + bias_ref[...]
    log_f = jax.nn.log_sigmoid(gates)
    b_all = jnp.dot(tril_ref[...], log_f, preferred_element_type=F32,
                    precision=lax.Precision.HIGHEST)
    lane = lax.broadcasted_iota(jnp.int32, gates.shape, 1)
    rows_t = jnp.where(lane >= S_MF, b_all, gates).T
    ri = lax.broadcasted_iota(jnp.int32, (cm, cm), 0)
    ci = lax.broadcasted_iota(jnp.int32, (cm, cm), 1)
    causal = ri >= ci
    scale = HEAD_DIM ** -0.5
    for h in range(N_HEADS_MLSTM):
        sl = slice(h * HEAD_DIM, (h + 1) * HEAD_DIM)
        q = qk[:, sl].astype(BF16)
        k = qk[:, MIX_MLSTM + h * HEAD_DIM:MIX_MLSTM + (h + 1) * HEAD_DIM]
        v = mm[:, 2 * MIX_MLSTM + h * HEAD_DIM:2 * MIX_MLSTM + (h + 1) * HEAD_DIM]
        og = mm[:, 3 * MIX_MLSTM + h * HEAD_DIM:3 * MIX_MLSTM + (h + 1) * HEAD_DIM].astype(F32)
        b_c = b_all[:, S_MF + h:S_MF + h + 1]
        i_c = gates[:, S_MI + h:S_MI + h + 1]
        b_r = rows_t[S_MF + h:S_MF + h + 1, :]
        i_r = rows_t[S_MI + h:S_MI + h + 1, :]
        m_prev = m_scr[h]
        c_mem = c_scr[h]
        n_mem = n_scr[h]

        log_w = jnp.where(causal, b_c - b_r + i_r, -jnp.inf)
        log_inter = b_c + m_prev
        m_q = jnp.maximum(log_inter, jnp.max(log_w, axis=1, keepdims=True))
        w = jnp.exp(log_w - m_q)
        inter = jnp.exp(log_inter - m_q)
        s = _dot_nt(q, k.astype(BF16)) * scale * w
        qf = q.astype(F32)
        num = _dot(s.astype(BF16), v) + inter * _dot(q, c_mem.astype(BF16))
        den = jnp.sum(s, axis=1, keepdims=True) + inter * jnp.sum(qf * n_mem, axis=1, keepdims=True)
        h_tilde = num / jnp.maximum(jnp.abs(den), jnp.exp(-m_q))

        b_last = b_c[cm - 1:cm, :]
        log_k = b_last - b_c + i_c
        m_new = jnp.maximum(b_last + m_prev, jnp.max(log_k, axis=0, keepdims=True))
        kw = k * (scale * jnp.exp(log_k - m_new))
        decay = jnp.exp(b_last + m_prev - m_new)
        c_scr[h] = decay * c_mem + _dot_tn(kw.astype(BF16), v)
        n_scr[h] = decay * n_mem + jnp.sum(kw, axis=0, keepdims=True)
        m_scr[h] = m_new

        o_ref[0, :, sl] = _head_norm(jax.nn.sigmoid(og) * h_tilde).astype(o_ref.dtype)


def _mlstm(m_proj, small, gate_bias, conv_w, conv_b, tril, cm):
    bsz, seq, _ = m_proj.shape
    const = lambda b, i: (0, 0)
    return pl.pallas_call(
        functools.partial(_mlstm_kernel, cm=cm),
        grid=(bsz, seq // cm),
        in_specs=[pl.BlockSpec((1, cm, M_WIDTH), lambda b, i: (b, i, 0)),
                  pl.BlockSpec((1, cm, S_WIDTH), lambda b, i: (b, i, 0)),
                  pl.BlockSpec((1, S_WIDTH), const),
                  pl.BlockSpec((CONV_WIDTH, 2 * MIX_MLSTM), const),
                  pl.BlockSpec((1, 2 * MIX_MLSTM), const),
                  pl.BlockSpec((cm, cm), const)],
        out_specs=pl.BlockSpec((1, cm, MIX_MLSTM), lambda b, i: (b, i, 0)),
        out_shape=jax.ShapeDtypeStruct((bsz, seq, MIX_MLSTM), BF16),
        scratch_shapes=[pltpu.VMEM((cm + 8, 2 * MIX_MLSTM), F32),
                        pltpu.VMEM((N_HEADS_MLSTM, HEAD_DIM, HEAD_DIM), F32),
                        pltpu.VMEM((N_HEADS_MLSTM, 1, HEAD_DIM), F32),
                        pltpu.VMEM((N_HEADS_MLSTM, 1, 1), F32)],
        compiler_params=_params(("parallel", "arbitrary")),
        name="mlstm",
    )(m_proj, small, gate_bias, conv_w, conv_b, tril)


def _layer_norm(z, g, b):
    mu = jnp.mean(z, axis=-1, keepdims=True)
    var = jnp.mean(jnp.square(z - mu), axis=-1, keepdims=True)
    return (z - mu) * lax.rsqrt(var + LN_EPS) * g + b


def _outproj_kernel(oa_ref, ob_ref, oc_ref, w_ref, x_ref, gm_ref, g_ref, b_ref, o_ref, *, alpha):
    mix = _dot(oa_ref[0], w_ref[0:V_PAD, :])
    mix = mix + _dot(ob_ref[0], w_ref[V_PAD:V_PAD + MIX_RET, :])
    mix = mix + _dot(oc_ref[0], w_ref[V_PAD + MIX_RET:, :])
    z = alpha * x_ref[0] + (1.0 + gm_ref[0]) * mix
    o_ref[0] = _layer_norm(z, g_ref[...], b_ref[...])


def _out_proj(o_a, o_b, o_c, w_out, x, g_m, ln_g, ln_b, tm, alpha):
    bsz, seq, d = x.shape
    row = lambda b, i: (b, i, 0)
    const = lambda b, i: (0, 0)
    return pl.pallas_call(
        functools.partial(_outproj_kernel, alpha=alpha),
        grid=(bsz, seq // tm),
        in_specs=[pl.BlockSpec((1, tm, V_PAD), row),
                  pl.BlockSpec((1, tm, MIX_RET), row),
                  pl.BlockSpec((1, tm, MIX_MLSTM), row),
                  pl.BlockSpec(w_out.shape, const),
                  pl.BlockSpec((1, tm, d), row),
                  pl.BlockSpec((1, 1, d), lambda b, i: (b, 0, 0)),
                  pl.BlockSpec((1, d), const),
                  pl.BlockSpec((1, d), const)],
        out_specs=pl.BlockSpec((1, tm, d), row),
        out_shape=jax.ShapeDtypeStruct((bsz, seq, d), F32),
        compiler_params=_params(("parallel", "parallel")),
        name="out_proj",
    )(o_a, o_b, o_c, w_out, x, g_m, ln_g, ln_b)


def _route(scores, biased):
    col = lambda a, e: a[:, e:e + 1]
    epg = EXPERTS_PER_GROUP
    group_scores = []
    for g in range(N_GROUPS):
        vals = [col(biased, g * epg + j) for j in range(epg)]
        best = None
        for a in range(epg):
            for b in range(a + 1, epg):
                pair = vals[a] + vals[b]
                best = pair if best is None else jnp.maximum(best, pair)
        group_scores.append(best)
    best_g = jnp.zeros_like(group_scores[0], dtype=jnp.int32)
    best_v = group_scores[0]
    for g in range(1, N_GROUPS):
        better = group_scores[g] > best_v
        best_g = jnp.where(better, g, best_g)
        best_v = jnp.where(better, group_scores[g], best_v)
    cand_b = [sum(jnp.where(best_g == g, col(biased, g * epg + j), 0.0) for g in range(N_GROUPS))
              for j in range(epg)]
    cand_s = [sum(jnp.where(best_g == g, col(scores, g * epg + j), 0.0) for g in range(N_GROUPS))
              for j in range(epg)]

    def argmax_first(vals, skip=None):
        idx = None
        val = None
        for j, vj in enumerate(vals):
            if skip is not None:
                vj = jnp.where(skip == j, -jnp.inf, vj)
            if idx is None:
                idx, val = jnp.zeros_like(best_g), vj
            else:
                better = vj > val
                idx = jnp.where(better, j, idx)
                val = jnp.where(better, vj, val)
        return idx

    first = argmax_first(cand_b)
    second = argmax_first(cand_b, skip=first)
    w1 = sum(jnp.where(first == j, cand_s[j], 0.0) for j in range(epg))
    w2 = sum(jnp.where(second == j, cand_s[j], 0.0) for j in range(epg))
    total = w1 + w2
    e1 = best_g * epg + first
    e2 = best_g * epg + second
    lane = lax.broadcasted_iota(jnp.int32, scores.shape, 1)
    return jnp.where(lane == e1, w1 / total, 0.0) + jnp.where(lane == e2, w2 / total, 0.0)


def _moe_kernel(x_ref, sc_ref, sh_ref, gf_ref, wr_ref, br_ref, wgu_ref, wd_ref, g_ref, b_ref, o_ref,
                h_scr, gate_scr, acc_scr, *, alpha):
    e = pl.program_id(2)

    @pl.when(e == 0)
    def _():
        h = x_ref[0] * (1.0 + sc_ref[0]) + sh_ref[0]
        h_scr[...] = h.astype(BF16)
        scores = jax.nn.sigmoid(_dot(h, wr_ref[...]))
        gate_scr[...] = _route(scores, scores + br_ref[...])
        acc_scr[...] = jnp.zeros_like(acc_scr)

    h = h_scr[...]
    gu = _dot(h, wgu_ref[0])
    gate_pre = gu[:, 0:D_FF_EXPERT]
    up = gu[:, D_FF_EXPERT:]
    lane = lax.broadcasted_iota(jnp.int32, gate_scr.shape, 1)
    ge = jnp.sum(jnp.where(lane == e, gate_scr[...], 0.0), axis=1, keepdims=True)
    hid = gate_pre * jax.nn.sigmoid(gate_pre) * up * ge
    acc_scr[...] += _dot(hid.astype(BF16), wd_ref[0])

    @pl.when(e == pl.num_programs(2) - 1)
    def _():
        z = alpha * x_ref[0] + (1.0 + gf_ref[0]) * acc_scr[...]
        o_ref[0] = _layer_norm(z, g_ref[...], b_ref[...])


def _moe(x, sc, sh, g_f, w_router, b_router, w_gu, w_down, ln_g, ln_b, tm, alpha):
    bsz, seq, d = x.shape
    n_exp = w_gu.shape[0]
    row = lambda b, i, e: (b, i, 0)
    per_b = lambda b, i, e: (b, 0, 0)
    const = lambda b, i, e: (0, 0)
    return pl.pallas_call(
        functools.partial(_moe_kernel, alpha=alpha),
        grid=(bsz, seq // tm, n_exp),
        in_specs=[pl.BlockSpec((1, tm, d), row),
                  pl.BlockSpec((1, 1, d), per_b),
                  pl.BlockSpec((1, 1, d), per_b),
                  pl.BlockSpec((1, 1, d), per_b),
                  pl.BlockSpec((d, LANES), const),
                  pl.BlockSpec((1, LANES), const),
                  pl.BlockSpec((1, d, 2 * D_FF_EXPERT), lambda b, i, e: (e, 0, 0)),
                  pl.BlockSpec((1, D_FF_EXPERT, d), lambda b, i, e: (e, 0, 0)),
                  pl.BlockSpec((1, d), const),
                  pl.BlockSpec((1, d), const)],
        out_specs=pl.BlockSpec((1, tm, d), row),
        out_shape=jax.ShapeDtypeStruct((bsz, seq, d), F32),
        scratch_shapes=[pltpu.VMEM((tm, d), BF16),
                        pltpu.VMEM((tm, LANES), F32),
                        pltpu.VMEM((tm, d), F32)],
        compiler_params=_params(("parallel", "parallel", "arbitrary")),
        name="moe",
    )(x, sc, sh, g_f, w_router, b_router, w_gu, w_down, ln_g, ln_b)


def _pick(n, pref):
    t = min(pref, n)
    while n % t:
        t //= 2
    return t


def _rope_tables(positions):
    half = HEAD_DIM // 2
    inv_freq = ROPE_THETA ** (-jnp.arange(half, dtype=F32) / half)
    ang = positions.astype(F32)[..., None] * inv_freq
    cos, sin = jnp.cos(ang), jnp.sin(ang)
    reps = LANES // HEAD_DIM
    return (jnp.concatenate([cos, cos] * reps, axis=-1),
            jnp.concatenate([-sin, sin] * reps, axis=-1))


def _reorder_w_in(w_in):
    d = w_in.shape[0]
    o_v = 2 * MIX_ATTN
    o_iq = o_v + MIX_ATTN
    o_ik = o_iq + IQ_WIDTH
    o_iw = o_ik + IDX_DIM
    o_r = o_iw + N_IDX_HEADS
    o_m = o_r + R_WIDTH
    o_g = o_m + M_WIDTH
    ik = w_in[:, o_ik:o_iw]
    small = jnp.concatenate([w_in[:, o_iw:o_r], w_in[:, o_g:],
                             jnp.zeros((d, S_WIDTH - N_IDX_HEADS - 2 * N_HEADS_MLSTM), w_in.dtype)], axis=1)
    v = w_in[:, o_v:o_iq].reshape(d, N_HEADS_ATTN, HEAD_DIM)
    v_pad = jnp.concatenate([v, jnp.zeros_like(v)], axis=-1).reshape(d, V_PAD)
    return jnp.concatenate([w_in[:, :o_v], w_in[:, o_iq:o_ik], w_in[:, o_r:o_g],
                            ik, ik, small, v_pad], axis=1).astype(BF16)


def _pad_w_out(w_out):
    d = w_out.shape[1]
    wa = w_out[:MIX_ATTN].reshape(N_HEADS_ATTN, HEAD_DIM, d)
    wa = jnp.concatenate([wa, jnp.zeros_like(wa)], axis=1).reshape(V_PAD, d)
    return jnp.concatenate([wa, w_out[MIX_ATTN:]], axis=0).astype(BF16)


def kernel(x, c, positions, w_ada, b_ada, w_in, i_bias, f_bias, conv_w, conv_b, w_out, ln_mix_g, ln_mix_b,
           w_router, b_router, w_gate, w_up, w_down, ln_ffn_g, ln_ffn_b):
    bsz, seq, d = x.shape
    depth = w_ada.shape[0]
    alpha = (2.0 * depth) ** 0.25

    tm = _pick(seq, 512)
    tq = _pick(seq, 256)
    kb = _pick(seq, 256)
    cr = _pick(seq, 256)
    cm = _pick(seq, 256)
    tmoe = _pick(seq, 1024)

    cos_t, sin_t = _rope_tables(positions)
    c_pad = jnp.zeros((8, d), F32).at[:bsz].set(c)
    mod = _ada_mod(c_pad, w_ada, b_ada, _pick(6 * d, 1536))

    tril_kb = (jnp.arange(kb)[:, None] >= jnp.arange(kb)[None, :]).astype(BF16)
    tril = (jnp.arange(cm)[:, None] >= jnp.arange(cm)[None, :]).astype(F32)
    w_router_p = jnp.zeros((d, LANES), F32).at[:, :N_EXPERTS].set(w_router)
    b_router_p = jnp.zeros((1, LANES), F32).at[0, :N_EXPERTS].set(b_router)

    for l in range(depth):
        parts = [mod[l, :bsz, j * d:(j + 1) * d].reshape(bsz, 1, d) for j in range(6)]
        sh_m, sc_m, g_m, sh_f, sc_f, g_f = parts
        q, k, iq, ik2, r_proj, m_proj, small, small_t, v_t = _in_proj(
            x, sc_m, sh_m, _reorder_w_in(w_in[l]), cos_t, sin_t, tm, kb)
        o_a = _dsa(q, k, iq, ik2, small_t, v_t, tril_kb, tq, kb)
        o_b = _retention(r_proj, cr)
        gate_bias = (jnp.zeros((1, S_WIDTH), F32).at[0, S_MI:S_MI + N_HEADS_MLSTM].set(i_bias[l])
                     .at[0, S_MF:S_MF + N_HEADS_MLSTM].set(f_bias[l]))
        o_c = _mlstm(m_proj, small, gate_bias, conv_w[l], conv_b[l].reshape(1, -1), tril, cm)
        x = _out_proj(o_a, o_b, o_c, _pad_w_out(w_out[l]), x, g_m,
                      ln_mix_g[l].reshape(1, d), ln_mix_b[l].reshape(1, d), tm, alpha)
        w_gu = jnp.concatenate([w_gate[l], w_up[l]], axis=-1).astype(BF16)
        x = _moe(x, sc_f, sh_f, g_f, w_router_p, b_router_p, w_gu, w_down[l].astype(BF16),
                 ln_ffn_g[l].reshape(1, d), ln_ffn_b[l].reshape(1, d), tmoe, alpha)
    return x
```

```python
import functools

import numpy as np
import jax
import jax.numpy as jnp
from jax import lax
from jax.experimental import pallas as pl
from jax.experimental.pallas import tpu as pltpu

F32 = jnp.float32
BF16 = jnp.bfloat16

HEAD_DIM = 64
CHUNK = 64
N_HEADS_ATTN = 8
N_IDX_HEADS = 4
IDX_DIM = 64
TOPK_MAX = 256
N_HEADS_RET = 4
N_HEADS_MLSTM = 4
CONV_WIDTH = 4
ROPE_THETA = 10000.0
N_EXPERTS = 16
N_GROUPS = 4
EXPERTS_PER_GROUP = N_EXPERTS // N_GROUPS
D_FF_EXPERT = 256
LN_EPS = 1e-5

MIX_ATTN = N_HEADS_ATTN * HEAD_DIM
MIX_RET = N_HEADS_RET * HEAD_DIM
MIX_MLSTM = N_HEADS_MLSTM * HEAD_DIM

LANES = 128
VMEM_LIMIT = 56 * 1024 * 1024

IQ_WIDTH = N_IDX_HEADS * IDX_DIM
R_WIDTH = 4 * MIX_RET
M_WIDTH = 4 * MIX_MLSTM
S_WIDTH = LANES
S_IW = 0
S_MI = S_IW + N_IDX_HEADS
S_MF = S_MI + N_HEADS_MLSTM
S_ROWS = 16
V_PAD = N_HEADS_ATTN * LANES
V_ROWS = HEAD_DIM + 16
OFF_Q = 0
OFF_K = OFF_Q + MIX_ATTN
OFF_IQ = OFF_K + MIX_ATTN
OFF_R = OFF_IQ + IQ_WIDTH
OFF_M = OFF_R + R_WIDTH
OFF_IK = OFF_M + M_WIDTH
OFF_S = OFF_IK + LANES
OFF_V = OFF_S + S_WIDTH
W_TOTAL = OFF_V + V_PAD

INT_MIN = -2 ** 31
NEG_BIG = -1e30
LOG2_E = 1.4426950408889634


def _dot(a, b):
    return jnp.dot(a, b, preferred_element_type=F32)


def _dot_nt(a, b):
    return lax.dot_general(a, b, (((1,), (1,)), ((), ())), preferred_element_type=F32)


def _dot_tn(a, b):
    return lax.dot_general(a, b, (((0,), (0,)), ((), ())), preferred_element_type=F32)


def _params(sem):
    return pltpu.CompilerParams(dimension_semantics=sem, vmem_limit_bytes=VMEM_LIMIT)


def _ada_kernel(c_ref, w_ref, b_ref, o_ref):
    c = c_ref[...]
    c_act = c * jax.nn.sigmoid(c)
    o_ref[0] = _dot(c_act, w_ref[0]) + b_ref[0]


def _ada_mod(c_pad, w_ada, b_ada, tn):
    depth, d, n = w_ada.shape
    rows = c_pad.shape[0]
    return pl.pallas_call(
        _ada_kernel,
        grid=(depth, n // tn),
        in_specs=[pl.BlockSpec((rows, d), lambda l, j: (0, 0)),
                  pl.BlockSpec((1, d, tn), lambda l, j: (l, 0, j)),
                  pl.BlockSpec((1, 1, tn), lambda l, j: (l, 0, j))],
        out_specs=pl.BlockSpec((1, rows, tn), lambda l, j: (l, 0, j)),
        out_shape=jax.ShapeDtypeStruct((depth, rows, n), F32),
        compiler_params=_params(("parallel", "parallel")),
        name="ada_mod",
    )(c_pad, w_ada, b_ada.reshape(depth, 1, n))


def _rope(y, cos, sin):
    w = y.shape[1]
    reps = w // LANES
    cosw = jnp.concatenate([cos] * reps, axis=1) if reps > 1 else cos
    sinw = jnp.concatenate([sin] * reps, axis=1) if reps > 1 else sin
    lane = lax.broadcasted_iota(jnp.int32, y.shape, 1)
    first = (lane % HEAD_DIM) < (HEAD_DIM // 2)
    partner = jnp.where(first, pltpu.roll(y, w - HEAD_DIM // 2, 1), pltpu.roll(y, HEAD_DIM // 2, 1))
    return y * cosw + partner * sinw


def _inproj_kernel(x_ref, sc_ref, sh_ref, w_ref, cos_ref, sin_ref,
                   q_ref, k_ref, iq_ref, ik_ref, r_ref, m_ref, s_ref, st_ref, vt_ref, *, kb):
    h = (x_ref[0] * (1.0 + sc_ref[0]) + sh_ref[0]).astype(BF16)
    cos = cos_ref[0]
    sin = sin_ref[0]

    def proj(start, width):
        return _dot(h, w_ref[:, start:start + width])

    q_ref[0] = (_rope(proj(OFF_Q, MIX_ATTN), cos, sin) * (HEAD_DIM ** -0.5 * LOG2_E)).astype(BF16)
    k_ref[0] = _rope(proj(OFF_K, MIX_ATTN), cos, sin).astype(BF16)
    iq_ref[0] = _rope(proj(OFF_IQ, IQ_WIDTH), cos, sin).astype(BF16)
    ik_ref[0] = _rope(proj(OFF_IK, LANES), cos, sin).astype(BF16)
    r_ref[0, :, 0:2 * MIX_RET] = _rope(proj(OFF_R, 2 * MIX_RET), cos, sin).astype(BF16)
    r_ref[0, :, 2 * MIX_RET:R_WIDTH] = proj(OFF_R + 2 * MIX_RET, 2 * MIX_RET).astype(BF16)
    m_ref[0, :, 0:2 * MIX_MLSTM] = proj(OFF_M, 2 * MIX_MLSTM).astype(BF16)
    m_ref[0, :, 2 * MIX_MLSTM:M_WIDTH] = proj(OFF_M + 2 * MIX_MLSTM, 2 * MIX_MLSTM).astype(BF16)
    y = proj(OFF_S, S_WIDTH)
    s_ref[0] = y
    st_ref[0] = y.T[0:S_ROWS, :]
    tm = h.shape[0]
    for part in range(V_PAD // MIX_ATTN):
        yv = proj(OFF_V + part * MIX_ATTN, MIX_ATTN)
        lane = lax.broadcasted_iota(jnp.int32, yv.shape, 1)
        yvt = jnp.where(lane % LANES == HEAD_DIM, 1.0, yv).T.astype(BF16)
        heads_per_part = MIX_ATTN // LANES
        for j in range(tm // kb):
            for hh in range(heads_per_part):
                row0 = (part * heads_per_part + hh) * V_ROWS
                vt_ref[0, j, row0:row0 + V_ROWS, :] = yvt[hh * LANES:hh * LANES + V_ROWS, j * kb:(j + 1) * kb]


def _in_proj(x, sc, sh, w, cos_t, sin_t, tm, kb):
    bsz, seq, d = x.shape
    row = lambda b, i: (b, i, 0)
    per_b = lambda b, i: (b, 0, 0)
    widths = (MIX_ATTN, MIX_ATTN, IQ_WIDTH, LANES, R_WIDTH, M_WIDTH)
    return pl.pallas_call(
        functools.partial(_inproj_kernel, kb=kb),
        grid=(bsz, seq // tm),
        in_specs=[pl.BlockSpec((1, tm, d), row),
                  pl.BlockSpec((1, 1, d), per_b),
                  pl.BlockSpec((1, 1, d), per_b),
                  pl.BlockSpec((d, W_TOTAL), lambda b, i: (0, 0)),
                  pl.BlockSpec((1, tm, LANES), row),
                  pl.BlockSpec((1, tm, LANES), row)],
        out_specs=[pl.BlockSpec((1, tm, wd), row) for wd in widths]
                  + [pl.BlockSpec((1, tm, S_WIDTH), row),
                     pl.BlockSpec((1, S_ROWS, tm), lambda b, i: (b, 0, i)),
                     pl.BlockSpec((1, tm // kb, N_HEADS_ATTN * V_ROWS, kb), lambda b, i: (b, i, 0, 0))],
        out_shape=[jax.ShapeDtypeStruct((bsz, seq, wd), BF16) for wd in widths]
                  + [jax.ShapeDtypeStruct((bsz, seq, S_WIDTH), F32),
                     jax.ShapeDtypeStruct((bsz, S_ROWS, seq), F32),
                     jax.ShapeDtypeStruct((bsz, seq // kb, N_HEADS_ATTN * V_ROWS, kb), BF16)],
        compiler_params=_params(("parallel", "parallel")),
        name="in_proj",
    )(x, sc, sh, w, cos_t, sin_t)


def _dsa_kernel(q_ref, k_ref, iq_ref, ik_ref, st_ref, vt_ref, tril_ref, o_ref,
                key_scr, byte_scr, cand_scr, bias_scr, s_scr, p_scr, m_scr, alpha_scr, acc_scr,
                *, tq, kb, topk):
    q0 = pl.program_id(1) * tq
    n_blocks = (q0 + tq + kb - 1) // kb
    qpos = q0 + lax.broadcasted_iota(jnp.int32, (1, tq), 1)
    q_limit = (qpos // CHUNK + 1) * CHUNK
    krow = lax.broadcasted_iota(jnp.int32, (kb, 1), 0)

    def head_of_pair(x, h):
        pair = x[:, (h // 2) * LANES:(h // 2 + 1) * LANES]
        lane = lax.broadcasted_iota(jnp.int32, pair.shape, 1)
        keep = (lane < HEAD_DIM) if h % 2 == 0 else (lane >= HEAD_DIM)
        return jnp.where(keep, pair, jnp.zeros_like(pair))

    iw = st_ref[0][S_IW:S_IW + N_IDX_HEADS, :] * (N_IDX_HEADS ** -0.5 * IDX_DIM ** -0.5)
    iq = iq_ref[0]
    iq_heads = [head_of_pair(iq, h) for h in range(N_IDX_HEADS)]

    def score_body(c, carry):
        k0 = pl.multiple_of(c * kb, kb)
        ik2 = ik_ref[0, pl.ds(k0, kb), :]
        score = jnp.zeros((kb, tq), F32)
        for h in range(N_IDX_HEADS):
            score = score + jnp.maximum(_dot_nt(ik2, iq_heads[h]), 0.0) * iw[h:h + 1, :]
        bits = pltpu.bitcast(score, jnp.int32)
        key = jnp.where(bits >= 0, bits, bits ^ jnp.int32(0x7FFFFFFF))
        key = jnp.where(k0 + krow < q_limit, key, jnp.int32(INT_MIN))
        key_scr[c] = key
        byte_scr[0, c] = ((key >> 24) + 128).astype(F32).astype(BF16)
        for lvl in range(1, 4):
            byte_scr[lvl, c] = ((key >> (24 - 8 * lvl)) & 255).astype(F32).astype(BF16)
        return carry

    lax.fori_loop(0, n_blocks, score_body, 0)

    pack = 16
    one = jnp.ones((kb, tq), BF16)
    zero = jnp.zeros((kb, tq), BF16)

    def count_ge(lvl, cand):
        cand_b = cand.astype(BF16)

        def body(c, acc):
            plane = byte_scr[0, c] if lvl == 0 else cand_scr[c]
            hit = jnp.where(plane >= cand_b, one, zero)
            parts = [hit[i * pack:(i + 1) * pack, :] for i in range(kb // pack)]
            while len(parts) > 1:
                parts = [parts[i] + parts[i + 1] for i in range(0, len(parts), 2)]
            return acc + parts[0]

        acc = lax.fori_loop(0, n_blocks, body, jnp.zeros((pack, tq), BF16))
        return jnp.sum(acc.astype(F32), axis=0, keepdims=True)

    above = jnp.zeros((1, tq), F32)
    t = jnp.zeros((1, tq), jnp.int32)
    for lvl in range(4):
        def bit_body(i, v, lvl=lvl, above=above):
            cand = v + lax.shift_left(jnp.int32(1), 7 - i).astype(F32)
            return jnp.where(above + count_ge(lvl, cand) >= topk, cand, v)

        v = lax.fori_loop(0, 8, bit_body, jnp.zeros((1, tq), F32))
        above = above + count_ge(lvl, v + 1.0)
        t = t | lax.shift_left(v.astype(jnp.int32), 24 - 8 * lvl)
        if lvl < 3:
            v_b = v.astype(BF16)

            def narrow(c, carry, lvl=lvl, v_b=v_b):
                plane = byte_scr[0, c] if lvl == 0 else cand_scr[c]
                cand_scr[c] = jnp.where(plane == v_b, byte_scr[lvl + 1, c], -one)
                return carry

            lax.fori_loop(0, n_blocks, narrow, 0)
    thr = jnp.maximum(t ^ jnp.int32(INT_MIN), jnp.int32(INT_MIN + 1))
    need = topk - above

    q = q_ref[0]
    q_heads = [head_of_pair(q, h) for h in range(N_HEADS_ATTN)]
    m_scr[...] = jnp.full(m_scr.shape, NEG_BIG, F32)
    alpha_scr[...] = jnp.ones(alpha_scr.shape, F32)
    acc_scr[...] = jnp.zeros(acc_scr.shape, F32)
    p_scr[...] = jnp.zeros(p_scr.shape, BF16)

    def stage_mask(c, ties_before):
        key = key_scr[c]
        tie = key == thr
        rank = _dot(tril_ref[...], jnp.where(tie, 1.0, 0.0).astype(BF16)) + ties_before
        sel = (key > thr) | (tie & (rank <= need))
        bias_scr[...] = jnp.where(sel, 0.0, NEG_BIG)
        return rank[kb - 1:kb, :]

    def stage_logits(c, h):
        k0 = pl.multiple_of(c * kb, kb)
        kp = k_ref[0, pl.ds(k0, kb), (h // 2) * LANES:(h // 2 + 1) * LANES]
        s_scr[h] = _dot_nt(kp, q_heads[h])

    def stage_softmax(h):
        for half in range(tq // LANES):
            ln = slice(half * LANES, (half + 1) * LANES)
            s = s_scr[h, :, ln] + bias_scr[:, ln]
            m_old = m_scr[h, :, ln]
            m_new = jnp.maximum(m_old, jnp.max(s, axis=0, keepdims=True))
            p_scr[h, :, ln] = jnp.exp2(s - m_new).astype(BF16)
            alpha_scr[h, :, ln] = jnp.exp2(m_old - m_new)
            m_scr[h, :, ln] = m_new

    def stage_values(c, h):
        vt = vt_ref[0, c, h * V_ROWS:(h + 1) * V_ROWS, :]
        acc_scr[h, 0:V_ROWS, :] = alpha_scr[h] * acc_scr[h, 0:V_ROWS, :] + _dot(vt, p_scr[h])

    ties0 = stage_mask(0, jnp.zeros((1, tq), F32))
    for h in range(N_HEADS_ATTN):
        stage_logits(0, h)

    def attn_body(j, ties_before):
        c_old = jnp.maximum(j - 2, 0)
        for h in range(N_HEADS_ATTN):
            stage_values(c_old, h)
            stage_softmax(h)
            stage_logits(j, h)
        return stage_mask(j, ties_before)

    lax.fori_loop(1, n_blocks, attn_body, ties0)
    for h in range(N_HEADS_ATTN):
        stage_values(jnp.maximum(n_blocks - 2, 0), h)
        stage_softmax(h)
    for h in range(N_HEADS_ATTN):
        stage_values(n_blocks - 1, h)
    for h in range(N_HEADS_ATTN):
        acc = acc_scr[h]
        out = acc * (1.0 / acc[HEAD_DIM:HEAD_DIM + 1, :])
        o_ref[0, :, h * LANES:(h + 1) * LANES] = out.T.astype(o_ref.dtype)


def _dsa(q, k, iq, ik2, small_t, v_t, tril, tq, kb):
    bsz, seq, _ = q.shape
    topk = min(TOPK_MAX, seq // 4)
    kern = functools.partial(_dsa_kernel, tq=tq, kb=kb, topk=topk)
    return pl.pallas_call(
        kern,
        grid=(bsz, seq // tq),
        in_specs=[pl.BlockSpec((1, tq, MIX_ATTN), lambda b, i: (b, i, 0)),
                  pl.BlockSpec((1, seq, MIX_ATTN), lambda b, i: (b, 0, 0)),
                  pl.BlockSpec((1, tq, IQ_WIDTH), lambda b, i: (b, i, 0)),
                  pl.BlockSpec((1, seq, LANES), lambda b, i: (b, 0, 0)),
                  pl.BlockSpec((1, S_ROWS, tq), lambda b, i: (b, 0, i)),
                  pl.BlockSpec((1, seq // kb, N_HEADS_ATTN * V_ROWS, kb), lambda b, i: (b, 0, 0, 0)),
                  pl.BlockSpec((kb, kb), lambda b, i: (0, 0))],
        out_specs=pl.BlockSpec((1, tq, V_PAD), lambda b, i: (b, i, 0)),
        out_shape=jax.ShapeDtypeStruct((bsz, seq, V_PAD), BF16),
        scratch_shapes=[pltpu.VMEM((seq // kb, kb, tq), jnp.int32),
                        pltpu.VMEM((4, seq // kb, kb, tq), BF16),
                        pltpu.VMEM((seq // kb, kb, tq), BF16),
                        pltpu.VMEM((kb, tq), F32),
                        pltpu.VMEM((N_HEADS_ATTN, kb, tq), F32),
                        pltpu.VMEM((N_HEADS_ATTN, kb, tq), BF16),
                        pltpu.VMEM((N_HEADS_ATTN, 1, tq), F32),
                        pltpu.VMEM((N_HEADS_ATTN, 1, tq), F32),
                        pltpu.VMEM((N_HEADS_ATTN, LANES, tq), F32)],
        compiler_params=_params(("parallel", "arbitrary")),
        name="dsa",
    )(q, k, iq, ik2, small_t, v_t, tril)


def _head_norm(y):
    mu = jnp.mean(y, axis=-1, keepdims=True)
    var = jnp.mean(jnp.square(y - mu), axis=-1, keepdims=True)
    return (y - mu) * lax.rsqrt(var + LN_EPS)


def _ret_kernel(r_ref, o_ref, state_scr, *, cr):
    @pl.when(pl.program_id(1) == 0)
    def _():
        state_scr[...] = jnp.zeros_like(state_scr)

    r = r_ref[0]
    ri = lax.broadcasted_iota(jnp.int32, (cr, cr), 0)
    ci = lax.broadcasted_iota(jnp.int32, (cr, cr), 1)
    diff = (ri - ci).astype(F32)
    pos = lax.broadcasted_iota(jnp.int32, (cr, 1), 0).astype(F32)
    for h in range(N_HEADS_RET):
        log_gamma = jnp.log1p(jnp.full((1, 1), -(2.0 ** (-5.0 - h)), F32))
        sl = lambda part: slice(part * MIX_RET + h * HEAD_DIM, part * MIX_RET + (h + 1) * HEAD_DIM)
        q = r[:, sl(0)]
        k = r[:, sl(1)]
        v = r[:, sl(2)]
        g = r[:, sl(3)].astype(F32)
        decay_in = jnp.where(diff >= 0, jnp.exp(diff * log_gamma), 0.0)
        scores = _dot_nt(q, k) * (HEAD_DIM ** -0.5) * decay_in
        inner = _dot(scores.astype(BF16), v)
        state = state_scr[h]
        cross = jnp.exp((pos + 1.0) * log_gamma) * _dot(q, state.astype(BF16))
        k_dec = (k.astype(F32) * (HEAD_DIM ** -0.5) * jnp.exp((cr - 1.0 - pos) * log_gamma)).astype(BF16)
        state_scr[h] = state * jnp.exp(cr * log_gamma) + _dot_tn(k_dec, v)
        y = _head_norm(inner + cross)
        o_ref[0, :, h * HEAD_DIM:(h + 1) * HEAD_DIM] = (y * (g * jax.nn.sigmoid(g))).astype(o_ref.dtype)


def _retention(r_proj, cr):
    bsz, seq, _ = r_proj.shape
    return pl.pallas_call(
        functools.partial(_ret_kernel, cr=cr),
        grid=(bsz, seq // cr),
        in_specs=[pl.BlockSpec((1, cr, R_WIDTH), lambda b, i: (b, i, 0))],
        out_specs=pl.BlockSpec((1, cr, MIX_RET), lambda b, i: (b, i, 0)),
        out_shape=jax.ShapeDtypeStruct((bsz, seq, MIX_RET), BF16),
        scratch_shapes=[pltpu.VMEM((N_HEADS_RET, HEAD_DIM, HEAD_DIM), F32)],
        compiler_params=_params(("parallel", "arbitrary")),
        name="retention",
    )(r_proj)


def _mlstm_kernel(m_ref, sm_ref, bias_ref, cw_ref, cb_ref, tril_ref, o_ref,
                  xbuf, c_scr, n_scr, m_scr, *, cm):
    halo = 8

    @pl.when(pl.program_id(1) == 0)
    def _():
        xbuf[0:halo, :] = jnp.zeros((halo, 2 * MIX_MLSTM), F32)
        c_scr[...] = jnp.zeros_like(c_scr)
        n_scr[...] = jnp.zeros_like(n_scr)
        m_scr[...] = jnp.zeros_like(m_scr)

    mm = m_ref[0]
    xbuf[halo:halo + cm, :] = mm[:, 0:2 * MIX_MLSTM].astype(F32)
    conv = cb_ref[...]
    for j in range(CONV_WIDTH):
        off = halo - (CONV_WIDTH - 1) + j
        conv = conv + xbuf[off:off + cm, :] * cw_ref[j:j + 1, :]
    xbuf[0:halo, :] = xbuf[cm:cm + halo, :]
    qk = conv * jax.nn.sigmoid(conv)

    gates = sm_ref[0] + bias_ref[...]
    log_f = jax.nn.log_sigmoid(gates)
    b_all = jnp.dot(tril_ref[...], log_f, preferred_element_type=F32,
                    precision=lax.Precision.HIGHEST)
    lane = lax.broadcasted_iota(jnp.int32, gates.shape, 1)
    rows_t = jnp.where(lane >= S_MF, b_all, gates).T
    ri = lax.broadcasted_iota(jnp.int32, (cm, cm), 0)
    ci = lax.broadcasted_iota(jnp.int32, (cm, cm), 1)
    causal = ri >= ci
    scale = HEAD_DIM ** -0.5
    for h in range(N_HEADS_MLSTM):
        sl = slice(h * HEAD_DIM, (h + 1) * HEAD_DIM)
        q = qk[:, sl].astype(BF16)
        k = qk[:, MIX_MLSTM + h * HEAD_DIM:MIX_MLSTM + (h + 1) * HEAD_DIM]
        v = mm[:, 2 * MIX_MLSTM + h * HEAD_DIM:2 * MIX_MLSTM + (h + 1) * HEAD_DIM]
        og = mm[:, 3 * MIX_MLSTM + h * HEAD_DIM:3 * MIX_MLSTM + (h + 1) * HEAD_DIM].astype(F32)
        b_c = b_all[:, S_MF + h:S_MF + h + 1]
        i_c = gates[:, S_MI + h:S_MI + h + 1]
        b_r = rows_t[S_MF + h:S_MF + h + 1, :]
        i_r = rows_t[S_MI + h:S_MI + h + 1, :]
        m_prev = m_scr[h]
        c_mem = c_scr[h]
        n_mem = n_scr[h]

        log_w = jnp.where(causal, b_c - b_r + i_r, -jnp.inf)
        log_inter = b_c + m_prev
        m_q = jnp.maximum(log_inter, jnp.max(log_w, axis=1, keepdims=True))
        w = jnp.exp(log_w - m_q)
        inter = jnp.exp(log_inter - m_q)
        s = _dot_nt(q, k.astype(BF16)) * scale * w
        qf = q.astype(F32)
        num = _dot(s.astype(BF16), v) + inter * _dot(q, c_mem.astype(BF16))
        den = jnp.sum(s, axis=1, keepdims=True) + inter * jnp.sum(qf * n_mem, axis=1, keepdims=True)
        h_tilde = num / jnp.maximum(jnp.abs(den), jnp.exp(-m_q))

        b_last = b_c[cm - 1:cm, :]
        log_k = b_last - b_c + i_c
        m_new = jnp.maximum(b_last + m_prev, jnp.max(log_k, axis=0, keepdims=True))
        kw = k * (scale * jnp.exp(log_k - m_new))
        decay = jnp.exp(b_last + m_prev - m_new)
        c_scr[h] = decay * c_mem + _dot_tn(kw.astype(BF16), v)
        n_scr[h] = decay * n_mem + jnp.sum(kw, axis=0, keepdims=True)
        m_scr[h] = m_new

        o_ref[0, :, sl] = _head_norm(jax.nn.sigmoid(og) * h_tilde).astype(o_ref.dtype)


def _mlstm(m_proj, small, gate_bias, conv_w, conv_b, tril, cm):
    bsz, seq, _ = m_proj.shape
    const = lambda b, i: (0, 0)
    return pl.pallas_call(
        functools.partial(_mlstm_kernel, cm=cm),
        grid=(bsz, seq // cm),
        in_specs=[pl.BlockSpec((1, cm, M_WIDTH), lambda b, i: (b, i, 0)),
                  pl.BlockSpec((1, cm, S_WIDTH), lambda b, i: (b, i, 0)),
                  pl.BlockSpec((1, S_WIDTH), const),
                  pl.BlockSpec((CONV_WIDTH, 2 * MIX_MLSTM), const),
                  pl.BlockSpec((1, 2 * MIX_MLSTM), const),
                  pl.BlockSpec((cm, cm), const)],
        out_specs=pl.BlockSpec((1, cm, MIX_MLSTM), lambda b, i: (b, i, 0)),
        out_shape=jax.ShapeDtypeStruct((bsz, seq, MIX_MLSTM), BF16),
        scratch_shapes=[pltpu.VMEM((cm + 8, 2 * MIX_MLSTM), F32),
                        pltpu.VMEM((N_HEADS_MLSTM, HEAD_DIM, HEAD_DIM), F32),
                        pltpu.VMEM((N_HEADS_MLSTM, 1, HEAD_DIM), F32),
                        pltpu.VMEM((N_HEADS_MLSTM, 1, 1), F32)],
        compiler_params=_params(("parallel", "arbitrary")),
        name="mlstm",
    )(m_proj, small, gate_bias, conv_w, conv_b, tril)


def _layer_norm(z, g, b):
    mu = jnp.mean(z, axis=-1, keepdims=True)
    var = jnp.mean(jnp.square(z - mu), axis=-1, keepdims=True)
    return (z - mu) * lax.rsqrt(var + LN_EPS) * g + b


def _outproj_kernel(oa_ref, ob_ref, oc_ref, w_ref, x_ref, gm_ref, g_ref, b_ref, o_ref, *, alpha):
    mix = _dot(oa_ref[0], w_ref[0:V_PAD, :])
    mix = mix + _dot(ob_ref[0], w_ref[V_PAD:V_PAD + MIX_RET, :])
    mix = mix + _dot(oc_ref[0], w_ref[V_PAD + MIX_RET:, :])
    z = alpha * x_ref[0] + (1.0 + gm_ref[0]) * mix
    o_ref[0] = _layer_norm(z, g_ref[...], b_ref[...])


def _out_proj(o_a, o_b, o_c, w_out, x, g_m, ln_g, ln_b, tm, alpha):
    bsz, seq, d = x.shape
    row = lambda b, i: (b, i, 0)
    const = lambda b, i: (0, 0)
    return pl.pallas_call(
        functools.partial(_outproj_kernel, alpha=alpha),
        grid=(bsz, seq // tm),
        in_specs=[pl.BlockSpec((1, tm, V_PAD), row),
                  pl.BlockSpec((1, tm, MIX_RET), row),
                  pl.BlockSpec((1, tm, MIX_MLSTM), row),
                  pl.BlockSpec(w_out.shape, const),
                  pl.BlockSpec((1, tm, d), row),
                  pl.BlockSpec((1, 1, d), lambda b, i: (b, 0, 0)),
                  pl.BlockSpec((1, d), const),
                  pl.BlockSpec((1, d), const)],
        out_specs=pl.BlockSpec((1, tm, d), row),
        out_shape=jax.ShapeDtypeStruct((bsz, seq, d), F32),
        compiler_params=_params(("parallel", "parallel")),
        name="out_proj",
    )(o_a, o_b, o_c, w_out, x, g_m, ln_g, ln_b)


def _route(scores, biased):
    col = lambda a, e: a[:, e:e + 1]
    epg = EXPERTS_PER_GROUP
    group_scores = []
    for g in range(N_GROUPS):
        vals = [col(biased, g * epg + j) for j in range(epg)]
        best = None
        for a in range(epg):
            for b in range(a + 1, epg):
                pair = vals[a] + vals[b]
                best = pair if best is None else jnp.maximum(best, pair)
        group_scores.append(best)
    best_g = jnp.zeros_like(group_scores[0], dtype=jnp.int32)
    best_v = group_scores[0]
    for g in range(1, N_GROUPS):
        better = group_scores[g] > best_v
        best_g = jnp.where(better, g, best_g)
        best_v = jnp.where(better, group_scores[g], best_v)
    cand_b = [sum(jnp.where(best_g == g, col(biased, g * epg + j), 0.0) for g in range(N_GROUPS))
              for j in range(epg)]
    cand_s = [sum(jnp.where(best_g == g, col(scores, g * epg + j), 0.0) for g in range(N_GROUPS))
              for j in range(epg)]

    def argmax_first(vals, skip=None):
        idx = None
        val = None
        for j, vj in enumerate(vals):
            if skip is not None:
                vj = jnp.where(skip == j, -jnp.inf, vj)
            if idx is None:
                idx, val = jnp.zeros_like(best_g), vj
            else:
                better = vj > val
                idx = jnp.where(better, j, idx)
                val = jnp.where(better, vj, val)
        return idx

    first = argmax_first(cand_b)
    second = argmax_first(cand_b, skip=first)
    w1 = sum(jnp.where(first == j, cand_s[j], 0.0) for j in range(epg))
    w2 = sum(jnp.where(second == j, cand_s[j], 0.0) for j in range(epg))
    total = w1 + w2
    e1 = best_g * epg + first
    e2 = best_g * epg + second
    lane = lax.broadcasted_iota(jnp.int32, scores.shape, 1)
    return jnp.where(lane == e1, w1 / total, 0.0) + jnp.where(lane == e2, w2 / total, 0.0)


def _moe_kernel(x_ref, sc_ref, sh_ref, gf_ref, wr_ref, br_ref, wg_ref, wu_ref, wd_ref, g_ref, b_ref, o_ref,
                hid_scr, *, alpha):
    x = x_ref[0]
    h = x * (1.0 + sc_ref[0]) + sh_ref[0]
    hb = h.astype(BF16)
    scores = jax.nn.sigmoid(_dot(h, wr_ref[...]))
    gate = _route(scores, scores + br_ref[...])
    n_exp, _, d_ff = wg_ref.shape
    for e in range(n_exp):
        gate_pre = _dot(hb, wg_ref[e])
        up = _dot(hb, wu_ref[e])
        hid = gate_pre * jax.nn.sigmoid(gate_pre) * up * gate[:, e:e + 1]
        hid_scr[:, e * d_ff:(e + 1) * d_ff] = hid.astype(BF16)
    y = _dot(hid_scr[...], wd_ref[...])
    z = alpha * x + (1.0 + gf_ref[0]) * y
    o_ref[0] = _layer_norm(z, g_ref[...], b_ref[...])


def _moe(x, sc, sh, g_f, w_router, b_router, w_gate, w_up, w_down, ln_g, ln_b, tm, alpha):
    bsz, seq, d = x.shape
    n_exp, _, d_ff = w_gate.shape
    row = lambda b, i: (b, i, 0)
    per_b = lambda b, i: (b, 0, 0)
    const = lambda b, i: (0, 0)
    const3 = lambda b, i: (0, 0, 0)
    resident = pl.Buffered(1)
    return pl.pallas_call(
        functools.partial(_moe_kernel, alpha=alpha),
        grid=(bsz, seq // tm),
        in_specs=[pl.BlockSpec((1, tm, d), row),
                  pl.BlockSpec((1, 1, d), per_b),
                  pl.BlockSpec((1, 1, d), per_b),
                  pl.BlockSpec((1, 1, d), per_b),
                  pl.BlockSpec((d, LANES), const),
                  pl.BlockSpec((1, LANES), const),
                  pl.BlockSpec((n_exp, d, d_ff), const3, pipeline_mode=resident),
                  pl.BlockSpec((n_exp, d, d_ff), const3, pipeline_mode=resident),
                  pl.BlockSpec((n_exp * d_ff, d), const, pipeline_mode=resident),
                  pl.BlockSpec((1, d), const),
                  pl.BlockSpec((1, d), const)],
        out_specs=pl.BlockSpec((1, tm, d), row),
        out_shape=jax.ShapeDtypeStruct((bsz, seq, d), F32),
        scratch_shapes=[pltpu.VMEM((tm, n_exp * d_ff), BF16)],
        compiler_params=_params(("parallel", "parallel")),
        name="moe",
    )(x, sc, sh, g_f, w_router, b_router, w_gate, w_up, w_down.reshape(n_exp * d_ff, d), ln_g, ln_b)


def _pick(n, pref):
    t = min(pref, n)
    while n % t:
        t //= 2
    return t


def _rope_tables(positions):
    half = HEAD_DIM // 2
    inv_freq = ROPE_THETA ** (-jnp.arange(half, dtype=F32) / half)
    ang = positions.astype(F32)[..., None] * inv_freq
    cos, sin = jnp.cos(ang), jnp.sin(ang)
    reps = LANES // HEAD_DIM
    return (jnp.concatenate([cos, cos] * reps, axis=-1),
            jnp.concatenate([-sin, sin] * reps, axis=-1))


def _reorder_w_in(w_in):
    d = w_in.shape[0]
    o_v = 2 * MIX_ATTN
    o_iq = o_v + MIX_ATTN
    o_ik = o_iq + IQ_WIDTH
    o_iw = o_ik + IDX_DIM
    o_r = o_iw + N_IDX_HEADS
    o_m = o_r + R_WIDTH
    o_g = o_m + M_WIDTH
    ik = w_in[:, o_ik:o_iw]
    small = jnp.concatenate([w_in[:, o_iw:o_r], w_in[:, o_g:],
                             jnp.zeros((d, S_WIDTH - N_IDX_HEADS - 2 * N_HEADS_MLSTM), w_in.dtype)], axis=1)
    v = w_in[:, o_v:o_iq].reshape(d, N_HEADS_ATTN, HEAD_DIM)
    v_pad = jnp.concatenate([v, jnp.zeros_like(v)], axis=-1).reshape(d, V_PAD)
    return jnp.concatenate([w_in[:, :o_v], w_in[:, o_iq:o_ik], w_in[:, o_r:o_g],
                            ik, ik, small, v_pad], axis=1).astype(BF16)


def _pad_w_out(w_out):
    d = w_out.shape[1]
    wa = w_out[:MIX_ATTN].reshape(N_HEADS_ATTN, HEAD_DIM, d)
    wa = jnp.concatenate([wa, jnp.zeros_like(wa)], axis=1).reshape(V_PAD, d)
    return jnp.concatenate([wa, w_out[MIX_ATTN:]], axis=0).astype(BF16)


def kernel(x, c, positions, w_ada, b_ada, w_in, i_bias, f_bias, conv_w, conv_b, w_out, ln_mix_g, ln_mix_b,
           w_router, b_router, w_gate, w_up, w_down, ln_ffn_g, ln_ffn_b):
    bsz, seq, d = x.shape
    depth = w_ada.shape[0]
    alpha = (2.0 * depth) ** 0.25

    tm = _pick(seq, 512)
    tq = _pick(seq, 256)
    kb = _pick(seq, 256)
    assert seq // 16 <= 256, "packed bf16 partial counts in the DSA threshold search must stay exact"
    cr = _pick(seq, 256)
    cm = _pick(seq, 256)
    tmoe = _pick(seq, 512)

    cos_t, sin_t = _rope_tables(positions)
    c_pad = jnp.zeros((8, d), F32).at[:bsz].set(c)
    mod = _ada_mod(c_pad, w_ada, b_ada, _pick(6 * d, 1536))

    tril_kb = (jnp.arange(kb)[:, None] >= jnp.arange(kb)[None, :]).astype(BF16)
    tril = (jnp.arange(cm)[:, None] >= jnp.arange(cm)[None, :]).astype(F32)
    w_router_p = jnp.zeros((d, LANES), F32).at[:, :N_EXPERTS].set(w_router)
    b_router_p = jnp.zeros((1, LANES), F32).at[0, :N_EXPERTS].set(b_router)

    for l in range(depth):
        parts = [mod[l, :bsz, j * d:(j + 1) * d].reshape(bsz, 1, d) for j in range(6)]
        sh_m, sc_m, g_m, sh_f, sc_f, g_f = parts
        q, k, iq, ik2, r_proj, m_proj, small, small_t, v_t = _in_proj(
            x, sc_m, sh_m, _reorder_w_in(w_in[l]), cos_t, sin_t, tm, kb)
        o_a = _dsa(q, k, iq, ik2, small_t, v_t, tril_kb, tq, kb)
        o_b = _retention(r_proj, cr)
        gate_bias = (jnp.zeros((1, S_WIDTH), F32).at[0, S_MI:S_MI + N_HEADS_MLSTM].set(i_bias[l])
                     .at[0, S_MF:S_MF + N_HEADS_MLSTM].set(f_bias[l]))
        o_c = _mlstm(m_proj, small, gate_bias, conv_w[l], conv_b[l].reshape(1, -1), tril, cm)
        x = _out_proj(o_a, o_b, o_c, _pad_w_out(w_out[l]), x, g_m,
                      ln_mix_g[l].reshape(1, d), ln_mix_b[l].reshape(1, d), tm, alpha)
        x = _moe(x, sc_f, sh_f, g_f, w_router_p, b_router_p,
                 w_gate[l].astype(BF16), w_up[l].astype(BF16), w_down[l].astype(BF16),
                 ln_ffn_g[l].reshape(1, d), ln_ffn_b[l].reshape(1, d), tmoe, alpha)
    return x
```

```python
import functools

import numpy as np
import jax
import jax.numpy as jnp
from jax import lax
from jax.experimental import pallas as pl
from jax.experimental.pallas import tpu as pltpu

F32 = jnp.float32
BF16 = jnp.bfloat16

HEAD_DIM = 64
CHUNK = 64
N_HEADS_ATTN = 8
N_IDX_HEADS = 4
IDX_DIM = 64
TOPK_MAX = 256
N_HEADS_RET = 4
N_HEADS_MLSTM = 4
CONV_WIDTH = 4
ROPE_THETA = 10000.0
N_EXPERTS = 16
N_GROUPS = 4
EXPERTS_PER_GROUP = N_EXPERTS // N_GROUPS
D_FF_EXPERT = 256
LN_EPS = 1e-5

MIX_ATTN = N_HEADS_ATTN * HEAD_DIM
MIX_RET = N_HEADS_RET * HEAD_DIM
MIX_MLSTM = N_HEADS_MLSTM * HEAD_DIM

LANES = 128
VMEM_LIMIT = 56 * 1024 * 1024

IQ_WIDTH = N_IDX_HEADS * IDX_DIM
R_WIDTH = 4 * MIX_RET
M_WIDTH = 4 * MIX_MLSTM
S_WIDTH = LANES
S_IW = 0
S_MI = S_IW + N_IDX_HEADS
S_MF = S_MI + N_HEADS_MLSTM
S_ROWS = 16
V_PAD = N_HEADS_ATTN * LANES
V_ROWS = HEAD_DIM + 16
OFF_Q = 0
OFF_K = OFF_Q + MIX_ATTN
OFF_IQ = OFF_K + MIX_ATTN
OFF_R = OFF_IQ + IQ_WIDTH
OFF_M = OFF_R + R_WIDTH
OFF_IK = OFF_M + M_WIDTH
OFF_S = OFF_IK + LANES
OFF_V = OFF_S + S_WIDTH
W_TOTAL = OFF_V + V_PAD

INT_MIN = -2 ** 31
NEG_BIG = -1e30
LOG2_E = 1.4426950408889634


def _dot(a, b):
    return jnp.dot(a, b, preferred_element_type=F32)


def _dot_nt(a, b):
    return lax.dot_general(a, b, (((1,), (1,)), ((), ())), preferred_element_type=F32)


def _dot_tn(a, b):
    return lax.dot_general(a, b, (((0,), (0,)), ((), ())), preferred_element_type=F32)


def _params(sem):
    return pltpu.CompilerParams(dimension_semantics=sem, vmem_limit_bytes=VMEM_LIMIT)


def _ada_kernel(c_ref, w_ref, b_ref, o_ref):
    c = c_ref[...]
    c_act = c * jax.nn.sigmoid(c)
    o_ref[0] = _dot(c_act, w_ref[0]) + b_ref[0]


def _ada_mod(c_pad, w_ada, b_ada, tn):
    depth, d, n = w_ada.shape
    rows = c_pad.shape[0]
    return pl.pallas_call(
        _ada_kernel,
        grid=(depth, n // tn),
        in_specs=[pl.BlockSpec((rows, d), lambda l, j: (0, 0)),
                  pl.BlockSpec((1, d, tn), lambda l, j: (l, 0, j)),
                  pl.BlockSpec((1, 1, tn), lambda l, j: (l, 0, j))],
        out_specs=pl.BlockSpec((1, rows, tn), lambda l, j: (l, 0, j)),
        out_shape=jax.ShapeDtypeStruct((depth, rows, n), F32),
        compiler_params=_params(("parallel", "parallel")),
        name="ada_mod",
    )(c_pad, w_ada, b_ada.reshape(depth, 1, n))


def _rope(y, cos, sin):
    w = y.shape[1]
    reps = w // LANES
    cosw = jnp.concatenate([cos] * reps, axis=1) if reps > 1 else cos
    sinw = jnp.concatenate([sin] * reps, axis=1) if reps > 1 else sin
    lane = lax.broadcasted_iota(jnp.int32, y.shape, 1)
    first = (lane % HEAD_DIM) < (HEAD_DIM // 2)
    partner = jnp.where(first, pltpu.roll(y, w - HEAD_DIM // 2, 1), pltpu.roll(y, HEAD_DIM // 2, 1))
    return y * cosw + partner * sinw


def _inproj_kernel(x_ref, sc_ref, sh_ref, w_ref, cos_ref, sin_ref,
                   q_ref, k_ref, iq_ref, ik_ref, r_ref, m_ref, s_ref, st_ref, vt_ref, *, kb):
    h = (x_ref[0] * (1.0 + sc_ref[0]) + sh_ref[0]).astype(BF16)
    cos = cos_ref[0]
    sin = sin_ref[0]

    def proj(start, width):
        return _dot(h, w_ref[:, start:start + width])

    q_ref[0] = (_rope(proj(OFF_Q, MIX_ATTN), cos, sin) * (HEAD_DIM ** -0.5 * LOG2_E)).astype(BF16)
    k_ref[0] = _rope(proj(OFF_K, MIX_ATTN), cos, sin).astype(BF16)
    iq_ref[0] = _rope(proj(OFF_IQ, IQ_WIDTH), cos, sin).astype(BF16)
    ik_ref[0] = _rope(proj(OFF_IK, LANES), cos, sin).astype(BF16)
    r_ref[0, :, 0:2 * MIX_RET] = _rope(proj(OFF_R, 2 * MIX_RET), cos, sin).astype(BF16)
    r_ref[0, :, 2 * MIX_RET:R_WIDTH] = proj(OFF_R + 2 * MIX_RET, 2 * MIX_RET).astype(BF16)
    m_ref[0, :, 0:2 * MIX_MLSTM] = proj(OFF_M, 2 * MIX_MLSTM).astype(BF16)
    m_ref[0, :, 2 * MIX_MLSTM:M_WIDTH] = proj(OFF_M + 2 * MIX_MLSTM, 2 * MIX_MLSTM).astype(BF16)
    y = proj(OFF_S, S_WIDTH)
    s_ref[0] = y
    st_ref[0] = y.T[0:S_ROWS, :]
    tm = h.shape[0]
    for part in range(V_PAD // MIX_ATTN):
        yv = proj(OFF_V + part * MIX_ATTN, MIX_ATTN)
        lane = lax.broadcasted_iota(jnp.int32, yv.shape, 1)
        yvt = jnp.where(lane % LANES == HEAD_DIM, 1.0, yv).T.astype(BF16)
        heads_per_part = MIX_ATTN // LANES
        for j in range(tm // kb):
            for hh in range(heads_per_part):
                row0 = (part * heads_per_part + hh) * V_ROWS
                vt_ref[0, j, row0:row0 + V_ROWS, :] = yvt[hh * LANES:hh * LANES + V_ROWS, j * kb:(j + 1) * kb]


def _in_proj(x, sc, sh, w, cos_t, sin_t, tm, kb):
    bsz, seq, d = x.shape
    row = lambda b, i: (b, i, 0)
    per_b = lambda b, i: (b, 0, 0)
    widths = (MIX_ATTN, MIX_ATTN, IQ_WIDTH, LANES, R_WIDTH, M_WIDTH)
    return pl.pallas_call(
        functools.partial(_inproj_kernel, kb=kb),
        grid=(bsz, seq // tm),
        in_specs=[pl.BlockSpec((1, tm, d), row),
                  pl.BlockSpec((1, 1, d), per_b),
                  pl.BlockSpec((1, 1, d), per_b),
                  pl.BlockSpec((d, W_TOTAL), lambda b, i: (0, 0)),
                  pl.BlockSpec((1, tm, LANES), row),
                  pl.BlockSpec((1, tm, LANES), row)],
        out_specs=[pl.BlockSpec((1, tm, wd), row) for wd in widths]
                  + [pl.BlockSpec((1, tm, S_WIDTH), row),
                     pl.BlockSpec((1, S_ROWS, tm), lambda b, i: (b, 0, i)),
                     pl.BlockSpec((1, tm // kb, N_HEADS_ATTN * V_ROWS, kb), lambda b, i: (b, i, 0, 0))],
        out_shape=[jax.ShapeDtypeStruct((bsz, seq, wd), BF16) for wd in widths]
                  + [jax.ShapeDtypeStruct((bsz, seq, S_WIDTH), F32),
                     jax.ShapeDtypeStruct((bsz, S_ROWS, seq), F32),
                     jax.ShapeDtypeStruct((bsz, seq // kb, N_HEADS_ATTN * V_ROWS, kb), BF16)],
        compiler_params=_params(("parallel", "parallel")),
        name="in_proj",
    )(x, sc, sh, w, cos_t, sin_t)


def _dsa_kernel(q_ref, k_ref, iq_ref, ik_ref, st_ref, vt_ref, tril_ref, o_ref,
                key_scr, byte_scr, cand_scr, bias_scr, s_scr, p_scr, m_scr, alpha_scr, acc_scr,
                *, tq, kb, topk):
    q0 = pl.program_id(1) * tq
    n_blocks = (q0 + tq + kb - 1) // kb
    qpos = q0 + lax.broadcasted_iota(jnp.int32, (1, tq), 1)
    q_limit = (qpos // CHUNK + 1) * CHUNK
    krow = lax.broadcasted_iota(jnp.int32, (kb, 1), 0)

    def head_of_pair(x, h):
        pair = x[:, (h // 2) * LANES:(h // 2 + 1) * LANES]
        lane = lax.broadcasted_iota(jnp.int32, pair.shape, 1)
        keep = (lane < HEAD_DIM) if h % 2 == 0 else (lane >= HEAD_DIM)
        return jnp.where(keep, pair, jnp.zeros_like(pair))

    iw = st_ref[0][S_IW:S_IW + N_IDX_HEADS, :] * (N_IDX_HEADS ** -0.5 * IDX_DIM ** -0.5)
    iq = iq_ref[0]
    iq_heads = [head_of_pair(iq, h) for h in range(N_IDX_HEADS)]

    def score_body(c, carry):
        k0 = pl.multiple_of(c * kb, kb)
        ik2 = ik_ref[0, pl.ds(k0, kb), :]
        score = jnp.zeros((kb, tq), F32)
        for h in range(N_IDX_HEADS):
            score = score + jnp.maximum(_dot_nt(ik2, iq_heads[h]), 0.0) * iw[h:h + 1, :]
        bits = pltpu.bitcast(score, jnp.int32)
        key = jnp.where(bits >= 0, bits, bits ^ jnp.int32(0x7FFFFFFF))
        key = jnp.where(k0 + krow < q_limit, key, jnp.int32(INT_MIN))
        key_scr[c] = key
        byte_scr[0, c] = ((key >> 24) + 128).astype(F32).astype(BF16)
        for lvl in range(1, 4):
            byte_scr[lvl, c] = ((key >> (24 - 8 * lvl)) & 255).astype(F32).astype(BF16)
        return carry

    lax.fori_loop(0, n_blocks, score_body, 0)

    pack = 16
    one = jnp.ones((kb, tq), BF16)
    zero = jnp.zeros((kb, tq), BF16)

    def count_ge(lvl, cand):
        cand_b = cand.astype(BF16)

        def body(c, acc):
            plane = byte_scr[0, c] if lvl == 0 else cand_scr[c]
            hit = jnp.where(plane >= cand_b, one, zero)
            parts = [hit[i * pack:(i + 1) * pack, :] for i in range(kb // pack)]
            while len(parts) > 1:
                parts = [parts[i] + parts[i + 1] for i in range(0, len(parts), 2)]
            return acc + parts[0]

        acc = lax.fori_loop(0, n_blocks, body, jnp.zeros((pack, tq), BF16))
        return jnp.sum(acc.astype(F32), axis=0, keepdims=True)

    above = jnp.zeros((1, tq), F32)
    t = jnp.zeros((1, tq), jnp.int32)
    for lvl in range(4):
        def bit_body(i, v, lvl=lvl, above=above):
            cand = v + lax.shift_left(jnp.int32(1), 7 - i).astype(F32)
            return jnp.where(above + count_ge(lvl, cand) >= topk, cand, v)

        v = lax.fori_loop(0, 8, bit_body, jnp.zeros((1, tq), F32))
        above = above + count_ge(lvl, v + 1.0)
        t = t | lax.shift_left(v.astype(jnp.int32), 24 - 8 * lvl)
        if lvl < 3:
            v_b = v.astype(BF16)

            def narrow(c, carry, lvl=lvl, v_b=v_b):
                plane = byte_scr[0, c] if lvl == 0 else cand_scr[c]
                cand_scr[c] = jnp.where(plane == v_b, byte_scr[lvl + 1, c], -one)
                return carry

            lax.fori_loop(0, n_blocks, narrow, 0)
    thr = jnp.maximum(t ^ jnp.int32(INT_MIN), jnp.int32(INT_MIN + 1))
    need = topk - above

    q = q_ref[0]
    q_heads = [head_of_pair(q, h) for h in range(N_HEADS_ATTN)]
    m_scr[...] = jnp.full(m_scr.shape, NEG_BIG, F32)
    alpha_scr[...] = jnp.ones(alpha_scr.shape, F32)
    acc_scr[...] = jnp.zeros(acc_scr.shape, F32)
    p_scr[...] = jnp.zeros(p_scr.shape, BF16)

    def stage_mask(c, ties_before):
        key = key_scr[c]
        tie = key == thr
        rank = _dot(tril_ref[...], jnp.where(tie, 1.0, 0.0).astype(BF16)) + ties_before
        sel = (key > thr) | (tie & (rank <= need))
        bias_scr[...] = jnp.where(sel, 0.0, NEG_BIG)
        return rank[kb - 1:kb, :]

    def stage_logits(c, h):
        k0 = pl.multiple_of(c * kb, kb)
        kp = k_ref[0, pl.ds(k0, kb), (h // 2) * LANES:(h // 2 + 1) * LANES]
        s_scr[h] = _dot_nt(kp, q_heads[h])

    def stage_softmax(h):
        for half in range(tq // LANES):
            ln = slice(half * LANES, (half + 1) * LANES)
            s = s_scr[h, :, ln] + bias_scr[:, ln]
            m_old = m_scr[h, :, ln]
            m_new = jnp.maximum(m_old, jnp.max(s, axis=0, keepdims=True))
            p_scr[h, :, ln] = jnp.exp2(s - m_new).astype(BF16)
            alpha_scr[h, :, ln] = jnp.exp2(m_old - m_new)
            m_scr[h, :, ln] = m_new

    def stage_values(c, h):
        vt = vt_ref[0, c, h * V_ROWS:(h + 1) * V_ROWS, :]
        acc_scr[h, 0:V_ROWS, :] = alpha_scr[h] * acc_scr[h, 0:V_ROWS, :] + _dot(vt, p_scr[h])

    ties0 = stage_mask(0, jnp.zeros((1, tq), F32))
    for h in range(N_HEADS_ATTN):
        stage_logits(0, h)

    def attn_body(j, ties_before):
        c_old = jnp.maximum(j - 2, 0)
        for h in range(N_HEADS_ATTN):
            stage_values(c_old, h)
            stage_softmax(h)
            stage_logits(j, h)
        return stage_mask(j, ties_before)

    lax.fori_loop(1, n_blocks, attn_body, ties0)
    for h in range(N_HEADS_ATTN):
        stage_values(jnp.maximum(n_blocks - 2, 0), h)
        stage_softmax(h)
    for h in range(N_HEADS_ATTN):
        stage_values(n_blocks - 1, h)
    for h in range(N_HEADS_ATTN):
        acc = acc_scr[h]
        out = acc * (1.0 / acc[HEAD_DIM:HEAD_DIM + 1, :])
        o_ref[0, :, h * LANES:(h + 1) * LANES] = out.T.astype(o_ref.dtype)


def _dsa(q, k, iq, ik2, small_t, v_t, tril, tq, kb):
    bsz, seq, _ = q.shape
    topk = min(TOPK_MAX, seq // 4)
    kern = functools.partial(_dsa_kernel, tq=tq, kb=kb, topk=topk)
    return pl.pallas_call(
        kern,
        grid=(bsz, seq // tq),
        in_specs=[pl.BlockSpec((1, tq, MIX_ATTN), lambda b, i: (b, i, 0)),
                  pl.BlockSpec((1, seq, MIX_ATTN), lambda b, i: (b, 0, 0)),
                  pl.BlockSpec((1, tq, IQ_WIDTH), lambda b, i: (b, i, 0)),
                  pl.BlockSpec((1, seq, LANES), lambda b, i: (b, 0, 0)),
                  pl.BlockSpec((1, S_ROWS, tq), lambda b, i: (b, 0, i)),
                  pl.BlockSpec((1, seq // kb, N_HEADS_ATTN * V_ROWS, kb), lambda b, i: (b, 0, 0, 0)),
                  pl.BlockSpec((kb, kb), lambda b, i: (0, 0))],
        out_specs=pl.BlockSpec((1, tq, V_PAD), lambda b, i: (b, i, 0)),
        out_shape=jax.ShapeDtypeStruct((bsz, seq, V_PAD), BF16),
        scratch_shapes=[pltpu.VMEM((seq // kb, kb, tq), jnp.int32),
                        pltpu.VMEM((4, seq // kb, kb, tq), BF16),
                        pltpu.VMEM((seq // kb, kb, tq), BF16),
                        pltpu.VMEM((kb, tq), F32),
                        pltpu.VMEM((N_HEADS_ATTN, kb, tq), F32),
                        pltpu.VMEM((N_HEADS_ATTN, kb, tq), BF16),
                        pltpu.VMEM((N_HEADS_ATTN, 1, tq), F32),
                        pltpu.VMEM((N_HEADS_ATTN, 1, tq), F32),
                        pltpu.VMEM((N_HEADS_ATTN, LANES, tq), F32)],
        compiler_params=_params(("parallel", "arbitrary")),
        name="dsa",
    )(q, k, iq, ik2, small_t, v_t, tril)


def _head_norm(y):
    mu = jnp.mean(y, axis=-1, keepdims=True)
    var = jnp.mean(jnp.square(y - mu), axis=-1, keepdims=True)
    return (y - mu) * lax.rsqrt(var + LN_EPS)


def _ret_kernel(r_ref, o_ref, state_scr, *, cr, grp):
    @pl.when(pl.program_id(1) == 0)
    def _():
        state_scr[...] = jnp.zeros_like(state_scr)

    ri = lax.broadcasted_iota(jnp.int32, (cr, cr), 0)
    ci = lax.broadcasted_iota(jnp.int32, (cr, cr), 1)
    diff = (ri - ci).astype(F32)
    pos = lax.broadcasted_iota(jnp.int32, (cr, 1), 0).astype(F32)
    for h in range(N_HEADS_RET):
        log_gamma = jnp.log1p(jnp.full((1, 1), -(2.0 ** (-5.0 - h)), F32))
        sl = lambda part: slice(part * MIX_RET + h * HEAD_DIM, part * MIX_RET + (h + 1) * HEAD_DIM)
        decay_in = jnp.where(diff >= 0, jnp.exp(diff * log_gamma), 0.0) * (HEAD_DIM ** -0.5)
        q_decay = jnp.exp((pos + 1.0) * log_gamma)
        k_decay = (HEAD_DIM ** -0.5) * jnp.exp((cr - 1.0 - pos) * log_gamma)
        chunk_decay = jnp.exp(cr * log_gamma)
        for g in range(grp):
            q = r_ref[g, :, sl(0)]
            k = r_ref[g, :, sl(1)]
            v = r_ref[g, :, sl(2)]
            gate = r_ref[g, :, sl(3)].astype(F32)
            scores = _dot_nt(q, k) * decay_in
            inner = _dot(scores.astype(BF16), v)
            state = state_scr[g, h]
            cross = q_decay * _dot(q, state.astype(BF16))
            k_dec = (k.astype(F32) * k_decay).astype(BF16)
            state_scr[g, h] = state * chunk_decay + _dot_tn(k_dec, v)
            y = _head_norm(inner + cross)
            o_ref[g, :, h * HEAD_DIM:(h + 1) * HEAD_DIM] = (y * (gate * jax.nn.sigmoid(gate))).astype(o_ref.dtype)


def _retention(r_proj, cr, grp):
    bsz, seq, _ = r_proj.shape
    return pl.pallas_call(
        functools.partial(_ret_kernel, cr=cr, grp=grp),
        grid=(bsz // grp, seq // cr),
        in_specs=[pl.BlockSpec((grp, cr, R_WIDTH), lambda b, i: (b, i, 0))],
        out_specs=pl.BlockSpec((grp, cr, MIX_RET), lambda b, i: (b, i, 0)),
        out_shape=jax.ShapeDtypeStruct((bsz, seq, MIX_RET), BF16),
        scratch_shapes=[pltpu.VMEM((grp, N_HEADS_RET, HEAD_DIM, HEAD_DIM), F32)],
        compiler_params=_params(("parallel", "arbitrary")),
        name="retention",
    )(r_proj)


def _mlstm_kernel(m_ref, sm_ref, bias_ref, cw_ref, cb_ref, tril_ref, o_ref,
                  xbuf, c_scr, n_scr, m_scr, *, cm, grp):
    halo = 8

    @pl.when(pl.program_id(1) == 0)
    def _():
        xbuf[:, 0:halo, :] = jnp.zeros((grp, halo, 2 * MIX_MLSTM), F32)
        c_scr[...] = jnp.zeros_like(c_scr)
        n_scr[...] = jnp.zeros_like(n_scr)
        m_scr[...] = jnp.zeros_like(m_scr)

    ri = lax.broadcasted_iota(jnp.int32, (cm, cm), 0)
    ci = lax.broadcasted_iota(jnp.int32, (cm, cm), 1)
    causal = ri >= ci
    scale = HEAD_DIM ** -0.5
    items = [(g, h) for g in range(grp) for h in range(N_HEADS_MLSTM)]
    mm, qk, gates, b_all, rows_t = {}, {}, {}, {}, {}
    for g in range(grp):
        mm[g] = m_ref[g]
        xbuf[g, halo:halo + cm, :] = mm[g][:, 0:2 * MIX_MLSTM].astype(F32)
        conv = cb_ref[...]
        for j in range(CONV_WIDTH):
            off = halo - (CONV_WIDTH - 1) + j
            conv = conv + xbuf[g, off:off + cm, :] * cw_ref[j:j + 1, :]
        xbuf[g, 0:halo, :] = xbuf[g, cm:cm + halo, :]
        qk[g] = conv * jax.nn.sigmoid(conv)
        gates[g] = sm_ref[g] + bias_ref[...]
        log_f = jax.nn.log_sigmoid(gates[g])
        b_all[g] = jnp.dot(tril_ref[...], log_f, preferred_element_type=F32,
                           precision=lax.Precision.HIGHEST)
        lane = lax.broadcasted_iota(jnp.int32, (cm, LANES), 1)
        rows_t[g] = jnp.where(lane >= S_MF, b_all[g], gates[g]).T

    def head_cols(x, h):
        return x[:, h * HEAD_DIM:(h + 1) * HEAD_DIM]

    q, k, v, b_c, i_c, m_prev, c_mem, n_mem = {}, {}, {}, {}, {}, {}, {}, {}
    for it in items:
        g, h = it
        q[it] = head_cols(qk[g], h).astype(BF16)
        k[it] = head_cols(qk[g][:, MIX_MLSTM:], h)
        v[it] = head_cols(mm[g][:, 2 * MIX_MLSTM:], h)
        b_c[it] = b_all[g][:, S_MF + h:S_MF + h + 1]
        i_c[it] = gates[g][:, S_MI + h:S_MI + h + 1]
        m_prev[it] = m_scr[g, h]
        c_mem[it] = c_scr[g, h]
        n_mem[it] = n_scr[g, h]

    w, inter, m_q = {}, {}, {}
    for it in items:
        g, h = it
        b_r = rows_t[g][S_MF + h:S_MF + h + 1, :]
        i_r = rows_t[g][S_MI + h:S_MI + h + 1, :]
        log_w = jnp.where(causal, b_c[it] - b_r + i_r, -jnp.inf)
        log_inter = b_c[it] + m_prev[it]
        m_q[it] = jnp.maximum(log_inter, jnp.max(log_w, axis=1, keepdims=True))
        w[it] = jnp.exp(log_w - m_q[it])
        inter[it] = jnp.exp(log_inter - m_q[it])

    s = {it: _dot_nt(q[it], k[it].astype(BF16)) * scale * w[it] for it in items}
    cross = {it: _dot(q[it], c_mem[it].astype(BF16)) for it in items}
    num = {it: _dot(s[it].astype(BF16), v[it]) + inter[it] * cross[it] for it in items}
    for it in items:
        g, h = it
        den = (jnp.sum(s[it], axis=1, keepdims=True)
               + inter[it] * jnp.sum(q[it].astype(F32) * n_mem[it], axis=1, keepdims=True))
        h_tilde = num[it] / jnp.maximum(jnp.abs(den), jnp.exp(-m_q[it]))
        og = head_cols(mm[g][:, 3 * MIX_MLSTM:], h).astype(F32)
        o_ref[g, :, h * HEAD_DIM:(h + 1) * HEAD_DIM] = _head_norm(jax.nn.sigmoid(og) * h_tilde).astype(o_ref.dtype)

    for it in items:
        g, h = it
        b_last = b_c[it][cm - 1:cm, :]
        log_k = b_last - b_c[it] + i_c[it]
        m_new = jnp.maximum(b_last + m_prev[it], jnp.max(log_k, axis=0, keepdims=True))
        kw = k[it] * (scale * jnp.exp(log_k - m_new))
        decay = jnp.exp(b_last + m_prev[it] - m_new)
        c_scr[g, h] = decay * c_mem[it] + _dot_tn(kw.astype(BF16), v[it])
        n_scr[g, h] = decay * n_mem[it] + jnp.sum(kw, axis=0, keepdims=True)
        m_scr[g, h] = m_new


def _mlstm(m_proj, small, gate_bias, conv_w, conv_b, tril, cm, grp):
    bsz, seq, _ = m_proj.shape
    const = lambda b, i: (0, 0)
    return pl.pallas_call(
        functools.partial(_mlstm_kernel, cm=cm, grp=grp),
        grid=(bsz // grp, seq // cm),
        in_specs=[pl.BlockSpec((grp, cm, M_WIDTH), lambda b, i: (b, i, 0)),
                  pl.BlockSpec((grp, cm, S_WIDTH), lambda b, i: (b, i, 0)),
                  pl.BlockSpec((1, S_WIDTH), const),
                  pl.BlockSpec((CONV_WIDTH, 2 * MIX_MLSTM), const),
                  pl.BlockSpec((1, 2 * MIX_MLSTM), const),
                  pl.BlockSpec((cm, cm), const)],
        out_specs=pl.BlockSpec((grp, cm, MIX_MLSTM), lambda b, i: (b, i, 0)),
        out_shape=jax.ShapeDtypeStruct((bsz, seq, MIX_MLSTM), BF16),
        scratch_shapes=[pltpu.VMEM((grp, cm + 8, 2 * MIX_MLSTM), F32),
                        pltpu.VMEM((grp, N_HEADS_MLSTM, HEAD_DIM, HEAD_DIM), F32),
                        pltpu.VMEM((grp, N_HEADS_MLSTM, 1, HEAD_DIM), F32),
                        pltpu.VMEM((grp, N_HEADS_MLSTM, 1, 1), F32)],
        compiler_params=_params(("parallel", "arbitrary")),
        name="mlstm",
    )(m_proj, small, gate_bias, conv_w, conv_b, tril)


def _layer_norm(z, g, b):
    mu = jnp.mean(z, axis=-1, keepdims=True)
    var = jnp.mean(jnp.square(z - mu), axis=-1, keepdims=True)
    return (z - mu) * lax.rsqrt(var + LN_EPS) * g + b


def _outproj_kernel(oa_ref, ob_ref, oc_ref, w_ref, x_ref, gm_ref, g_ref, b_ref, o_ref, *, alpha):
    mix = _dot(oa_ref[0], w_ref[0:V_PAD, :])
    mix = mix + _dot(ob_ref[0], w_ref[V_PAD:V_PAD + MIX_RET, :])
    mix = mix + _dot(oc_ref[0], w_ref[V_PAD + MIX_RET:, :])
    z = alpha * x_ref[0] + (1.0 + gm_ref[0]) * mix
    o_ref[0] = _layer_norm(z, g_ref[...], b_ref[...])


def _out_proj(o_a, o_b, o_c, w_out, x, g_m, ln_g, ln_b, tm, alpha):
    bsz, seq, d = x.shape
    row = lambda b, i: (b, i, 0)
    const = lambda b, i: (0, 0)
    return pl.pallas_call(
        functools.partial(_outproj_kernel, alpha=alpha),
        grid=(bsz, seq // tm),
        in_specs=[pl.BlockSpec((1, tm, V_PAD), row),
                  pl.BlockSpec((1, tm, MIX_RET), row),
                  pl.BlockSpec((1, tm, MIX_MLSTM), row),
                  pl.BlockSpec(w_out.shape, const),
                  pl.BlockSpec((1, tm, d), row),
                  pl.BlockSpec((1, 1, d), lambda b, i: (b, 0, 0)),
                  pl.BlockSpec((1, d), const),
                  pl.BlockSpec((1, d), const)],
        out_specs=pl.BlockSpec((1, tm, d), row),
        out_shape=jax.ShapeDtypeStruct((bsz, seq, d), F32),
        compiler_params=_params(("parallel", "parallel")),
        name="out_proj",
    )(o_a, o_b, o_c, w_out, x, g_m, ln_g, ln_b)


def _route(scores, biased):
    col = lambda a, e: a[:, e:e + 1]
    epg = EXPERTS_PER_GROUP
    group_scores = []
    for g in range(N_GROUPS):
        vals = [col(biased, g * epg + j) for j in range(epg)]
        best = None
        for a in range(epg):
            for b in range(a + 1, epg):
                pair = vals[a] + vals[b]
                best = pair if best is None else jnp.maximum(best, pair)
        group_scores.append(best)
    best_g = jnp.zeros_like(group_scores[0], dtype=jnp.int32)
    best_v = group_scores[0]
    for g in range(1, N_GROUPS):
        better = group_scores[g] > best_v
        best_g = jnp.where(better, g, best_g)
        best_v = jnp.where(better, group_scores[g], best_v)
    cand_b = [sum(jnp.where(best_g == g, col(biased, g * epg + j), 0.0) for g in range(N_GROUPS))
              for j in range(epg)]
    cand_s = [sum(jnp.where(best_g == g, col(scores, g * epg + j), 0.0) for g in range(N_GROUPS))
              for j in range(epg)]

    def argmax_first(vals, skip=None):
        idx = None
        val = None
        for j, vj in enumerate(vals):
            if skip is not None:
                vj = jnp.where(skip == j, -jnp.inf, vj)
            if idx is None:
                idx, val = jnp.zeros_like(best_g), vj
            else:
                better = vj > val
                idx = jnp.where(better, j, idx)
                val = jnp.where(better, vj, val)
        return idx

    first = argmax_first(cand_b)
    second = argmax_first(cand_b, skip=first)
    w1 = sum(jnp.where(first == j, cand_s[j], 0.0) for j in range(epg))
    w2 = sum(jnp.where(second == j, cand_s[j], 0.0) for j in range(epg))
    total = w1 + w2
    e1 = best_g * epg + first
    e2 = best_g * epg + second
    lane = lax.broadcasted_iota(jnp.int32, scores.shape, 1)
    return jnp.where(lane == e1, w1 / total, 0.0) + jnp.where(lane == e2, w2 / total, 0.0)


def _moe_kernel(x_ref, sc_ref, sh_ref, gf_ref, wr_ref, br_ref, wg_ref, wu_ref, wd_ref, g_ref, b_ref, o_ref,
                hid_scr, *, alpha):
    x = x_ref[0]
    h = x * (1.0 + sc_ref[0]) + sh_ref[0]
    hb = h.astype(BF16)
    scores = jax.nn.sigmoid(_dot(h, wr_ref[...]))
    gate = _route(scores, scores + br_ref[...])
    n_exp, _, d_ff = wg_ref.shape
    for e in range(n_exp):
        gate_pre = _dot(hb, wg_ref[e])
        up = _dot(hb, wu_ref[e])
        hid = gate_pre * jax.nn.sigmoid(gate_pre) * up * gate[:, e:e + 1]
        hid_scr[:, e * d_ff:(e + 1) * d_ff] = hid.astype(BF16)
    y = _dot(hid_scr[...], wd_ref[...])
    z = alpha * x + (1.0 + gf_ref[0]) * y
    o_ref[0] = _layer_norm(z, g_ref[...], b_ref[...])


def _moe(x, sc, sh, g_f, w_router, b_router, w_gate, w_up, w_down, ln_g, ln_b, tm, alpha):
    bsz, seq, d = x.shape
    n_exp, _, d_ff = w_gate.shape
    row = lambda b, i: (b, i, 0)
    per_b = lambda b, i: (b, 0, 0)
    const = lambda b, i: (0, 0)
    const3 = lambda b, i: (0, 0, 0)
    resident = pl.Buffered(1)
    return pl.pallas_call(
        functools.partial(_moe_kernel, alpha=alpha),
        grid=(bsz, seq // tm),
        in_specs=[pl.BlockSpec((1, tm, d), row),
                  pl.BlockSpec((1, 1, d), per_b),
                  pl.BlockSpec((1, 1, d), per_b),
                  pl.BlockSpec((1, 1, d), per_b),
                  pl.BlockSpec((d, LANES), const),
                  pl.BlockSpec((1, LANES), const),
                  pl.BlockSpec((n_exp, d, d_ff), const3, pipeline_mode=resident),
                  pl.BlockSpec((n_exp, d, d_ff), const3, pipeline_mode=resident),
                  pl.BlockSpec((n_exp * d_ff, d), const, pipeline_mode=resident),
                  pl.BlockSpec((1, d), const),
                  pl.BlockSpec((1, d), const)],
        out_specs=pl.BlockSpec((1, tm, d), row),
        out_shape=jax.ShapeDtypeStruct((bsz, seq, d), F32),
        scratch_shapes=[pltpu.VMEM((tm, n_exp * d_ff), BF16)],
        compiler_params=_params(("parallel", "parallel")),
        name="moe",
    )(x, sc, sh, g_f, w_router, b_router, w_gate, w_up, w_down.reshape(n_exp * d_ff, d), ln_g, ln_b)


def _pick(n, pref):
    t = min(pref, n)
    while n % t:
        t //= 2
    return t


def _rope_tables(positions):
    half = HEAD_DIM // 2
    inv_freq = ROPE_THETA ** (-jnp.arange(half, dtype=F32) / half)
    ang = positions.astype(F32)[..., None] * inv_freq
    cos, sin = jnp.cos(ang), jnp.sin(ang)
    reps = LANES // HEAD_DIM
    return (jnp.concatenate([cos, cos] * reps, axis=-1),
            jnp.concatenate([-sin, sin] * reps, axis=-1))


def _prep_w_kernel(w_ref, o_ref):
    o_v = 2 * MIX_ATTN
    o_iq = o_v + MIX_ATTN
    o_ik = o_iq + IQ_WIDTH
    o_iw = o_ik + IDX_DIM
    o_r = o_iw + N_IDX_HEADS
    o_g = o_r + R_WIDTH + M_WIDTH
    n_gate = 2 * N_HEADS_MLSTM

    def put(dst, src, width):
        o_ref[0, :, dst:dst + width] = w_ref[0, :, src:src + width].astype(o_ref.dtype)

    o_ref[0] = jnp.zeros(o_ref.shape[1:], o_ref.dtype)
    put(OFF_Q, 0, 2 * MIX_ATTN)
    put(OFF_IQ, o_iq, IQ_WIDTH)
    put(OFF_R, o_r, R_WIDTH + M_WIDTH)
    put(OFF_IK, o_ik, IDX_DIM)
    put(OFF_IK + IDX_DIM, o_ik, IDX_DIM)
    put(OFF_S + S_IW, o_iw, N_IDX_HEADS)
    put(OFF_S + S_MI, o_g, n_gate)
    for h in range(N_HEADS_ATTN):
        put(OFF_V + h * LANES, o_v + h * HEAD_DIM, HEAD_DIM)


def _prep_w_in(w_in, tr):
    depth, d, n = w_in.shape
    return pl.pallas_call(
        _prep_w_kernel,
        grid=(depth, d // tr),
        in_specs=[pl.BlockSpec((1, tr, n), lambda l, i: (l, i, 0))],
        out_specs=pl.BlockSpec((1, tr, W_TOTAL), lambda l, i: (l, i, 0)),
        out_shape=jax.ShapeDtypeStruct((depth, d, W_TOTAL), F32),
        compiler_params=_params(("parallel", "parallel")),
        name="prep_w_in",
    )(w_in)


def _pad_w_out(w_out):
    d = w_out.shape[1]
    wa = w_out[:MIX_ATTN].reshape(N_HEADS_ATTN, HEAD_DIM, d)
    wa = jnp.concatenate([wa, jnp.zeros_like(wa)], axis=1).reshape(V_PAD, d)
    return jnp.concatenate([wa, w_out[MIX_ATTN:]], axis=0).astype(BF16)


def kernel(x, c, positions, w_ada, b_ada, w_in, i_bias, f_bias, conv_w, conv_b, w_out, ln_mix_g, ln_mix_b,
           w_router, b_router, w_gate, w_up, w_down, ln_ffn_g, ln_ffn_b):
    bsz, seq, d = x.shape
    depth = w_ada.shape[0]
    alpha = (2.0 * depth) ** 0.25

    tm = _pick(seq, 512)
    tq = _pick(seq, 256)
    kb = _pick(seq, 256)
    assert seq // 16 <= 256, "packed bf16 partial counts in the DSA threshold search must stay exact"
    cr = _pick(seq, 256)
    cm = _pick(seq, 256)
    tmoe = _pick(seq, 512)

    cos_t, sin_t = _rope_tables(positions)
    c_pad = jnp.zeros((8, d), F32).at[:bsz].set(c)
    mod = _ada_mod(c_pad, w_ada, b_ada, _pick(6 * d, 1536))
    w_in_p = _prep_w_in(w_in, _pick(d, 256)).astype(BF16)
    grp = 2 if bsz % 2 == 0 else 1

    tril_kb = (jnp.arange(kb)[:, None] >= jnp.arange(kb)[None, :]).astype(BF16)
    tril = (jnp.arange(cm)[:, None] >= jnp.arange(cm)[None, :]).astype(F32)
    w_router_p = jnp.zeros((d, LANES), F32).at[:, :N_EXPERTS].set(w_router)
    b_router_p = jnp.zeros((1, LANES), F32).at[0, :N_EXPERTS].set(b_router)

    for l in range(depth):
        parts = [mod[l, :bsz, j * d:(j + 1) * d].reshape(bsz, 1, d) for j in range(6)]
        sh_m, sc_m, g_m, sh_f, sc_f, g_f = parts
        q, k, iq, ik2, r_proj, m_proj, small, small_t, v_t = _in_proj(
            x, sc_m, sh_m, w_in_p[l], cos_t, sin_t, tm, kb)
        o_a = _dsa(q, k, iq, ik2, small_t, v_t, tril_kb, tq, kb)
        o_b = _retention(r_proj, cr, grp)
        gate_bias = (jnp.zeros((1, S_WIDTH), F32).at[0, S_MI:S_MI + N_HEADS_MLSTM].set(i_bias[l])
                     .at[0, S_MF:S_MF + N_HEADS_MLSTM].set(f_bias[l]))
        o_c = _mlstm(m_proj, small, gate_bias, conv_w[l], conv_b[l].reshape(1, -1), tril, cm, grp)
        x = _out_proj(o_a, o_b, o_c, _pad_w_out(w_out[l]), x, g_m,
                      ln_mix_g[l].reshape(1, d), ln_mix_b[l].reshape(1, d), tm, alpha)
        x = _moe(x, sc_f, sh_f, g_f, w_router_p, b_router_p,
                 w_gate[l].astype(BF16), w_up[l].astype(BF16), w_down[l].astype(BF16),
                 ln_ffn_g[l].reshape(1, d), ln_ffn_b[l].reshape(1, d), tmoe, alpha)
    return x
```

```python
import functools

import numpy as np
import jax
import jax.numpy as jnp
from jax import lax
from jax.experimental import pallas as pl
from jax.experimental.pallas import tpu as pltpu

F32 = jnp.float32
BF16 = jnp.bfloat16

HEAD_DIM = 64
CHUNK = 64
N_HEADS_ATTN = 8
N_IDX_HEADS = 4
IDX_DIM = 64
TOPK_MAX = 256
N_HEADS_RET = 4
N_HEADS_MLSTM = 4
CONV_WIDTH = 4
ROPE_THETA = 10000.0
N_EXPERTS = 16
N_GROUPS = 4
EXPERTS_PER_GROUP = N_EXPERTS // N_GROUPS
D_FF_EXPERT = 256
LN_EPS = 1e-5

MIX_ATTN = N_HEADS_ATTN * HEAD_DIM
MIX_RET = N_HEADS_RET * HEAD_DIM
MIX_MLSTM = N_HEADS_MLSTM * HEAD_DIM

LANES = 128
VMEM_LIMIT = 56 * 1024 * 1024

IQ_WIDTH = N_IDX_HEADS * IDX_DIM
R_WIDTH = 4 * MIX_RET
M_WIDTH = 4 * MIX_MLSTM
S_WIDTH = LANES
S_IW = 0
S_MI = S_IW + N_IDX_HEADS
S_MF = S_MI + N_HEADS_MLSTM
S_ROWS = 16
V_PAD = N_HEADS_ATTN * LANES
V_ROWS = HEAD_DIM + 16
OFF_Q = 0
OFF_K = OFF_Q + MIX_ATTN
OFF_IQ = OFF_K + MIX_ATTN
OFF_R = OFF_IQ + IQ_WIDTH
OFF_M = OFF_R + R_WIDTH
OFF_IK = OFF_M + M_WIDTH
OFF_S = OFF_IK + LANES
OFF_V = OFF_S + S_WIDTH
W_TOTAL = OFF_V + V_PAD

INT_MIN = -2 ** 31
NEG_BIG = -1e30
LOG2_E = 1.4426950408889634


def _dot(a, b):
    return jnp.dot(a, b, preferred_element_type=F32)


def _dot_nt(a, b):
    return lax.dot_general(a, b, (((1,), (1,)), ((), ())), preferred_element_type=F32)


def _dot_tn(a, b):
    return lax.dot_general(a, b, (((0,), (0,)), ((), ())), preferred_element_type=F32)


def _params(sem):
    return pltpu.CompilerParams(dimension_semantics=sem, vmem_limit_bytes=VMEM_LIMIT)


def _ada_kernel(c_ref, w_ref, b_ref, o_ref):
    c = c_ref[...]
    c_act = c * jax.nn.sigmoid(c)
    o_ref[0] = _dot(c_act, w_ref[0]) + b_ref[0]


def _ada_mod(c_pad, w_ada, b_ada, tn):
    depth, d, n = w_ada.shape
    rows = c_pad.shape[0]
    return pl.pallas_call(
        _ada_kernel,
        grid=(depth, n // tn),
        in_specs=[pl.BlockSpec((rows, d), lambda l, j: (0, 0)),
                  pl.BlockSpec((1, d, tn), lambda l, j: (l, 0, j)),
                  pl.BlockSpec((1, 1, tn), lambda l, j: (l, 0, j))],
        out_specs=pl.BlockSpec((1, rows, tn), lambda l, j: (l, 0, j)),
        out_shape=jax.ShapeDtypeStruct((depth, rows, n), F32),
        compiler_params=_params(("parallel", "parallel")),
        name="ada_mod",
    )(c_pad, w_ada, b_ada.reshape(depth, 1, n))


def _rope(y, cos, sin):
    w = y.shape[1]
    reps = w // LANES
    cosw = jnp.concatenate([cos] * reps, axis=1) if reps > 1 else cos
    sinw = jnp.concatenate([sin] * reps, axis=1) if reps > 1 else sin
    lane = lax.broadcasted_iota(jnp.int32, y.shape, 1)
    first = (lane % HEAD_DIM) < (HEAD_DIM // 2)
    partner = jnp.where(first, pltpu.roll(y, w - HEAD_DIM // 2, 1), pltpu.roll(y, HEAD_DIM // 2, 1))
    return y * cosw + partner * sinw


def _inproj_kernel(x_ref, sc_ref, sh_ref, w_ref, cos_ref, sin_ref,
                   q_ref, k_ref, iq_ref, ik_ref, r_ref, m_ref, s_ref, st_ref, vt_ref, *, kb):
    h = (x_ref[0] * (1.0 + sc_ref[0]) + sh_ref[0]).astype(BF16)
    cos = cos_ref[0]
    sin = sin_ref[0]

    def proj(start, width):
        return _dot(h, w_ref[:, start:start + width])

    q_ref[0] = (_rope(proj(OFF_Q, MIX_ATTN), cos, sin) * (HEAD_DIM ** -0.5 * LOG2_E)).astype(BF16)
    k_ref[0] = _rope(proj(OFF_K, MIX_ATTN), cos, sin).astype(BF16)
    iq_ref[0] = _rope(proj(OFF_IQ, IQ_WIDTH), cos, sin).astype(BF16)
    ik_ref[0] = _rope(proj(OFF_IK, LANES), cos, sin).astype(BF16)
    r_ref[0, :, 0:2 * MIX_RET] = _rope(proj(OFF_R, 2 * MIX_RET), cos, sin).astype(BF16)
    r_ref[0, :, 2 * MIX_RET:R_WIDTH] = proj(OFF_R + 2 * MIX_RET, 2 * MIX_RET).astype(BF16)
    m_ref[0, :, 0:2 * MIX_MLSTM] = proj(OFF_M, 2 * MIX_MLSTM).astype(BF16)
    m_ref[0, :, 2 * MIX_MLSTM:M_WIDTH] = proj(OFF_M + 2 * MIX_MLSTM, 2 * MIX_MLSTM).astype(BF16)
    y = proj(OFF_S, S_WIDTH)
    s_ref[0] = y
    st_ref[0] = y.T[0:S_ROWS, :]
    tm = h.shape[0]
    for part in range(V_PAD // MIX_ATTN):
        yv = proj(OFF_V + part * MIX_ATTN, MIX_ATTN)
        lane = lax.broadcasted_iota(jnp.int32, yv.shape, 1)
        yvt = jnp.where(lane % LANES == HEAD_DIM, 1.0, yv).T.astype(BF16)
        heads_per_part = MIX_ATTN // LANES
        for j in range(tm // kb):
            for hh in range(heads_per_part):
                row0 = (part * heads_per_part + hh) * V_ROWS
                vt_ref[0, j, row0:row0 + V_ROWS, :] = yvt[hh * LANES:hh * LANES + V_ROWS, j * kb:(j + 1) * kb]


def _in_proj(x, sc, sh, w, cos_t, sin_t, tm, kb):
    bsz, seq, d = x.shape
    row = lambda b, i: (b, i, 0)
    per_b = lambda b, i: (b, 0, 0)
    widths = (MIX_ATTN, MIX_ATTN, IQ_WIDTH, LANES, R_WIDTH, M_WIDTH)
    return pl.pallas_call(
        functools.partial(_inproj_kernel, kb=kb),
        grid=(bsz, seq // tm),
        in_specs=[pl.BlockSpec((1, tm, d), row),
                  pl.BlockSpec((1, 1, d), per_b),
                  pl.BlockSpec((1, 1, d), per_b),
                  pl.BlockSpec((d, W_TOTAL), lambda b, i: (0, 0)),
                  pl.BlockSpec((1, tm, LANES), row),
                  pl.BlockSpec((1, tm, LANES), row)],
        out_specs=[pl.BlockSpec((1, tm, wd), row) for wd in widths]
                  + [pl.BlockSpec((1, tm, S_WIDTH), row),
                     pl.BlockSpec((1, S_ROWS, tm), lambda b, i: (b, 0, i)),
                     pl.BlockSpec((1, tm // kb, N_HEADS_ATTN * V_ROWS, kb), lambda b, i: (b, i, 0, 0))],
        out_shape=[jax.ShapeDtypeStruct((bsz, seq, wd), BF16) for wd in widths]
                  + [jax.ShapeDtypeStruct((bsz, seq, S_WIDTH), F32),
                     jax.ShapeDtypeStruct((bsz, S_ROWS, seq), F32),
                     jax.ShapeDtypeStruct((bsz, seq // kb, N_HEADS_ATTN * V_ROWS, kb), BF16)],
        compiler_params=_params(("parallel", "parallel")),
        name="in_proj",
    )(x, sc, sh, w, cos_t, sin_t)


def _dsa_kernel(q_ref, k_ref, iq_ref, ik_ref, st_ref, vt_ref, tril_ref, o_ref,
                key_scr, byte_scr, cand_scr, bias_scr, s_scr, p_scr, m_scr, alpha_scr, acc_scr,
                *, tq, kb, topk):
    q0 = pl.program_id(1) * tq
    n_blocks = (q0 + tq + kb - 1) // kb
    qpos = q0 + lax.broadcasted_iota(jnp.int32, (1, tq), 1)
    q_limit = (qpos // CHUNK + 1) * CHUNK
    krow = lax.broadcasted_iota(jnp.int32, (kb, 1), 0)

    def head_of_pair(x, h):
        pair = x[:, (h // 2) * LANES:(h // 2 + 1) * LANES]
        lane = lax.broadcasted_iota(jnp.int32, pair.shape, 1)
        keep = (lane < HEAD_DIM) if h % 2 == 0 else (lane >= HEAD_DIM)
        return jnp.where(keep, pair, jnp.zeros_like(pair))

    iw = st_ref[0][S_IW:S_IW + N_IDX_HEADS, :] * (N_IDX_HEADS ** -0.5 * IDX_DIM ** -0.5)
    iq = iq_ref[0]
    iq_heads = [head_of_pair(iq, h) for h in range(N_IDX_HEADS)]

    def score_body(c, carry):
        k0 = pl.multiple_of(c * kb, kb)
        ik2 = ik_ref[0, pl.ds(k0, kb), :]
        score = jnp.zeros((kb, tq), F32)
        for h in range(N_IDX_HEADS):
            score = score + jnp.maximum(_dot_nt(ik2, iq_heads[h]), 0.0) * iw[h:h + 1, :]
        bits = pltpu.bitcast(score, jnp.int32)
        key = jnp.where(bits >= 0, bits, bits ^ jnp.int32(0x7FFFFFFF))
        key = jnp.where(k0 + krow < q_limit, key, jnp.int32(INT_MIN))
        key_scr[c] = key
        byte_scr[0, c] = ((key >> 24) + 128).astype(F32).astype(BF16)
        for lvl in range(1, 4):
            byte_scr[lvl, c] = ((key >> (24 - 8 * lvl)) & 255).astype(F32).astype(BF16)
        return carry

    lax.fori_loop(0, n_blocks, score_body, 0)

    pack = 16
    one = jnp.ones((kb, tq), BF16)
    zero = jnp.zeros((kb, tq), BF16)

    def count_ge(lvl, cand, narrow_to=None):
        cand_b = cand.astype(BF16)

        def hits(c):
            plane = byte_scr[0, c] if lvl == 0 else cand_scr[c]
            hit = jnp.where(plane >= cand_b, one, zero)
            if narrow_to is not None:
                cand_scr[c] = jnp.where(plane == narrow_to, byte_scr[lvl + 1, c], -one)
            parts = [hit[i * pack:(i + 1) * pack, :] for i in range(kb // pack)]
            while len(parts) > 1:
                parts = [parts[i] + parts[i + 1] for i in range(0, len(parts), 2)]
            return parts[0]

        acc = lax.fori_loop(0, n_blocks // 2, lambda c2, a: a + hits(2 * c2) + hits(2 * c2 + 1),
                            jnp.zeros((pack, tq), BF16))
        acc = lax.cond(n_blocks % 2 == 1, lambda a: a + hits(n_blocks - 1), lambda a: a, acc)
        return jnp.sum(acc.astype(F32), axis=0, keepdims=True)

    above = jnp.zeros((1, tq), F32)
    t = jnp.zeros((1, tq), jnp.int32)
    for lvl in range(4):
        def bit_body(i, v, lvl=lvl, above=above):
            cand = v + lax.shift_left(jnp.int32(1), 7 - i).astype(F32)
            return jnp.where(above + count_ge(lvl, cand) >= topk, cand, v)

        v = lax.fori_loop(0, 8, bit_body, jnp.zeros((1, tq), F32))
        above = above + count_ge(lvl, v + 1.0, narrow_to=v.astype(BF16) if lvl < 3 else None)
        t = t | lax.shift_left(v.astype(jnp.int32), 24 - 8 * lvl)
    thr = jnp.maximum(t ^ jnp.int32(INT_MIN), jnp.int32(INT_MIN + 1))
    need = topk - above

    q = q_ref[0]
    q_heads = [head_of_pair(q, h) for h in range(N_HEADS_ATTN)]
    m_scr[...] = jnp.full(m_scr.shape, NEG_BIG, F32)
    alpha_scr[...] = jnp.ones(alpha_scr.shape, F32)
    acc_scr[...] = jnp.zeros(acc_scr.shape, F32)
    p_scr[...] = jnp.zeros(p_scr.shape, BF16)

    def stage_mask(c, ties_before):
        key = key_scr[c]
        tie = key == thr
        rank = _dot(tril_ref[...], jnp.where(tie, 1.0, 0.0).astype(BF16)) + ties_before
        sel = (key > thr) | (tie & (rank <= need))
        bias_scr[...] = jnp.where(sel, 0.0, NEG_BIG)
        return rank[kb - 1:kb, :]

    def stage_logits(c, h):
        k0 = pl.multiple_of(c * kb, kb)
        kp = k_ref[0, pl.ds(k0, kb), (h // 2) * LANES:(h // 2 + 1) * LANES]
        s_scr[h] = _dot_nt(kp, q_heads[h])

    def stage_softmax(h):
        for half in range(tq // LANES):
            ln = slice(half * LANES, (half + 1) * LANES)
            s = s_scr[h, :, ln] + bias_scr[:, ln]
            m_old = m_scr[h, :, ln]
            m_new = jnp.maximum(m_old, jnp.max(s, axis=0, keepdims=True))
            p_scr[h, :, ln] = jnp.exp2(s - m_new).astype(BF16)
            alpha_scr[h, :, ln] = jnp.exp2(m_old - m_new)
            m_scr[h, :, ln] = m_new

    def stage_values(c, h):
        vt = vt_ref[0, c, h * V_ROWS:(h + 1) * V_ROWS, :]
        acc_scr[h, 0:V_ROWS, :] = alpha_scr[h] * acc_scr[h, 0:V_ROWS, :] + _dot(vt, p_scr[h])

    ties0 = stage_mask(0, jnp.zeros((1, tq), F32))
    for h in range(N_HEADS_ATTN):
        stage_logits(0, h)

    def attn_body(j, ties_before):
        c_old = jnp.maximum(j - 2, 0)
        for h in range(N_HEADS_ATTN):
            stage_values(c_old, h)
            stage_softmax(h)
            stage_logits(j, h)
        return stage_mask(j, ties_before)

    lax.fori_loop(1, n_blocks, attn_body, ties0)
    for h in range(N_HEADS_ATTN):
        stage_values(jnp.maximum(n_blocks - 2, 0), h)
        stage_softmax(h)
    for h in range(N_HEADS_ATTN):
        stage_values(n_blocks - 1, h)
    for h in range(N_HEADS_ATTN):
        acc = acc_scr[h]
        out = acc * (1.0 / acc[HEAD_DIM:HEAD_DIM + 1, :])
        o_ref[0, :, h * LANES:(h + 1) * LANES] = out.T.astype(o_ref.dtype)


def _dsa(q, k, iq, ik2, small_t, v_t, tril, tq, kb):
    bsz, seq, _ = q.shape
    topk = min(TOPK_MAX, seq // 4)
    kern = functools.partial(_dsa_kernel, tq=tq, kb=kb, topk=topk)
    return pl.pallas_call(
        kern,
        grid=(bsz, seq // tq),
        in_specs=[pl.BlockSpec((1, tq, MIX_ATTN), lambda b, i: (b, i, 0)),
                  pl.BlockSpec((1, seq, MIX_ATTN), lambda b, i: (b, 0, 0)),
                  pl.BlockSpec((1, tq, IQ_WIDTH), lambda b, i: (b, i, 0)),
                  pl.BlockSpec((1, seq, LANES), lambda b, i: (b, 0, 0)),
                  pl.BlockSpec((1, S_ROWS, tq), lambda b, i: (b, 0, i)),
                  pl.BlockSpec((1, seq // kb, N_HEADS_ATTN * V_ROWS, kb), lambda b, i: (b, 0, 0, 0)),
                  pl.BlockSpec((kb, kb), lambda b, i: (0, 0))],
        out_specs=pl.BlockSpec((1, tq, V_PAD), lambda b, i: (b, i, 0)),
        out_shape=jax.ShapeDtypeStruct((bsz, seq, V_PAD), BF16),
        scratch_shapes=[pltpu.VMEM((seq // kb, kb, tq), jnp.int32),
                        pltpu.VMEM((4, seq // kb, kb, tq), BF16),
                        pltpu.VMEM((seq // kb, kb, tq), BF16),
                        pltpu.VMEM((kb, tq), F32),
                        pltpu.VMEM((N_HEADS_ATTN, kb, tq), F32),
                        pltpu.VMEM((N_HEADS_ATTN, kb, tq), BF16),
                        pltpu.VMEM((N_HEADS_ATTN, 1, tq), F32),
                        pltpu.VMEM((N_HEADS_ATTN, 1, tq), F32),
                        pltpu.VMEM((N_HEADS_ATTN, LANES, tq), F32)],
        compiler_params=_params(("parallel", "arbitrary")),
        name="dsa",
    )(q, k, iq, ik2, small_t, v_t, tril)


def _head_norm(y):
    mean_mat = jnp.full((HEAD_DIM, HEAD_DIM), 1.0 / HEAD_DIM, BF16)

    def mean_bcast(x):
        hi = x.astype(BF16)
        lo = (x - hi.astype(F32)).astype(BF16)
        return _dot(hi, mean_mat) + _dot(lo, mean_mat)

    yc = y - mean_bcast(y)
    return yc * lax.rsqrt(mean_bcast(yc * yc) + LN_EPS)


def _ret_kernel(r_ref, o_ref, state_scr, *, cr, grp):
    @pl.when(pl.program_id(1) == 0)
    def _():
        state_scr[...] = jnp.zeros_like(state_scr)

    ri = lax.broadcasted_iota(jnp.int32, (cr, cr), 0)
    ci = lax.broadcasted_iota(jnp.int32, (cr, cr), 1)
    diff = (ri - ci).astype(F32)
    pos = lax.broadcasted_iota(jnp.int32, (cr, 1), 0).astype(F32)
    items = [(h, g) for h in range(N_HEADS_RET) for g in range(grp)]
    sl = lambda part, h: slice(part * MIX_RET + h * HEAD_DIM, part * MIX_RET + (h + 1) * HEAD_DIM)
    log_gamma = [jnp.log1p(jnp.full((1, 1), -(2.0 ** (-5.0 - h)), F32)) for h in range(N_HEADS_RET)]
    decay_in = [jnp.where(diff >= 0, jnp.exp(diff * lg), 0.0) * (HEAD_DIM ** -0.5) for lg in log_gamma]
    q = {(h, g): r_ref[g, :, sl(0, h)] for h, g in items}
    k = {(h, g): r_ref[g, :, sl(1, h)] for h, g in items}
    v = {(h, g): r_ref[g, :, sl(2, h)] for h, g in items}
    state = {(h, g): state_scr[g, h] for h, g in items}
    scores = {it: _dot_nt(q[it], k[it]) * decay_in[it[0]] for it in items}
    cross = {it: jnp.exp((pos + 1.0) * log_gamma[it[0]]) * _dot(q[it], state[it].astype(BF16)) for it in items}
    inner = {it: _dot(scores[it].astype(BF16), v[it]) for it in items}
    for it in items:
        h, g = it
        gate = r_ref[g, :, sl(3, h)].astype(F32)
        y = _head_norm(inner[it] + cross[it])
        o_ref[g, :, h * HEAD_DIM:(h + 1) * HEAD_DIM] = (y * (gate * jax.nn.sigmoid(gate))).astype(o_ref.dtype)
    for it in items:
        h, g = it
        k_decay = (HEAD_DIM ** -0.5) * jnp.exp((cr - 1.0 - pos) * log_gamma[h])
        k_dec = (k[it].astype(F32) * k_decay).astype(BF16)
        state_scr[g, h] = state[it] * jnp.exp(cr * log_gamma[h]) + _dot_tn(k_dec, v[it])


def _retention(r_proj, cr, grp):
    bsz, seq, _ = r_proj.shape
    return pl.pallas_call(
        functools.partial(_ret_kernel, cr=cr, grp=grp),
        grid=(bsz // grp, seq // cr),
        in_specs=[pl.BlockSpec((grp, cr, R_WIDTH), lambda b, i: (b, i, 0))],
        out_specs=pl.BlockSpec((grp, cr, MIX_RET), lambda b, i: (b, i, 0)),
        out_shape=jax.ShapeDtypeStruct((bsz, seq, MIX_RET), BF16),
        scratch_shapes=[pltpu.VMEM((grp, N_HEADS_RET, HEAD_DIM, HEAD_DIM), F32)],
        compiler_params=_params(("parallel", "arbitrary")),
        name="retention",
    )(r_proj)


def _mlstm_kernel(m_ref, sm_ref, bias_ref, cw_ref, cb_ref, tril_ref, o_ref,
                  xbuf, a_scr, m_scr, *, cm, grp):
    halo = 8
    gate_shift = S_MF - S_MI

    @pl.when(pl.program_id(1) == 0)
    def _():
        xbuf[:, 0:halo, :] = jnp.zeros((grp, halo, 2 * MIX_MLSTM), F32)
        a_scr[...] = jnp.zeros_like(a_scr)
        m_scr[...] = jnp.zeros_like(m_scr)

    ri = lax.broadcasted_iota(jnp.int32, (cm, cm), 0)
    ci = lax.broadcasted_iota(jnp.int32, (cm, cm), 1)
    causal = ri >= ci
    row = lax.broadcasted_iota(jnp.int32, (cm, LANES), 0)
    lane = lax.broadcasted_iota(jnp.int32, (cm, LANES), 1)
    gate_lanes = (lane >= S_MI) & (lane < S_MI + N_HEADS_MLSTM)
    lane64 = lax.broadcasted_iota(jnp.int32, (cm, HEAD_DIM), 1)
    ones_col = jnp.where(lane64 == 0, 1.0, 0.0).astype(BF16)
    scale = HEAD_DIM ** -0.5

    def head_cols(x, h):
        return x[:, h * HEAD_DIM:(h + 1) * HEAD_DIM]

    items = [(g, h) for g in range(grp) for h in range(N_HEADS_MLSTM)]
    mm, qk, u_row, m_col, inter, e_inv, kw_col, decay = {}, {}, {}, {}, {}, {}, {}, {}
    for g in range(grp):
        mm[g] = m_ref[g]
        xbuf[g, halo:halo + cm, :] = mm[g][:, 0:2 * MIX_MLSTM].astype(F32)
        conv = cb_ref[...]
        for j in range(CONV_WIDTH):
            off = halo - (CONV_WIDTH - 1) + j
            conv = conv + xbuf[g, off:off + cm, :] * cw_ref[j:j + 1, :]
        xbuf[g, 0:halo, :] = xbuf[g, cm:cm + halo, :]
        qk[g] = conv * jax.nn.sigmoid(conv)

        gates = sm_ref[g] + bias_ref[...]
        b_all = jnp.dot(tril_ref[...], jax.nn.log_sigmoid(gates), preferred_element_type=F32,
                        precision=lax.Precision.HIGHEST)
        b_i = jnp.where(gate_lanes, pltpu.roll(b_all, LANES - gate_shift, 1), 0.0)
        u = jnp.where(gate_lanes, gates, 0.0) - b_i
        run = u
        step = 1
        while step < cm:
            run = jnp.maximum(run, jnp.where(row >= step, pltpu.roll(run, step, 0), -jnp.inf))
            step *= 2
        m_prev = m_scr[g]
        m_c = jnp.maximum(m_prev, run)
        m_last = m_c[cm - 1:cm, :]
        u_row[g] = u.T
        m_col[g] = m_c
        inter[g] = jnp.exp(m_prev - m_c)
        e_inv[g] = jnp.exp(-(b_i + m_c))
        kw_col[g] = scale * jnp.exp(u - m_last)
        decay[g] = jnp.exp(m_prev - m_last)
        m_scr[g] = b_i[cm - 1:cm, :] + m_last

    q, k, v_aug, a_mem = {}, {}, {}, {}
    for it in items:
        g, h = it
        q[it] = head_cols(qk[g], h).astype(BF16)
        k[it] = head_cols(qk[g][:, MIX_MLSTM:], h)
        v_aug[it] = jnp.concatenate([head_cols(mm[g][:, 2 * MIX_MLSTM:], h), ones_col], axis=1)
        a_mem[it] = a_scr[g, h]

    def col(x, h):
        return x[:, S_MI + h:S_MI + h + 1]

    w = {}
    for it in items:
        g, h = it
        u_r = u_row[g][S_MI + h:S_MI + h + 1, :]
        w[it] = jnp.exp(jnp.where(causal, u_r - col(m_col[g], h), -jnp.inf))

    s = {it: _dot_nt(q[it], k[it].astype(BF16)) * scale * w[it] for it in items}
    cross = {it: _dot(q[it], a_mem[it].astype(BF16)) for it in items}
    both = {it: _dot(s[it].astype(BF16), v_aug[it]) + col(inter[it[0]], it[1]) * cross[it] for it in items}
    for it in items:
        g, h = it
        den = both[it][:, HEAD_DIM:HEAD_DIM + 1]
        h_tilde = both[it][:, 0:HEAD_DIM] * (1.0 / jnp.maximum(jnp.abs(den), col(e_inv[g], h)))
        og = head_cols(mm[g][:, 3 * MIX_MLSTM:], h).astype(F32)
        o_ref[g, :, h * HEAD_DIM:(h + 1) * HEAD_DIM] = _head_norm(jax.nn.sigmoid(og) * h_tilde).astype(o_ref.dtype)

    for it in items:
        g, h = it
        kw = k[it] * col(kw_col[g], h)
        a_scr[g, h] = col(decay[g], h) * a_mem[it] + _dot_tn(kw.astype(BF16), v_aug[it])


def _mlstm(m_proj, small, gate_bias, conv_w, conv_b, tril, cm, grp):
    bsz, seq, _ = m_proj.shape
    const = lambda b, i: (0, 0)
    return pl.pallas_call(
        functools.partial(_mlstm_kernel, cm=cm, grp=grp),
        grid=(bsz // grp, seq // cm),
        in_specs=[pl.BlockSpec((grp, cm, M_WIDTH), lambda b, i: (b, i, 0)),
                  pl.BlockSpec((grp, cm, S_WIDTH), lambda b, i: (b, i, 0)),
                  pl.BlockSpec((1, S_WIDTH), const),
                  pl.BlockSpec((CONV_WIDTH, 2 * MIX_MLSTM), const),
                  pl.BlockSpec((1, 2 * MIX_MLSTM), const),
                  pl.BlockSpec((cm, cm), const)],
        out_specs=pl.BlockSpec((grp, cm, MIX_MLSTM), lambda b, i: (b, i, 0)),
        out_shape=jax.ShapeDtypeStruct((bsz, seq, MIX_MLSTM), BF16),
        scratch_shapes=[pltpu.VMEM((grp, cm + 8, 2 * MIX_MLSTM), F32),
                        pltpu.VMEM((grp, N_HEADS_MLSTM, HEAD_DIM, LANES), F32),
                        pltpu.VMEM((grp, 1, LANES), F32)],
        compiler_params=_params(("parallel", "arbitrary")),
        name="mlstm",
    )(m_proj, small, gate_bias, conv_w, conv_b, tril)


def _layer_norm(z, g, b):
    mu = jnp.mean(z, axis=-1, keepdims=True)
    var = jnp.mean(jnp.square(z - mu), axis=-1, keepdims=True)
    return (z - mu) * lax.rsqrt(var + LN_EPS) * g + b


def _outproj_kernel(oa_ref, ob_ref, oc_ref, w_ref, x_ref, gm_ref, g_ref, b_ref, o_ref, *, alpha):
    mix = _dot(oa_ref[0], w_ref[0:V_PAD, :])
    mix = mix + _dot(ob_ref[0], w_ref[V_PAD:V_PAD + MIX_RET, :])
    mix = mix + _dot(oc_ref[0], w_ref[V_PAD + MIX_RET:, :])
    z = alpha * x_ref[0] + (1.0 + gm_ref[0]) * mix
    o_ref[0] = _layer_norm(z, g_ref[...], b_ref[...])


def _out_proj(o_a, o_b, o_c, w_out, x, g_m, ln_g, ln_b, tm, alpha):
    bsz, seq, d = x.shape
    row = lambda b, i: (b, i, 0)
    const = lambda b, i: (0, 0)
    return pl.pallas_call(
        functools.partial(_outproj_kernel, alpha=alpha),
        grid=(bsz, seq // tm),
        in_specs=[pl.BlockSpec((1, tm, V_PAD), row),
                  pl.BlockSpec((1, tm, MIX_RET), row),
                  pl.BlockSpec((1, tm, MIX_MLSTM), row),
                  pl.BlockSpec(w_out.shape, const),
                  pl.BlockSpec((1, tm, d), row),
                  pl.BlockSpec((1, 1, d), lambda b, i: (b, 0, 0)),
                  pl.BlockSpec((1, d), const),
                  pl.BlockSpec((1, d), const)],
        out_specs=pl.BlockSpec((1, tm, d), row),
        out_shape=jax.ShapeDtypeStruct((bsz, seq, d), F32),
        compiler_params=_params(("parallel", "parallel")),
        name="out_proj",
    )(o_a, o_b, o_c, w_out, x, g_m, ln_g, ln_b)


def _route(scores, biased):
    col = lambda a, e: a[:, e:e + 1]
    epg = EXPERTS_PER_GROUP
    group_scores = []
    for g in range(N_GROUPS):
        vals = [col(biased, g * epg + j) for j in range(epg)]
        best = None
        for a in range(epg):
            for b in range(a + 1, epg):
                pair = vals[a] + vals[b]
                best = pair if best is None else jnp.maximum(best, pair)
        group_scores.append(best)
    best_g = jnp.zeros_like(group_scores[0], dtype=jnp.int32)
    best_v = group_scores[0]
    for g in range(1, N_GROUPS):
        better = group_scores[g] > best_v
        best_g = jnp.where(better, g, best_g)
        best_v = jnp.where(better, group_scores[g], best_v)
    cand_b = [sum(jnp.where(best_g == g, col(biased, g * epg + j), 0.0) for g in range(N_GROUPS))
              for j in range(epg)]
    cand_s = [sum(jnp.where(best_g == g, col(scores, g * epg + j), 0.0) for g in range(N_GROUPS))
              for j in range(epg)]

    def argmax_first(vals, skip=None):
        idx = None
        val = None
        for j, vj in enumerate(vals):
            if skip is not None:
                vj = jnp.where(skip == j, -jnp.inf, vj)
            if idx is None:
                idx, val = jnp.zeros_like(best_g), vj
            else:
                better = vj > val
                idx = jnp.where(better, j, idx)
                val = jnp.where(better, vj, val)
        return idx

    first = argmax_first(cand_b)
    second = argmax_first(cand_b, skip=first)
    w1 = sum(jnp.where(first == j, cand_s[j], 0.0) for j in range(epg))
    w2 = sum(jnp.where(second == j, cand_s[j], 0.0) for j in range(epg))
    total = w1 + w2
    e1 = best_g * epg + first
    e2 = best_g * epg + second
    lane = lax.broadcasted_iota(jnp.int32, scores.shape, 1)
    return jnp.where(lane == e1, w1 / total, 0.0) + jnp.where(lane == e2, w2 / total, 0.0)


def _moe_kernel(x_ref, sc_ref, sh_ref, gf_ref, wr_ref, br_ref, wg_ref, wu_ref, wd_ref, g_ref, b_ref, o_ref,
                hid_scr, *, alpha):
    x = x_ref[0]
    h = x * (1.0 + sc_ref[0]) + sh_ref[0]
    hb = h.astype(BF16)
    scores = jax.nn.sigmoid(_dot(h, wr_ref[...]))
    gate = _route(scores, scores + br_ref[...])
    n_exp, _, d_ff = wg_ref.shape
    for e in range(n_exp):
        gate_pre = _dot(hb, wg_ref[e])
        up = _dot(hb, wu_ref[e])
        hid = gate_pre * jax.nn.sigmoid(gate_pre) * up * gate[:, e:e + 1]
        hid_scr[:, e * d_ff:(e + 1) * d_ff] = hid.astype(BF16)
    y = _dot(hid_scr[...], wd_ref[...])
    z = alpha * x + (1.0 + gf_ref[0]) * y
    o_ref[0] = _layer_norm(z, g_ref[...], b_ref[...])


def _moe(x, sc, sh, g_f, w_router, b_router, w_gate, w_up, w_down, ln_g, ln_b, tm, alpha):
    bsz, seq, d = x.shape
    n_exp, _, d_ff = w_gate.shape
    row = lambda b, i: (b, i, 0)
    per_b = lambda b, i: (b, 0, 0)
    const = lambda b, i: (0, 0)
    const3 = lambda b, i: (0, 0, 0)
    resident = pl.Buffered(1)
    return pl.pallas_call(
        functools.partial(_moe_kernel, alpha=alpha),
        grid=(bsz, seq // tm),
        in_specs=[pl.BlockSpec((1, tm, d), row),
                  pl.BlockSpec((1, 1, d), per_b),
                  pl.BlockSpec((1, 1, d), per_b),
                  pl.BlockSpec((1, 1, d), per_b),
                  pl.BlockSpec((d, LANES), const),
                  pl.BlockSpec((1, LANES), const),
                  pl.BlockSpec((n_exp, d, d_ff), const3, pipeline_mode=resident),
                  pl.BlockSpec((n_exp, d, d_ff), const3, pipeline_mode=resident),
                  pl.BlockSpec((n_exp * d_ff, d), const, pipeline_mode=resident),
                  pl.BlockSpec((1, d), const),
                  pl.BlockSpec((1, d), const)],
        out_specs=pl.BlockSpec((1, tm, d), row),
        out_shape=jax.ShapeDtypeStruct((bsz, seq, d), F32),
        scratch_shapes=[pltpu.VMEM((tm, n_exp * d_ff), BF16)],
        compiler_params=_params(("parallel", "parallel")),
        name="moe",
    )(x, sc, sh, g_f, w_router, b_router, w_gate, w_up, w_down.reshape(n_exp * d_ff, d), ln_g, ln_b)


def _pick(n, pref):
    t = min(pref, n)
    while n % t:
        t //= 2
    return t


def _rope_tables(positions):
    half = HEAD_DIM // 2
    inv_freq = ROPE_THETA ** (-jnp.arange(half, dtype=F32) / half)
    ang = positions.astype(F32)[..., None] * inv_freq
    cos, sin = jnp.cos(ang), jnp.sin(ang)
    reps = LANES // HEAD_DIM
    return (jnp.concatenate([cos, cos] * reps, axis=-1),
            jnp.concatenate([-sin, sin] * reps, axis=-1))


def _prep_w_kernel(w_ref, o_ref):
    o_v = 2 * MIX_ATTN
    o_iq = o_v + MIX_ATTN
    o_ik = o_iq + IQ_WIDTH
    o_iw = o_ik + IDX_DIM
    o_r = o_iw + N_IDX_HEADS
    o_g = o_r + R_WIDTH + M_WIDTH
    n_gate = 2 * N_HEADS_MLSTM

    def put(dst, src, width):
        o_ref[0, :, dst:dst + width] = w_ref[0, :, src:src + width].astype(o_ref.dtype)

    o_ref[0] = jnp.zeros(o_ref.shape[1:], o_ref.dtype)
    put(OFF_Q, 0, 2 * MIX_ATTN)
    put(OFF_IQ, o_iq, IQ_WIDTH)
    put(OFF_R, o_r, R_WIDTH + M_WIDTH)
    put(OFF_IK, o_ik, IDX_DIM)
    put(OFF_IK + IDX_DIM, o_ik, IDX_DIM)
    put(OFF_S + S_IW, o_iw, N_IDX_HEADS)
    put(OFF_S + S_MI, o_g, n_gate)
    for h in range(N_HEADS_ATTN):
        put(OFF_V + h * LANES, o_v + h * HEAD_DIM, HEAD_DIM)


def _prep_w_in(w_in, tr):
    depth, d, n = w_in.shape
    return pl.pallas_call(
        _prep_w_kernel,
        grid=(depth, d // tr),
        in_specs=[pl.BlockSpec((1, tr, n), lambda l, i: (l, i, 0))],
        out_specs=pl.BlockSpec((1, tr, W_TOTAL), lambda l, i: (l, i, 0)),
        out_shape=jax.ShapeDtypeStruct((depth, d, W_TOTAL), F32),
        compiler_params=_params(("parallel", "parallel")),
        name="prep_w_in",
    )(w_in)


def _pad_w_out(w_out):
    d = w_out.shape[1]
    wa = w_out[:MIX_ATTN].reshape(N_HEADS_ATTN, HEAD_DIM, d)
    wa = jnp.concatenate([wa, jnp.zeros_like(wa)], axis=1).reshape(V_PAD, d)
    return jnp.concatenate([wa, w_out[MIX_ATTN:]], axis=0).astype(BF16)


def kernel(x, c, positions, w_ada, b_ada, w_in, i_bias, f_bias, conv_w, conv_b, w_out, ln_mix_g, ln_mix_b,
           w_router, b_router, w_gate, w_up, w_down, ln_ffn_g, ln_ffn_b):
    bsz, seq, d = x.shape
    depth = w_ada.shape[0]
    alpha = (2.0 * depth) ** 0.25

    tm = _pick(seq, 512)
    tq = _pick(seq, 256)
    kb = _pick(seq, 256)
    assert seq // 16 <= 256, "packed bf16 partial counts in the DSA threshold search must stay exact"
    cr = _pick(seq, 256)
    cm = _pick(seq, 256)
    tmoe = _pick(seq, 512)

    cos_t, sin_t = _rope_tables(positions)
    c_pad = jnp.zeros((8, d), F32).at[:bsz].set(c)
    mod = _ada_mod(c_pad, w_ada, b_ada, _pick(6 * d, 1536))
    w_in_p = _prep_w_in(w_in, _pick(d, 256)).astype(BF16)
    grp = 2 if bsz % 2 == 0 else 1

    tril_kb = (jnp.arange(kb)[:, None] >= jnp.arange(kb)[None, :]).astype(BF16)
    tril = (jnp.arange(cm)[:, None] >= jnp.arange(cm)[None, :]).astype(F32)
    w_router_p = jnp.zeros((d, LANES), F32).at[:, :N_EXPERTS].set(w_router)
    b_router_p = jnp.zeros((1, LANES), F32).at[0, :N_EXPERTS].set(b_router)

    for l in range(depth):
        parts = [mod[l, :bsz, j * d:(j + 1) * d].reshape(bsz, 1, d) for j in range(6)]
        sh_m, sc_m, g_m, sh_f, sc_f, g_f = parts
        q, k, iq, ik2, r_proj, m_proj, small, small_t, v_t = _in_proj(
            x, sc_m, sh_m, w_in_p[l], cos_t, sin_t, tm, kb)
        o_a = _dsa(q, k, iq, ik2, small_t, v_t, tril_kb, tq, kb)
        o_b = _retention(r_proj, cr, grp)
        gate_bias = (jnp.zeros((1, S_WIDTH), F32).at[0, S_MI:S_MI + N_HEADS_MLSTM].set(i_bias[l])
                     .at[0, S_MF:S_MF + N_HEADS_MLSTM].set(f_bias[l]))
        o_c = _mlstm(m_proj, small, gate_bias, conv_w[l], conv_b[l].reshape(1, -1), tril, cm, grp)
        x = _out_proj(o_a, o_b, o_c, _pad_w_out(w_out[l]), x, g_m,
                      ln_mix_g[l].reshape(1, d), ln_mix_b[l].reshape(1, d), tm, alpha)
        x = _moe(x, sc_f, sh_f, g_f, w_router_p, b_router_p,
                 w_gate[l].astype(BF16), w_up[l].astype(BF16), w_down[l].astype(BF16),
                 ln_ffn_g[l].reshape(1, d), ln_ffn_b[l].reshape(1, d), tmoe, alpha)
    return x
```

```python
import functools

import numpy as np
import jax
import jax.numpy as jnp
from jax import lax
from jax.experimental import pallas as pl
from jax.experimental.pallas import tpu as pltpu

F32 = jnp.float32
BF16 = jnp.bfloat16

HEAD_DIM = 64
CHUNK = 64
N_HEADS_ATTN = 8
N_IDX_HEADS = 4
IDX_DIM = 64
TOPK_MAX = 256
N_HEADS_RET = 4
N_HEADS_MLSTM = 4
CONV_WIDTH = 4
ROPE_THETA = 10000.0
N_EXPERTS = 16
N_GROUPS = 4
EXPERTS_PER_GROUP = N_EXPERTS // N_GROUPS
D_FF_EXPERT = 256
LN_EPS = 1e-5

MIX_ATTN = N_HEADS_ATTN * HEAD_DIM
MIX_RET = N_HEADS_RET * HEAD_DIM
MIX_MLSTM = N_HEADS_MLSTM * HEAD_DIM

LANES = 128
SUBLANES = 8
VMEM_LIMIT = 56 * 1024 * 1024

IQ_WIDTH = N_IDX_HEADS * IDX_DIM
R_WIDTH = 4 * MIX_RET
M_WIDTH = 4 * MIX_MLSTM
S_WIDTH = LANES
S_IW = 0
S_MI = S_IW + N_IDX_HEADS
S_MF = S_MI + N_HEADS_MLSTM
S_ROWS = 16
V_PAD = N_HEADS_ATTN * LANES
V_ROWS = HEAD_DIM + 16
OFF_Q = 0
OFF_K = OFF_Q + MIX_ATTN
OFF_IQ = OFF_K + MIX_ATTN
OFF_R = OFF_IQ + IQ_WIDTH
OFF_M = OFF_R + R_WIDTH
OFF_IK = OFF_M + M_WIDTH
OFF_S = OFF_IK + LANES
OFF_V = OFF_S + S_WIDTH
W_TOTAL = OFF_V + V_PAD

INT_MIN = -2 ** 31
NEG_BIG = -1e30
LOG2_E = 1.4426950408889634


def _dot(a, b):
    return jnp.dot(a, b, preferred_element_type=F32)


def _dot_nt(a, b):
    return lax.dot_general(a, b, (((1,), (1,)), ((), ())), preferred_element_type=F32)


def _dot_tn(a, b):
    return lax.dot_general(a, b, (((0,), (0,)), ((), ())), preferred_element_type=F32)


def _params(sem):
    return pltpu.CompilerParams(dimension_semantics=sem, vmem_limit_bytes=VMEM_LIMIT)


def _ada_kernel(c_ref, w_ref, b_ref, o_ref):
    c = c_ref[...]
    c_act = c * jax.nn.sigmoid(c)
    o_ref[0] = _dot(c_act, w_ref[0]) + b_ref[0]


def _ada_mod(c_pad, w_ada, b_ada, tn):
    depth, d, n = w_ada.shape
    rows = c_pad.shape[0]
    return pl.pallas_call(
        _ada_kernel,
        grid=(depth, n // tn),
        in_specs=[pl.BlockSpec((rows, d), lambda l, j: (0, 0)),
                  pl.BlockSpec((1, d, tn), lambda l, j: (l, 0, j)),
                  pl.BlockSpec((1, 1, tn), lambda l, j: (l, 0, j))],
        out_specs=pl.BlockSpec((1, rows, tn), lambda l, j: (l, 0, j)),
        out_shape=jax.ShapeDtypeStruct((depth, rows, n), F32),
        compiler_params=_params(("parallel", "parallel")),
        name="ada_mod",
    )(c_pad, w_ada, b_ada.reshape(depth, 1, n))


def _rope(y, cos, sin):
    w = y.shape[1]
    reps = w // LANES
    cosw = jnp.concatenate([cos] * reps, axis=1) if reps > 1 else cos
    sinw = jnp.concatenate([sin] * reps, axis=1) if reps > 1 else sin
    lane = lax.broadcasted_iota(jnp.int32, y.shape, 1)
    first = (lane % HEAD_DIM) < (HEAD_DIM // 2)
    partner = jnp.where(first, pltpu.roll(y, w - HEAD_DIM // 2, 1), pltpu.roll(y, HEAD_DIM // 2, 1))
    return y * cosw + partner * sinw


def _inproj_kernel(x_ref, sc_ref, sh_ref, w_ref, cos_ref, sin_ref,
                   q_ref, k_ref, iq_ref, ik_ref, r_ref, m_ref, s_ref, st_ref, vt_ref, *, kb):
    h = (x_ref[0] * (1.0 + sc_ref[0]) + sh_ref[0]).astype(BF16)
    cos = cos_ref[0]
    sin = sin_ref[0]

    def proj(start, width):
        return _dot(h, w_ref[:, start:start + width])

    q_ref[0] = (_rope(proj(OFF_Q, MIX_ATTN), cos, sin) * (HEAD_DIM ** -0.5 * LOG2_E)).astype(BF16)
    k_ref[0] = _rope(proj(OFF_K, MIX_ATTN), cos, sin).astype(BF16)
    iq_ref[0] = _rope(proj(OFF_IQ, IQ_WIDTH), cos, sin).astype(BF16)
    ik_ref[0] = _rope(proj(OFF_IK, LANES), cos, sin).astype(BF16)
    r_ref[0, :, 0:2 * MIX_RET] = _rope(proj(OFF_R, 2 * MIX_RET), cos, sin).astype(BF16)
    r_ref[0, :, 2 * MIX_RET:R_WIDTH] = proj(OFF_R + 2 * MIX_RET, 2 * MIX_RET).astype(BF16)
    m_ref[0, :, 0:2 * MIX_MLSTM] = proj(OFF_M, 2 * MIX_MLSTM).astype(BF16)
    m_ref[0, :, 2 * MIX_MLSTM:M_WIDTH] = proj(OFF_M + 2 * MIX_MLSTM, 2 * MIX_MLSTM).astype(BF16)
    y = proj(OFF_S, S_WIDTH)
    s_ref[0] = y
    st_ref[0] = y.T[0:S_ROWS, :]
    tm = h.shape[0]
    for part in range(V_PAD // MIX_ATTN):
        yv = proj(OFF_V + part * MIX_ATTN, MIX_ATTN)
        lane = lax.broadcasted_iota(jnp.int32, yv.shape, 1)
        yvt = jnp.where(lane % LANES == HEAD_DIM, 1.0, yv).T.astype(BF16)
        heads_per_part = MIX_ATTN // LANES
        for j in range(tm // kb):
            for hh in range(heads_per_part):
                row0 = (part * heads_per_part + hh) * V_ROWS
                vt_ref[0, j, row0:row0 + V_ROWS, :] = yvt[hh * LANES:hh * LANES + V_ROWS, j * kb:(j + 1) * kb]


def _in_proj(x, sc, sh, w, cos_t, sin_t, tm, kb):
    bsz, seq, d = x.shape
    row = lambda b, i: (b, i, 0)
    per_b = lambda b, i: (b, 0, 0)
    widths = (MIX_ATTN, MIX_ATTN, IQ_WIDTH, LANES, R_WIDTH, M_WIDTH)
    return pl.pallas_call(
        functools.partial(_inproj_kernel, kb=kb),
        grid=(bsz, seq // tm),
        in_specs=[pl.BlockSpec((1, tm, d), row),
                  pl.BlockSpec((1, 1, d), per_b),
                  pl.BlockSpec((1, 1, d), per_b),
                  pl.BlockSpec((d, W_TOTAL), lambda b, i: (0, 0)),
                  pl.BlockSpec((1, tm, LANES), row),
                  pl.BlockSpec((1, tm, LANES), row)],
        out_specs=[pl.BlockSpec((1, tm, wd), row) for wd in widths]
                  + [pl.BlockSpec((1, tm, S_WIDTH), row),
                     pl.BlockSpec((1, S_ROWS, tm), lambda b, i: (b, 0, i)),
                     pl.BlockSpec((1, tm // kb, N_HEADS_ATTN * V_ROWS, kb), lambda b, i: (b, i, 0, 0))],
        out_shape=[jax.ShapeDtypeStruct((bsz, seq, wd), BF16) for wd in widths]
                  + [jax.ShapeDtypeStruct((bsz, seq, S_WIDTH), F32),
                     jax.ShapeDtypeStruct((bsz, S_ROWS, seq), F32),
                     jax.ShapeDtypeStruct((bsz, seq // kb, N_HEADS_ATTN * V_ROWS, kb), BF16)],
        compiler_params=_params(("parallel", "parallel")),
        name="in_proj",
    )(x, sc, sh, w, cos_t, sin_t)


def _dsa_kernel(q_ref, k_ref, iq_ref, ik_ref, st_ref, vt_ref, tril_ref, o_ref,
                key_scr, byte_scr, cand_scr, bias_scr, s_scr, p_scr, m_scr, alpha_scr, acc_scr,
                *, tq, kb, topk):
    q0 = pl.program_id(1) * tq
    n_blocks = (q0 + tq + kb - 1) // kb
    qpos = q0 + lax.broadcasted_iota(jnp.int32, (1, tq), 1)
    q_limit = (qpos // CHUNK + 1) * CHUNK
    krow = lax.broadcasted_iota(jnp.int32, (kb, 1), 0)

    def head_of_pair(x, h):
        pair = x[:, (h // 2) * LANES:(h // 2 + 1) * LANES]
        lane = lax.broadcasted_iota(jnp.int32, pair.shape, 1)
        keep = (lane < HEAD_DIM) if h % 2 == 0 else (lane >= HEAD_DIM)
        return jnp.where(keep, pair, jnp.zeros_like(pair))

    iw = st_ref[0][S_IW:S_IW + N_IDX_HEADS, :] * (N_IDX_HEADS ** -0.5 * IDX_DIM ** -0.5)
    iq = iq_ref[0]
    iq_heads = [head_of_pair(iq, h) for h in range(N_IDX_HEADS)]

    def score_body(c, carry):
        k0 = pl.multiple_of(c * kb, kb)
        ik2 = ik_ref[0, pl.ds(k0, kb), :]
        score = jnp.zeros((kb, tq), F32)
        for h in range(N_IDX_HEADS):
            score = score + jnp.maximum(_dot_nt(ik2, iq_heads[h]), 0.0) * iw[h:h + 1, :]
        bits = pltpu.bitcast(score, jnp.int32)
        key = jnp.where(bits >= 0, bits, bits ^ jnp.int32(0x7FFFFFFF))
        key = jnp.where(k0 + krow < q_limit, key, jnp.int32(INT_MIN))
        key_scr[c] = key
        byte_scr[0, c] = ((key >> 24) + 128).astype(F32).astype(BF16)
        for lvl in range(1, 4):
            byte_scr[lvl, c] = ((key >> (24 - 8 * lvl)) & 255).astype(F32).astype(BF16)
        return carry

    lax.fori_loop(0, n_blocks, score_body, 0)

    pack = 16
    one = jnp.ones((kb, tq), BF16)
    zero = jnp.zeros((kb, tq), BF16)

    def count_ge(lvl, cand, narrow_to=None):
        cand_b = cand.astype(BF16)

        def hits(c):
            plane = byte_scr[0, c] if lvl == 0 else cand_scr[c]
            hit = jnp.where(plane >= cand_b, one, zero)
            if narrow_to is not None:
                cand_scr[c] = jnp.where(plane == narrow_to, byte_scr[lvl + 1, c], -one)
            parts = [hit[i * pack:(i + 1) * pack, :] for i in range(kb // pack)]
            while len(parts) > 1:
                parts = [parts[i] + parts[i + 1] for i in range(0, len(parts), 2)]
            return parts[0]

        acc = lax.fori_loop(0, n_blocks // 2, lambda c2, a: a + hits(2 * c2) + hits(2 * c2 + 1),
                            jnp.zeros((pack, tq), BF16))
        acc = lax.cond(n_blocks % 2 == 1, lambda a: a + hits(n_blocks - 1), lambda a: a, acc)
        return jnp.sum(acc.astype(F32), axis=0, keepdims=True)

    above = jnp.zeros((1, tq), F32)
    t = jnp.zeros((1, tq), jnp.int32)
    for lvl in range(4):
        def bit_body(i, v, lvl=lvl, above=above):
            cand = v + lax.shift_left(jnp.int32(1), 7 - i).astype(F32)
            return jnp.where(above + count_ge(lvl, cand) >= topk, cand, v)

        v = lax.fori_loop(0, 8, bit_body, jnp.zeros((1, tq), F32))
        above = above + count_ge(lvl, v + 1.0, narrow_to=v.astype(BF16) if lvl < 3 else None)
        t = t | lax.shift_left(v.astype(jnp.int32), 24 - 8 * lvl)
    thr = jnp.maximum(t ^ jnp.int32(INT_MIN), jnp.int32(INT_MIN + 1))
    need = topk - above

    q = q_ref[0]
    q_heads = [head_of_pair(q, h) for h in range(N_HEADS_ATTN)]
    m_scr[...] = jnp.full(m_scr.shape, NEG_BIG, F32)
    alpha_scr[...] = jnp.ones(alpha_scr.shape, F32)
    acc_scr[...] = jnp.zeros(acc_scr.shape, F32)
    p_scr[...] = jnp.zeros(p_scr.shape, BF16)

    def stage_mask(c, ties_before):
        key = key_scr[c]
        tie = key == thr
        rank = _dot(tril_ref[...], jnp.where(tie, 1.0, 0.0).astype(BF16)) + ties_before
        sel = (key > thr) | (tie & (rank <= need))
        bias_scr[...] = jnp.where(sel, 0.0, NEG_BIG)
        return rank[kb - 1:kb, :]

    def stage_logits(c, h):
        k0 = pl.multiple_of(c * kb, kb)
        kp = k_ref[0, pl.ds(k0, kb), (h // 2) * LANES:(h // 2 + 1) * LANES]
        s_scr[h] = _dot_nt(kp, q_heads[h])

    def stage_softmax(h):
        for half in range(tq // LANES):
            ln = slice(half * LANES, (half + 1) * LANES)
            s = s_scr[h, :, ln] + bias_scr[:, ln]
            m_old = m_scr[h, :, ln]
            m_new = jnp.maximum(m_old, jnp.max(s, axis=0, keepdims=True))
            p_scr[h, :, ln] = jnp.exp2(s - m_new).astype(BF16)
            alpha_scr[h, :, ln] = jnp.exp2(m_old - m_new)
            m_scr[h, :, ln] = m_new

    def stage_values(c, h):
        vt = vt_ref[0, c, h * V_ROWS:(h + 1) * V_ROWS, :]
        acc_scr[h, 0:V_ROWS, :] = alpha_scr[h] * acc_scr[h, 0:V_ROWS, :] + _dot(vt, p_scr[h])

    ties0 = stage_mask(0, jnp.zeros((1, tq), F32))
    for h in range(N_HEADS_ATTN):
        stage_logits(0, h)

    def attn_body(j, ties_before):
        c_old = jnp.maximum(j - 2, 0)
        for h in range(N_HEADS_ATTN):
            stage_values(c_old, h)
            stage_softmax(h)
            stage_logits(j, h)
        return stage_mask(j, ties_before)

    lax.fori_loop(1, n_blocks, attn_body, ties0)
    for h in range(N_HEADS_ATTN):
        stage_values(jnp.maximum(n_blocks - 2, 0), h)
        stage_softmax(h)
    for h in range(N_HEADS_ATTN):
        stage_values(n_blocks - 1, h)
    for h in range(N_HEADS_ATTN):
        acc = acc_scr[h]
        out = acc * (1.0 / acc[HEAD_DIM:HEAD_DIM + 1, :])
        o_ref[0, :, h * LANES:(h + 1) * LANES] = out.T.astype(o_ref.dtype)


def _dsa(q, k, iq, ik2, small_t, v_t, tril, tq, kb):
    bsz, seq, _ = q.shape
    topk = min(TOPK_MAX, seq // 4)
    kern = functools.partial(_dsa_kernel, tq=tq, kb=kb, topk=topk)
    return pl.pallas_call(
        kern,
        grid=(bsz, seq // tq),
        in_specs=[pl.BlockSpec((1, tq, MIX_ATTN), lambda b, i: (b, i, 0)),
                  pl.BlockSpec((1, seq, MIX_ATTN), lambda b, i: (b, 0, 0)),
                  pl.BlockSpec((1, tq, IQ_WIDTH), lambda b, i: (b, i, 0)),
                  pl.BlockSpec((1, seq, LANES), lambda b, i: (b, 0, 0)),
                  pl.BlockSpec((1, S_ROWS, tq), lambda b, i: (b, 0, i)),
                  pl.BlockSpec((1, seq // kb, N_HEADS_ATTN * V_ROWS, kb), lambda b, i: (b, 0, 0, 0)),
                  pl.BlockSpec((kb, kb), lambda b, i: (0, 0))],
        out_specs=pl.BlockSpec((1, tq, V_PAD), lambda b, i: (b, i, 0)),
        out_shape=jax.ShapeDtypeStruct((bsz, seq, V_PAD), BF16),
        scratch_shapes=[pltpu.VMEM((seq // kb, kb, tq), jnp.int32),
                        pltpu.VMEM((4, seq // kb, kb, tq), BF16),
                        pltpu.VMEM((seq // kb, kb, tq), BF16),
                        pltpu.VMEM((kb, tq), F32),
                        pltpu.VMEM((N_HEADS_ATTN, kb, tq), F32),
                        pltpu.VMEM((N_HEADS_ATTN, kb, tq), BF16),
                        pltpu.VMEM((N_HEADS_ATTN, 1, tq), F32),
                        pltpu.VMEM((N_HEADS_ATTN, 1, tq), F32),
                        pltpu.VMEM((N_HEADS_ATTN, LANES, tq), F32)],
        compiler_params=_params(("parallel", "arbitrary")),
        name="dsa",
    )(q, k, iq, ik2, small_t, v_t, tril)


def _head_norm(y):
    mean_mat = jnp.full((HEAD_DIM, HEAD_DIM), 1.0 / HEAD_DIM, BF16)

    def mean_bcast(x):
        hi = x.astype(BF16)
        lo = (x - hi.astype(F32)).astype(BF16)
        return _dot(hi, mean_mat) + _dot(lo, mean_mat)

    yc = y - mean_bcast(y)
    return yc * lax.rsqrt(mean_bcast(yc * yc) + LN_EPS)


def _ret_kernel(r_ref, o_ref, state_scr, *, cr, grp):
    @pl.when(pl.program_id(1) == 0)
    def _():
        state_scr[...] = jnp.zeros_like(state_scr)

    ri = lax.broadcasted_iota(jnp.int32, (cr, cr), 0)
    ci = lax.broadcasted_iota(jnp.int32, (cr, cr), 1)
    diff = (ri - ci).astype(F32)
    pos = lax.broadcasted_iota(jnp.int32, (cr, 1), 0).astype(F32)
    items = [(h, g) for h in range(N_HEADS_RET) for g in range(grp)]
    sl = lambda part, h: slice(part * MIX_RET + h * HEAD_DIM, part * MIX_RET + (h + 1) * HEAD_DIM)
    log_gamma = [jnp.log1p(jnp.full((1, 1), -(2.0 ** (-5.0 - h)), F32)) for h in range(N_HEADS_RET)]
    decay_in = [jnp.where(diff >= 0, jnp.exp(diff * lg), 0.0) * (HEAD_DIM ** -0.5) for lg in log_gamma]
    q = {(h, g): r_ref[g, :, sl(0, h)] for h, g in items}
    k = {(h, g): r_ref[g, :, sl(1, h)] for h, g in items}
    v = {(h, g): r_ref[g, :, sl(2, h)] for h, g in items}
    state = {(h, g): state_scr[g, h] for h, g in items}
    scores = {it: _dot_nt(q[it], k[it]) * decay_in[it[0]] for it in items}
    cross = {it: jnp.exp((pos + 1.0) * log_gamma[it[0]]) * _dot(q[it], state[it].astype(BF16)) for it in items}
    inner = {it: _dot(scores[it].astype(BF16), v[it]) for it in items}
    for it in items:
        h, g = it
        gate = r_ref[g, :, sl(3, h)].astype(F32)
        y = _head_norm(inner[it] + cross[it])
        o_ref[g, :, h * HEAD_DIM:(h + 1) * HEAD_DIM] = (y * (gate * jax.nn.sigmoid(gate))).astype(o_ref.dtype)
    for it in items:
        h, g = it
        k_decay = (HEAD_DIM ** -0.5) * jnp.exp((cr - 1.0 - pos) * log_gamma[h])
        k_dec = (k[it].astype(F32) * k_decay).astype(BF16)
        state_scr[g, h] = state[it] * jnp.exp(cr * log_gamma[h]) + _dot_tn(k_dec, v[it])


def _retention(r_proj, cr, grp):
    bsz, seq, _ = r_proj.shape
    return pl.pallas_call(
        functools.partial(_ret_kernel, cr=cr, grp=grp),
        grid=(bsz // grp, seq // cr),
        in_specs=[pl.BlockSpec((grp, cr, R_WIDTH), lambda b, i: (b, i, 0))],
        out_specs=pl.BlockSpec((grp, cr, MIX_RET), lambda b, i: (b, i, 0)),
        out_shape=jax.ShapeDtypeStruct((bsz, seq, MIX_RET), BF16),
        scratch_shapes=[pltpu.VMEM((grp, N_HEADS_RET, HEAD_DIM, HEAD_DIM), F32)],
        compiler_params=_params(("parallel", "arbitrary")),
        name="retention",
    )(r_proj)


def _mlstm_kernel(m_ref, sm_ref, bias_ref, cw_ref, cb_ref, tril_ref, o_ref,
                  xbuf, a_scr, m_scr, *, cm, grp):
    halo = 8
    gate_shift = S_MF - S_MI

    @pl.when(pl.program_id(1) == 0)
    def _():
        xbuf[:, 0:halo, :] = jnp.zeros((grp, halo, 2 * MIX_MLSTM), F32)
        a_scr[...] = jnp.zeros_like(a_scr)
        m_scr[...] = jnp.zeros_like(m_scr)

    ri = lax.broadcasted_iota(jnp.int32, (cm, cm), 0)
    ci = lax.broadcasted_iota(jnp.int32, (cm, cm), 1)
    causal = ri >= ci
    row = lax.broadcasted_iota(jnp.int32, (cm, LANES), 0)
    lane = lax.broadcasted_iota(jnp.int32, (cm, LANES), 1)
    gate_lanes = (lane >= S_MI) & (lane < S_MI + N_HEADS_MLSTM)
    lane64 = lax.broadcasted_iota(jnp.int32, (cm, HEAD_DIM), 1)
    ones_col = jnp.where(lane64 == 0, 1.0, 0.0).astype(BF16)
    scale = HEAD_DIM ** -0.5

    def head_cols(x, h):
        return x[:, h * HEAD_DIM:(h + 1) * HEAD_DIM]

    items = [(g, h) for g in range(grp) for h in range(N_HEADS_MLSTM)]
    mm, qk, u_row, m_col, inter, e_inv, kw_col, decay = {}, {}, {}, {}, {}, {}, {}, {}
    for g in range(grp):
        mm[g] = m_ref[g]
        xbuf[g, halo:halo + cm, :] = mm[g][:, 0:2 * MIX_MLSTM].astype(F32)
        conv = cb_ref[...]
        for j in range(CONV_WIDTH):
            off = halo - (CONV_WIDTH - 1) + j
            conv = conv + xbuf[g, off:off + cm, :] * cw_ref[j:j + 1, :]
        xbuf[g, 0:halo, :] = xbuf[g, cm:cm + halo, :]
        qk[g] = conv * jax.nn.sigmoid(conv)

        gates = sm_ref[g] + bias_ref[...]
        b_all = jnp.dot(tril_ref[...], jax.nn.log_sigmoid(gates), preferred_element_type=F32,
                        precision=lax.Precision.HIGHEST)
        b_i = jnp.where(gate_lanes, pltpu.roll(b_all, LANES - gate_shift, 1), 0.0)
        u = jnp.where(gate_lanes, gates, 0.0) - b_i
        run = u
        step = 1
        while step < cm:
            run = jnp.maximum(run, jnp.where(row >= step, pltpu.roll(run, step, 0), -jnp.inf))
            step *= 2
        m_prev = m_scr[g]
        m_c = jnp.maximum(m_prev, run)
        m_last = m_c[cm - 1:cm, :]
        u_row[g] = u.T
        m_col[g] = m_c
        inter[g] = jnp.exp(m_prev - m_c)
        e_inv[g] = jnp.exp(-(b_i + m_c))
        kw_col[g] = scale * jnp.exp(u - m_last)
        decay[g] = jnp.exp(m_prev - m_last)
        m_scr[g] = b_i[cm - 1:cm, :] + m_last

    q, k, v_aug, a_mem = {}, {}, {}, {}
    for it in items:
        g, h = it
        q[it] = head_cols(qk[g], h).astype(BF16)
        k[it] = head_cols(qk[g][:, MIX_MLSTM:], h)
        v_aug[it] = jnp.concatenate([head_cols(mm[g][:, 2 * MIX_MLSTM:], h), ones_col], axis=1)
        a_mem[it] = a_scr[g, h]

    def col(x, h):
        return x[:, S_MI + h:S_MI + h + 1]

    w = {}
    for it in items:
        g, h = it
        u_r = u_row[g][S_MI + h:S_MI + h + 1, :]
        w[it] = jnp.exp(jnp.where(causal, u_r - col(m_col[g], h), -jnp.inf))

    s = {it: _dot_nt(q[it], k[it].astype(BF16)) * scale * w[it] for it in items}
    cross = {it: _dot(q[it], a_mem[it].astype(BF16)) for it in items}
    both = {it: _dot(s[it].astype(BF16), v_aug[it]) + col(inter[it[0]], it[1]) * cross[it] for it in items}
    for it in items:
        g, h = it
        den = both[it][:, HEAD_DIM:HEAD_DIM + 1]
        h_tilde = both[it][:, 0:HEAD_DIM] * (1.0 / jnp.maximum(jnp.abs(den), col(e_inv[g], h)))
        og = head_cols(mm[g][:, 3 * MIX_MLSTM:], h).astype(F32)
        o_ref[g, :, h * HEAD_DIM:(h + 1) * HEAD_DIM] = _head_norm(jax.nn.sigmoid(og) * h_tilde).astype(o_ref.dtype)

    for it in items:
        g, h = it
        kw = k[it] * col(kw_col[g], h)
        a_scr[g, h] = col(decay[g], h) * a_mem[it] + _dot_tn(kw.astype(BF16), v_aug[it])


def _mlstm(m_proj, small, gate_bias, conv_w, conv_b, tril, cm, grp):
    bsz, seq, _ = m_proj.shape
    const = lambda b, i: (0, 0)
    return pl.pallas_call(
        functools.partial(_mlstm_kernel, cm=cm, grp=grp),
        grid=(bsz // grp, seq // cm),
        in_specs=[pl.BlockSpec((grp, cm, M_WIDTH), lambda b, i: (b, i, 0)),
                  pl.BlockSpec((grp, cm, S_WIDTH), lambda b, i: (b, i, 0)),
                  pl.BlockSpec((1, S_WIDTH), const),
                  pl.BlockSpec((CONV_WIDTH, 2 * MIX_MLSTM), const),
                  pl.BlockSpec((1, 2 * MIX_MLSTM), const),
                  pl.BlockSpec((cm, cm), const)],
        out_specs=pl.BlockSpec((grp, cm, MIX_MLSTM), lambda b, i: (b, i, 0)),
        out_shape=jax.ShapeDtypeStruct((bsz, seq, MIX_MLSTM), BF16),
        scratch_shapes=[pltpu.VMEM((grp, cm + 8, 2 * MIX_MLSTM), F32),
                        pltpu.VMEM((grp, N_HEADS_MLSTM, HEAD_DIM, LANES), F32),
                        pltpu.VMEM((grp, 1, LANES), F32)],
        compiler_params=_params(("parallel", "arbitrary")),
        name="mlstm",
    )(m_proj, small, gate_bias, conv_w, conv_b, tril)


def _layer_norm(z, g, b):
    mu = jnp.mean(z, axis=-1, keepdims=True)
    var = jnp.mean(jnp.square(z - mu), axis=-1, keepdims=True)
    return (z - mu) * lax.rsqrt(var + LN_EPS) * g + b


def _outproj_kernel(oa_ref, ob_ref, oc_ref, w_ref, x_ref, gm_ref, g_ref, b_ref, o_ref, *, alpha):
    mix = _dot(oa_ref[0], w_ref[0:V_PAD, :])
    mix = mix + _dot(ob_ref[0], w_ref[V_PAD:V_PAD + MIX_RET, :])
    mix = mix + _dot(oc_ref[0], w_ref[V_PAD + MIX_RET:, :])
    z = alpha * x_ref[0] + (1.0 + gm_ref[0]) * mix
    o_ref[0] = _layer_norm(z, g_ref[...], b_ref[...])


def _out_proj(o_a, o_b, o_c, w_out, x, g_m, ln_g, ln_b, tm, alpha):
    bsz, seq, d = x.shape
    row = lambda b, i: (b, i, 0)
    const = lambda b, i: (0, 0)
    return pl.pallas_call(
        functools.partial(_outproj_kernel, alpha=alpha),
        grid=(bsz, seq // tm),
        in_specs=[pl.BlockSpec((1, tm, V_PAD), row),
                  pl.BlockSpec((1, tm, MIX_RET), row),
                  pl.BlockSpec((1, tm, MIX_MLSTM), row),
                  pl.BlockSpec(w_out.shape, const),
                  pl.BlockSpec((1, tm, d), row),
                  pl.BlockSpec((1, 1, d), lambda b, i: (b, 0, 0)),
                  pl.BlockSpec((1, d), const),
                  pl.BlockSpec((1, d), const)],
        out_specs=pl.BlockSpec((1, tm, d), row),
        out_shape=jax.ShapeDtypeStruct((bsz, seq, d), F32),
        compiler_params=_params(("parallel", "parallel")),
        name="out_proj",
    )(o_a, o_b, o_c, w_out, x, g_m, ln_g, ln_b)


def _route(scores, biased):
    col = lambda a, e: a[e:e + 1, :]
    epg = EXPERTS_PER_GROUP
    group_scores = []
    for g in range(N_GROUPS):
        vals = [col(biased, g * epg + j) for j in range(epg)]
        best = None
        for a in range(epg):
            for b in range(a + 1, epg):
                pair = vals[a] + vals[b]
                best = pair if best is None else jnp.maximum(best, pair)
        group_scores.append(best)
    best_g = jnp.zeros_like(group_scores[0], dtype=jnp.int32)
    best_v = group_scores[0]
    for g in range(1, N_GROUPS):
        better = group_scores[g] > best_v
        best_g = jnp.where(better, g, best_g)
        best_v = jnp.where(better, group_scores[g], best_v)
    cand_b = [sum(jnp.where(best_g == g, col(biased, g * epg + j), 0.0) for g in range(N_GROUPS))
              for j in range(epg)]
    cand_s = [sum(jnp.where(best_g == g, col(scores, g * epg + j), 0.0) for g in range(N_GROUPS))
              for j in range(epg)]

    def argmax_first(vals, skip=None):
        idx = None
        val = None
        for j, vj in enumerate(vals):
            if skip is not None:
                vj = jnp.where(skip == j, -jnp.inf, vj)
            if idx is None:
                idx, val = jnp.zeros_like(best_g), vj
            else:
                better = vj > val
                idx = jnp.where(better, j, idx)
                val = jnp.where(better, vj, val)
        return idx

    first = argmax_first(cand_b)
    second = argmax_first(cand_b, skip=first)
    w1 = sum(jnp.where(first == j, cand_s[j], 0.0) for j in range(epg))
    w2 = sum(jnp.where(second == j, cand_s[j], 0.0) for j in range(epg))
    total = w1 + w2
    e1 = best_g * epg + first
    e2 = best_g * epg + second
    expert = lax.broadcasted_iota(jnp.int32, scores.shape, 0)
    gate = jnp.where(expert == e1, w1 / total, 0.0) + jnp.where(expert == e2, w2 / total, 0.0)
    return gate, best_g


def _split_bf16(x):
    hi = x.astype(BF16)
    return hi, (x - hi.astype(F32)).astype(BF16)


def _moe_kernel(x_ref, sc_ref, sh_ref, gf_ref, wr_ref, br_ref, wg_ref, wu_ref, wd_ref, tri_ref, g_ref, b_ref,
                o_ref, hid_scr, *, alpha, cap):
    x = x_ref[0]
    tm = x.shape[0]
    h = x * (1.0 + sc_ref[0]) + sh_ref[0]
    hb = h.astype(BF16)
    n_exp, _, d_ff = wg_ref.shape
    scores_t = jax.nn.sigmoid(_dot(h, wr_ref[...]).T[0:n_exp, :])
    gate_t, best_g = _route(scores_t, scores_t + br_ref[...])
    sub = lax.broadcasted_iota(jnp.int32, (2 * SUBLANES, tm), 0)
    member_t = jnp.where(sub == best_g, 1.0, 0.0)
    before_t = _dot(member_t.astype(BF16), tri_ref[...])
    rank = jnp.sum(member_t * before_t, axis=0, keepdims=True)
    count = jnp.max(jnp.sum(member_t, axis=1, keepdims=True))
    grp_f = best_g.astype(F32)
    sub8 = lax.broadcasted_iota(jnp.int32, (SUBLANES, tm), 0)
    extra = jnp.where(sub8 == 0, grp_f, jnp.where(sub8 == 1, rank, 0.0))
    info = jnp.concatenate([gate_t, extra, jnp.zeros((LANES - n_exp - SUBLANES, tm), F32)], axis=0).T
    gate = info
    epg = EXPERTS_PER_GROUP

    def expert_hidden(rows_b, gate_rows, e, dst):
        gate_pre = _dot(rows_b, wg_ref[e])
        up = _dot(rows_b, wu_ref[e])
        hid = gate_pre * jax.nn.sigmoid(gate_pre) * up * gate_rows[:, e:e + 1]
        hid_scr[0:rows_b.shape[0], dst * d_ff:(dst + 1) * d_ff] = hid.astype(BF16)

    def dense(_):
        for e in range(n_exp):
            expert_hidden(hb, gate, e, e)
        return _dot(hid_scr[...], wd_ref[...])


    def grouped(_):
        grp_c = info[:, n_exp:n_exp + 1]
        rank_c = info[:, n_exp + 1:n_exp + 2]
        slot_c = lax.broadcasted_iota(jnp.int32, (cap, 1), 0).astype(F32)
        slot_r = lax.broadcasted_iota(jnp.int32, (1, cap), 1).astype(F32)
        gate_hi, gate_lo = _split_bf16(gate)
        y = jnp.zeros((tm, x.shape[1]), F32)
        for g in range(N_GROUPS):
            take = jnp.where(jnp.where(grp_f == g, rank, -1.0) == slot_c, 1.0, 0.0).astype(BF16)
            give = jnp.where(jnp.where(grp_c == g, rank_c, -1.0) == slot_r, 1.0, 0.0).astype(BF16)
            rows_b = _dot(take, hb).astype(BF16)
            gate_rows = _dot(take, gate_hi) + _dot(take, gate_lo)
            for j in range(epg):
                expert_hidden(rows_b, gate_rows, g * epg + j, j)
            out = _dot(hid_scr[0:cap, 0:epg * d_ff], wd_ref[g * epg * d_ff:(g + 1) * epg * d_ff, :])
            y = y + _dot(give, out.astype(BF16))
        return y

    y = lax.cond(count <= cap, grouped, dense, 0)
    z = alpha * x + (1.0 + gf_ref[0]) * y
    o_ref[0] = _layer_norm(z, g_ref[...], b_ref[...])


def _moe(x, sc, sh, g_f, w_router, b_router, w_gate, w_up, w_down, tri, ln_g, ln_b, tm, cap, alpha):
    bsz, seq, d = x.shape
    n_exp, _, d_ff = w_gate.shape
    row = lambda b, i: (b, i, 0)
    per_b = lambda b, i: (b, 0, 0)
    const = lambda b, i: (0, 0)
    const3 = lambda b, i: (0, 0, 0)
    resident = pl.Buffered(1)
    return pl.pallas_call(
        functools.partial(_moe_kernel, alpha=alpha, cap=cap),
        grid=(bsz, seq // tm),
        in_specs=[pl.BlockSpec((1, tm, d), row),
                  pl.BlockSpec((1, 1, d), per_b),
                  pl.BlockSpec((1, 1, d), per_b),
                  pl.BlockSpec((1, 1, d), per_b),
                  pl.BlockSpec((d, LANES), const),
                  pl.BlockSpec((n_exp, 1), const),
                  pl.BlockSpec((n_exp, d, d_ff), const3, pipeline_mode=resident),
                  pl.BlockSpec((n_exp, d, d_ff), const3, pipeline_mode=resident),
                  pl.BlockSpec((n_exp * d_ff, d), const, pipeline_mode=resident),
                  pl.BlockSpec((tm, tm), const, pipeline_mode=resident),
                  pl.BlockSpec((1, d), const),
                  pl.BlockSpec((1, d), const)],
        out_specs=pl.BlockSpec((1, tm, d), row),
        out_shape=jax.ShapeDtypeStruct((bsz, seq, d), F32),
        scratch_shapes=[pltpu.VMEM((tm, n_exp * d_ff), BF16)],
        compiler_params=_params(("parallel", "parallel")),
        name="moe",
    )(x, sc, sh, g_f, w_router, b_router, w_gate, w_up, w_down.reshape(n_exp * d_ff, d), tri, ln_g, ln_b)


def _pick(n, pref):
    t = min(pref, n)
    while n % t:
        t //= 2
    return t


def _rope_tables(positions):
    half = HEAD_DIM // 2
    inv_freq = ROPE_THETA ** (-jnp.arange(half, dtype=F32) / half)
    ang = positions.astype(F32)[..., None] * inv_freq
    cos, sin = jnp.cos(ang), jnp.sin(ang)
    reps = LANES // HEAD_DIM
    return (jnp.concatenate([cos, cos] * reps, axis=-1),
            jnp.concatenate([-sin, sin] * reps, axis=-1))


def _prep_w_kernel(w_ref, o_ref):
    o_v = 2 * MIX_ATTN
    o_iq = o_v + MIX_ATTN
    o_ik = o_iq + IQ_WIDTH
    o_iw = o_ik + IDX_DIM
    o_r = o_iw + N_IDX_HEADS
    o_g = o_r + R_WIDTH + M_WIDTH
    n_gate = 2 * N_HEADS_MLSTM

    def put(dst, src, width):
        o_ref[0, :, dst:dst + width] = w_ref[0, :, src:src + width].astype(o_ref.dtype)

    o_ref[0] = jnp.zeros(o_ref.shape[1:], o_ref.dtype)
    put(OFF_Q, 0, 2 * MIX_ATTN)
    put(OFF_IQ, o_iq, IQ_WIDTH)
    put(OFF_R, o_r, R_WIDTH + M_WIDTH)
    put(OFF_IK, o_ik, IDX_DIM)
    put(OFF_IK + IDX_DIM, o_ik, IDX_DIM)
    put(OFF_S + S_IW, o_iw, N_IDX_HEADS)
    put(OFF_S + S_MI, o_g, n_gate)
    for h in range(N_HEADS_ATTN):
        put(OFF_V + h * LANES, o_v + h * HEAD_DIM, HEAD_DIM)


def _prep_w_in(w_in, tr):
    depth, d, n = w_in.shape
    return pl.pallas_call(
        _prep_w_kernel,
        grid=(depth, d // tr),
        in_specs=[pl.BlockSpec((1, tr, n), lambda l, i: (l, i, 0))],
        out_specs=pl.BlockSpec((1, tr, W_TOTAL), lambda l, i: (l, i, 0)),
        out_shape=jax.ShapeDtypeStruct((depth, d, W_TOTAL), F32),
        compiler_params=_params(("parallel", "parallel")),
        name="prep_w_in",
    )(w_in)


def _pad_w_out(w_out):
    d = w_out.shape[1]
    wa = w_out[:MIX_ATTN].reshape(N_HEADS_ATTN, HEAD_DIM, d)
    wa = jnp.concatenate([wa, jnp.zeros_like(wa)], axis=1).reshape(V_PAD, d)
    return jnp.concatenate([wa, w_out[MIX_ATTN:]], axis=0).astype(BF16)


def kernel(x, c, positions, w_ada, b_ada, w_in, i_bias, f_bias, conv_w, conv_b, w_out, ln_mix_g, ln_mix_b,
           w_router, b_router, w_gate, w_up, w_down, ln_ffn_g, ln_ffn_b):
    bsz, seq, d = x.shape
    depth = w_ada.shape[0]
    alpha = (2.0 * depth) ** 0.25

    tm = _pick(seq, 512)
    tq = _pick(seq, 256)
    kb = _pick(seq, 256)
    assert seq // 16 <= 256, "packed bf16 partial counts in the DSA threshold search must stay exact"
    cr = _pick(seq, 256)
    cm = _pick(seq, 256)
    tmoe = _pick(seq, 512)
    moe_cap = min(tmoe, (3 * tmoe // (2 * N_GROUPS) + 15) // 16 * 16)

    cos_t, sin_t = _rope_tables(positions)
    c_pad = jnp.zeros((8, d), F32).at[:bsz].set(c)
    mod = _ada_mod(c_pad, w_ada, b_ada, _pick(6 * d, 1536))
    w_in_p = _prep_w_in(w_in, _pick(d, 256)).astype(BF16)
    grp = 2 if bsz % 2 == 0 else 1

    tril_kb = (jnp.arange(kb)[:, None] >= jnp.arange(kb)[None, :]).astype(BF16)
    tril = (jnp.arange(cm)[:, None] >= jnp.arange(cm)[None, :]).astype(F32)
    w_router_p = jnp.zeros((d, LANES), F32).at[:, :N_EXPERTS].set(w_router)
    b_router_p = b_router.reshape(N_EXPERTS, 1)
    tri_moe = (jnp.arange(tmoe)[:, None] < jnp.arange(tmoe)[None, :]).astype(BF16)

    for l in range(depth):
        parts = [mod[l, :bsz, j * d:(j + 1) * d].reshape(bsz, 1, d) for j in range(6)]
        sh_m, sc_m, g_m, sh_f, sc_f, g_f = parts
        q, k, iq, ik2, r_proj, m_proj, small, small_t, v_t = _in_proj(
            x, sc_m, sh_m, w_in_p[l], cos_t, sin_t, tm, kb)
        o_a = _dsa(q, k, iq, ik2, small_t, v_t, tril_kb, tq, kb)
        o_b = _retention(r_proj, cr, grp)
        gate_bias = (jnp.zeros((1, S_WIDTH), F32).at[0, S_MI:S_MI + N_HEADS_MLSTM].set(i_bias[l])
                     .at[0, S_MF:S_MF + N_HEADS_MLSTM].set(f_bias[l]))
        o_c = _mlstm(m_proj, small, gate_bias, conv_w[l], conv_b[l].reshape(1, -1), tril, cm, grp)
        x = _out_proj(o_a, o_b, o_c, _pad_w_out(w_out[l]), x, g_m,
                      ln_mix_g[l].reshape(1, d), ln_mix_b[l].reshape(1, d), tm, alpha)
        x = _moe(x, sc_f, sh_f, g_f, w_router_p, b_router_p,
                 w_gate[l].astype(BF16), w_up[l].astype(BF16), w_down[l].astype(BF16), tri_moe,
                 ln_ffn_g[l].reshape(1, d), ln_ffn_b[l].reshape(1, d), tmoe, moe_cap, alpha)
    return x
```

```python
import functools

import numpy as np
import jax
import jax.numpy as jnp
from jax import lax
from jax.experimental import pallas as pl
from jax.experimental.pallas import tpu as pltpu

F32 = jnp.float32
BF16 = jnp.bfloat16

HEAD_DIM = 64
CHUNK = 64
N_HEADS_ATTN = 8
N_IDX_HEADS = 4
IDX_DIM = 64
TOPK_MAX = 256
N_HEADS_RET = 4
N_HEADS_MLSTM = 4
CONV_WIDTH = 4
ROPE_THETA = 10000.0
N_EXPERTS = 16
N_GROUPS = 4
EXPERTS_PER_GROUP = N_EXPERTS // N_GROUPS
D_FF_EXPERT = 256
LN_EPS = 1e-5

MIX_ATTN = N_HEADS_ATTN * HEAD_DIM
MIX_RET = N_HEADS_RET * HEAD_DIM
MIX_MLSTM = N_HEADS_MLSTM * HEAD_DIM

LANES = 128
SUBLANES = 8
VMEM_LIMIT = 56 * 1024 * 1024

IQ_WIDTH = N_IDX_HEADS * IDX_DIM
R_WIDTH = 4 * MIX_RET
M_WIDTH = 4 * MIX_MLSTM
S_WIDTH = LANES
S_IW = 0
S_MI = S_IW + N_IDX_HEADS
S_MF = S_MI + N_HEADS_MLSTM
S_ROWS = 16
V_ROWS = HEAD_DIM + 16
OFF_Q = 0
OFF_K = OFF_Q + MIX_ATTN
OFF_IQ = OFF_K + MIX_ATTN
OFF_R = OFF_IQ + IQ_WIDTH
OFF_M = OFF_R + R_WIDTH
OFF_IK = OFF_M + M_WIDTH
OFF_S = OFF_IK + LANES
OFF_V = OFF_S + S_WIDTH
W_TOTAL = OFF_V + MIX_ATTN

INT_MIN = -2 ** 31
NEG_BIG = -1e30
LOG2_E = 1.4426950408889634


def _dot(a, b):
    return jnp.dot(a, b, preferred_element_type=F32)


def _dot_nt(a, b):
    return lax.dot_general(a, b, (((1,), (1,)), ((), ())), preferred_element_type=F32)


def _dot_tn(a, b):
    return lax.dot_general(a, b, (((0,), (0,)), ((), ())), preferred_element_type=F32)


def _params(sem):
    return pltpu.CompilerParams(dimension_semantics=sem, vmem_limit_bytes=VMEM_LIMIT)


def _ada_kernel(c_ref, w_ref, b_ref, o_ref):
    c = c_ref[...]
    c_act = c * jax.nn.sigmoid(c)
    o_ref[0] = _dot(c_act, w_ref[0]) + b_ref[0]


def _ada_mod(c_pad, w_ada, b_ada, tn):
    depth, d, n = w_ada.shape
    rows = c_pad.shape[0]
    return pl.pallas_call(
        _ada_kernel,
        grid=(depth, n // tn),
        in_specs=[pl.BlockSpec((rows, d), lambda l, j: (0, 0)),
                  pl.BlockSpec((1, d, tn), lambda l, j: (l, 0, j)),
                  pl.BlockSpec((1, 1, tn), lambda l, j: (l, 0, j))],
        out_specs=pl.BlockSpec((1, rows, tn), lambda l, j: (l, 0, j)),
        out_shape=jax.ShapeDtypeStruct((depth, rows, n), F32),
        compiler_params=_params(("parallel", "parallel")),
        name="ada_mod",
    )(c_pad, w_ada, b_ada.reshape(depth, 1, n))


def _rope(y, cos, sin):
    w = y.shape[1]
    reps = w // LANES
    cosw = jnp.concatenate([cos] * reps, axis=1) if reps > 1 else cos
    sinw = jnp.concatenate([sin] * reps, axis=1) if reps > 1 else sin
    lane = lax.broadcasted_iota(jnp.int32, y.shape, 1)
    first = (lane % HEAD_DIM) < (HEAD_DIM // 2)
    partner = jnp.where(first, pltpu.roll(y, w - HEAD_DIM // 2, 1), pltpu.roll(y, HEAD_DIM // 2, 1))
    return y * cosw + partner * sinw


def _inproj_kernel(x_ref, sc_ref, sh_ref, w_ref, cos_ref, sin_ref,
                   q_ref, k_ref, iq_ref, ik_ref, r_ref, m_ref, s_ref, st_ref, vt_ref, *, kb):
    h = (x_ref[0] * (1.0 + sc_ref[0]) + sh_ref[0]).astype(BF16)
    cos = cos_ref[0]
    sin = sin_ref[0]

    def proj(start, width):
        return _dot(h, w_ref[:, start:start + width])

    q_ref[0] = (_rope(proj(OFF_Q, MIX_ATTN), cos, sin) * (HEAD_DIM ** -0.5 * LOG2_E)).astype(BF16)
    k_ref[0] = _rope(proj(OFF_K, MIX_ATTN), cos, sin).astype(BF16)
    iq_ref[0] = _rope(proj(OFF_IQ, IQ_WIDTH), cos, sin).astype(BF16)
    ik_ref[0] = _rope(proj(OFF_IK, LANES), cos, sin).astype(BF16)
    r_ref[0, :, 0:2 * MIX_RET] = _rope(proj(OFF_R, 2 * MIX_RET), cos, sin).astype(BF16)
    r_ref[0, :, 2 * MIX_RET:R_WIDTH] = proj(OFF_R + 2 * MIX_RET, 2 * MIX_RET).astype(BF16)
    m_ref[0, :, 0:2 * MIX_MLSTM] = proj(OFF_M, 2 * MIX_MLSTM).astype(BF16)
    m_ref[0, :, 2 * MIX_MLSTM:M_WIDTH] = proj(OFF_M + 2 * MIX_MLSTM, 2 * MIX_MLSTM).astype(BF16)
    y = proj(OFF_S, S_WIDTH)
    s_ref[0] = y
    st_ref[0] = y.T[0:S_ROWS, :]
    tm = h.shape[0]
    yvt = proj(OFF_V, MIX_ATTN).T.astype(BF16)
    pad_rows = lax.broadcasted_iota(jnp.int32, (V_ROWS - HEAD_DIM, kb), 0)
    ones_rows = jnp.where(pad_rows == 0, 1.0, 0.0).astype(BF16)
    for j in range(tm // kb):
        for hh in range(N_HEADS_ATTN):
            vt_ref[0, j, hh * V_ROWS:hh * V_ROWS + HEAD_DIM, :] = yvt[hh * HEAD_DIM:(hh + 1) * HEAD_DIM,
                                                                      j * kb:(j + 1) * kb]
            vt_ref[0, j, hh * V_ROWS + HEAD_DIM:(hh + 1) * V_ROWS, :] = ones_rows


def _in_proj(x, sc, sh, w, layer, cos_t, sin_t, tm, kb):
    bsz, seq, d = x.shape
    row = lambda b, i: (b, i, 0)
    per_b = lambda b, i: (b, 0, 0)
    widths = (MIX_ATTN, MIX_ATTN, IQ_WIDTH, LANES, R_WIDTH, M_WIDTH)
    return pl.pallas_call(
        functools.partial(_inproj_kernel, kb=kb),
        grid=(bsz, seq // tm),
        in_specs=[pl.BlockSpec((1, tm, d), row),
                  pl.BlockSpec((1, 1, d), per_b),
                  pl.BlockSpec((1, 1, d), per_b),
                  pl.BlockSpec((None, d, W_TOTAL), lambda b, i: (layer, 0, 0)),
                  pl.BlockSpec((1, tm, LANES), row),
                  pl.BlockSpec((1, tm, LANES), row)],
        out_specs=[pl.BlockSpec((1, tm, wd), row) for wd in widths]
                  + [pl.BlockSpec((1, tm, S_WIDTH), row),
                     pl.BlockSpec((1, S_ROWS, tm), lambda b, i: (b, 0, i)),
                     pl.BlockSpec((1, tm // kb, N_HEADS_ATTN * V_ROWS, kb), lambda b, i: (b, i, 0, 0))],
        out_shape=[jax.ShapeDtypeStruct((bsz, seq, wd), BF16) for wd in widths]
                  + [jax.ShapeDtypeStruct((bsz, seq, S_WIDTH), F32),
                     jax.ShapeDtypeStruct((bsz, S_ROWS, seq), F32),
                     jax.ShapeDtypeStruct((bsz, seq // kb, N_HEADS_ATTN * V_ROWS, kb), BF16)],
        compiler_params=_params(("parallel", "parallel")),
        name="in_proj",
    )(x, sc, sh, w, cos_t, sin_t)


def _dsa_kernel(q_ref, k_ref, iq_ref, ik_ref, st_ref, vt_ref, tril_ref, o_ref,
                key_scr, byte_scr, cand_scr, bias_scr, s_scr, p_scr, m_scr, alpha_scr, acc_scr,
                *, tq, kb, topk):
    q0 = pl.program_id(1) * tq
    n_blocks = (q0 + tq + kb - 1) // kb
    qpos = q0 + lax.broadcasted_iota(jnp.int32, (1, tq), 1)
    q_limit = (qpos // CHUNK + 1) * CHUNK
    krow = lax.broadcasted_iota(jnp.int32, (kb, 1), 0)

    def head_of_pair(x, h):
        pair = x[:, (h // 2) * LANES:(h // 2 + 1) * LANES]
        lane = lax.broadcasted_iota(jnp.int32, pair.shape, 1)
        keep = (lane < HEAD_DIM) if h % 2 == 0 else (lane >= HEAD_DIM)
        return jnp.where(keep, pair, jnp.zeros_like(pair))

    iw = st_ref[0][S_IW:S_IW + N_IDX_HEADS, :] * (N_IDX_HEADS ** -0.5 * IDX_DIM ** -0.5)
    iq = iq_ref[0]
    iq_heads = [head_of_pair(iq, h) for h in range(N_IDX_HEADS)]

    def score_block(c):
        k0 = pl.multiple_of(c * kb, kb)
        ik2 = ik_ref[0, pl.ds(k0, kb), :]
        score = jnp.zeros((kb, tq), F32)
        for h in range(N_IDX_HEADS):
            score = score + jnp.maximum(_dot_nt(ik2, iq_heads[h]), 0.0) * iw[h:h + 1, :]
        bits = pltpu.bitcast(score, jnp.int32)
        key = jnp.where(bits >= 0, bits, bits ^ jnp.int32(0x7FFFFFFF))
        key = jnp.where(k0 + krow < q_limit, key, jnp.int32(INT_MIN))
        key_scr[c] = key
        byte_scr[0, c] = ((key >> 24) + 128).astype(F32).astype(BF16)
        for lvl in range(1, 4):
            byte_scr[lvl, c] = ((key >> (24 - 8 * lvl)) & 255).astype(F32).astype(BF16)

    def score_pair(c2, carry):
        score_block(2 * c2)
        score_block(2 * c2 + 1)
        return carry

    lax.fori_loop(0, n_blocks // 2, score_pair, 0)

    @pl.when(n_blocks % 2 == 1)
    def _():
        score_block(n_blocks - 1)

    pack = 16
    one = jnp.ones((kb, tq), BF16)
    zero = jnp.zeros((kb, tq), BF16)

    def count_ge(lvl, cand, narrow_to=None):
        cand_b = cand.astype(BF16)

        def hits(c):
            plane = byte_scr[0, c] if lvl == 0 else cand_scr[c]
            hit = jnp.where(plane >= cand_b, one, zero)
            if narrow_to is not None:
                cand_scr[c] = jnp.where(plane == narrow_to, byte_scr[lvl + 1, c], -one)
            parts = [hit[i * pack:(i + 1) * pack, :] for i in range(kb // pack)]
            while len(parts) > 1:
                parts = [parts[i] + parts[i + 1] for i in range(0, len(parts), 2)]
            return parts[0]

        acc = lax.fori_loop(0, n_blocks // 2, lambda c2, a: a + hits(2 * c2) + hits(2 * c2 + 1),
                            jnp.zeros((pack, tq), BF16))
        acc = lax.cond(n_blocks % 2 == 1, lambda a: a + hits(n_blocks - 1), lambda a: a, acc)
        return jnp.sum(acc.astype(F32), axis=0, keepdims=True)

    above = jnp.zeros((1, tq), F32)
    t = jnp.zeros((1, tq), jnp.int32)
    for lvl in range(4):
        def bit_body(i, v, lvl=lvl, above=above):
            cand = v + lax.shift_left(jnp.int32(1), 7 - i).astype(F32)
            return jnp.where(above + count_ge(lvl, cand) >= topk, cand, v)

        v = lax.fori_loop(0, 8, bit_body, jnp.zeros((1, tq), F32))
        above = above + count_ge(lvl, v + 1.0, narrow_to=v.astype(BF16) if lvl < 3 else None)
        t = t | lax.shift_left(v.astype(jnp.int32), 24 - 8 * lvl)
    thr = jnp.maximum(t ^ jnp.int32(INT_MIN), jnp.int32(INT_MIN + 1))
    need = topk - above

    q = q_ref[0]
    q_heads = [head_of_pair(q, h) for h in range(N_HEADS_ATTN)]
    m_scr[...] = jnp.full(m_scr.shape, NEG_BIG, F32)
    alpha_scr[...] = jnp.ones(alpha_scr.shape, F32)
    acc_scr[...] = jnp.zeros(acc_scr.shape, F32)
    p_scr[...] = jnp.zeros(p_scr.shape, BF16)

    def stage_mask(c, ties_before):
        key = key_scr[c]
        tie = key == thr
        rank = _dot(tril_ref[...], jnp.where(tie, 1.0, 0.0).astype(BF16)) + ties_before
        sel = (key > thr) | (tie & (rank <= need))
        bias_scr[...] = jnp.where(sel, 0.0, NEG_BIG)
        return rank[kb - 1:kb, :]

    def stage_logits(c, h):
        k0 = pl.multiple_of(c * kb, kb)
        kp = k_ref[0, pl.ds(k0, kb), (h // 2) * LANES:(h // 2 + 1) * LANES]
        s_scr[h] = _dot_nt(kp, q_heads[h])

    def stage_softmax(h):
        for half in range(tq // LANES):
            ln = slice(half * LANES, (half + 1) * LANES)
            s = s_scr[h, :, ln] + bias_scr[:, ln]
            m_old = m_scr[h, :, ln]
            m_new = jnp.maximum(m_old, jnp.max(s, axis=0, keepdims=True))
            p_scr[h, :, ln] = jnp.exp2(s - m_new).astype(BF16)
            alpha_scr[h, :, ln] = jnp.exp2(m_old - m_new)
            m_scr[h, :, ln] = m_new

    def stage_values(c, h):
        vt = vt_ref[0, c, h * V_ROWS:(h + 1) * V_ROWS, :]
        acc_scr[h, 0:V_ROWS, :] = alpha_scr[h] * acc_scr[h, 0:V_ROWS, :] + _dot(vt, p_scr[h])

    ties0 = stage_mask(0, jnp.zeros((1, tq), F32))
    for h in range(N_HEADS_ATTN):
        stage_logits(0, h)

    def attn_body(j, ties_before):
        c_old = jnp.maximum(j - 2, 0)
        for h in range(N_HEADS_ATTN):
            stage_values(c_old, h)
            stage_softmax(h)
            stage_logits(j, h)
        return stage_mask(j, ties_before)

    lax.fori_loop(1, n_blocks, attn_body, ties0)
    for h in range(N_HEADS_ATTN):
        stage_values(jnp.maximum(n_blocks - 2, 0), h)
        stage_softmax(h)
    for h in range(N_HEADS_ATTN):
        stage_values(n_blocks - 1, h)
    for h in range(N_HEADS_ATTN):
        acc = acc_scr[h]
        out = acc * (1.0 / acc[HEAD_DIM:HEAD_DIM + 1, :])
        o_ref[0, :, h * HEAD_DIM:(h + 1) * HEAD_DIM] = out.T[:, 0:HEAD_DIM].astype(o_ref.dtype)


def _dsa(q, k, iq, ik2, small_t, v_t, tril, tq, kb):
    bsz, seq, _ = q.shape
    topk = min(TOPK_MAX, seq // 4)
    kern = functools.partial(_dsa_kernel, tq=tq, kb=kb, topk=topk)
    return pl.pallas_call(
        kern,
        grid=(bsz, seq // tq),
        in_specs=[pl.BlockSpec((1, tq, MIX_ATTN), lambda b, i: (b, i, 0)),
                  pl.BlockSpec((1, seq, MIX_ATTN), lambda b, i: (b, 0, 0)),
                  pl.BlockSpec((1, tq, IQ_WIDTH), lambda b, i: (b, i, 0)),
                  pl.BlockSpec((1, seq, LANES), lambda b, i: (b, 0, 0)),
                  pl.BlockSpec((1, S_ROWS, tq), lambda b, i: (b, 0, i)),
                  pl.BlockSpec((1, seq // kb, N_HEADS_ATTN * V_ROWS, kb), lambda b, i: (b, 0, 0, 0)),
                  pl.BlockSpec((kb, kb), lambda b, i: (0, 0))],
        out_specs=pl.BlockSpec((1, tq, MIX_ATTN), lambda b, i: (b, i, 0)),
        out_shape=jax.ShapeDtypeStruct((bsz, seq, MIX_ATTN), BF16),
        scratch_shapes=[pltpu.VMEM((seq // kb, kb, tq), jnp.int32),
                        pltpu.VMEM((4, seq // kb, kb, tq), BF16),
                        pltpu.VMEM((seq // kb, kb, tq), BF16),
                        pltpu.VMEM((kb, tq), F32),
                        pltpu.VMEM((N_HEADS_ATTN, kb, tq), F32),
                        pltpu.VMEM((N_HEADS_ATTN, kb, tq), BF16),
                        pltpu.VMEM((N_HEADS_ATTN, 1, tq), F32),
                        pltpu.VMEM((N_HEADS_ATTN, 1, tq), F32),
                        pltpu.VMEM((N_HEADS_ATTN, LANES, tq), F32)],
        compiler_params=_params(("parallel", "arbitrary")),
        name="dsa",
    )(q, k, iq, ik2, small_t, v_t, tril)


def _head_norm(y):
    mean_mat = jnp.full((HEAD_DIM, HEAD_DIM), 1.0 / HEAD_DIM, BF16)

    def mean_bcast(x):
        hi = x.astype(BF16)
        lo = (x - hi.astype(F32)).astype(BF16)
        return _dot(hi, mean_mat) + _dot(lo, mean_mat)

    yc = y - mean_bcast(y)
    return yc * lax.rsqrt(mean_bcast(yc * yc) + LN_EPS)


def _ret_kernel(r_ref, o_ref, state_scr, *, cr, grp):
    @pl.when(pl.program_id(1) == 0)
    def _():
        state_scr[...] = jnp.zeros_like(state_scr)

    ri = lax.broadcasted_iota(jnp.int32, (cr, cr), 0)
    ci = lax.broadcasted_iota(jnp.int32, (cr, cr), 1)
    diff = (ri - ci).astype(F32)
    pos = lax.broadcasted_iota(jnp.int32, (cr, 1), 0).astype(F32)
    items = [(h, g) for h in range(N_HEADS_RET) for g in range(grp)]
    sl = lambda part, h: slice(part * MIX_RET + h * HEAD_DIM, part * MIX_RET + (h + 1) * HEAD_DIM)
    log_gamma = [jnp.log1p(jnp.full((1, 1), -(2.0 ** (-5.0 - h)), F32)) for h in range(N_HEADS_RET)]
    decay_in = [jnp.where(diff >= 0, jnp.exp(diff * lg), 0.0) * (HEAD_DIM ** -0.5) for lg in log_gamma]
    q = {(h, g): r_ref[g, :, sl(0, h)] for h, g in items}
    k = {(h, g): r_ref[g, :, sl(1, h)] for h, g in items}
    v = {(h, g): r_ref[g, :, sl(2, h)] for h, g in items}
    state = {(h, g): state_scr[g, h] for h, g in items}
    scores = {it: _dot_nt(q[it], k[it]) * decay_in[it[0]] for it in items}
    cross = {it: jnp.exp((pos + 1.0) * log_gamma[it[0]]) * _dot(q[it], state[it].astype(BF16)) for it in items}
    inner = {it: _dot(scores[it].astype(BF16), v[it]) for it in items}
    for it in items:
        h, g = it
        gate = r_ref[g, :, sl(3, h)].astype(F32)
        y = _head_norm(inner[it] + cross[it])
        o_ref[g, :, h * HEAD_DIM:(h + 1) * HEAD_DIM] = (y * (gate * jax.nn.sigmoid(gate))).astype(o_ref.dtype)
    for it in items:
        h, g = it
        k_decay = (HEAD_DIM ** -0.5) * jnp.exp((cr - 1.0 - pos) * log_gamma[h])
        k_dec = (k[it].astype(F32) * k_decay).astype(BF16)
        state_scr[g, h] = state[it] * jnp.exp(cr * log_gamma[h]) + _dot_tn(k_dec, v[it])


def _retention(r_proj, cr, grp):
    bsz, seq, _ = r_proj.shape
    return pl.pallas_call(
        functools.partial(_ret_kernel, cr=cr, grp=grp),
        grid=(bsz // grp, seq // cr),
        in_specs=[pl.BlockSpec((grp, cr, R_WIDTH), lambda b, i: (b, i, 0))],
        out_specs=pl.BlockSpec((grp, cr, MIX_RET), lambda b, i: (b, i, 0)),
        out_shape=jax.ShapeDtypeStruct((bsz, seq, MIX_RET), BF16),
        scratch_shapes=[pltpu.VMEM((grp, N_HEADS_RET, HEAD_DIM, HEAD_DIM), F32)],
        compiler_params=_params(("parallel", "arbitrary")),
        name="retention",
    )(r_proj)


def _mlstm_kernel(m_ref, sm_ref, bias_ref, cw_ref, cb_ref, tril_ref, o_ref,
                  xbuf, a_scr, m_scr, *, cm, grp):
    halo = 8
    gate_shift = S_MF - S_MI

    @pl.when(pl.program_id(1) == 0)
    def _():
        xbuf[:, 0:halo, :] = jnp.zeros((grp, halo, 2 * MIX_MLSTM), F32)
        a_scr[...] = jnp.zeros_like(a_scr)
        m_scr[...] = jnp.zeros_like(m_scr)

    ri = lax.broadcasted_iota(jnp.int32, (cm, cm), 0)
    ci = lax.broadcasted_iota(jnp.int32, (cm, cm), 1)
    causal = ri >= ci
    row = lax.broadcasted_iota(jnp.int32, (cm, LANES), 0)
    lane = lax.broadcasted_iota(jnp.int32, (cm, LANES), 1)
    gate_lanes = (lane >= S_MI) & (lane < S_MI + N_HEADS_MLSTM)
    lane64 = lax.broadcasted_iota(jnp.int32, (cm, HEAD_DIM), 1)
    ones_col = jnp.where(lane64 == 0, 1.0, 0.0).astype(BF16)
    scale = HEAD_DIM ** -0.5

    def head_cols(x, h):
        return x[:, h * HEAD_DIM:(h + 1) * HEAD_DIM]

    items = [(g, h) for g in range(grp) for h in range(N_HEADS_MLSTM)]
    mm, qk, u_row, m_col, inter, e_inv, kw_col, decay = {}, {}, {}, {}, {}, {}, {}, {}
    for g in range(grp):
        mm[g] = m_ref[g]
        xbuf[g, halo:halo + cm, :] = mm[g][:, 0:2 * MIX_MLSTM].astype(F32)
        conv = cb_ref[...]
        for j in range(CONV_WIDTH):
            off = halo - (CONV_WIDTH - 1) + j
            conv = conv + xbuf[g, off:off + cm, :] * cw_ref[j:j + 1, :]
        xbuf[g, 0:halo, :] = xbuf[g, cm:cm + halo, :]
        qk[g] = conv * jax.nn.sigmoid(conv)

        gates = sm_ref[g] + bias_ref[...]
        b_all = jnp.dot(tril_ref[...], jax.nn.log_sigmoid(gates), preferred_element_type=F32,
                        precision=lax.Precision.HIGHEST)
        b_i = jnp.where(gate_lanes, pltpu.roll(b_all, LANES - gate_shift, 1), 0.0)
        u = jnp.where(gate_lanes, gates, 0.0) - b_i
        run = u
        step = 1
        while step < cm:
            run = jnp.maximum(run, jnp.where(row >= step, pltpu.roll(run, step, 0), -jnp.inf))
            step *= 2
        m_prev = m_scr[g]
        m_c = jnp.maximum(m_prev, run)
        m_last = m_c[cm - 1:cm, :]
        u_row[g] = u.T
        m_col[g] = m_c
        inter[g] = jnp.exp(m_prev - m_c)
        e_inv[g] = jnp.exp(-(b_i + m_c))
        kw_col[g] = scale * jnp.exp(u - m_last)
        decay[g] = jnp.exp(m_prev - m_last)
        m_scr[g] = b_i[cm - 1:cm, :] + m_last

    q, k, v_aug, a_mem = {}, {}, {}, {}
    for it in items:
        g, h = it
        q[it] = head_cols(qk[g], h).astype(BF16)
        k[it] = head_cols(qk[g][:, MIX_MLSTM:], h)
        v_aug[it] = jnp.concatenate([head_cols(mm[g][:, 2 * MIX_MLSTM:], h), ones_col], axis=1)
        a_mem[it] = a_scr[g, h]

    def col(x, h):
        return x[:, S_MI + h:S_MI + h + 1]

    w = {}
    for it in items:
        g, h = it
        u_r = u_row[g][S_MI + h:S_MI + h + 1, :]
        w[it] = jnp.exp(jnp.where(causal, u_r - col(m_col[g], h), -jnp.inf))

    s = {it: _dot_nt(q[it], k[it].astype(BF16)) * scale * w[it] for it in items}
    cross = {it: _dot(q[it], a_mem[it].astype(BF16)) for it in items}
    both = {it: _dot(s[it].astype(BF16), v_aug[it]) + col(inter[it[0]], it[1]) * cross[it] for it in items}
    for it in items:
        g, h = it
        den = both[it][:, HEAD_DIM:HEAD_DIM + 1]
        h_tilde = both[it][:, 0:HEAD_DIM] * (1.0 / jnp.maximum(jnp.abs(den), col(e_inv[g], h)))
        og = head_cols(mm[g][:, 3 * MIX_MLSTM:], h).astype(F32)
        o_ref[g, :, h * HEAD_DIM:(h + 1) * HEAD_DIM] = _head_norm(jax.nn.sigmoid(og) * h_tilde).astype(o_ref.dtype)

    for it in items:
        g, h = it
        kw = k[it] * col(kw_col[g], h)
        a_scr[g, h] = col(decay[g], h) * a_mem[it] + _dot_tn(kw.astype(BF16), v_aug[it])


def _mlstm(m_proj, small, gate_bias, conv_w, conv_b, tril, cm, grp):
    bsz, seq, _ = m_proj.shape
    const = lambda b, i: (0, 0)
    return pl.pallas_call(
        functools.partial(_mlstm_kernel, cm=cm, grp=grp),
        grid=(bsz // grp, seq // cm),
        in_specs=[pl.BlockSpec((grp, cm, M_WIDTH), lambda b, i: (b, i, 0)),
                  pl.BlockSpec((grp, cm, S_WIDTH), lambda b, i: (b, i, 0)),
                  pl.BlockSpec((1, S_WIDTH), const),
                  pl.BlockSpec((CONV_WIDTH, 2 * MIX_MLSTM), const),
                  pl.BlockSpec((1, 2 * MIX_MLSTM), const),
                  pl.BlockSpec((cm, cm), const)],
        out_specs=pl.BlockSpec((grp, cm, MIX_MLSTM), lambda b, i: (b, i, 0)),
        out_shape=jax.ShapeDtypeStruct((bsz, seq, MIX_MLSTM), BF16),
        scratch_shapes=[pltpu.VMEM((grp, cm + 8, 2 * MIX_MLSTM), F32),
                        pltpu.VMEM((grp, N_HEADS_MLSTM, HEAD_DIM, LANES), F32),
                        pltpu.VMEM((grp, 1, LANES), F32)],
        compiler_params=_params(("parallel", "arbitrary")),
        name="mlstm",
    )(m_proj, small, gate_bias, conv_w, conv_b, tril)


def _layer_norm(z, g, b):
    mu = jnp.mean(z, axis=-1, keepdims=True)
    var = jnp.mean(jnp.square(z - mu), axis=-1, keepdims=True)
    return (z - mu) * lax.rsqrt(var + LN_EPS) * g + b


def _outproj_kernel(oa_ref, ob_ref, oc_ref, w_ref, x_ref, gm_ref, g_ref, b_ref, o_ref, *, alpha):
    mix = _dot(oa_ref[0], w_ref[0:MIX_ATTN, :])
    mix = mix + _dot(ob_ref[0], w_ref[MIX_ATTN:MIX_ATTN + MIX_RET, :])
    mix = mix + _dot(oc_ref[0], w_ref[MIX_ATTN + MIX_RET:, :])
    z = alpha * x_ref[0] + (1.0 + gm_ref[0]) * mix
    o_ref[0] = _layer_norm(z, g_ref[...], b_ref[...])


def _out_proj(o_a, o_b, o_c, w_out, layer, x, g_m, ln_g, ln_b, tm, alpha):
    bsz, seq, d = x.shape
    row = lambda b, i: (b, i, 0)
    const = lambda b, i: (0, 0)
    return pl.pallas_call(
        functools.partial(_outproj_kernel, alpha=alpha),
        grid=(bsz, seq // tm),
        in_specs=[pl.BlockSpec((1, tm, MIX_ATTN), row),
                  pl.BlockSpec((1, tm, MIX_RET), row),
                  pl.BlockSpec((1, tm, MIX_MLSTM), row),
                  pl.BlockSpec((None,) + w_out.shape[1:], lambda b, i: (layer, 0, 0)),
                  pl.BlockSpec((1, tm, d), row),
                  pl.BlockSpec((1, 1, d), lambda b, i: (b, 0, 0)),
                  pl.BlockSpec((1, d), const),
                  pl.BlockSpec((1, d), const)],
        out_specs=pl.BlockSpec((1, tm, d), row),
        out_shape=jax.ShapeDtypeStruct((bsz, seq, d), F32),
        compiler_params=_params(("parallel", "parallel")),
        name="out_proj",
    )(o_a, o_b, o_c, w_out, x, g_m, ln_g, ln_b)


def _route(scores, biased):
    col = lambda a, e: a[e:e + 1, :]
    epg = EXPERTS_PER_GROUP
    group_scores = []
    for g in range(N_GROUPS):
        vals = [col(biased, g * epg + j) for j in range(epg)]
        best = None
        for a in range(epg):
            for b in range(a + 1, epg):
                pair = vals[a] + vals[b]
                best = pair if best is None else jnp.maximum(best, pair)
        group_scores.append(best)
    best_g = jnp.zeros_like(group_scores[0], dtype=jnp.int32)
    best_v = group_scores[0]
    for g in range(1, N_GROUPS):
        better = group_scores[g] > best_v
        best_g = jnp.where(better, g, best_g)
        best_v = jnp.where(better, group_scores[g], best_v)
    cand_b = [sum(jnp.where(best_g == g, col(biased, g * epg + j), 0.0) for g in range(N_GROUPS))
              for j in range(epg)]
    cand_s = [sum(jnp.where(best_g == g, col(scores, g * epg + j), 0.0) for g in range(N_GROUPS))
              for j in range(epg)]

    def argmax_first(vals, skip=None):
        idx = None
        val = None
        for j, vj in enumerate(vals):
            if skip is not None:
                vj = jnp.where(skip == j, -jnp.inf, vj)
            if idx is None:
                idx, val = jnp.zeros_like(best_g), vj
            else:
                better = vj > val
                idx = jnp.where(better, j, idx)
                val = jnp.where(better, vj, val)
        return idx

    first = argmax_first(cand_b)
    second = argmax_first(cand_b, skip=first)
    w1 = sum(jnp.where(first == j, cand_s[j], 0.0) for j in range(epg))
    w2 = sum(jnp.where(second == j, cand_s[j], 0.0) for j in range(epg))
    total = w1 + w2
    e1 = best_g * epg + first
    e2 = best_g * epg + second
    expert = lax.broadcasted_iota(jnp.int32, scores.shape, 0)
    gate = jnp.where(expert == e1, w1 / total, 0.0) + jnp.where(expert == e2, w2 / total, 0.0)
    return gate, best_g


def _split_bf16(x):
    hi = x.astype(BF16)
    return hi, (x - hi.astype(F32)).astype(BF16)


def _moe_kernel(x_ref, sc_ref, sh_ref, gf_ref, wr_ref, br_ref, wg_ref, wu_ref, wd_ref, tri_ref, g_ref, b_ref,
                o_ref, hid_scr, *, alpha, cap):
    x = x_ref[0]
    tm = x.shape[0]
    h = x * (1.0 + sc_ref[0]) + sh_ref[0]
    hb = h.astype(BF16)
    n_exp, _, d_ff = wg_ref.shape
    scores_t = jax.nn.sigmoid(_dot(h, wr_ref[...]).T[0:n_exp, :])
    gate_t, best_g = _route(scores_t, scores_t + br_ref[...])
    sub = lax.broadcasted_iota(jnp.int32, (2 * SUBLANES, tm), 0)
    member_t = jnp.where(sub == best_g, 1.0, 0.0)
    before_t = _dot(member_t.astype(BF16), tri_ref[...])
    rank = jnp.sum(member_t * before_t, axis=0, keepdims=True)
    count = jnp.max(jnp.sum(member_t, axis=1, keepdims=True))
    grp_f = best_g.astype(F32)
    sub8 = lax.broadcasted_iota(jnp.int32, (SUBLANES, tm), 0)
    extra = jnp.where(sub8 == 0, grp_f, jnp.where(sub8 == 1, rank, 0.0))
    info = jnp.concatenate([gate_t, extra, jnp.zeros((LANES - n_exp - SUBLANES, tm), F32)], axis=0).T
    gate = info
    epg = EXPERTS_PER_GROUP

    def expert_hidden(rows_b, gate_rows, e, dst):
        gate_pre = _dot(rows_b, wg_ref[e])
        up = _dot(rows_b, wu_ref[e])
        hid = gate_pre * jax.nn.sigmoid(gate_pre) * up * gate_rows[:, e:e + 1]
        hid_scr[0:rows_b.shape[0], dst * d_ff:(dst + 1) * d_ff] = hid.astype(BF16)

    def dense(_):
        for e in range(n_exp):
            expert_hidden(hb, gate, e, e)
        return _dot(hid_scr[...], wd_ref[...])


    def grouped(_):
        grp_c = info[:, n_exp:n_exp + 1]
        rank_c = info[:, n_exp + 1:n_exp + 2]
        slot_c = lax.broadcasted_iota(jnp.int32, (cap, 1), 0).astype(F32)
        slot_r = lax.broadcasted_iota(jnp.int32, (1, cap), 1).astype(F32)
        gate_hi, gate_lo = _split_bf16(gate)
        y = jnp.zeros((tm, x.shape[1]), F32)
        for g in range(N_GROUPS):
            take = jnp.where(jnp.where(grp_f == g, rank, -1.0) == slot_c, 1.0, 0.0).astype(BF16)
            give = jnp.where(jnp.where(grp_c == g, rank_c, -1.0) == slot_r, 1.0, 0.0).astype(BF16)
            rows_b = _dot(take, hb).astype(BF16)
            gate_rows = _dot(take, gate_hi) + _dot(take, gate_lo)
            for j in range(epg):
                expert_hidden(rows_b, gate_rows, g * epg + j, j)
            out = _dot(hid_scr[0:cap, 0:epg * d_ff], wd_ref[g * epg * d_ff:(g + 1) * epg * d_ff, :])
            y = y + _dot(give, out.astype(BF16))
        return y

    y = lax.cond(count <= cap, grouped, dense, 0)
    z = alpha * x + (1.0 + gf_ref[0]) * y
    o_ref[0] = _layer_norm(z, g_ref[...], b_ref[...])


def _moe(x, sc, sh, g_f, w_router, b_router, w_gate, w_up, w_down, layer, tri, ln_g, ln_b, tm, cap, alpha):
    bsz, seq, d = x.shape
    _, n_exp, _, d_ff = w_gate.shape
    row = lambda b, i: (b, i, 0)
    per_b = lambda b, i: (b, 0, 0)
    const = lambda b, i: (0, 0)
    of_layer4 = lambda b, i: (layer, 0, 0, 0)
    of_layer3 = lambda b, i: (layer, 0, 0)
    resident = pl.Buffered(1)
    return pl.pallas_call(
        functools.partial(_moe_kernel, alpha=alpha, cap=cap),
        grid=(bsz, seq // tm),
        in_specs=[pl.BlockSpec((1, tm, d), row),
                  pl.BlockSpec((1, 1, d), per_b),
                  pl.BlockSpec((1, 1, d), per_b),
                  pl.BlockSpec((1, 1, d), per_b),
                  pl.BlockSpec((d, LANES), const),
                  pl.BlockSpec((n_exp, 1), const),
                  pl.BlockSpec((None, n_exp, d, d_ff), of_layer4, pipeline_mode=resident),
                  pl.BlockSpec((None, n_exp, d, d_ff), of_layer4, pipeline_mode=resident),
                  pl.BlockSpec((None, n_exp * d_ff, d), of_layer3, pipeline_mode=resident),
                  pl.BlockSpec((tm, tm), const, pipeline_mode=resident),
                  pl.BlockSpec((1, d), const),
                  pl.BlockSpec((1, d), const)],
        out_specs=pl.BlockSpec((1, tm, d), row),
        out_shape=jax.ShapeDtypeStruct((bsz, seq, d), F32),
        scratch_shapes=[pltpu.VMEM((tm, n_exp * d_ff), BF16)],
        compiler_params=_params(("parallel", "parallel")),
        name="moe",
    )(x, sc, sh, g_f, w_router, b_router, w_gate, w_up, w_down, tri, ln_g, ln_b)


def _pick(n, pref):
    t = min(pref, n)
    while n % t:
        t //= 2
    return t


def _rope_tables(positions):
    half = HEAD_DIM // 2
    inv_freq = ROPE_THETA ** (-jnp.arange(half, dtype=F32) / half)
    ang = positions.astype(F32)[..., None] * inv_freq
    cos, sin = jnp.cos(ang), jnp.sin(ang)
    reps = LANES // HEAD_DIM
    return (jnp.concatenate([cos, cos] * reps, axis=-1),
            jnp.concatenate([-sin, sin] * reps, axis=-1))


def _prep_w_kernel(w_ref, o_ref):
    o_v = 2 * MIX_ATTN
    o_iq = o_v + MIX_ATTN
    o_ik = o_iq + IQ_WIDTH
    o_iw = o_ik + IDX_DIM
    o_r = o_iw + N_IDX_HEADS
    o_g = o_r + R_WIDTH + M_WIDTH
    n_gate = 2 * N_HEADS_MLSTM

    def put(dst, src, width):
        o_ref[0, :, dst:dst + width] = w_ref[0, :, src:src + width].astype(o_ref.dtype)

    o_ref[0] = jnp.zeros(o_ref.shape[1:], o_ref.dtype)
    put(OFF_Q, 0, 2 * MIX_ATTN)
    put(OFF_IQ, o_iq, IQ_WIDTH)
    put(OFF_R, o_r, R_WIDTH + M_WIDTH)
    put(OFF_IK, o_ik, IDX_DIM)
    put(OFF_IK + IDX_DIM, o_ik, IDX_DIM)
    put(OFF_S + S_IW, o_iw, N_IDX_HEADS)
    put(OFF_S + S_MI, o_g, n_gate)
    put(OFF_V, o_v, MIX_ATTN)


def _prep_w_in(w_in, tr):
    depth, d, n = w_in.shape
    return pl.pallas_call(
        _prep_w_kernel,
        grid=(depth, d // tr),
        in_specs=[pl.BlockSpec((1, tr, n), lambda l, i: (l, i, 0))],
        out_specs=pl.BlockSpec((1, tr, W_TOTAL), lambda l, i: (l, i, 0)),
        out_shape=jax.ShapeDtypeStruct((depth, d, W_TOTAL), F32),
        compiler_params=_params(("parallel", "parallel")),
        name="prep_w_in",
    )(w_in)


def kernel(x, c, positions, w_ada, b_ada, w_in, i_bias, f_bias, conv_w, conv_b, w_out, ln_mix_g, ln_mix_b,
           w_router, b_router, w_gate, w_up, w_down, ln_ffn_g, ln_ffn_b):
    bsz, seq, d = x.shape
    depth = w_ada.shape[0]
    alpha = (2.0 * depth) ** 0.25

    tm = _pick(seq, 512)
    tq = _pick(seq, 256)
    kb = _pick(seq, 256)
    assert seq // 16 <= 256, "packed bf16 partial counts in the DSA threshold search must stay exact"
    cr = _pick(seq, 256)
    cm = _pick(seq, 256)
    tmoe = _pick(seq, 512)
    moe_cap = min(tmoe, (3 * tmoe // (2 * N_GROUPS) + 15) // 16 * 16)

    cos_t, sin_t = _rope_tables(positions)
    c_pad = jnp.zeros((8, d), F32).at[:bsz].set(c)
    mod = _ada_mod(c_pad, w_ada, b_ada, _pick(6 * d, 1536))
    w_in_p = _prep_w_in(w_in, _pick(d, 256)).astype(BF16)
    w_out_b = w_out.astype(BF16)
    w_gate_b, w_up_b = w_gate.astype(BF16), w_up.astype(BF16)
    w_down_b = w_down.astype(BF16).reshape(depth, -1, d)
    grp = 2 if bsz % 2 == 0 else 1

    tril_kb = (jnp.arange(kb)[:, None] >= jnp.arange(kb)[None, :]).astype(BF16)
    tril = (jnp.arange(cm)[:, None] >= jnp.arange(cm)[None, :]).astype(F32)
    w_router_p = jnp.zeros((d, LANES), F32).at[:, :N_EXPERTS].set(w_router)
    b_router_p = b_router.reshape(N_EXPERTS, 1)
    tri_moe = (jnp.arange(tmoe)[:, None] < jnp.arange(tmoe)[None, :]).astype(BF16)

    for l in range(depth):
        parts = [mod[l, :bsz, j * d:(j + 1) * d].reshape(bsz, 1, d) for j in range(6)]
        sh_m, sc_m, g_m, sh_f, sc_f, g_f = parts
        q, k, iq, ik2, r_proj, m_proj, small, small_t, v_t = _in_proj(
            x, sc_m, sh_m, w_in_p, l, cos_t, sin_t, tm, kb)
        o_a = _dsa(q, k, iq, ik2, small_t, v_t, tril_kb, tq, kb)
        o_b = _retention(r_proj, cr, grp)
        gate_bias = (jnp.zeros((1, S_WIDTH), F32).at[0, S_MI:S_MI + N_HEADS_MLSTM].set(i_bias[l])
                     .at[0, S_MF:S_MF + N_HEADS_MLSTM].set(f_bias[l]))
        o_c = _mlstm(m_proj, small, gate_bias, conv_w[l], conv_b[l].reshape(1, -1), tril, cm, grp)
        x = _out_proj(o_a, o_b, o_c, w_out_b, l, x, g_m,
                      ln_mix_g[l].reshape(1, d), ln_mix_b[l].reshape(1, d), tm, alpha)
        x = _moe(x, sc_f, sh_f, g_f, w_router_p, b_router_p,
                 w_gate_b, w_up_b, w_down_b, l, tri_moe,
                 ln_ffn_g[l].reshape(1, d), ln_ffn_b[l].reshape(1, d), tmoe, moe_cap, alpha)
    return x
```

```python
import functools

import numpy as np
import jax
import jax.numpy as jnp
from jax import lax
from jax.experimental import pallas as pl
from jax.experimental.pallas import tpu as pltpu

F32 = jnp.float32
BF16 = jnp.bfloat16

HEAD_DIM = 64
CHUNK = 64
N_HEADS_ATTN = 8
N_IDX_HEADS = 4
IDX_DIM = 64
TOPK_MAX = 256
N_HEADS_RET = 4
N_HEADS_MLSTM = 4
CONV_WIDTH = 4
ROPE_THETA = 10000.0
N_EXPERTS = 16
N_GROUPS = 4
EXPERTS_PER_GROUP = N_EXPERTS // N_GROUPS
D_FF_EXPERT = 256
LN_EPS = 1e-5

MIX_ATTN = N_HEADS_ATTN * HEAD_DIM
MIX_RET = N_HEADS_RET * HEAD_DIM
MIX_MLSTM = N_HEADS_MLSTM * HEAD_DIM

LANES = 128
SUBLANES = 8
VMEM_LIMIT = 56 * 1024 * 1024

IQ_WIDTH = N_IDX_HEADS * IDX_DIM
R_WIDTH = 4 * MIX_RET
M_WIDTH = 4 * MIX_MLSTM
S_WIDTH = LANES
S_IW = 0
S_MI = S_IW + N_IDX_HEADS
S_MF = S_MI + N_HEADS_MLSTM
S_ROWS = 16
V_ROWS = HEAD_DIM + 16
OFF_Q = 0
OFF_K = OFF_Q + MIX_ATTN
OFF_IQ = OFF_K + MIX_ATTN
OFF_R = OFF_IQ + IQ_WIDTH
OFF_M = OFF_R + R_WIDTH
OFF_IK = OFF_M + M_WIDTH
OFF_S = OFF_IK + LANES
OFF_V = OFF_S + S_WIDTH
W_TOTAL = OFF_V + MIX_ATTN

INT_MIN = -2 ** 31
NEG_BIG = -1e30
LOG2_E = 1.4426950408889634


def _dot(a, b):
    return jnp.dot(a, b, preferred_element_type=F32)


def _dot_nt(a, b):
    return lax.dot_general(a, b, (((1,), (1,)), ((), ())), preferred_element_type=F32)


def _dot_tn(a, b):
    return lax.dot_general(a, b, (((0,), (0,)), ((), ())), preferred_element_type=F32)


def _params(sem):
    return pltpu.CompilerParams(dimension_semantics=sem, vmem_limit_bytes=VMEM_LIMIT)


def _ada_kernel(c_ref, w_ref, b_ref, o_ref):
    c = c_ref[...]
    c_act = c * jax.nn.sigmoid(c)
    o_ref[0] = _dot(c_act, w_ref[0]) + b_ref[0]


def _ada_mod(c_pad, w_ada, b_ada, tn):
    depth, d, n = w_ada.shape
    rows = c_pad.shape[0]
    return pl.pallas_call(
        _ada_kernel,
        grid=(depth, n // tn),
        in_specs=[pl.BlockSpec((rows, d), lambda l, j: (0, 0)),
                  pl.BlockSpec((1, d, tn), lambda l, j: (l, 0, j)),
                  pl.BlockSpec((1, 1, tn), lambda l, j: (l, 0, j))],
        out_specs=pl.BlockSpec((1, rows, tn), lambda l, j: (l, 0, j)),
        out_shape=jax.ShapeDtypeStruct((depth, rows, n), F32),
        compiler_params=_params(("parallel", "parallel")),
        name="ada_mod",
    )(c_pad, w_ada, b_ada.reshape(depth, 1, n))


def _rope(y, cos, sin):
    w = y.shape[1]
    reps = w // LANES
    cosw = jnp.concatenate([cos] * reps, axis=1) if reps > 1 else cos
    sinw = jnp.concatenate([sin] * reps, axis=1) if reps > 1 else sin
    lane = lax.broadcasted_iota(jnp.int32, y.shape, 1)
    first = (lane % HEAD_DIM) < (HEAD_DIM // 2)
    partner = jnp.where(first, pltpu.roll(y, w - HEAD_DIM // 2, 1), pltpu.roll(y, HEAD_DIM // 2, 1))
    return y * cosw + partner * sinw


def _inproj_kernel(x_ref, sc_ref, sh_ref, w_ref, cos_ref, sin_ref,
                   q_ref, k_ref, iq_ref, ik_ref, r_ref, m_ref, s_ref, st_ref, vt_ref, *, kb):
    h = (x_ref[0] * (1.0 + sc_ref[0]) + sh_ref[0]).astype(BF16)
    cos = cos_ref[0]
    sin = sin_ref[0]

    def proj(start, width):
        return _dot(h, w_ref[:, start:start + width])

    q_ref[0] = (_rope(proj(OFF_Q, MIX_ATTN), cos, sin) * (HEAD_DIM ** -0.5 * LOG2_E)).astype(BF16)
    k_ref[0] = _rope(proj(OFF_K, MIX_ATTN), cos, sin).astype(BF16)
    iq_ref[0] = _rope(proj(OFF_IQ, IQ_WIDTH), cos, sin).astype(BF16)
    ik_ref[0] = _rope(proj(OFF_IK, LANES), cos, sin).astype(BF16)
    r_ref[0, :, 0:2 * MIX_RET] = _rope(proj(OFF_R, 2 * MIX_RET), cos, sin).astype(BF16)
    r_ref[0, :, 2 * MIX_RET:R_WIDTH] = proj(OFF_R + 2 * MIX_RET, 2 * MIX_RET).astype(BF16)
    m_ref[0, :, 0:2 * MIX_MLSTM] = proj(OFF_M, 2 * MIX_MLSTM).astype(BF16)
    m_ref[0, :, 2 * MIX_MLSTM:M_WIDTH] = proj(OFF_M + 2 * MIX_MLSTM, 2 * MIX_MLSTM).astype(BF16)
    y = proj(OFF_S, S_WIDTH)
    s_ref[0] = y
    st_ref[0] = y.T[0:S_ROWS, :]
    tm = h.shape[0]
    yvt = proj(OFF_V, MIX_ATTN).T.astype(BF16)
    pad_rows = lax.broadcasted_iota(jnp.int32, (V_ROWS - HEAD_DIM, kb), 0)
    ones_rows = jnp.where(pad_rows == 0, 1.0, 0.0).astype(BF16)
    for j in range(tm // kb):
        for hh in range(N_HEADS_ATTN):
            vt_ref[0, j, hh * V_ROWS:hh * V_ROWS + HEAD_DIM, :] = yvt[hh * HEAD_DIM:(hh + 1) * HEAD_DIM,
                                                                      j * kb:(j + 1) * kb]
            vt_ref[0, j, hh * V_ROWS + HEAD_DIM:(hh + 1) * V_ROWS, :] = ones_rows


def _in_proj(x, sc, sh, w, layer, cos_t, sin_t, tm, kb):
    bsz, seq, d = x.shape
    row = lambda b, i: (b, i, 0)
    per_b = lambda b, i: (b, 0, 0)
    widths = (MIX_ATTN, MIX_ATTN, IQ_WIDTH, LANES, R_WIDTH, M_WIDTH)
    return pl.pallas_call(
        functools.partial(_inproj_kernel, kb=kb),
        grid=(bsz, seq // tm),
        in_specs=[pl.BlockSpec((1, tm, d), row),
                  pl.BlockSpec((1, 1, d), per_b),
                  pl.BlockSpec((1, 1, d), per_b),
                  pl.BlockSpec((None, d, W_TOTAL), lambda b, i: (layer, 0, 0)),
                  pl.BlockSpec((1, tm, LANES), row),
                  pl.BlockSpec((1, tm, LANES), row)],
        out_specs=[pl.BlockSpec((1, tm, wd), row) for wd in widths]
                  + [pl.BlockSpec((1, tm, S_WIDTH), row),
                     pl.BlockSpec((1, S_ROWS, tm), lambda b, i: (b, 0, i)),
                     pl.BlockSpec((1, tm // kb, N_HEADS_ATTN * V_ROWS, kb), lambda b, i: (b, i, 0, 0))],
        out_shape=[jax.ShapeDtypeStruct((bsz, seq, wd), BF16) for wd in widths]
                  + [jax.ShapeDtypeStruct((bsz, seq, S_WIDTH), F32),
                     jax.ShapeDtypeStruct((bsz, S_ROWS, seq), F32),
                     jax.ShapeDtypeStruct((bsz, seq // kb, N_HEADS_ATTN * V_ROWS, kb), BF16)],
        compiler_params=_params(("parallel", "parallel")),
        name="in_proj",
    )(x, sc, sh, w, cos_t, sin_t)


def _dsa_kernel(q_ref, k_ref, iq_ref, ik_ref, st_ref, vt_ref, tril_ref, o_ref,
                key_scr, byte_scr, cand_scr, bias_scr, s_scr, p_scr, m_scr, alpha_scr, acc_scr,
                *, tq, kb, topk):
    q0 = pl.program_id(1) * tq
    n_blocks = (q0 + tq + kb - 1) // kb
    qpos = q0 + lax.broadcasted_iota(jnp.int32, (1, tq), 1)
    q_limit = (qpos // CHUNK + 1) * CHUNK
    krow = lax.broadcasted_iota(jnp.int32, (kb, 1), 0)

    def head_of_pair(x, h):
        pair = x[:, (h // 2) * LANES:(h // 2 + 1) * LANES]
        lane = lax.broadcasted_iota(jnp.int32, pair.shape, 1)
        keep = (lane < HEAD_DIM) if h % 2 == 0 else (lane >= HEAD_DIM)
        return jnp.where(keep, pair, jnp.zeros_like(pair))

    iw = st_ref[0][S_IW:S_IW + N_IDX_HEADS, :] * (N_IDX_HEADS ** -0.5 * IDX_DIM ** -0.5)
    iq = iq_ref[0]
    iq_heads = [head_of_pair(iq, h) for h in range(N_IDX_HEADS)]

    def score_block(c):
        k0 = pl.multiple_of(c * kb, kb)
        ik2 = ik_ref[0, pl.ds(k0, kb), :]
        score = jnp.zeros((kb, tq), F32)
        for h in range(N_IDX_HEADS):
            score = score + jnp.maximum(_dot_nt(ik2, iq_heads[h]), 0.0) * iw[h:h + 1, :]
        bits = pltpu.bitcast(score, jnp.int32)
        key = jnp.where(bits >= 0, bits, bits ^ jnp.int32(0x7FFFFFFF))
        key = jnp.where(k0 + krow < q_limit, key, jnp.int32(INT_MIN))
        key_scr[c] = key
        byte_scr[0, c] = ((key >> 24) + 128).astype(F32).astype(BF16)
        for lvl in range(1, 4):
            byte_scr[lvl, c] = ((key >> (24 - 8 * lvl)) & 255).astype(F32).astype(BF16)

    def score_pair(c2, carry):
        score_block(2 * c2)
        score_block(2 * c2 + 1)
        return carry

    lax.fori_loop(0, n_blocks // 2, score_pair, 0)

    @pl.when(n_blocks % 2 == 1)
    def _():
        score_block(n_blocks - 1)

    pack = 16
    one = jnp.ones((kb, tq), BF16)
    zero = jnp.zeros((kb, tq), BF16)

    def count_ge(lvl, cand, narrow_to=None):
        cand_b = cand.astype(BF16)

        def hits(c):
            plane = byte_scr[0, c] if lvl == 0 else cand_scr[c]
            hit = jnp.where(plane >= cand_b, one, zero)
            if narrow_to is not None:
                cand_scr[c] = jnp.where(plane == narrow_to, byte_scr[lvl + 1, c], -one)
            parts = [hit[i * pack:(i + 1) * pack, :] for i in range(kb // pack)]
            while len(parts) > 1:
                parts = [parts[i] + parts[i + 1] for i in range(0, len(parts), 2)]
            return parts[0]

        acc = lax.fori_loop(0, n_blocks // 2, lambda c2, a: a + hits(2 * c2) + hits(2 * c2 + 1),
                            jnp.zeros((pack, tq), BF16))
        acc = lax.cond(n_blocks % 2 == 1, lambda a: a + hits(n_blocks - 1), lambda a: a, acc)
        return jnp.sum(acc.astype(F32), axis=0, keepdims=True)

    above = jnp.zeros((1, tq), F32)
    t = jnp.zeros((1, tq), jnp.int32)
    for lvl in range(4):
        def bit_body(i, v, lvl=lvl, above=above):
            cand = v + lax.shift_left(jnp.int32(1), 7 - i).astype(F32)
            return jnp.where(above + count_ge(lvl, cand) >= topk, cand, v)

        v = lax.fori_loop(0, 8, bit_body, jnp.zeros((1, tq), F32))
        above = above + count_ge(lvl, v + 1.0, narrow_to=v.astype(BF16) if lvl < 3 else None)
        t = t | lax.shift_left(v.astype(jnp.int32), 24 - 8 * lvl)
    thr = jnp.maximum(t ^ jnp.int32(INT_MIN), jnp.int32(INT_MIN + 1))
    need = topk - above

    q = q_ref[0]
    q_heads = [head_of_pair(q, h) for h in range(N_HEADS_ATTN)]
    m_scr[...] = jnp.full(m_scr.shape, NEG_BIG, F32)
    alpha_scr[...] = jnp.ones(alpha_scr.shape, F32)
    acc_scr[...] = jnp.zeros(acc_scr.shape, F32)
    p_scr[...] = jnp.zeros(p_scr.shape, BF16)

    def stage_mask(c, ties_before):
        key = key_scr[c]
        tie = key == thr
        rank = _dot(tril_ref[...], jnp.where(tie, 1.0, 0.0).astype(BF16)) + ties_before
        sel = (key > thr) | (tie & (rank <= need))
        bias_scr[...] = jnp.where(sel, 0.0, NEG_BIG)
        return rank[kb - 1:kb, :]

    def stage_logits(c, h):
        k0 = pl.multiple_of(c * kb, kb)
        kp = k_ref[0, pl.ds(k0, kb), (h // 2) * LANES:(h // 2 + 1) * LANES]
        s_scr[h] = _dot_nt(kp, q_heads[h])

    def stage_softmax(h):
        for half in range(tq // LANES):
            ln = slice(half * LANES, (half + 1) * LANES)
            s = s_scr[h, :, ln] + bias_scr[:, ln]
            m_old = m_scr[h, :, ln]
            m_new = jnp.maximum(m_old, jnp.max(s, axis=0, keepdims=True))
            p_scr[h, :, ln] = jnp.exp2(s - m_new).astype(BF16)
            alpha_scr[h, :, ln] = jnp.exp2(m_old - m_new)
            m_scr[h, :, ln] = m_new

    def stage_values(c, h):
        vt = vt_ref[0, c, h * V_ROWS:(h + 1) * V_ROWS, :]
        acc_scr[h, 0:V_ROWS, :] = alpha_scr[h] * acc_scr[h, 0:V_ROWS, :] + _dot(vt, p_scr[h])

    ties0 = stage_mask(0, jnp.zeros((1, tq), F32))
    for h in range(N_HEADS_ATTN):
        stage_logits(0, h)

    def attn_body(j, ties_before):
        c_old = jnp.maximum(j - 2, 0)
        for h in range(N_HEADS_ATTN):
            stage_values(c_old, h)
            stage_softmax(h)
            stage_logits(j, h)
        return stage_mask(j, ties_before)

    lax.fori_loop(1, n_blocks, attn_body, ties0)
    for h in range(N_HEADS_ATTN):
        stage_values(jnp.maximum(n_blocks - 2, 0), h)
        stage_softmax(h)
    for h in range(N_HEADS_ATTN):
        stage_values(n_blocks - 1, h)
    for h in range(N_HEADS_ATTN):
        acc = acc_scr[h]
        out = acc * (1.0 / acc[HEAD_DIM:HEAD_DIM + 1, :])
        o_ref[0, :, h * HEAD_DIM:(h + 1) * HEAD_DIM] = out.T[:, 0:HEAD_DIM].astype(o_ref.dtype)


def _dsa(q, k, iq, ik2, small_t, v_t, tril, tq, kb):
    bsz, seq, _ = q.shape
    topk = min(TOPK_MAX, seq // 4)
    kern = functools.partial(_dsa_kernel, tq=tq, kb=kb, topk=topk)
    return pl.pallas_call(
        kern,
        grid=(bsz, seq // tq),
        in_specs=[pl.BlockSpec((1, tq, MIX_ATTN), lambda b, i: (b, i, 0)),
                  pl.BlockSpec((1, seq, MIX_ATTN), lambda b, i: (b, 0, 0)),
                  pl.BlockSpec((1, tq, IQ_WIDTH), lambda b, i: (b, i, 0)),
                  pl.BlockSpec((1, seq, LANES), lambda b, i: (b, 0, 0)),
                  pl.BlockSpec((1, S_ROWS, tq), lambda b, i: (b, 0, i)),
                  pl.BlockSpec((1, seq // kb, N_HEADS_ATTN * V_ROWS, kb), lambda b, i: (b, 0, 0, 0)),
                  pl.BlockSpec((kb, kb), lambda b, i: (0, 0))],
        out_specs=pl.BlockSpec((1, tq, MIX_ATTN), lambda b, i: (b, i, 0)),
        out_shape=jax.ShapeDtypeStruct((bsz, seq, MIX_ATTN), BF16),
        scratch_shapes=[pltpu.VMEM((seq // kb, kb, tq), jnp.int32),
                        pltpu.VMEM((4, seq // kb, kb, tq), BF16),
                        pltpu.VMEM((seq // kb, kb, tq), BF16),
                        pltpu.VMEM((kb, tq), F32),
                        pltpu.VMEM((N_HEADS_ATTN, kb, tq), F32),
                        pltpu.VMEM((N_HEADS_ATTN, kb, tq), BF16),
                        pltpu.VMEM((N_HEADS_ATTN, 1, tq), F32),
                        pltpu.VMEM((N_HEADS_ATTN, 1, tq), F32),
                        pltpu.VMEM((N_HEADS_ATTN, LANES, tq), F32)],
        compiler_params=_params(("parallel", "arbitrary")),
        name="dsa",
    )(q, k, iq, ik2, small_t, v_t, tril)


def _head_norm(y):
    mean_mat = jnp.full((HEAD_DIM, HEAD_DIM), 1.0 / HEAD_DIM, BF16)

    def mean_bcast(x):
        hi = x.astype(BF16)
        lo = (x - hi.astype(F32)).astype(BF16)
        return _dot(hi, mean_mat) + _dot(lo, mean_mat)

    yc = y - mean_bcast(y)
    return yc * lax.rsqrt(mean_bcast(yc * yc) + LN_EPS)


def _ret_kernel(r_ref, o_ref, state_scr, *, cr, grp):
    @pl.when(pl.program_id(1) == 0)
    def _():
        state_scr[...] = jnp.zeros_like(state_scr)

    ri = lax.broadcasted_iota(jnp.int32, (cr, cr), 0)
    ci = lax.broadcasted_iota(jnp.int32, (cr, cr), 1)
    diff = (ri - ci).astype(F32)
    pos = lax.broadcasted_iota(jnp.int32, (cr, 1), 0).astype(F32)
    items = [(h, g) for h in range(N_HEADS_RET) for g in range(grp)]
    sl = lambda part, h: slice(part * MIX_RET + h * HEAD_DIM, part * MIX_RET + (h + 1) * HEAD_DIM)
    log_gamma = [jnp.log1p(jnp.full((1, 1), -(2.0 ** (-5.0 - h)), F32)) for h in range(N_HEADS_RET)]
    decay_in = [jnp.where(diff >= 0, jnp.exp(diff * lg), 0.0) * (HEAD_DIM ** -0.5) for lg in log_gamma]
    q = {(h, g): r_ref[g, :, sl(0, h)] for h, g in items}
    k = {(h, g): r_ref[g, :, sl(1, h)] for h, g in items}
    v = {(h, g): r_ref[g, :, sl(2, h)] for h, g in items}
    state = {(h, g): state_scr[g, h] for h, g in items}
    scores = {it: _dot_nt(q[it], k[it]) * decay_in[it[0]] for it in items}
    cross = {it: jnp.exp((pos + 1.0) * log_gamma[it[0]]) * _dot(q[it], state[it].astype(BF16)) for it in items}
    inner = {it: _dot(scores[it].astype(BF16), v[it]) for it in items}
    for it in items:
        h, g = it
        gate = r_ref[g, :, sl(3, h)].astype(F32)
        y = _head_norm(inner[it] + cross[it])
        o_ref[g, :, h * HEAD_DIM:(h + 1) * HEAD_DIM] = (y * (gate * jax.nn.sigmoid(gate))).astype(o_ref.dtype)
    for it in items:
        h, g = it
        k_decay = (HEAD_DIM ** -0.5) * jnp.exp((cr - 1.0 - pos) * log_gamma[h])
        k_dec = (k[it].astype(F32) * k_decay).astype(BF16)
        state_scr[g, h] = state[it] * jnp.exp(cr * log_gamma[h]) + _dot_tn(k_dec, v[it])


def _retention(r_proj, cr, grp):
    bsz, seq, _ = r_proj.shape
    return pl.pallas_call(
        functools.partial(_ret_kernel, cr=cr, grp=grp),
        grid=(bsz // grp, seq // cr),
        in_specs=[pl.BlockSpec((grp, cr, R_WIDTH), lambda b, i: (b, i, 0))],
        out_specs=pl.BlockSpec((grp, cr, MIX_RET), lambda b, i: (b, i, 0)),
        out_shape=jax.ShapeDtypeStruct((bsz, seq, MIX_RET), BF16),
        scratch_shapes=[pltpu.VMEM((grp, N_HEADS_RET, HEAD_DIM, HEAD_DIM), F32)],
        compiler_params=_params(("parallel", "arbitrary")),
        name="retention",
    )(r_proj)


def _mlstm_kernel(m_ref, sm_ref, bias_ref, cw_ref, cb_ref, tril_ref, o_ref,
                  xbuf, a_scr, m_scr, *, cm, grp):
    halo = 8
    gate_shift = S_MF - S_MI

    @pl.when(pl.program_id(1) == 0)
    def _():
        xbuf[:, 0:halo, :] = jnp.zeros((grp, halo, 2 * MIX_MLSTM), F32)
        a_scr[...] = jnp.zeros_like(a_scr)
        m_scr[...] = jnp.zeros_like(m_scr)

    ri = lax.broadcasted_iota(jnp.int32, (cm, cm), 0)
    ci = lax.broadcasted_iota(jnp.int32, (cm, cm), 1)
    causal = ri >= ci
    row = lax.broadcasted_iota(jnp.int32, (cm, LANES), 0)
    lane = lax.broadcasted_iota(jnp.int32, (cm, LANES), 1)
    gate_lanes = (lane >= S_MI) & (lane < S_MI + N_HEADS_MLSTM)
    lane64 = lax.broadcasted_iota(jnp.int32, (cm, HEAD_DIM), 1)
    ones_col = jnp.where(lane64 == 0, 1.0, 0.0).astype(BF16)
    scale = HEAD_DIM ** -0.5

    def head_cols(x, h):
        return x[:, h * HEAD_DIM:(h + 1) * HEAD_DIM]

    items = [(g, h) for g in range(grp) for h in range(N_HEADS_MLSTM)]
    mm, qk, u_row, m_col, inter, e_inv, kw_col, decay = {}, {}, {}, {}, {}, {}, {}, {}
    for g in range(grp):
        mm[g] = m_ref[g]
        xbuf[g, halo:halo + cm, :] = mm[g][:, 0:2 * MIX_MLSTM].astype(F32)
        conv = cb_ref[...]
        for j in range(CONV_WIDTH):
            off = halo - (CONV_WIDTH - 1) + j
            conv = conv + xbuf[g, off:off + cm, :] * cw_ref[j:j + 1, :]
        xbuf[g, 0:halo, :] = xbuf[g, cm:cm + halo, :]
        qk[g] = conv * jax.nn.sigmoid(conv)

        gates = sm_ref[g] + bias_ref[...]
        b_all = jnp.dot(tril_ref[...], jax.nn.log_sigmoid(gates), preferred_element_type=F32,
                        precision=lax.Precision.HIGHEST)
        b_i = jnp.where(gate_lanes, pltpu.roll(b_all, LANES - gate_shift, 1), 0.0)
        u = jnp.where(gate_lanes, gates, 0.0) - b_i
        run = u
        step = 1
        while step < cm:
            run = jnp.maximum(run, jnp.where(row >= step, pltpu.roll(run, step, 0), -jnp.inf))
            step *= 2
        m_prev = m_scr[g]
        m_c = jnp.maximum(m_prev, run)
        m_last = m_c[cm - 1:cm, :]
        u_row[g] = u.T
        m_col[g] = m_c
        inter[g] = jnp.exp(m_prev - m_c)
        e_inv[g] = jnp.exp(-(b_i + m_c))
        kw_col[g] = scale * jnp.exp(u - m_last)
        decay[g] = jnp.exp(m_prev - m_last)
        m_scr[g] = b_i[cm - 1:cm, :] + m_last

    q, k, v_aug, a_mem = {}, {}, {}, {}
    for it in items:
        g, h = it
        q[it] = head_cols(qk[g], h).astype(BF16)
        k[it] = head_cols(qk[g][:, MIX_MLSTM:], h)
        v_aug[it] = jnp.concatenate([head_cols(mm[g][:, 2 * MIX_MLSTM:], h), ones_col], axis=1)
        a_mem[it] = a_scr[g, h]

    def col(x, h):
        return x[:, S_MI + h:S_MI + h + 1]

    w = {}
    for it in items:
        g, h = it
        u_r = u_row[g][S_MI + h:S_MI + h + 1, :]
        w[it] = jnp.exp(jnp.where(causal, u_r - col(m_col[g], h), -jnp.inf))

    s = {it: _dot_nt(q[it], k[it].astype(BF16)) * scale * w[it] for it in items}
    cross = {it: _dot(q[it], a_mem[it].astype(BF16)) for it in items}
    both = {it: _dot(s[it].astype(BF16), v_aug[it]) + col(inter[it[0]], it[1]) * cross[it] for it in items}
    for it in items:
        g, h = it
        den = both[it][:, HEAD_DIM:HEAD_DIM + 1]
        h_tilde = both[it][:, 0:HEAD_DIM] * (1.0 / jnp.maximum(jnp.abs(den), col(e_inv[g], h)))
        og = head_cols(mm[g][:, 3 * MIX_MLSTM:], h).astype(F32)
        o_ref[g, :, h * HEAD_DIM:(h + 1) * HEAD_DIM] = _head_norm(jax.nn.sigmoid(og) * h_tilde).astype(o_ref.dtype)

    for it in items:
        g, h = it
        kw = k[it] * col(kw_col[g], h)
        a_scr[g, h] = col(decay[g], h) * a_mem[it] + _dot_tn(kw.astype(BF16), v_aug[it])


def _mlstm(m_proj, small, gate_bias, conv_w, conv_b, tril, cm, grp):
    bsz, seq, _ = m_proj.shape
    const = lambda b, i: (0, 0)
    return pl.pallas_call(
        functools.partial(_mlstm_kernel, cm=cm, grp=grp),
        grid=(bsz // grp, seq // cm),
        in_specs=[pl.BlockSpec((grp, cm, M_WIDTH), lambda b, i: (b, i, 0)),
                  pl.BlockSpec((grp, cm, S_WIDTH), lambda b, i: (b, i, 0)),
                  pl.BlockSpec((1, S_WIDTH), const),
                  pl.BlockSpec((CONV_WIDTH, 2 * MIX_MLSTM), const),
                  pl.BlockSpec((1, 2 * MIX_MLSTM), const),
                  pl.BlockSpec((cm, cm), const)],
        out_specs=pl.BlockSpec((grp, cm, MIX_MLSTM), lambda b, i: (b, i, 0)),
        out_shape=jax.ShapeDtypeStruct((bsz, seq, MIX_MLSTM), BF16),
        scratch_shapes=[pltpu.VMEM((grp, cm + 8, 2 * MIX_MLSTM), F32),
                        pltpu.VMEM((grp, N_HEADS_MLSTM, HEAD_DIM, LANES), F32),
                        pltpu.VMEM((grp, 1, LANES), F32)],
        compiler_params=_params(("parallel", "arbitrary")),
        name="mlstm",
    )(m_proj, small, gate_bias, conv_w, conv_b, tril)


def _layer_norm(z, g, b):
    mu = jnp.mean(z, axis=-1, keepdims=True)
    var = jnp.mean(jnp.square(z - mu), axis=-1, keepdims=True)
    return (z - mu) * lax.rsqrt(var + LN_EPS) * g + b


def _outproj_kernel(oa_ref, ob_ref, oc_ref, w_ref, x_ref, gm_ref, g_ref, b_ref, o_ref, *, alpha):
    mix = _dot(oa_ref[0], w_ref[0:MIX_ATTN, :])
    mix = mix + _dot(ob_ref[0], w_ref[MIX_ATTN:MIX_ATTN + MIX_RET, :])
    mix = mix + _dot(oc_ref[0], w_ref[MIX_ATTN + MIX_RET:, :])
    z = alpha * x_ref[0] + (1.0 + gm_ref[0]) * mix
    o_ref[0] = _layer_norm(z, g_ref[...], b_ref[...])


def _out_proj(o_a, o_b, o_c, w_out, layer, x, g_m, ln_g, ln_b, tm, alpha):
    bsz, seq, d = x.shape
    row = lambda b, i: (b, i, 0)
    const = lambda b, i: (0, 0)
    return pl.pallas_call(
        functools.partial(_outproj_kernel, alpha=alpha),
        grid=(bsz, seq // tm),
        in_specs=[pl.BlockSpec((1, tm, MIX_ATTN), row),
                  pl.BlockSpec((1, tm, MIX_RET), row),
                  pl.BlockSpec((1, tm, MIX_MLSTM), row),
                  pl.BlockSpec((None,) + w_out.shape[1:], lambda b, i: (layer, 0, 0)),
                  pl.BlockSpec((1, tm, d), row),
                  pl.BlockSpec((1, 1, d), lambda b, i: (b, 0, 0)),
                  pl.BlockSpec((1, d), const),
                  pl.BlockSpec((1, d), const)],
        out_specs=pl.BlockSpec((1, tm, d), row),
        out_shape=jax.ShapeDtypeStruct((bsz, seq, d), F32),
        compiler_params=_params(("parallel", "parallel")),
        name="out_proj",
    )(o_a, o_b, o_c, w_out, x, g_m, ln_g, ln_b)


def _route(scores, biased):
    col = lambda a, e: a[e:e + 1, :]
    epg = EXPERTS_PER_GROUP
    group_scores = []
    for g in range(N_GROUPS):
        vals = [col(biased, g * epg + j) for j in range(epg)]
        best = None
        for a in range(epg):
            for b in range(a + 1, epg):
                pair = vals[a] + vals[b]
                best = pair if best is None else jnp.maximum(best, pair)
        group_scores.append(best)
    best_g = jnp.zeros_like(group_scores[0], dtype=jnp.int32)
    best_v = group_scores[0]
    for g in range(1, N_GROUPS):
        better = group_scores[g] > best_v
        best_g = jnp.where(better, g, best_g)
        best_v = jnp.where(better, group_scores[g], best_v)
    cand_b = [sum(jnp.where(best_g == g, col(biased, g * epg + j), 0.0) for g in range(N_GROUPS))
              for j in range(epg)]
    cand_s = [sum(jnp.where(best_g == g, col(scores, g * epg + j), 0.0) for g in range(N_GROUPS))
              for j in range(epg)]

    def argmax_first(vals, skip=None):
        idx = None
        val = None
        for j, vj in enumerate(vals):
            if skip is not None:
                vj = jnp.where(skip == j, -jnp.inf, vj)
            if idx is None:
                idx, val = jnp.zeros_like(best_g), vj
            else:
                better = vj > val
                idx = jnp.where(better, j, idx)
                val = jnp.where(better, vj, val)
        return idx

    first = argmax_first(cand_b)
    second = argmax_first(cand_b, skip=first)
    w1 = sum(jnp.where(first == j, cand_s[j], 0.0) for j in range(epg))
    w2 = sum(jnp.where(second == j, cand_s[j], 0.0) for j in range(epg))
    total = w1 + w2
    e1 = best_g * epg + first
    e2 = best_g * epg + second
    expert = lax.broadcasted_iota(jnp.int32, scores.shape, 0)
    gate = jnp.where(expert == e1, w1 / total, 0.0) + jnp.where(expert == e2, w2 / total, 0.0)
    return gate, best_g


def _split_bf16(x):
    hi = x.astype(BF16)
    return hi, (x - hi.astype(F32)).astype(BF16)


def _moe_kernel(x_ref, sc_ref, sh_ref, gf_ref, wr_ref, br_ref, wg_ref, wu_ref, wd_ref, tri_ref, g_ref, b_ref,
                o_ref, hid_scr, *, alpha, cap):
    x = x_ref[0]
    tm = x.shape[0]
    h = x * (1.0 + sc_ref[0]) + sh_ref[0]
    hb = h.astype(BF16)
    n_exp, _, d_ff = wg_ref.shape
    scores_t = jax.nn.sigmoid(_dot(h, wr_ref[...]).T[0:n_exp, :])
    gate_t, best_g = _route(scores_t, scores_t + br_ref[...])
    sub = lax.broadcasted_iota(jnp.int32, (2 * SUBLANES, tm), 0)
    member_t = jnp.where(sub == best_g, 1.0, 0.0)
    before_t = _dot(member_t.astype(BF16), tri_ref[...])
    rank = jnp.sum(member_t * before_t, axis=0, keepdims=True)
    count = jnp.max(jnp.sum(member_t, axis=1, keepdims=True))
    grp_f = best_g.astype(F32)
    sub8 = lax.broadcasted_iota(jnp.int32, (SUBLANES, tm), 0)
    extra = jnp.where(sub8 == 0, grp_f, jnp.where(sub8 == 1, rank, 0.0))
    info = jnp.concatenate([gate_t, extra, jnp.zeros((LANES - n_exp - SUBLANES, tm), F32)], axis=0).T
    gate = info
    epg = EXPERTS_PER_GROUP

    def expert_hidden(rows_b, gate_rows, e, dst):
        gate_pre = _dot(rows_b, wg_ref[e])
        up = _dot(rows_b, wu_ref[e])
        hid = gate_pre * jax.nn.sigmoid(gate_pre) * up * gate_rows[:, e:e + 1]
        hid_scr[0:rows_b.shape[0], dst * d_ff:(dst + 1) * d_ff] = hid.astype(BF16)

    def dense(_):
        for e in range(n_exp):
            expert_hidden(hb, gate, e, e)
        return _dot(hid_scr[...], wd_ref[...])


    def grouped(_):
        grp_c = info[:, n_exp:n_exp + 1]
        rank_c = info[:, n_exp + 1:n_exp + 2]
        slot_c = lax.broadcasted_iota(jnp.int32, (cap, 1), 0).astype(F32)
        slot_r = lax.broadcasted_iota(jnp.int32, (1, cap), 1).astype(F32)
        gate_hi, gate_lo = _split_bf16(gate)
        y = jnp.zeros((tm, x.shape[1]), F32)
        for g in range(N_GROUPS):
            take = jnp.where(jnp.where(grp_f == g, rank, -1.0) == slot_c, 1.0, 0.0).astype(BF16)
            give = jnp.where(jnp.where(grp_c == g, rank_c, -1.0) == slot_r, 1.0, 0.0).astype(BF16)
            rows_b = _dot(take, hb).astype(BF16)
            gate_rows = _dot(take, gate_hi) + _dot(take, gate_lo)
            for j in range(epg):
                expert_hidden(rows_b, gate_rows, g * epg + j, j)
            out = _dot(hid_scr[0:cap, 0:epg * d_ff], wd_ref[g * epg * d_ff:(g + 1) * epg * d_ff, :])
            y = y + _dot(give, out.astype(BF16))
        return y

    y = lax.cond(count <= cap, grouped, dense, 0)
    z = alpha * x + (1.0 + gf_ref[0]) * y
    o_ref[0] = _layer_norm(z, g_ref[...], b_ref[...])


def _moe(x, sc, sh, g_f, w_router, b_router, w_gate, w_up, w_down, layer, tri, ln_g, ln_b, tm, cap, alpha):
    bsz, seq, d = x.shape
    _, n_exp, _, d_ff = w_gate.shape
    row = lambda b, i: (b, i, 0)
    per_b = lambda b, i: (b, 0, 0)
    const = lambda b, i: (0, 0)
    of_layer4 = lambda b, i: (layer, 0, 0, 0)
    of_layer3 = lambda b, i: (layer, 0, 0)
    resident = pl.Buffered(1)
    return pl.pallas_call(
        functools.partial(_moe_kernel, alpha=alpha, cap=cap),
        grid=(bsz, seq // tm),
        in_specs=[pl.BlockSpec((1, tm, d), row),
                  pl.BlockSpec((1, 1, d), per_b),
                  pl.BlockSpec((1, 1, d), per_b),
                  pl.BlockSpec((1, 1, d), per_b),
                  pl.BlockSpec((d, LANES), const),
                  pl.BlockSpec((n_exp, 1), const),
                  pl.BlockSpec((None, n_exp, d, d_ff), of_layer4, pipeline_mode=resident),
                  pl.BlockSpec((None, n_exp, d, d_ff), of_layer4, pipeline_mode=resident),
                  pl.BlockSpec((None, n_exp * d_ff, d), of_layer3, pipeline_mode=resident),
                  pl.BlockSpec((tm, tm), const, pipeline_mode=resident),
                  pl.BlockSpec((1, d), const),
                  pl.BlockSpec((1, d), const)],
        out_specs=pl.BlockSpec((1, tm, d), row),
        out_shape=jax.ShapeDtypeStruct((bsz, seq, d), F32),
        scratch_shapes=[pltpu.VMEM((tm, n_exp * d_ff), BF16)],
        compiler_params=_params(("parallel", "parallel")),
        name="moe",
    )(x, sc, sh, g_f, w_router, b_router, w_gate, w_up, w_down, tri, ln_g, ln_b)


def _pick(n, pref):
    t = min(pref, n)
    while n % t:
        t //= 2
    return t


def _rope_tables(positions):
    half = HEAD_DIM // 2
    inv_freq = ROPE_THETA ** (-jnp.arange(half, dtype=F32) / half)
    ang = positions.astype(F32)[..., None] * inv_freq
    cos, sin = jnp.cos(ang), jnp.sin(ang)
    reps = LANES // HEAD_DIM
    return (jnp.concatenate([cos, cos] * reps, axis=-1),
            jnp.concatenate([-sin, sin] * reps, axis=-1))


def _prep_w_kernel(w_ref, o_ref):
    o_v = 2 * MIX_ATTN
    o_iq = o_v + MIX_ATTN
    o_ik = o_iq + IQ_WIDTH
    o_iw = o_ik + IDX_DIM
    o_r = o_iw + N_IDX_HEADS
    o_g = o_r + R_WIDTH + M_WIDTH
    n_gate = 2 * N_HEADS_MLSTM

    def put(dst, src, width):
        o_ref[0, :, dst:dst + width] = w_ref[0, :, src:src + width].astype(o_ref.dtype)

    o_ref[0] = jnp.zeros(o_ref.shape[1:], o_ref.dtype)
    put(OFF_Q, 0, 2 * MIX_ATTN)
    put(OFF_IQ, o_iq, IQ_WIDTH)
    put(OFF_R, o_r, R_WIDTH + M_WIDTH)
    put(OFF_IK, o_ik, IDX_DIM)
    put(OFF_IK + IDX_DIM, o_ik, IDX_DIM)
    put(OFF_S + S_IW, o_iw, N_IDX_HEADS)
    put(OFF_S + S_MI, o_g, n_gate)
    put(OFF_V, o_v, MIX_ATTN)


def _prep_w_in(w_in, tr):
    depth, d, n = w_in.shape
    return pl.pallas_call(
        _prep_w_kernel,
        grid=(depth, d // tr),
        in_specs=[pl.BlockSpec((1, tr, n), lambda l, i: (l, i, 0))],
        out_specs=pl.BlockSpec((1, tr, W_TOTAL), lambda l, i: (l, i, 0)),
        out_shape=jax.ShapeDtypeStruct((depth, d, W_TOTAL), F32),
        compiler_params=_params(("parallel", "parallel")),
        name="prep_w_in",
    )(w_in)


def kernel(x, c, positions, w_ada, b_ada, w_in, i_bias, f_bias, conv_w, conv_b, w_out, ln_mix_g, ln_mix_b,
           w_router, b_router, w_gate, w_up, w_down, ln_ffn_g, ln_ffn_b):
    bsz, seq, d = x.shape
    depth = w_ada.shape[0]
    alpha = (2.0 * depth) ** 0.25

    tm = _pick(seq, 512)
    tq = _pick(seq, 256)
    kb = _pick(seq, 256)
    assert seq // 16 <= 256, "packed bf16 partial counts in the DSA threshold search must stay exact"
    cr = _pick(seq, 256)
    cm = _pick(seq, 256)
    tmoe = _pick(seq, 512)
    moe_cap = min(tmoe, (5 * tmoe // (4 * N_GROUPS) + 15) // 16 * 16)

    cos_t, sin_t = _rope_tables(positions)
    c_pad = jnp.zeros((8, d), F32).at[:bsz].set(c)
    mod = _ada_mod(c_pad, w_ada, b_ada, _pick(6 * d, 1536))
    w_in_p = _prep_w_in(w_in, _pick(d, 256)).astype(BF16)
    w_out_b = w_out.astype(BF16)
    w_gate_b, w_up_b = w_gate.astype(BF16), w_up.astype(BF16)
    w_down_b = w_down.astype(BF16).reshape(depth, -1, d)
    grp = 2 if bsz % 2 == 0 else 1
    grp_ret = 4 if bsz % 4 == 0 else grp

    tril_kb = (jnp.arange(kb)[:, None] >= jnp.arange(kb)[None, :]).astype(BF16)
    tril = (jnp.arange(cm)[:, None] >= jnp.arange(cm)[None, :]).astype(F32)
    w_router_p = jnp.zeros((d, LANES), F32).at[:, :N_EXPERTS].set(w_router)
    b_router_p = b_router.reshape(N_EXPERTS, 1)
    tri_moe = (jnp.arange(tmoe)[:, None] < jnp.arange(tmoe)[None, :]).astype(BF16)

    for l in range(depth):
        parts = [mod[l, :bsz, j * d:(j + 1) * d].reshape(bsz, 1, d) for j in range(6)]
        sh_m, sc_m, g_m, sh_f, sc_f, g_f = parts
        q, k, iq, ik2, r_proj, m_proj, small, small_t, v_t = _in_proj(
            x, sc_m, sh_m, w_in_p, l, cos_t, sin_t, tm, kb)
        o_a = _dsa(q, k, iq, ik2, small_t, v_t, tril_kb, tq, kb)
        o_b = _retention(r_proj, cr, grp_ret)
        gate_bias = (jnp.zeros((1, S_WIDTH), F32).at[0, S_MI:S_MI + N_HEADS_MLSTM].set(i_bias[l])
                     .at[0, S_MF:S_MF + N_HEADS_MLSTM].set(f_bias[l]))
        o_c = _mlstm(m_proj, small, gate_bias, conv_w[l], conv_b[l].reshape(1, -1), tril, cm, grp)
        x = _out_proj(o_a, o_b, o_c, w_out_b, l, x, g_m,
                      ln_mix_g[l].reshape(1, d), ln_mix_b[l].reshape(1, d), tm, alpha)
        x = _moe(x, sc_f, sh_f, g_f, w_router_p, b_router_p,
                 w_gate_b, w_up_b, w_down_b, l, tri_moe,
                 ln_ffn_g[l].reshape(1, d), ln_ffn_b[l].reshape(1, d), tmoe, moe_cap, alpha)
    return x
```

```python
import functools

import numpy as np
import jax
import jax.numpy as jnp
from jax import lax
from jax.experimental import pallas as pl
from jax.experimental.pallas import tpu as pltpu

F32 = jnp.float32
BF16 = jnp.bfloat16

HEAD_DIM = 64
CHUNK = 64
N_HEADS_ATTN = 8
N_IDX_HEADS = 4
IDX_DIM = 64
TOPK_MAX = 256
N_HEADS_RET = 4
N_HEADS_MLSTM = 4
CONV_WIDTH = 4
ROPE_THETA = 10000.0
N_EXPERTS = 16
N_GROUPS = 4
EXPERTS_PER_GROUP = N_EXPERTS // N_GROUPS
D_FF_EXPERT = 256
LN_EPS = 1e-5

MIX_ATTN = N_HEADS_ATTN * HEAD_DIM
MIX_RET = N_HEADS_RET * HEAD_DIM
MIX_MLSTM = N_HEADS_MLSTM * HEAD_DIM

LANES = 128
SUBLANES = 8
VMEM_LIMIT = 56 * 1024 * 1024

IQ_WIDTH = N_IDX_HEADS * IDX_DIM
R_WIDTH = 4 * MIX_RET
M_WIDTH = 4 * MIX_MLSTM
S_WIDTH = LANES
S_IW = 0
S_MI = S_IW + N_IDX_HEADS
S_MF = S_MI + N_HEADS_MLSTM
S_ROWS = 16
V_ROWS = HEAD_DIM + 16
OFF_Q = 0
OFF_K = OFF_Q + MIX_ATTN
OFF_IQ = OFF_K + MIX_ATTN
OFF_R = OFF_IQ + IQ_WIDTH
OFF_M = OFF_R + R_WIDTH
OFF_IK = OFF_M + M_WIDTH
OFF_S = OFF_IK + LANES
OFF_V = OFF_S + S_WIDTH
W_TOTAL = OFF_V + MIX_ATTN

INT_MIN = -2 ** 31
NEG_BIG = -1e30
LOG2_E = 1.4426950408889634


def _dot(a, b):
    return jnp.dot(a, b, preferred_element_type=F32)


def _dot_nt(a, b):
    return lax.dot_general(a, b, (((1,), (1,)), ((), ())), preferred_element_type=F32)


def _dot_tn(a, b):
    return lax.dot_general(a, b, (((0,), (0,)), ((), ())), preferred_element_type=F32)


def _params(sem):
    return pltpu.CompilerParams(dimension_semantics=sem, vmem_limit_bytes=VMEM_LIMIT)


def _ada_kernel(c_ref, w_ref, b_ref, o_ref):
    c = c_ref[...]
    c_act = c * jax.nn.sigmoid(c)
    o_ref[0] = _dot(c_act, w_ref[0]) + b_ref[0]


def _ada_mod(c_pad, w_ada, b_ada, tn):
    depth, d, n = w_ada.shape
    rows = c_pad.shape[0]
    return pl.pallas_call(
        _ada_kernel,
        grid=(depth, n // tn),
        in_specs=[pl.BlockSpec((rows, d), lambda l, j: (0, 0)),
                  pl.BlockSpec((1, d, tn), lambda l, j: (l, 0, j)),
                  pl.BlockSpec((1, 1, tn), lambda l, j: (l, 0, j))],
        out_specs=pl.BlockSpec((1, rows, tn), lambda l, j: (l, 0, j)),
        out_shape=jax.ShapeDtypeStruct((depth, rows, n), F32),
        compiler_params=_params(("parallel", "parallel")),
        name="ada_mod",
    )(c_pad, w_ada, b_ada.reshape(depth, 1, n))


def _rope(y, cos, sin):
    w = y.shape[1]
    reps = w // LANES
    cosw = jnp.concatenate([cos] * reps, axis=1) if reps > 1 else cos
    sinw = jnp.concatenate([sin] * reps, axis=1) if reps > 1 else sin
    lane = lax.broadcasted_iota(jnp.int32, y.shape, 1)
    first = (lane % HEAD_DIM) < (HEAD_DIM // 2)
    partner = jnp.where(first, pltpu.roll(y, w - HEAD_DIM // 2, 1), pltpu.roll(y, HEAD_DIM // 2, 1))
    return y * cosw + partner * sinw


def _inproj_kernel(x_ref, sc_ref, sh_ref, w_ref, cos_ref, sin_ref,
                   q_ref, k_ref, iq_ref, ik_ref, r_ref, m_ref, s_ref, st_ref, vt_ref, *, kb):
    h = (x_ref[0] * (1.0 + sc_ref[0]) + sh_ref[0]).astype(BF16)
    cos = cos_ref[0]
    sin = sin_ref[0]

    def proj(start, width):
        return _dot(h, w_ref[:, start:start + width])

    q_ref[0] = (_rope(proj(OFF_Q, MIX_ATTN), cos, sin) * (HEAD_DIM ** -0.5 * LOG2_E)).astype(BF16)
    k_ref[0] = _rope(proj(OFF_K, MIX_ATTN), cos, sin).astype(BF16)
    iq_ref[0] = _rope(proj(OFF_IQ, IQ_WIDTH), cos, sin).astype(BF16)
    ik_ref[0] = _rope(proj(OFF_IK, LANES), cos, sin).astype(BF16)
    r_ref[0, :, 0:2 * MIX_RET] = _rope(proj(OFF_R, 2 * MIX_RET), cos, sin).astype(BF16)
    r_ref[0, :, 2 * MIX_RET:R_WIDTH] = proj(OFF_R + 2 * MIX_RET, 2 * MIX_RET).astype(BF16)
    m_ref[0, :, 0:2 * MIX_MLSTM] = proj(OFF_M, 2 * MIX_MLSTM).astype(BF16)
    m_ref[0, :, 2 * MIX_MLSTM:M_WIDTH] = proj(OFF_M + 2 * MIX_MLSTM, 2 * MIX_MLSTM).astype(BF16)
    y = proj(OFF_S, S_WIDTH)
    s_ref[0] = y
    st_ref[0] = y.T[0:S_ROWS, :]
    tm = h.shape[0]
    yvt = proj(OFF_V, MIX_ATTN).T.astype(BF16)
    pad_rows = lax.broadcasted_iota(jnp.int32, (V_ROWS - HEAD_DIM, kb), 0)
    ones_rows = jnp.where(pad_rows == 0, 1.0, 0.0).astype(BF16)
    for j in range(tm // kb):
        for hh in range(N_HEADS_ATTN):
            vt_ref[0, j, hh * V_ROWS:hh * V_ROWS + HEAD_DIM, :] = yvt[hh * HEAD_DIM:(hh + 1) * HEAD_DIM,
                                                                      j * kb:(j + 1) * kb]
            vt_ref[0, j, hh * V_ROWS + HEAD_DIM:(hh + 1) * V_ROWS, :] = ones_rows


def _in_proj(x, sc, sh, w, layer, cos_t, sin_t, tm, kb):
    bsz, seq, d = x.shape
    row = lambda b, i: (b, i, 0)
    per_b = lambda b, i: (b, 0, 0)
    widths = (MIX_ATTN, MIX_ATTN, IQ_WIDTH, LANES, R_WIDTH, M_WIDTH)
    return pl.pallas_call(
        functools.partial(_inproj_kernel, kb=kb),
        grid=(bsz, seq // tm),
        in_specs=[pl.BlockSpec((1, tm, d), row),
                  pl.BlockSpec((1, 1, d), per_b),
                  pl.BlockSpec((1, 1, d), per_b),
                  pl.BlockSpec((None, d, W_TOTAL), lambda b, i: (layer, 0, 0)),
                  pl.BlockSpec((1, tm, LANES), row),
                  pl.BlockSpec((1, tm, LANES), row)],
        out_specs=[pl.BlockSpec((1, tm, wd), row) for wd in widths]
                  + [pl.BlockSpec((1, tm, S_WIDTH), row),
                     pl.BlockSpec((1, S_ROWS, tm), lambda b, i: (b, 0, i)),
                     pl.BlockSpec((1, tm // kb, N_HEADS_ATTN * V_ROWS, kb), lambda b, i: (b, i, 0, 0))],
        out_shape=[jax.ShapeDtypeStruct((bsz, seq, wd), BF16) for wd in widths]
                  + [jax.ShapeDtypeStruct((bsz, seq, S_WIDTH), F32),
                     jax.ShapeDtypeStruct((bsz, S_ROWS, seq), F32),
                     jax.ShapeDtypeStruct((bsz, seq // kb, N_HEADS_ATTN * V_ROWS, kb), BF16)],
        compiler_params=_params(("parallel", "parallel")),
        name="in_proj",
    )(x, sc, sh, w, cos_t, sin_t)


def _dsa_kernel(q_ref, k_ref, iq_ref, ik_ref, st_ref, vt_ref, tril_ref, o_ref,
                key_scr, byte_scr, cand_scr, bias_scr, s_scr, p_scr, m_scr, alpha_scr, acc_scr,
                *, tq, kb, topk):
    q0 = pl.program_id(1) * tq
    n_blocks = (q0 + tq + kb - 1) // kb
    qpos = q0 + lax.broadcasted_iota(jnp.int32, (1, tq), 1)
    q_limit = (qpos // CHUNK + 1) * CHUNK
    krow = lax.broadcasted_iota(jnp.int32, (kb, 1), 0)

    def head_of_pair(x, h):
        pair = x[:, (h // 2) * LANES:(h // 2 + 1) * LANES]
        lane = lax.broadcasted_iota(jnp.int32, pair.shape, 1)
        keep = (lane < HEAD_DIM) if h % 2 == 0 else (lane >= HEAD_DIM)
        return jnp.where(keep, pair, jnp.zeros_like(pair))

    iw = st_ref[0][S_IW:S_IW + N_IDX_HEADS, :] * (N_IDX_HEADS ** -0.5 * IDX_DIM ** -0.5)
    iq = iq_ref[0]
    iq_heads = [head_of_pair(iq, h) for h in range(N_IDX_HEADS)]

    def score_block(c):
        k0 = pl.multiple_of(c * kb, kb)
        ik2 = ik_ref[0, pl.ds(k0, kb), :]
        score = jnp.zeros((kb, tq), F32)
        for h in range(N_IDX_HEADS):
            score = score + jnp.maximum(_dot_nt(ik2, iq_heads[h]), 0.0) * iw[h:h + 1, :]
        bits = pltpu.bitcast(score, jnp.int32)
        key = jnp.where(bits >= 0, bits, bits ^ jnp.int32(0x7FFFFFFF))
        key = jnp.where(k0 + krow < q_limit, key, jnp.int32(INT_MIN))
        key_scr[c] = key
        byte_scr[0, c] = ((key >> 24) + 128).astype(F32).astype(BF16)
        for lvl in range(1, 4):
            byte_scr[lvl, c] = ((key >> (24 - 8 * lvl)) & 255).astype(F32).astype(BF16)

    def score_pair(c2, carry):
        score_block(2 * c2)
        score_block(2 * c2 + 1)
        return carry

    lax.fori_loop(0, n_blocks // 2, score_pair, 0)

    @pl.when(n_blocks % 2 == 1)
    def _():
        score_block(n_blocks - 1)

    pack = 16
    one = jnp.ones((kb, tq), BF16)
    zero = jnp.zeros((kb, tq), BF16)

    def count_ge(lvl, cand, narrow_to=None):
        cand_b = cand.astype(BF16)

        def hits(c):
            plane = byte_scr[0, c] if lvl == 0 else cand_scr[c]
            hit = jnp.where(plane >= cand_b, one, zero)
            if narrow_to is not None:
                cand_scr[c] = jnp.where(plane == narrow_to, byte_scr[lvl + 1, c], -one)
            parts = [hit[i * pack:(i + 1) * pack, :] for i in range(kb // pack)]
            while len(parts) > 1:
                parts = [parts[i] + parts[i + 1] for i in range(0, len(parts), 2)]
            return parts[0]

        acc = lax.fori_loop(0, n_blocks // 2, lambda c2, a: a + hits(2 * c2) + hits(2 * c2 + 1),
                            jnp.zeros((pack, tq), BF16))
        acc = lax.cond(n_blocks % 2 == 1, lambda a: a + hits(n_blocks - 1), lambda a: a, acc)
        return jnp.sum(acc.astype(F32), axis=0, keepdims=True)

    above = jnp.zeros((1, tq), F32)
    t = jnp.zeros((1, tq), jnp.int32)
    for lvl in range(4):
        def bit_body(i, carry, lvl=lvl, above=above):
            v, rejected = carry
            cand = v + lax.shift_left(jnp.int32(1), 7 - i).astype(F32)
            cnt = count_ge(lvl, cand)
            ok = above + cnt >= topk
            return jnp.where(ok, cand, v), jnp.where(ok, rejected, cnt)

        v, rejected = lax.fori_loop(0, 8, bit_body, (jnp.zeros((1, tq), F32), jnp.zeros((1, tq), F32)))
        if lvl < 3:
            count_ge(lvl, v + 1.0, narrow_to=v.astype(BF16))
        else:
            at_thr = count_ge(lvl, v) - rejected
        above = above + rejected
        t = t | lax.shift_left(v.astype(jnp.int32), 24 - 8 * lvl)
    thr = jnp.maximum(t ^ jnp.int32(INT_MIN), jnp.int32(INT_MIN + 1))
    need = topk - above
    surplus = jnp.max(jnp.where(t == 0, -1.0, at_thr - need))

    def mask_plain(_):
        def body(c, carry):
            bias_scr[c] = jnp.where(key_scr[c] >= thr, 0.0, NEG_BIG)
            return carry
        lax.fori_loop(0, n_blocks, body, 0)
        return 0

    def mask_ranked(_):
        def body(c, ties_before):
            key = key_scr[c]
            tie = key == thr
            rank = _dot(tril_ref[...], jnp.where(tie, 1.0, 0.0).astype(BF16)) + ties_before
            sel = (key > thr) | (tie & (rank <= need))
            bias_scr[c] = jnp.where(sel, 0.0, NEG_BIG)
            return rank[kb - 1:kb, :]
        lax.fori_loop(0, n_blocks, body, jnp.zeros((1, tq), F32))
        return 0

    lax.cond(surplus > 0, mask_ranked, mask_plain, 0)

    q = q_ref[0]
    q_heads = [head_of_pair(q, h) for h in range(N_HEADS_ATTN)]
    m_scr[...] = jnp.full(m_scr.shape, NEG_BIG, F32)
    alpha_scr[...] = jnp.ones(alpha_scr.shape, F32)
    acc_scr[...] = jnp.zeros(acc_scr.shape, F32)
    p_scr[...] = jnp.zeros(p_scr.shape, BF16)

    def stage_logits(c, h):
        k0 = pl.multiple_of(c * kb, kb)
        kp = k_ref[0, pl.ds(k0, kb), (h // 2) * LANES:(h // 2 + 1) * LANES]
        s_scr[h] = _dot_nt(kp, q_heads[h])

    def stage_softmax(c, h):
        for half in range(tq // LANES):
            ln = slice(half * LANES, (half + 1) * LANES)
            s = s_scr[h, :, ln] + bias_scr[c, :, ln]
            m_old = m_scr[h, :, ln]
            m_new = jnp.maximum(m_old, jnp.max(s, axis=0, keepdims=True))
            p_scr[h, :, ln] = jnp.exp2(s - m_new).astype(BF16)
            alpha_scr[h, :, ln] = jnp.exp2(m_old - m_new)
            m_scr[h, :, ln] = m_new

    def stage_values(c, h):
        vt = vt_ref[0, c, h * V_ROWS:(h + 1) * V_ROWS, :]
        acc_scr[h, 0:V_ROWS, :] = alpha_scr[h] * acc_scr[h, 0:V_ROWS, :] + _dot(vt, p_scr[h])

    for h in range(N_HEADS_ATTN):
        stage_logits(0, h)

    def attn_body(j, carry):
        c_old = jnp.maximum(j - 2, 0)
        for h in range(N_HEADS_ATTN):
            stage_values(c_old, h)
            stage_softmax(j - 1, h)
            stage_logits(j, h)
        return carry

    lax.fori_loop(1, n_blocks, attn_body, 0)
    for h in range(N_HEADS_ATTN):
        stage_values(jnp.maximum(n_blocks - 2, 0), h)
        stage_softmax(n_blocks - 1, h)
    for h in range(N_HEADS_ATTN):
        stage_values(n_blocks - 1, h)
    for h in range(N_HEADS_ATTN):
        acc = acc_scr[h]
        out = acc * (1.0 / acc[HEAD_DIM:HEAD_DIM + 1, :])
        o_ref[0, :, h * HEAD_DIM:(h + 1) * HEAD_DIM] = out.T[:, 0:HEAD_DIM].astype(o_ref.dtype)


def _dsa(q, k, iq, ik2, small_t, v_t, tril, tq, kb):
    bsz, seq, _ = q.shape
    topk = min(TOPK_MAX, seq // 4)
    kern = functools.partial(_dsa_kernel, tq=tq, kb=kb, topk=topk)
    return pl.pallas_call(
        kern,
        grid=(bsz, seq // tq),
        in_specs=[pl.BlockSpec((1, tq, MIX_ATTN), lambda b, i: (b, i, 0)),
                  pl.BlockSpec((1, seq, MIX_ATTN), lambda b, i: (b, 0, 0)),
                  pl.BlockSpec((1, tq, IQ_WIDTH), lambda b, i: (b, i, 0)),
                  pl.BlockSpec((1, seq, LANES), lambda b, i: (b, 0, 0)),
                  pl.BlockSpec((1, S_ROWS, tq), lambda b, i: (b, 0, i)),
                  pl.BlockSpec((1, seq // kb, N_HEADS_ATTN * V_ROWS, kb), lambda b, i: (b, 0, 0, 0)),
                  pl.BlockSpec((kb, kb), lambda b, i: (0, 0))],
        out_specs=pl.BlockSpec((1, tq, MIX_ATTN), lambda b, i: (b, i, 0)),
        out_shape=jax.ShapeDtypeStruct((bsz, seq, MIX_ATTN), BF16),
        scratch_shapes=[pltpu.VMEM((seq // kb, kb, tq), jnp.int32),
                        pltpu.VMEM((4, seq // kb, kb, tq), BF16),
                        pltpu.VMEM((seq // kb, kb, tq), BF16),
                        pltpu.VMEM((seq // kb, kb, tq), F32),
                        pltpu.VMEM((N_HEADS_ATTN, kb, tq), F32),
                        pltpu.VMEM((N_HEADS_ATTN, kb, tq), BF16),
                        pltpu.VMEM((N_HEADS_ATTN, 1, tq), F32),
                        pltpu.VMEM((N_HEADS_ATTN, 1, tq), F32),
                        pltpu.VMEM((N_HEADS_ATTN, LANES, tq), F32)],
        compiler_params=_params(("parallel", "arbitrary")),
        name="dsa",
    )(q, k, iq, ik2, small_t, v_t, tril)


def _head_norm(y):
    mean_mat = jnp.full((HEAD_DIM, HEAD_DIM), 1.0 / HEAD_DIM, BF16)

    def mean_bcast(x):
        hi = x.astype(BF16)
        lo = (x - hi.astype(F32)).astype(BF16)
        return _dot(hi, mean_mat) + _dot(lo, mean_mat)

    yc = y - mean_bcast(y)
    return yc * lax.rsqrt(mean_bcast(yc * yc) + LN_EPS)


def _ret_kernel(r_ref, o_ref, state_scr, *, cr, grp):
    @pl.when(pl.program_id(1) == 0)
    def _():
        state_scr[...] = jnp.zeros_like(state_scr)

    ri = lax.broadcasted_iota(jnp.int32, (cr, cr), 0)
    ci = lax.broadcasted_iota(jnp.int32, (cr, cr), 1)
    diff = (ri - ci).astype(F32)
    pos = lax.broadcasted_iota(jnp.int32, (cr, 1), 0).astype(F32)
    items = [(h, g) for h in range(N_HEADS_RET) for g in range(grp)]
    sl = lambda part, h: slice(part * MIX_RET + h * HEAD_DIM, part * MIX_RET + (h + 1) * HEAD_DIM)
    log_gamma = [jnp.log1p(jnp.full((1, 1), -(2.0 ** (-5.0 - h)), F32)) for h in range(N_HEADS_RET)]
    decay_in = [jnp.where(diff >= 0, jnp.exp(diff * lg), 0.0) * (HEAD_DIM ** -0.5) for lg in log_gamma]
    q = {(h, g): r_ref[g, :, sl(0, h)] for h, g in items}
    k = {(h, g): r_ref[g, :, sl(1, h)] for h, g in items}
    v = {(h, g): r_ref[g, :, sl(2, h)] for h, g in items}
    state = {(h, g): state_scr[g, h] for h, g in items}
    scores = {it: _dot_nt(q[it], k[it]) * decay_in[it[0]] for it in items}
    cross = {it: jnp.exp((pos + 1.0) * log_gamma[it[0]]) * _dot(q[it], state[it].astype(BF16)) for it in items}
    inner = {it: _dot(scores[it].astype(BF16), v[it]) for it in items}
    for it in items:
        h, g = it
        gate = r_ref[g, :, sl(3, h)].astype(F32)
        y = _head_norm(inner[it] + cross[it])
        o_ref[g, :, h * HEAD_DIM:(h + 1) * HEAD_DIM] = (y * (gate * jax.nn.sigmoid(gate))).astype(o_ref.dtype)
    for it in items:
        h, g = it
        k_decay = (HEAD_DIM ** -0.5) * jnp.exp((cr - 1.0 - pos) * log_gamma[h])
        k_dec = (k[it].astype(F32) * k_decay).astype(BF16)
        state_scr[g, h] = state[it] * jnp.exp(cr * log_gamma[h]) + _dot_tn(k_dec, v[it])


def _retention(r_proj, cr, grp):
    bsz, seq, _ = r_proj.shape
    return pl.pallas_call(
        functools.partial(_ret_kernel, cr=cr, grp=grp),
        grid=(bsz // grp, seq // cr),
        in_specs=[pl.BlockSpec((grp, cr, R_WIDTH), lambda b, i: (b, i, 0))],
        out_specs=pl.BlockSpec((grp, cr, MIX_RET), lambda b, i: (b, i, 0)),
        out_shape=jax.ShapeDtypeStruct((bsz, seq, MIX_RET), BF16),
        scratch_shapes=[pltpu.VMEM((grp, N_HEADS_RET, HEAD_DIM, HEAD_DIM), F32)],
        compiler_params=_params(("parallel", "arbitrary")),
        name="retention",
    )(r_proj)


def _mlstm_kernel(m_ref, sm_ref, bias_ref, cw_ref, cb_ref, tril_ref, o_ref,
                  xbuf, a_scr, m_scr, *, cm, grp):
    halo = 8
    gate_shift = S_MF - S_MI

    @pl.when(pl.program_id(1) == 0)
    def _():
        xbuf[:, 0:halo, :] = jnp.zeros((grp, halo, 2 * MIX_MLSTM), F32)
        a_scr[...] = jnp.zeros_like(a_scr)
        m_scr[...] = jnp.zeros_like(m_scr)

    ri = lax.broadcasted_iota(jnp.int32, (cm, cm), 0)
    ci = lax.broadcasted_iota(jnp.int32, (cm, cm), 1)
    causal = ri >= ci
    row = lax.broadcasted_iota(jnp.int32, (cm, LANES), 0)
    lane = lax.broadcasted_iota(jnp.int32, (cm, LANES), 1)
    gate_lanes = (lane >= S_MI) & (lane < S_MI + N_HEADS_MLSTM)
    lane64 = lax.broadcasted_iota(jnp.int32, (cm, HEAD_DIM), 1)
    ones_col = jnp.where(lane64 == 0, 1.0, 0.0).astype(BF16)
    scale = HEAD_DIM ** -0.5

    def head_cols(x, h):
        return x[:, h * HEAD_DIM:(h + 1) * HEAD_DIM]

    items = [(g, h) for g in range(grp) for h in range(N_HEADS_MLSTM)]
    mm, qk, u_row, m_col, inter, e_inv, kw_col, decay = {}, {}, {}, {}, {}, {}, {}, {}
    for g in range(grp):
        mm[g] = m_ref[g]
        xbuf[g, halo:halo + cm, :] = mm[g][:, 0:2 * MIX_MLSTM].astype(F32)
        conv = cb_ref[...]
        for j in range(CONV_WIDTH):
            off = halo - (CONV_WIDTH - 1) + j
            conv = conv + xbuf[g, off:off + cm, :] * cw_ref[j:j + 1, :]
        xbuf[g, 0:halo, :] = xbuf[g, cm:cm + halo, :]
        qk[g] = conv * jax.nn.sigmoid(conv)

        gates = sm_ref[g] + bias_ref[...]
        b_all = jnp.dot(tril_ref[...], jax.nn.log_sigmoid(gates), preferred_element_type=F32,
                        precision=lax.Precision.HIGHEST)
        b_i = jnp.where(gate_lanes, pltpu.roll(b_all, LANES - gate_shift, 1), 0.0)
        u = jnp.where(gate_lanes, gates, 0.0) - b_i
        run = u
        step = 1
        while step < cm:
            run = jnp.maximum(run, jnp.where(row >= step, pltpu.roll(run, step, 0), -jnp.inf))
            step *= 2
        m_prev = m_scr[g]
        m_c = jnp.maximum(m_prev, run)
        m_last = m_c[cm - 1:cm, :]
        u_row[g] = u.T
        m_col[g] = m_c
        inter[g] = jnp.exp(m_prev - m_c)
        e_inv[g] = jnp.exp(-(b_i + m_c))
        kw_col[g] = scale * jnp.exp(u - m_last)
        decay[g] = jnp.exp(m_prev - m_last)
        m_scr[g] = b_i[cm - 1:cm, :] + m_last

    q, k, v_aug, a_mem = {}, {}, {}, {}
    for it in items:
        g, h = it
        q[it] = head_cols(qk[g], h).astype(BF16)
        k[it] = head_cols(qk[g][:, MIX_MLSTM:], h)
        v_aug[it] = jnp.concatenate([head_cols(mm[g][:, 2 * MIX_MLSTM:], h), ones_col], axis=1)
        a_mem[it] = a_scr[g, h]

    def col(x, h):
        return x[:, S_MI + h:S_MI + h + 1]

    w = {}
    for it in items:
        g, h = it
        u_r = u_row[g][S_MI + h:S_MI + h + 1, :]
        w[it] = jnp.exp(jnp.where(causal, u_r - col(m_col[g], h), -jnp.inf))

    s = {it: _dot_nt(q[it], k[it].astype(BF16)) * scale * w[it] for it in items}
    cross = {it: _dot(q[it], a_mem[it].astype(BF16)) for it in items}
    both = {it: _dot(s[it].astype(BF16), v_aug[it]) + col(inter[it[0]], it[1]) * cross[it] for it in items}
    for it in items:
        g, h = it
        den = both[it][:, HEAD_DIM:HEAD_DIM + 1]
        h_tilde = both[it][:, 0:HEAD_DIM] * (1.0 / jnp.maximum(jnp.abs(den), col(e_inv[g], h)))
        og = head_cols(mm[g][:, 3 * MIX_MLSTM:], h).astype(F32)
        o_ref[g, :, h * HEAD_DIM:(h + 1) * HEAD_DIM] = _head_norm(jax.nn.sigmoid(og) * h_tilde).astype(o_ref.dtype)

    for it in items:
        g, h = it
        kw = k[it] * col(kw_col[g], h)
        a_scr[g, h] = col(decay[g], h) * a_mem[it] + _dot_tn(kw.astype(BF16), v_aug[it])


def _mlstm(m_proj, small, gate_bias, conv_w, conv_b, tril, cm, grp):
    bsz, seq, _ = m_proj.shape
    const = lambda b, i: (0, 0)
    return pl.pallas_call(
        functools.partial(_mlstm_kernel, cm=cm, grp=grp),
        grid=(bsz // grp, seq // cm),
        in_specs=[pl.BlockSpec((grp, cm, M_WIDTH), lambda b, i: (b, i, 0)),
                  pl.BlockSpec((grp, cm, S_WIDTH), lambda b, i: (b, i, 0)),
                  pl.BlockSpec((1, S_WIDTH), const),
                  pl.BlockSpec((CONV_WIDTH, 2 * MIX_MLSTM), const),
                  pl.BlockSpec((1, 2 * MIX_MLSTM), const),
                  pl.BlockSpec((cm, cm), const)],
        out_specs=pl.BlockSpec((grp, cm, MIX_MLSTM), lambda b, i: (b, i, 0)),
        out_shape=jax.ShapeDtypeStruct((bsz, seq, MIX_MLSTM), BF16),
        scratch_shapes=[pltpu.VMEM((grp, cm + 8, 2 * MIX_MLSTM), F32),
                        pltpu.VMEM((grp, N_HEADS_MLSTM, HEAD_DIM, LANES), F32),
                        pltpu.VMEM((grp, 1, LANES), F32)],
        compiler_params=_params(("parallel", "arbitrary")),
        name="mlstm",
    )(m_proj, small, gate_bias, conv_w, conv_b, tril)


def _layer_norm(z, g, b):
    mu = jnp.mean(z, axis=-1, keepdims=True)
    var = jnp.mean(jnp.square(z - mu), axis=-1, keepdims=True)
    return (z - mu) * lax.rsqrt(var + LN_EPS) * g + b


def _outproj_kernel(oa_ref, ob_ref, oc_ref, w_ref, x_ref, gm_ref, g_ref, b_ref, o_ref, *, alpha):
    mix = _dot(oa_ref[0], w_ref[0:MIX_ATTN, :])
    mix = mix + _dot(ob_ref[0], w_ref[MIX_ATTN:MIX_ATTN + MIX_RET, :])
    mix = mix + _dot(oc_ref[0], w_ref[MIX_ATTN + MIX_RET:, :])
    z = alpha * x_ref[0] + (1.0 + gm_ref[0]) * mix
    o_ref[0] = _layer_norm(z, g_ref[...], b_ref[...])


def _out_proj(o_a, o_b, o_c, w_out, layer, x, g_m, ln_g, ln_b, tm, alpha):
    bsz, seq, d = x.shape
    row = lambda b, i: (b, i, 0)
    const = lambda b, i: (0, 0)
    return pl.pallas_call(
        functools.partial(_outproj_kernel, alpha=alpha),
        grid=(bsz, seq // tm),
        in_specs=[pl.BlockSpec((1, tm, MIX_ATTN), row),
                  pl.BlockSpec((1, tm, MIX_RET), row),
                  pl.BlockSpec((1, tm, MIX_MLSTM), row),
                  pl.BlockSpec((None,) + w_out.shape[1:], lambda b, i: (layer, 0, 0)),
                  pl.BlockSpec((1, tm, d), row),
                  pl.BlockSpec((1, 1, d), lambda b, i: (b, 0, 0)),
                  pl.BlockSpec((1, d), const),
                  pl.BlockSpec((1, d), const)],
        out_specs=pl.BlockSpec((1, tm, d), row),
        out_shape=jax.ShapeDtypeStruct((bsz, seq, d), F32),
        compiler_params=_params(("parallel", "parallel")),
        name="out_proj",
    )(o_a, o_b, o_c, w_out, x, g_m, ln_g, ln_b)


def _route(scores, biased):
    col = lambda a, e: a[e:e + 1, :]
    epg = EXPERTS_PER_GROUP
    group_scores = []
    for g in range(N_GROUPS):
        vals = [col(biased, g * epg + j) for j in range(epg)]
        best = None
        for a in range(epg):
            for b in range(a + 1, epg):
                pair = vals[a] + vals[b]
                best = pair if best is None else jnp.maximum(best, pair)
        group_scores.append(best)
    best_g = jnp.zeros_like(group_scores[0], dtype=jnp.int32)
    best_v = group_scores[0]
    for g in range(1, N_GROUPS):
        better = group_scores[g] > best_v
        best_g = jnp.where(better, g, best_g)
        best_v = jnp.where(better, group_scores[g], best_v)
    cand_b = [sum(jnp.where(best_g == g, col(biased, g * epg + j), 0.0) for g in range(N_GROUPS))
              for j in range(epg)]
    cand_s = [sum(jnp.where(best_g == g, col(scores, g * epg + j), 0.0) for g in range(N_GROUPS))
              for j in range(epg)]

    def argmax_first(vals, skip=None):
        idx = None
        val = None
        for j, vj in enumerate(vals):
            if skip is not None:
                vj = jnp.where(skip == j, -jnp.inf, vj)
            if idx is None:
                idx, val = jnp.zeros_like(best_g), vj
            else:
                better = vj > val
                idx = jnp.where(better, j, idx)
                val = jnp.where(better, vj, val)
        return idx

    first = argmax_first(cand_b)
    second = argmax_first(cand_b, skip=first)
    w1 = sum(jnp.where(first == j, cand_s[j], 0.0) for j in range(epg))
    w2 = sum(jnp.where(second == j, cand_s[j], 0.0) for j in range(epg))
    total = w1 + w2
    e1 = best_g * epg + first
    e2 = best_g * epg + second
    expert = lax.broadcasted_iota(jnp.int32, scores.shape, 0)
    gate = jnp.where(expert == e1, w1 / total, 0.0) + jnp.where(expert == e2, w2 / total, 0.0)
    return gate, best_g


def _split_bf16(x):
    hi = x.astype(BF16)
    return hi, (x - hi.astype(F32)).astype(BF16)


def _moe_kernel(x_ref, sc_ref, sh_ref, gf_ref, wr_ref, br_ref, wg_ref, wu_ref, wd_ref, tri_ref, g_ref, b_ref,
                o_ref, hid_scr, *, alpha, cap):
    x = x_ref[0]
    tm = x.shape[0]
    h = x * (1.0 + sc_ref[0]) + sh_ref[0]
    hb = h.astype(BF16)
    n_exp, _, d_ff = wg_ref.shape
    scores_t = jax.nn.sigmoid(_dot(h, wr_ref[...]).T[0:n_exp, :])
    gate_t, best_g = _route(scores_t, scores_t + br_ref[...])
    sub = lax.broadcasted_iota(jnp.int32, (2 * SUBLANES, tm), 0)
    member_t = jnp.where(sub == best_g, 1.0, 0.0)
    before_t = _dot(member_t.astype(BF16), tri_ref[...])
    rank = jnp.sum(member_t * before_t, axis=0, keepdims=True)
    count = jnp.max(jnp.sum(member_t, axis=1, keepdims=True))
    grp_f = best_g.astype(F32)
    sub8 = lax.broadcasted_iota(jnp.int32, (SUBLANES, tm), 0)
    extra = jnp.where(sub8 == 0, grp_f, jnp.where(sub8 == 1, rank, 0.0))
    info = jnp.concatenate([gate_t, extra, jnp.zeros((LANES - n_exp - SUBLANES, tm), F32)], axis=0).T
    gate = info
    epg = EXPERTS_PER_GROUP

    def expert_hidden(rows_b, gate_rows, e, dst):
        gate_pre = _dot(rows_b, wg_ref[e])
        up = _dot(rows_b, wu_ref[e])
        hid = gate_pre * jax.nn.sigmoid(gate_pre) * up * gate_rows[:, e:e + 1]
        hid_scr[0:rows_b.shape[0], dst * d_ff:(dst + 1) * d_ff] = hid.astype(BF16)

    def dense(_):
        for e in range(n_exp):
            expert_hidden(hb, gate, e, e)
        return _dot(hid_scr[...], wd_ref[...])


    def grouped(_):
        grp_c = info[:, n_exp:n_exp + 1]
        rank_c = info[:, n_exp + 1:n_exp + 2]
        slot_c = lax.broadcasted_iota(jnp.int32, (cap, 1), 0).astype(F32)
        slot_r = lax.broadcasted_iota(jnp.int32, (1, cap), 1).astype(F32)
        gate_hi, gate_lo = _split_bf16(gate)
        y = jnp.zeros((tm, x.shape[1]), F32)
        for g in range(N_GROUPS):
            take = jnp.where(jnp.where(grp_f == g, rank, -1.0) == slot_c, 1.0, 0.0).astype(BF16)
            give = jnp.where(jnp.where(grp_c == g, rank_c, -1.0) == slot_r, 1.0, 0.0).astype(BF16)
            rows_b = _dot(take, hb).astype(BF16)
            gate_rows = _dot(take, gate_hi) + _dot(take, gate_lo)
            for j in range(epg):
                expert_hidden(rows_b, gate_rows, g * epg + j, j)
            out = _dot(hid_scr[0:cap, 0:epg * d_ff], wd_ref[g * epg * d_ff:(g + 1) * epg * d_ff, :])
            y = y + _dot(give, out.astype(BF16))
        return y

    y = lax.cond(count <= cap, grouped, dense, 0)
    z = alpha * x + (1.0 + gf_ref[0]) * y
    o_ref[0] = _layer_norm(z, g_ref[...], b_ref[...])


def _moe(x, sc, sh, g_f, w_router, b_router, w_gate, w_up, w_down, layer, tri, ln_g, ln_b, tm, cap, alpha):
    bsz, seq, d = x.shape
    _, n_exp, _, d_ff = w_gate.shape
    row = lambda b, i: (b, i, 0)
    per_b = lambda b, i: (b, 0, 0)
    const = lambda b, i: (0, 0)
    of_layer4 = lambda b, i: (layer, 0, 0, 0)
    of_layer3 = lambda b, i: (layer, 0, 0)
    resident = pl.Buffered(1)
    return pl.pallas_call(
        functools.partial(_moe_kernel, alpha=alpha, cap=cap),
        grid=(bsz, seq // tm),
        in_specs=[pl.BlockSpec((1, tm, d), row),
                  pl.BlockSpec((1, 1, d), per_b),
                  pl.BlockSpec((1, 1, d), per_b),
                  pl.BlockSpec((1, 1, d), per_b),
                  pl.BlockSpec((d, LANES), const),
                  pl.BlockSpec((n_exp, 1), const),
                  pl.BlockSpec((None, n_exp, d, d_ff), of_layer4, pipeline_mode=resident),
                  pl.BlockSpec((None, n_exp, d, d_ff), of_layer4, pipeline_mode=resident),
                  pl.BlockSpec((None, n_exp * d_ff, d), of_layer3, pipeline_mode=resident),
                  pl.BlockSpec((tm, tm), const, pipeline_mode=resident),
                  pl.BlockSpec((1, d), const),
                  pl.BlockSpec((1, d), const)],
        out_specs=pl.BlockSpec((1, tm, d), row),
        out_shape=jax.ShapeDtypeStruct((bsz, seq, d), F32),
        scratch_shapes=[pltpu.VMEM((tm, n_exp * d_ff), BF16)],
        compiler_params=_params(("parallel", "parallel")),
        name="moe",
    )(x, sc, sh, g_f, w_router, b_router, w_gate, w_up, w_down, tri, ln_g, ln_b)


def _pick(n, pref):
    t = min(pref, n)
    while n % t:
        t //= 2
    return t


def _rope_tables(positions):
    half = HEAD_DIM // 2
    inv_freq = ROPE_THETA ** (-jnp.arange(half, dtype=F32) / half)
    ang = positions.astype(F32)[..., None] * inv_freq
    cos, sin = jnp.cos(ang), jnp.sin(ang)
    reps = LANES // HEAD_DIM
    return (jnp.concatenate([cos, cos] * reps, axis=-1),
            jnp.concatenate([-sin, sin] * reps, axis=-1))


def _prep_w_kernel(w_ref, o_ref):
    o_v = 2 * MIX_ATTN
    o_iq = o_v + MIX_ATTN
    o_ik = o_iq + IQ_WIDTH
    o_iw = o_ik + IDX_DIM
    o_r = o_iw + N_IDX_HEADS
    o_g = o_r + R_WIDTH + M_WIDTH
    n_gate = 2 * N_HEADS_MLSTM

    def put(dst, src, width):
        o_ref[0, :, dst:dst + width] = w_ref[0, :, src:src + width].astype(o_ref.dtype)

    o_ref[0] = jnp.zeros(o_ref.shape[1:], o_ref.dtype)
    put(OFF_Q, 0, 2 * MIX_ATTN)
    put(OFF_IQ, o_iq, IQ_WIDTH)
    put(OFF_R, o_r, R_WIDTH + M_WIDTH)
    put(OFF_IK, o_ik, IDX_DIM)
    put(OFF_IK + IDX_DIM, o_ik, IDX_DIM)
    put(OFF_S + S_IW, o_iw, N_IDX_HEADS)
    put(OFF_S + S_MI, o_g, n_gate)
    put(OFF_V, o_v, MIX_ATTN)


def _prep_w_in(w_in, tr):
    depth, d, n = w_in.shape
    return pl.pallas_call(
        _prep_w_kernel,
        grid=(depth, d // tr),
        in_specs=[pl.BlockSpec((1, tr, n), lambda l, i: (l, i, 0))],
        out_specs=pl.BlockSpec((1, tr, W_TOTAL), lambda l, i: (l, i, 0)),
        out_shape=jax.ShapeDtypeStruct((depth, d, W_TOTAL), F32),
        compiler_params=_params(("parallel", "parallel")),
        name="prep_w_in",
    )(w_in)


def kernel(x, c, positions, w_ada, b_ada, w_in, i_bias, f_bias, conv_w, conv_b, w_out, ln_mix_g, ln_mix_b,
           w_router, b_router, w_gate, w_up, w_down, ln_ffn_g, ln_ffn_b):
    bsz, seq, d = x.shape
    depth = w_ada.shape[0]
    alpha = (2.0 * depth) ** 0.25

    tm = _pick(seq, 512)
    tq = _pick(seq, 256)
    kb = _pick(seq, 256)
    assert seq // 16 <= 256, "packed bf16 partial counts in the DSA threshold search must stay exact"
    cr = _pick(seq, 256)
    cm = _pick(seq, 256)
    tmoe = _pick(seq, 512)
    moe_cap = min(tmoe, (3 * tmoe // (2 * N_GROUPS) + 15) // 16 * 16)

    cos_t, sin_t = _rope_tables(positions)
    c_pad = jnp.zeros((8, d), F32).at[:bsz].set(c)
    mod = _ada_mod(c_pad, w_ada, b_ada, _pick(6 * d, 1536))
    w_in_p = _prep_w_in(w_in, _pick(d, 256)).astype(BF16)
    w_out_b = w_out.astype(BF16)
    w_gate_b, w_up_b = w_gate.astype(BF16), w_up.astype(BF16)
    w_down_b = w_down.astype(BF16).reshape(depth, -1, d)
    grp = 2 if bsz % 2 == 0 else 1
    grp_ret = 4 if bsz % 4 == 0 else grp

    tril_kb = (jnp.arange(kb)[:, None] >= jnp.arange(kb)[None, :]).astype(BF16)
    tril = (jnp.arange(cm)[:, None] >= jnp.arange(cm)[None, :]).astype(F32)
    w_router_p = jnp.zeros((d, LANES), F32).at[:, :N_EXPERTS].set(w_router)
    b_router_p = b_router.reshape(N_EXPERTS, 1)
    tri_moe = (jnp.arange(tmoe)[:, None] < jnp.arange(tmoe)[None, :]).astype(BF16)

    for l in range(depth):
        parts = [mod[l, :bsz, j * d:(j + 1) * d].reshape(bsz, 1, d) for j in range(6)]
        sh_m, sc_m, g_m, sh_f, sc_f, g_f = parts
        q, k, iq, ik2, r_proj, m_proj, small, small_t, v_t = _in_proj(
            x, sc_m, sh_m, w_in_p, l, cos_t, sin_t, tm, kb)
        o_a = _dsa(q, k, iq, ik2, small_t, v_t, tril_kb, tq, kb)
        o_b = _retention(r_proj, cr, grp_ret)
        gate_bias = (jnp.zeros((1, S_WIDTH), F32).at[0, S_MI:S_MI + N_HEADS_MLSTM].set(i_bias[l])
                     .at[0, S_MF:S_MF + N_HEADS_MLSTM].set(f_bias[l]))
        o_c = _mlstm(m_proj, small, gate_bias, conv_w[l], conv_b[l].reshape(1, -1), tril, cm, grp)
        x = _out_proj(o_a, o_b, o_c, w_out_b, l, x, g_m,
                      ln_mix_g[l].reshape(1, d), ln_mix_b[l].reshape(1, d), tm, alpha)
        x = _moe(x, sc_f, sh_f, g_f, w_router_p, b_router_p,
                 w_gate_b, w_up_b, w_down_b, l, tri_moe,
                 ln_ffn_g[l].reshape(1, d), ln_ffn_b[l].reshape(1, d), tmoe, moe_cap, alpha)
    return x
```

```python
import functools

import numpy as np
import jax
import jax.numpy as jnp
from jax import lax
from jax.experimental import pallas as pl
from jax.experimental.pallas import tpu as pltpu

F32 = jnp.float32
BF16 = jnp.bfloat16

HEAD_DIM = 64
CHUNK = 64
N_HEADS_ATTN = 8
N_IDX_HEADS = 4
IDX_DIM = 64
TOPK_MAX = 256
N_HEADS_RET = 4
N_HEADS_MLSTM = 4
CONV_WIDTH = 4
ROPE_THETA = 10000.0
N_EXPERTS = 16
N_GROUPS = 4
EXPERTS_PER_GROUP = N_EXPERTS // N_GROUPS
D_FF_EXPERT = 256
LN_EPS = 1e-5

MIX_ATTN = N_HEADS_ATTN * HEAD_DIM
MIX_RET = N_HEADS_RET * HEAD_DIM
MIX_MLSTM = N_HEADS_MLSTM * HEAD_DIM

LANES = 128
SUBLANES = 8
VMEM_LIMIT = 56 * 1024 * 1024

IQ_WIDTH = N_IDX_HEADS * IDX_DIM
R_WIDTH = 4 * MIX_RET
M_WIDTH = 4 * MIX_MLSTM
S_WIDTH = LANES
S_IW = 0
S_MI = S_IW + N_IDX_HEADS
S_MF = S_MI + N_HEADS_MLSTM
S_ROWS = 16
V_ROWS = HEAD_DIM + 16
OFF_Q = 0
OFF_K = OFF_Q + MIX_ATTN
OFF_IQ = OFF_K + MIX_ATTN
OFF_R = OFF_IQ + IQ_WIDTH
OFF_M = OFF_R + R_WIDTH
OFF_IK = OFF_M + M_WIDTH
OFF_S = OFF_IK + LANES
OFF_V = OFF_S + S_WIDTH
W_TOTAL = OFF_V + MIX_ATTN

INT_MIN = -2 ** 31
NEG_BIG = -1e30
LOG2_E = 1.4426950408889634


def _dot(a, b):
    return jnp.dot(a, b, preferred_element_type=F32)


def _dot_nt(a, b):
    return lax.dot_general(a, b, (((1,), (1,)), ((), ())), preferred_element_type=F32)


def _dot_tn(a, b):
    return lax.dot_general(a, b, (((0,), (0,)), ((), ())), preferred_element_type=F32)


def _params(sem):
    return pltpu.CompilerParams(dimension_semantics=sem, vmem_limit_bytes=VMEM_LIMIT)


def _ada_kernel(c_ref, w_ref, b_ref, o_ref):
    c = c_ref[...]
    c_act = c * jax.nn.sigmoid(c)
    o_ref[0] = _dot(c_act, w_ref[0]) + b_ref[0]


def _ada_mod(c_pad, w_ada, b_ada, tn):
    depth, d, n = w_ada.shape
    rows = c_pad.shape[0]
    return pl.pallas_call(
        _ada_kernel,
        grid=(depth, n // tn),
        in_specs=[pl.BlockSpec((rows, d), lambda l, j: (0, 0)),
                  pl.BlockSpec((1, d, tn), lambda l, j: (l, 0, j)),
                  pl.BlockSpec((1, 1, tn), lambda l, j: (l, 0, j))],
        out_specs=pl.BlockSpec((1, rows, tn), lambda l, j: (l, 0, j)),
        out_shape=jax.ShapeDtypeStruct((depth, rows, n), F32),
        compiler_params=_params(("parallel", "parallel")),
        name="ada_mod",
    )(c_pad, w_ada, b_ada.reshape(depth, 1, n))


def _rope(y, cos, sin):
    w = y.shape[1]
    reps = w // LANES
    cosw = jnp.concatenate([cos] * reps, axis=1) if reps > 1 else cos
    sinw = jnp.concatenate([sin] * reps, axis=1) if reps > 1 else sin
    lane = lax.broadcasted_iota(jnp.int32, y.shape, 1)
    first = (lane % HEAD_DIM) < (HEAD_DIM // 2)
    partner = jnp.where(first, pltpu.roll(y, w - HEAD_DIM // 2, 1), pltpu.roll(y, HEAD_DIM // 2, 1))
    return y * cosw + partner * sinw


def _inproj_kernel(x_ref, sc_ref, sh_ref, w_ref, cos_ref, sin_ref,
                   q_ref, k_ref, iq_ref, ik_ref, r_ref, m_ref, s_ref, st_ref, vt_ref, *, kb):
    h = (x_ref[0] * (1.0 + sc_ref[0]) + sh_ref[0]).astype(BF16)
    cos = cos_ref[0]
    sin = sin_ref[0]

    def proj(start, width):
        return _dot(h, w_ref[:, start:start + width])

    q_ref[0] = (_rope(proj(OFF_Q, MIX_ATTN), cos, sin) * (HEAD_DIM ** -0.5 * LOG2_E)).astype(BF16)
    k_ref[0] = _rope(proj(OFF_K, MIX_ATTN), cos, sin).astype(BF16)
    iq_ref[0] = _rope(proj(OFF_IQ, IQ_WIDTH), cos, sin).astype(BF16)
    ik_ref[0] = _rope(proj(OFF_IK, LANES), cos, sin).astype(BF16)
    r_ref[0, :, 0:2 * MIX_RET] = _rope(proj(OFF_R, 2 * MIX_RET), cos, sin).astype(BF16)
    r_ref[0, :, 2 * MIX_RET:R_WIDTH] = proj(OFF_R + 2 * MIX_RET, 2 * MIX_RET).astype(BF16)
    m_ref[0, :, 0:2 * MIX_MLSTM] = proj(OFF_M, 2 * MIX_MLSTM).astype(BF16)
    m_ref[0, :, 2 * MIX_MLSTM:M_WIDTH] = proj(OFF_M + 2 * MIX_MLSTM, 2 * MIX_MLSTM).astype(BF16)
    y = proj(OFF_S, S_WIDTH)
    s_ref[0] = y
    st_ref[0] = y.T[0:S_ROWS, :]
    tm = h.shape[0]
    yvt = proj(OFF_V, MIX_ATTN).T.astype(BF16)
    pad_rows = lax.broadcasted_iota(jnp.int32, (V_ROWS - HEAD_DIM, kb), 0)
    ones_rows = jnp.where(pad_rows == 0, 1.0, 0.0).astype(BF16)
    for j in range(tm // kb):
        for hh in range(N_HEADS_ATTN):
            vt_ref[0, j, hh * V_ROWS:hh * V_ROWS + HEAD_DIM, :] = yvt[hh * HEAD_DIM:(hh + 1) * HEAD_DIM,
                                                                      j * kb:(j + 1) * kb]
            vt_ref[0, j, hh * V_ROWS + HEAD_DIM:(hh + 1) * V_ROWS, :] = ones_rows


def _in_proj(x, sc, sh, w, layer, cos_t, sin_t, tm, kb):
    bsz, seq, d = x.shape
    row = lambda b, i: (b, i, 0)
    per_b = lambda b, i: (b, 0, 0)
    widths = (MIX_ATTN, MIX_ATTN, IQ_WIDTH, LANES, R_WIDTH, M_WIDTH)
    return pl.pallas_call(
        functools.partial(_inproj_kernel, kb=kb),
        grid=(bsz, seq // tm),
        in_specs=[pl.BlockSpec((1, tm, d), row),
                  pl.BlockSpec((1, 1, d), per_b),
                  pl.BlockSpec((1, 1, d), per_b),
                  pl.BlockSpec((None, d, W_TOTAL), lambda b, i: (layer, 0, 0)),
                  pl.BlockSpec((1, tm, LANES), row),
                  pl.BlockSpec((1, tm, LANES), row)],
        out_specs=[pl.BlockSpec((1, tm, wd), row) for wd in widths]
                  + [pl.BlockSpec((1, tm, S_WIDTH), row),
                     pl.BlockSpec((1, S_ROWS, tm), lambda b, i: (b, 0, i)),
                     pl.BlockSpec((1, tm // kb, N_HEADS_ATTN * V_ROWS, kb), lambda b, i: (b, i, 0, 0))],
        out_shape=[jax.ShapeDtypeStruct((bsz, seq, wd), BF16) for wd in widths]
                  + [jax.ShapeDtypeStruct((bsz, seq, S_WIDTH), F32),
                     jax.ShapeDtypeStruct((bsz, S_ROWS, seq), F32),
                     jax.ShapeDtypeStruct((bsz, seq // kb, N_HEADS_ATTN * V_ROWS, kb), BF16)],
        compiler_params=_params(("parallel", "parallel")),
        name="in_proj",
    )(x, sc, sh, w, cos_t, sin_t)


def _dsa_kernel(q_ref, k_ref, iq_ref, ik_ref, st_ref, vt_ref, tril_ref, o_ref,
                key_scr, byte_scr, cand_scr, bias_scr, s_scr, p_scr, m_scr, alpha_scr, acc_scr,
                *, tq, kb, topk):
    q0 = pl.program_id(1) * tq
    n_blocks = (q0 + tq + kb - 1) // kb
    qpos = q0 + lax.broadcasted_iota(jnp.int32, (1, tq), 1)
    q_limit = (qpos // CHUNK + 1) * CHUNK
    krow = lax.broadcasted_iota(jnp.int32, (kb, 1), 0)

    def head_of_pair(x, h):
        pair = x[:, (h // 2) * LANES:(h // 2 + 1) * LANES]
        lane = lax.broadcasted_iota(jnp.int32, pair.shape, 1)
        keep = (lane < HEAD_DIM) if h % 2 == 0 else (lane >= HEAD_DIM)
        return jnp.where(keep, pair, jnp.zeros_like(pair))

    iw = st_ref[0][S_IW:S_IW + N_IDX_HEADS, :] * (N_IDX_HEADS ** -0.5 * IDX_DIM ** -0.5)
    iq = iq_ref[0]
    iq_heads = [head_of_pair(iq, h) for h in range(N_IDX_HEADS)]

    def score_block(c):
        k0 = pl.multiple_of(c * kb, kb)
        ik2 = ik_ref[0, pl.ds(k0, kb), :]
        score = jnp.zeros((kb, tq), F32)
        for h in range(N_IDX_HEADS):
            score = score + jnp.maximum(_dot_nt(ik2, iq_heads[h]), 0.0) * iw[h:h + 1, :]
        bits = pltpu.bitcast(score, jnp.int32)
        key = jnp.where(bits >= 0, bits, bits ^ jnp.int32(0x7FFFFFFF))
        key = jnp.where(k0 + krow < q_limit, key, jnp.int32(INT_MIN))
        key_scr[c] = key
        byte_scr[0, c] = ((key >> 24) + 128).astype(F32).astype(BF16)
        for lvl in range(1, 4):
            byte_scr[lvl, c] = ((key >> (24 - 8 * lvl)) & 255).astype(F32).astype(BF16)

    def score_pair(c2, carry):
        score_block(2 * c2)
        score_block(2 * c2 + 1)
        return carry

    lax.fori_loop(0, n_blocks // 2, score_pair, 0)

    @pl.when(n_blocks % 2 == 1)
    def _():
        score_block(n_blocks - 1)

    pack = 16
    one = jnp.ones((kb, tq), BF16)
    zero = jnp.zeros((kb, tq), BF16)

    def count_ge(lvl, cand, narrow_to=None):
        cand_b = cand.astype(BF16)

        def hits(c):
            plane = byte_scr[0, c] if lvl == 0 else cand_scr[c]
            hit = jnp.where(plane >= cand_b, one, zero)
            if narrow_to is not None:
                cand_scr[c] = jnp.where(plane == narrow_to, byte_scr[lvl + 1, c], -one)
            parts = [hit[i * pack:(i + 1) * pack, :] for i in range(kb // pack)]
            while len(parts) > 1:
                parts = [parts[i] + parts[i + 1] for i in range(0, len(parts), 2)]
            return parts[0]

        acc = lax.fori_loop(0, n_blocks // 2, lambda c2, a: a + hits(2 * c2) + hits(2 * c2 + 1),
                            jnp.zeros((pack, tq), BF16))
        acc = lax.cond(n_blocks % 2 == 1, lambda a: a + hits(n_blocks - 1), lambda a: a, acc)
        return jnp.sum(acc.astype(F32), axis=0, keepdims=True)

    above = jnp.zeros((1, tq), F32)
    t = jnp.zeros((1, tq), jnp.int32)
    for lvl in range(4):
        def bit_body(i, v, lvl=lvl, above=above):
            cand = v + lax.shift_left(jnp.int32(1), 7 - i).astype(F32)
            return jnp.where(above + count_ge(lvl, cand) >= topk, cand, v)

        v = lax.fori_loop(0, 8, bit_body, jnp.zeros((1, tq), F32))
        above = above + count_ge(lvl, v + 1.0, narrow_to=v.astype(BF16) if lvl < 3 else None)
        t = t | lax.shift_left(v.astype(jnp.int32), 24 - 8 * lvl)
    thr = jnp.maximum(t ^ jnp.int32(INT_MIN), jnp.int32(INT_MIN + 1))
    need = topk - above

    q = q_ref[0]
    q_heads = [head_of_pair(q, h) for h in range(N_HEADS_ATTN)]
    m_scr[...] = jnp.full(m_scr.shape, NEG_BIG, F32)
    alpha_scr[...] = jnp.ones(alpha_scr.shape, F32)
    acc_scr[...] = jnp.zeros(acc_scr.shape, F32)
    p_scr[...] = jnp.zeros(p_scr.shape, BF16)

    def stage_logits(c, h):
        k0 = pl.multiple_of(c * kb, kb)
        kp = k_ref[0, pl.ds(k0, kb), (h // 2) * LANES:(h // 2 + 1) * LANES]
        s_scr[h] = _dot_nt(kp, q_heads[h])

    def stage_mask(c, ties_before):
        key = key_scr[c]
        tie = key == thr
        rank = _dot(tril_ref[...], jnp.where(tie, 1.0, 0.0).astype(BF16)) + ties_before
        sel = (key > thr) | (tie & (rank <= need))
        bias_scr[...] = jnp.where(sel, 0.0, NEG_BIG)
        return rank[kb - 1:kb, :]

    def stage_softmax(h):
        for half in range(tq // LANES):
            ln = slice(half * LANES, (half + 1) * LANES)
            s = s_scr[h, :, ln] + bias_scr[:, ln]
            m_old = m_scr[h, :, ln]
            m_new = jnp.maximum(m_old, jnp.max(s, axis=0, keepdims=True))
            p_scr[h, :, ln] = jnp.exp2(s - m_new).astype(BF16)
            alpha_scr[h, :, ln] = jnp.exp2(m_old - m_new)
            m_scr[h, :, ln] = m_new

    def stage_values(c, h):
        vt = vt_ref[0, c, h * V_ROWS:(h + 1) * V_ROWS, :]
        acc_scr[h, 0:V_ROWS, :] = alpha_scr[h] * acc_scr[h, 0:V_ROWS, :] + _dot(vt, p_scr[h])

    ties0 = stage_mask(0, jnp.zeros((1, tq), F32))
    for h in range(N_HEADS_ATTN):
        stage_logits(0, h)

    def attn_body(j, ties_before):
        c_old = jnp.maximum(j - 2, 0)
        for h in range(N_HEADS_ATTN):
            stage_values(c_old, h)
            stage_softmax(h)
            stage_logits(j, h)
        return stage_mask(j, ties_before)

    lax.fori_loop(1, n_blocks, attn_body, ties0)
    for h in range(N_HEADS_ATTN):
        stage_values(jnp.maximum(n_blocks - 2, 0), h)
        stage_softmax(h)
    for h in range(N_HEADS_ATTN):
        stage_values(n_blocks - 1, h)
    for h in range(N_HEADS_ATTN):
        acc = acc_scr[h]
        out = acc * (1.0 / acc[HEAD_DIM:HEAD_DIM + 1, :])
        o_ref[0, :, h * HEAD_DIM:(h + 1) * HEAD_DIM] = out.T[:, 0:HEAD_DIM].astype(o_ref.dtype)


def _dsa(q, k, iq, ik2, small_t, v_t, tril, tq, kb):
    bsz, seq, _ = q.shape
    topk = min(TOPK_MAX, seq // 4)
    kern = functools.partial(_dsa_kernel, tq=tq, kb=kb, topk=topk)
    return pl.pallas_call(
        kern,
        grid=(bsz, seq // tq),
        in_specs=[pl.BlockSpec((1, tq, MIX_ATTN), lambda b, i: (b, i, 0)),
                  pl.BlockSpec((1, seq, MIX_ATTN), lambda b, i: (b, 0, 0)),
                  pl.BlockSpec((1, tq, IQ_WIDTH), lambda b, i: (b, i, 0)),
                  pl.BlockSpec((1, seq, LANES), lambda b, i: (b, 0, 0)),
                  pl.BlockSpec((1, S_ROWS, tq), lambda b, i: (b, 0, i)),
                  pl.BlockSpec((1, seq // kb, N_HEADS_ATTN * V_ROWS, kb), lambda b, i: (b, 0, 0, 0)),
                  pl.BlockSpec((kb, kb), lambda b, i: (0, 0))],
        out_specs=pl.BlockSpec((1, tq, MIX_ATTN), lambda b, i: (b, i, 0)),
        out_shape=jax.ShapeDtypeStruct((bsz, seq, MIX_ATTN), BF16),
        scratch_shapes=[pltpu.VMEM((seq // kb, kb, tq), jnp.int32),
                        pltpu.VMEM((4, seq // kb, kb, tq), BF16),
                        pltpu.VMEM((seq // kb, kb, tq), BF16),
                        pltpu.VMEM((kb, tq), F32),
                        pltpu.VMEM((N_HEADS_ATTN, kb, tq), F32),
                        pltpu.VMEM((N_HEADS_ATTN, kb, tq), BF16),
                        pltpu.VMEM((N_HEADS_ATTN, 1, tq), F32),
                        pltpu.VMEM((N_HEADS_ATTN, 1, tq), F32),
                        pltpu.VMEM((N_HEADS_ATTN, LANES, tq), F32)],
        compiler_params=_params(("parallel", "arbitrary")),
        name="dsa",
    )(q, k, iq, ik2, small_t, v_t, tril)


def _head_norm(y):
    mean_mat = jnp.full((HEAD_DIM, HEAD_DIM), 1.0 / HEAD_DIM, BF16)

    def mean_bcast(x):
        hi = x.astype(BF16)
        lo = (x - hi.astype(F32)).astype(BF16)
        return _dot(hi, mean_mat) + _dot(lo, mean_mat)

    yc = y - mean_bcast(y)
    return yc * lax.rsqrt(mean_bcast(yc * yc) + LN_EPS)


def _ret_kernel(r_ref, o_ref, state_scr, *, cr, grp):
    @pl.when(pl.program_id(1) == 0)
    def _():
        state_scr[...] = jnp.zeros_like(state_scr)

    ri = lax.broadcasted_iota(jnp.int32, (cr, cr), 0)
    ci = lax.broadcasted_iota(jnp.int32, (cr, cr), 1)
    diff = (ri - ci).astype(F32)
    pos = lax.broadcasted_iota(jnp.int32, (cr, 1), 0).astype(F32)
    items = [(h, g) for h in range(N_HEADS_RET) for g in range(grp)]
    sl = lambda part, h: slice(part * MIX_RET + h * HEAD_DIM, part * MIX_RET + (h + 1) * HEAD_DIM)
    log_gamma = [jnp.log1p(jnp.full((1, 1), -(2.0 ** (-5.0 - h)), F32)) for h in range(N_HEADS_RET)]
    decay_in = [jnp.where(diff >= 0, jnp.exp(diff * lg), 0.0) * (HEAD_DIM ** -0.5) for lg in log_gamma]
    q = {(h, g): r_ref[g, :, sl(0, h)] for h, g in items}
    k = {(h, g): r_ref[g, :, sl(1, h)] for h, g in items}
    v = {(h, g): r_ref[g, :, sl(2, h)] for h, g in items}
    state = {(h, g): state_scr[g, h] for h, g in items}
    scores = {it: _dot_nt(q[it], k[it]) * decay_in[it[0]] for it in items}
    cross = {it: jnp.exp((pos + 1.0) * log_gamma[it[0]]) * _dot(q[it], state[it].astype(BF16)) for it in items}
    inner = {it: _dot(scores[it].astype(BF16), v[it]) for it in items}
    for it in items:
        h, g = it
        gate = r_ref[g, :, sl(3, h)].astype(F32)
        y = _head_norm(inner[it] + cross[it])
        o_ref[g, :, h * HEAD_DIM:(h + 1) * HEAD_DIM] = (y * (gate * jax.nn.sigmoid(gate))).astype(o_ref.dtype)
    for it in items:
        h, g = it
        k_decay = (HEAD_DIM ** -0.5) * jnp.exp((cr - 1.0 - pos) * log_gamma[h])
        k_dec = (k[it].astype(F32) * k_decay).astype(BF16)
        state_scr[g, h] = state[it] * jnp.exp(cr * log_gamma[h]) + _dot_tn(k_dec, v[it])


def _retention(r_proj, cr, grp):
    bsz, seq, _ = r_proj.shape
    return pl.pallas_call(
        functools.partial(_ret_kernel, cr=cr, grp=grp),
        grid=(bsz // grp, seq // cr),
        in_specs=[pl.BlockSpec((grp, cr, R_WIDTH), lambda b, i: (b, i, 0))],
        out_specs=pl.BlockSpec((grp, cr, MIX_RET), lambda b, i: (b, i, 0)),
        out_shape=jax.ShapeDtypeStruct((bsz, seq, MIX_RET), BF16),
        scratch_shapes=[pltpu.VMEM((grp, N_HEADS_RET, HEAD_DIM, HEAD_DIM), F32)],
        compiler_params=_params(("parallel", "arbitrary")),
        name="retention",
    )(r_proj)


def _mlstm_kernel(m_ref, sm_ref, bias_ref, cw_ref, cb_ref, tril_ref, o_ref,
                  xbuf, a_scr, m_scr, *, cm, grp):
    halo = 8
    gate_shift = S_MF - S_MI

    @pl.when(pl.program_id(1) == 0)
    def _():
        xbuf[:, 0:halo, :] = jnp.zeros((grp, halo, 2 * MIX_MLSTM), F32)
        a_scr[...] = jnp.zeros_like(a_scr)
        m_scr[...] = jnp.zeros_like(m_scr)

    ri = lax.broadcasted_iota(jnp.int32, (cm, cm), 0)
    ci = lax.broadcasted_iota(jnp.int32, (cm, cm), 1)
    causal = ri >= ci
    row = lax.broadcasted_iota(jnp.int32, (cm, LANES), 0)
    lane = lax.broadcasted_iota(jnp.int32, (cm, LANES), 1)
    gate_lanes = (lane >= S_MI) & (lane < S_MI + N_HEADS_MLSTM)
    lane64 = lax.broadcasted_iota(jnp.int32, (cm, HEAD_DIM), 1)
    ones_col = jnp.where(lane64 == 0, 1.0, 0.0).astype(BF16)
    scale = HEAD_DIM ** -0.5

    def head_cols(x, h):
        return x[:, h * HEAD_DIM:(h + 1) * HEAD_DIM]

    items = [(g, h) for g in range(grp) for h in range(N_HEADS_MLSTM)]
    mm, qk, u_row, m_col, inter, e_inv, kw_col, decay = {}, {}, {}, {}, {}, {}, {}, {}
    for g in range(grp):
        mm[g] = m_ref[g]
        xbuf[g, halo:halo + cm, :] = mm[g][:, 0:2 * MIX_MLSTM].astype(F32)
        conv = cb_ref[...]
        for j in range(CONV_WIDTH):
            off = halo - (CONV_WIDTH - 1) + j
            conv = conv + xbuf[g, off:off + cm, :] * cw_ref[j:j + 1, :]
        xbuf[g, 0:halo, :] = xbuf[g, cm:cm + halo, :]
        qk[g] = conv * jax.nn.sigmoid(conv)

        gates = sm_ref[g] + bias_ref[...]
        b_all = jnp.dot(tril_ref[...], jax.nn.log_sigmoid(gates), preferred_element_type=F32,
                        precision=lax.Precision.HIGHEST)
        b_i = jnp.where(gate_lanes, pltpu.roll(b_all, LANES - gate_shift, 1), 0.0)
        u = jnp.where(gate_lanes, gates, 0.0) - b_i
        run = u
        step = 1
        while step < cm:
            run = jnp.maximum(run, jnp.where(row >= step, pltpu.roll(run, step, 0), -jnp.inf))
            step *= 2
        m_prev = m_scr[g]
        m_c = jnp.maximum(m_prev, run)
        m_last = m_c[cm - 1:cm, :]
        u_row[g] = u.T
        m_col[g] = m_c
        inter[g] = jnp.exp(m_prev - m_c)
        e_inv[g] = jnp.exp(-(b_i + m_c))
        kw_col[g] = scale * jnp.exp(u - m_last)
        decay[g] = jnp.exp(m_prev - m_last)
        m_scr[g] = b_i[cm - 1:cm, :] + m_last

    q, k, v_aug, a_mem = {}, {}, {}, {}
    for it in items:
        g, h = it
        q[it] = head_cols(qk[g], h).astype(BF16)
        k[it] = head_cols(qk[g][:, MIX_MLSTM:], h)
        v_aug[it] = jnp.concatenate([head_cols(mm[g][:, 2 * MIX_MLSTM:], h), ones_col], axis=1)
        a_mem[it] = a_scr[g, h]

    def col(x, h):
        return x[:, S_MI + h:S_MI + h + 1]

    w = {}
    for it in items:
        g, h = it
        u_r = u_row[g][S_MI + h:S_MI + h + 1, :]
        w[it] = jnp.exp(jnp.where(causal, u_r - col(m_col[g], h), -jnp.inf))

    s = {it: _dot_nt(q[it], k[it].astype(BF16)) * scale * w[it] for it in items}
    cross = {it: _dot(q[it], a_mem[it].astype(BF16)) for it in items}
    both = {it: _dot(s[it].astype(BF16), v_aug[it]) + col(inter[it[0]], it[1]) * cross[it] for it in items}
    for it in items:
        g, h = it
        den = both[it][:, HEAD_DIM:HEAD_DIM + 1]
        h_tilde = both[it][:, 0:HEAD_DIM] * (1.0 / jnp.maximum(jnp.abs(den), col(e_inv[g], h)))
        og = head_cols(mm[g][:, 3 * MIX_MLSTM:], h).astype(F32)
        o_ref[g, :, h * HEAD_DIM:(h + 1) * HEAD_DIM] = _head_norm(jax.nn.sigmoid(og) * h_tilde).astype(o_ref.dtype)

    for it in items:
        g, h = it
        kw = k[it] * col(kw_col[g], h)
        a_scr[g, h] = col(decay[g], h) * a_mem[it] + _dot_tn(kw.astype(BF16), v_aug[it])


def _mlstm(m_proj, small, gate_bias, conv_w, conv_b, tril, cm, grp):
    bsz, seq, _ = m_proj.shape
    const = lambda b, i: (0, 0)
    return pl.pallas_call(
        functools.partial(_mlstm_kernel, cm=cm, grp=grp),
        grid=(bsz // grp, seq // cm),
        in_specs=[pl.BlockSpec((grp, cm, M_WIDTH), lambda b, i: (b, i, 0)),
                  pl.BlockSpec((grp, cm, S_WIDTH), lambda b, i: (b, i, 0)),
                  pl.BlockSpec((1, S_WIDTH), const),
                  pl.BlockSpec((CONV_WIDTH, 2 * MIX_MLSTM), const),
                  pl.BlockSpec((1, 2 * MIX_MLSTM), const),
                  pl.BlockSpec((cm, cm), const)],
        out_specs=pl.BlockSpec((grp, cm, MIX_MLSTM), lambda b, i: (b, i, 0)),
        out_shape=jax.ShapeDtypeStruct((bsz, seq, MIX_MLSTM), BF16),
        scratch_shapes=[pltpu.VMEM((grp, cm + 8, 2 * MIX_MLSTM), F32),
                        pltpu.VMEM((grp, N_HEADS_MLSTM, HEAD_DIM, LANES), F32),
                        pltpu.VMEM((grp, 1, LANES), F32)],
        compiler_params=_params(("parallel", "arbitrary")),
        name="mlstm",
    )(m_proj, small, gate_bias, conv_w, conv_b, tril)


def _layer_norm(z, g, b):
    mu = jnp.mean(z, axis=-1, keepdims=True)
    var = jnp.mean(jnp.square(z - mu), axis=-1, keepdims=True)
    return (z - mu) * lax.rsqrt(var + LN_EPS) * g + b


def _outproj_kernel(oa_ref, ob_ref, oc_ref, w_ref, x_ref, gm_ref, g_ref, b_ref, o_ref, *, alpha):
    mix = _dot(oa_ref[0], w_ref[0:MIX_ATTN, :])
    mix = mix + _dot(ob_ref[0], w_ref[MIX_ATTN:MIX_ATTN + MIX_RET, :])
    mix = mix + _dot(oc_ref[0], w_ref[MIX_ATTN + MIX_RET:, :])
    z = alpha * x_ref[0] + (1.0 + gm_ref[0]) * mix
    o_ref[0] = _layer_norm(z, g_ref[...], b_ref[...])


def _out_proj(o_a, o_b, o_c, w_out, layer, x, g_m, ln_g, ln_b, tm, alpha):
    bsz, seq, d = x.shape
    row = lambda b, i: (b, i, 0)
    const = lambda b, i: (0, 0)
    return pl.pallas_call(
        functools.partial(_outproj_kernel, alpha=alpha),
        grid=(bsz, seq // tm),
        in_specs=[pl.BlockSpec((1, tm, MIX_ATTN), row),
                  pl.BlockSpec((1, tm, MIX_RET), row),
                  pl.BlockSpec((1, tm, MIX_MLSTM), row),
                  pl.BlockSpec((None,) + w_out.shape[1:], lambda b, i: (layer, 0, 0)),
                  pl.BlockSpec((1, tm, d), row),
                  pl.BlockSpec((1, 1, d), lambda b, i: (b, 0, 0)),
                  pl.BlockSpec((1, d), const),
                  pl.BlockSpec((1, d), const)],
        out_specs=pl.BlockSpec((1, tm, d), row),
        out_shape=jax.ShapeDtypeStruct((bsz, seq, d), F32),
        compiler_params=_params(("parallel", "parallel")),
        name="out_proj",
    )(o_a, o_b, o_c, w_out, x, g_m, ln_g, ln_b)


def _route(scores, biased):
    col = lambda a, e: a[e:e + 1, :]
    epg = EXPERTS_PER_GROUP
    group_scores = []
    for g in range(N_GROUPS):
        vals = [col(biased, g * epg + j) for j in range(epg)]
        best = None
        for a in range(epg):
            for b in range(a + 1, epg):
                pair = vals[a] + vals[b]
                best = pair if best is None else jnp.maximum(best, pair)
        group_scores.append(best)
    best_g = jnp.zeros_like(group_scores[0], dtype=jnp.int32)
    best_v = group_scores[0]
    for g in range(1, N_GROUPS):
        better = group_scores[g] > best_v
        best_g = jnp.where(better, g, best_g)
        best_v = jnp.where(better, group_scores[g], best_v)
    cand_b = [sum(jnp.where(best_g == g, col(biased, g * epg + j), 0.0) for g in range(N_GROUPS))
              for j in range(epg)]
    cand_s = [sum(jnp.where(best_g == g, col(scores, g * epg + j), 0.0) for g in range(N_GROUPS))
              for j in range(epg)]

    def argmax_first(vals, skip=None):
        idx = None
        val = None
        for j, vj in enumerate(vals):
            if skip is not None:
                vj = jnp.where(skip == j, -jnp.inf, vj)
            if idx is None:
                idx, val = jnp.zeros_like(best_g), vj
            else:
                better = vj > val
                idx = jnp.where(better, j, idx)
                val = jnp.where(better, vj, val)
        return idx

    first = argmax_first(cand_b)
    second = argmax_first(cand_b, skip=first)
    w1 = sum(jnp.where(first == j, cand_s[j], 0.0) for j in range(epg))
    w2 = sum(jnp.where(second == j, cand_s[j], 0.0) for j in range(epg))
    total = w1 + w2
    e1 = best_g * epg + first
    e2 = best_g * epg + second
    expert = lax.broadcasted_iota(jnp.int32, scores.shape, 0)
    gate = jnp.where(expert == e1, w1 / total, 0.0) + jnp.where(expert == e2, w2 / total, 0.0)
    return gate, best_g


def _split_bf16(x):
    hi = x.astype(BF16)
    return hi, (x - hi.astype(F32)).astype(BF16)


def _moe_kernel(x_ref, sc_ref, sh_ref, gf_ref, wr_ref, br_ref, wg_ref, wu_ref, wd_ref, tri_ref, g_ref, b_ref,
                o_ref, hid_scr, *, alpha, cap):
    x = x_ref[0]
    tm = x.shape[0]
    h = x * (1.0 + sc_ref[0]) + sh_ref[0]
    hb = h.astype(BF16)
    n_exp, _, d_ff = wg_ref.shape
    scores_t = jax.nn.sigmoid(_dot(h, wr_ref[...]).T[0:n_exp, :])
    gate_t, best_g = _route(scores_t, scores_t + br_ref[...])
    sub = lax.broadcasted_iota(jnp.int32, (2 * SUBLANES, tm), 0)
    member_t = jnp.where(sub == best_g, 1.0, 0.0)
    before_t = _dot(member_t.astype(BF16), tri_ref[...])
    rank = jnp.sum(member_t * before_t, axis=0, keepdims=True)
    count = jnp.max(jnp.sum(member_t, axis=1, keepdims=True))
    grp_f = best_g.astype(F32)
    sub8 = lax.broadcasted_iota(jnp.int32, (SUBLANES, tm), 0)
    extra = jnp.where(sub8 == 0, grp_f, jnp.where(sub8 == 1, rank, 0.0))
    info = jnp.concatenate([gate_t, extra, jnp.zeros((LANES - n_exp - SUBLANES, tm), F32)], axis=0).T
    gate = info
    epg = EXPERTS_PER_GROUP

    def expert_hidden(rows_b, gate_rows, e, dst):
        gate_pre = _dot(rows_b, wg_ref[e])
        up = _dot(rows_b, wu_ref[e])
        hid = gate_pre * jax.nn.sigmoid(gate_pre) * up * gate_rows[:, e:e + 1]
        hid_scr[0:rows_b.shape[0], dst * d_ff:(dst + 1) * d_ff] = hid.astype(BF16)

    def dense(_):
        for e in range(n_exp):
            expert_hidden(hb, gate, e, e)
        return _dot(hid_scr[...], wd_ref[...])


    def grouped(_):
        grp_c = info[:, n_exp:n_exp + 1]
        rank_c = info[:, n_exp + 1:n_exp + 2]
        slot_c = lax.broadcasted_iota(jnp.int32, (cap, 1), 0).astype(F32)
        slot_r = lax.broadcasted_iota(jnp.int32, (1, cap), 1).astype(F32)
        gate_hi, gate_lo = _split_bf16(gate)
        y = jnp.zeros((tm, x.shape[1]), F32)
        for g in range(N_GROUPS):
            take = jnp.where(jnp.where(grp_f == g, rank, -1.0) == slot_c, 1.0, 0.0).astype(BF16)
            give = jnp.where(jnp.where(grp_c == g, rank_c, -1.0) == slot_r, 1.0, 0.0).astype(BF16)
            rows_b = _dot(take, hb).astype(BF16)
            gate_rows = _dot(take, gate_hi) + _dot(take, gate_lo)
            for j in range(epg):
                expert_hidden(rows_b, gate_rows, g * epg + j, j)
            out = _dot(hid_scr[0:cap, 0:epg * d_ff], wd_ref[g * epg * d_ff:(g + 1) * epg * d_ff, :])
            y = y + _dot(give, out.astype(BF16))
        return y

    y = lax.cond(count <= cap, grouped, dense, 0)
    z = alpha * x + (1.0 + gf_ref[0]) * y
    o_ref[0] = _layer_norm(z, g_ref[...], b_ref[...])


def _moe(x, sc, sh, g_f, w_router, b_router, w_gate, w_up, w_down, layer, tri, ln_g, ln_b, tm, cap, alpha):
    bsz, seq, d = x.shape
    _, n_exp, _, d_ff = w_gate.shape
    row = lambda b, i: (b, i, 0)
    per_b = lambda b, i: (b, 0, 0)
    const = lambda b, i: (0, 0)
    of_layer4 = lambda b, i: (layer, 0, 0, 0)
    of_layer3 = lambda b, i: (layer, 0, 0)
    resident = pl.Buffered(1)
    return pl.pallas_call(
        functools.partial(_moe_kernel, alpha=alpha, cap=cap),
        grid=(bsz, seq // tm),
        in_specs=[pl.BlockSpec((1, tm, d), row),
                  pl.BlockSpec((1, 1, d), per_b),
                  pl.BlockSpec((1, 1, d), per_b),
                  pl.BlockSpec((1, 1, d), per_b),
                  pl.BlockSpec((d, LANES), const),
                  pl.BlockSpec((n_exp, 1), const),
                  pl.BlockSpec((None, n_exp, d, d_ff), of_layer4, pipeline_mode=resident),
                  pl.BlockSpec((None, n_exp, d, d_ff), of_layer4, pipeline_mode=resident),
                  pl.BlockSpec((None, n_exp * d_ff, d), of_layer3, pipeline_mode=resident),
                  pl.BlockSpec((tm, tm), const, pipeline_mode=resident),
                  pl.BlockSpec((1, d), const),
                  pl.BlockSpec((1, d), const)],
        out_specs=pl.BlockSpec((1, tm, d), row),
        out_shape=jax.ShapeDtypeStruct((bsz, seq, d), F32),
        scratch_shapes=[pltpu.VMEM((tm, n_exp * d_ff), BF16)],
        compiler_params=_params(("parallel", "parallel")),
        name="moe",
    )(x, sc, sh, g_f, w_router, b_router, w_gate, w_up, w_down, tri, ln_g, ln_b)


def _pick(n, pref):
    t = min(pref, n)
    while n % t:
        t //= 2
    return t


def _rope_tables(positions):
    half = HEAD_DIM // 2
    inv_freq = ROPE_THETA ** (-jnp.arange(half, dtype=F32) / half)
    ang = positions.astype(F32)[..., None] * inv_freq
    cos, sin = jnp.cos(ang), jnp.sin(ang)
    reps = LANES // HEAD_DIM
    return (jnp.concatenate([cos, cos] * reps, axis=-1),
            jnp.concatenate([-sin, sin] * reps, axis=-1))


def _prep_w_kernel(w_ref, o_ref):
    o_v = 2 * MIX_ATTN
    o_iq = o_v + MIX_ATTN
    o_ik = o_iq + IQ_WIDTH
    o_iw = o_ik + IDX_DIM
    o_r = o_iw + N_IDX_HEADS
    o_g = o_r + R_WIDTH + M_WIDTH
    n_gate = 2 * N_HEADS_MLSTM

    def put(dst, src, width):
        o_ref[0, :, dst:dst + width] = w_ref[0, :, src:src + width].astype(o_ref.dtype)

    o_ref[0] = jnp.zeros(o_ref.shape[1:], o_ref.dtype)
    put(OFF_Q, 0, 2 * MIX_ATTN)
    put(OFF_IQ, o_iq, IQ_WIDTH)
    put(OFF_R, o_r, R_WIDTH + M_WIDTH)
    put(OFF_IK, o_ik, IDX_DIM)
    put(OFF_IK + IDX_DIM, o_ik, IDX_DIM)
    put(OFF_S + S_IW, o_iw, N_IDX_HEADS)
    put(OFF_S + S_MI, o_g, n_gate)
    put(OFF_V, o_v, MIX_ATTN)


def _prep_w_in(w_in, tr):
    depth, d, n = w_in.shape
    return pl.pallas_call(
        _prep_w_kernel,
        grid=(depth, d // tr),
        in_specs=[pl.BlockSpec((1, tr, n), lambda l, i: (l, i, 0))],
        out_specs=pl.BlockSpec((1, tr, W_TOTAL), lambda l, i: (l, i, 0)),
        out_shape=jax.ShapeDtypeStruct((depth, d, W_TOTAL), F32),
        compiler_params=_params(("parallel", "parallel")),
        name="prep_w_in",
    )(w_in)


def kernel(x, c, positions, w_ada, b_ada, w_in, i_bias, f_bias, conv_w, conv_b, w_out, ln_mix_g, ln_mix_b,
           w_router, b_router, w_gate, w_up, w_down, ln_ffn_g, ln_ffn_b):
    bsz, seq, d = x.shape
    depth = w_ada.shape[0]
    alpha = (2.0 * depth) ** 0.25

    tm = _pick(seq, 512)
    tq = _pick(seq, 256)
    kb = _pick(seq, 256)
    assert seq // 16 <= 256, "packed bf16 partial counts in the DSA threshold search must stay exact"
    cr = _pick(seq, 256)
    cm = _pick(seq, 256)
    tmoe = _pick(seq, 512)
    moe_cap = min(tmoe, (3 * tmoe // (2 * N_GROUPS) + 15) // 16 * 16)

    cos_t, sin_t = _rope_tables(positions)
    c_pad = jnp.zeros((8, d), F32).at[:bsz].set(c)
    mod = _ada_mod(c_pad, w_ada, b_ada, _pick(6 * d, 1536))
    w_in_p = _prep_w_in(w_in, _pick(d, 256)).astype(BF16)
    w_out_b = w_out.astype(BF16)
    w_gate_b, w_up_b = w_gate.astype(BF16), w_up.astype(BF16)
    w_down_b = w_down.astype(BF16).reshape(depth, -1, d)
    grp = 2 if bsz % 2 == 0 else 1
    grp_ret = 4 if bsz % 4 == 0 else grp

    tril_kb = (jnp.arange(kb)[:, None] >= jnp.arange(kb)[None, :]).astype(BF16)
    tril = (jnp.arange(cm)[:, None] >= jnp.arange(cm)[None, :]).astype(F32)
    w_router_p = jnp.zeros((d, LANES), F32).at[:, :N_EXPERTS].set(w_router)
    b_router_p = b_router.reshape(N_EXPERTS, 1)
    tri_moe = (jnp.arange(tmoe)[:, None] < jnp.arange(tmoe)[None, :]).astype(BF16)

    for l in range(depth):
        parts = [mod[l, :bsz, j * d:(j + 1) * d].reshape(bsz, 1, d) for j in range(6)]
        sh_m, sc_m, g_m, sh_f, sc_f, g_f = parts
        q, k, iq, ik2, r_proj, m_proj, small, small_t, v_t = _in_proj(
            x, sc_m, sh_m, w_in_p, l, cos_t, sin_t, tm, kb)
        o_a = _dsa(q, k, iq, ik2, small_t, v_t, tril_kb, tq, kb)
        o_b = _retention(r_proj, cr, grp_ret)
        gate_bias = (jnp.zeros((1, S_WIDTH), F32).at[0, S_MI:S_MI + N_HEADS_MLSTM].set(i_bias[l])
                     .at[0, S_MF:S_MF + N_HEADS_MLSTM].set(f_bias[l]))
        o_c = _mlstm(m_proj, small, gate_bias, conv_w[l], conv_b[l].reshape(1, -1), tril, cm, grp)
        x = _out_proj(o_a, o_b, o_c, w_out_b, l, x, g_m,
                      ln_mix_g[l].reshape(1, d), ln_mix_b[l].reshape(1, d), tm, alpha)
        x = _moe(x, sc_f, sh_f, g_f, w_router_p, b_router_p,
                 w_gate_b, w_up_b, w_down_b, l, tri_moe,
                 ln_ffn_g[l].reshape(1, d), ln_ffn_b[l].reshape(1, d), tmoe, moe_cap, alpha)
    return x
```

```python
import functools

import numpy as np
import jax
import jax.numpy as jnp
from jax import lax
from jax.experimental import pallas as pl
from jax.experimental.pallas import tpu as pltpu

F32 = jnp.float32
BF16 = jnp.bfloat16

HEAD_DIM = 64
CHUNK = 64
N_HEADS_ATTN = 8
N_IDX_HEADS = 4
IDX_DIM = 64
TOPK_MAX = 256
N_HEADS_RET = 4
N_HEADS_MLSTM = 4
CONV_WIDTH = 4
ROPE_THETA = 10000.0
N_EXPERTS = 16
N_GROUPS = 4
EXPERTS_PER_GROUP = N_EXPERTS // N_GROUPS
D_FF_EXPERT = 256
LN_EPS = 1e-5

MIX_ATTN = N_HEADS_ATTN * HEAD_DIM
MIX_RET = N_HEADS_RET * HEAD_DIM
MIX_MLSTM = N_HEADS_MLSTM * HEAD_DIM

LANES = 128
SUBLANES = 8
VMEM_LIMIT = 56 * 1024 * 1024

IQ_WIDTH = N_IDX_HEADS * IDX_DIM
R_WIDTH = 4 * MIX_RET
M_WIDTH = 4 * MIX_MLSTM
S_WIDTH = LANES
S_IW = 0
S_MI = S_IW + N_IDX_HEADS
S_MF = S_MI + N_HEADS_MLSTM
S_ROWS = 16
V_ROWS = HEAD_DIM + 16
OFF_Q = 0
OFF_K = OFF_Q + MIX_ATTN
OFF_IQ = OFF_K + MIX_ATTN
OFF_R = OFF_IQ + IQ_WIDTH
OFF_M = OFF_R + R_WIDTH
OFF_IK = OFF_M + M_WIDTH
OFF_S = OFF_IK + LANES
OFF_V = OFF_S + S_WIDTH
W_TOTAL = OFF_V + MIX_ATTN

INT_MIN = -2 ** 31
NEG_BIG = -1e30
LOG2_E = 1.4426950408889634


def _dot(a, b):
    return jnp.dot(a, b, preferred_element_type=F32)


def _dot_nt(a, b):
    return lax.dot_general(a, b, (((1,), (1,)), ((), ())), preferred_element_type=F32)


def _dot_tn(a, b):
    return lax.dot_general(a, b, (((0,), (0,)), ((), ())), preferred_element_type=F32)


def _params(sem):
    return pltpu.CompilerParams(dimension_semantics=sem, vmem_limit_bytes=VMEM_LIMIT)


def _ada_kernel(c_ref, w_ref, b_ref, o_ref):
    c = c_ref[...]
    c_act = c * jax.nn.sigmoid(c)
    o_ref[0] = _dot(c_act, w_ref[0]) + b_ref[0]


def _ada_mod(c_pad, w_ada, b_ada, tn):
    depth, d, n = w_ada.shape
    rows = c_pad.shape[0]
    return pl.pallas_call(
        _ada_kernel,
        grid=(depth, n // tn),
        in_specs=[pl.BlockSpec((rows, d), lambda l, j: (0, 0)),
                  pl.BlockSpec((1, d, tn), lambda l, j: (l, 0, j)),
                  pl.BlockSpec((1, 1, tn), lambda l, j: (l, 0, j))],
        out_specs=pl.BlockSpec((1, rows, tn), lambda l, j: (l, 0, j)),
        out_shape=jax.ShapeDtypeStruct((depth, rows, n), F32),
        compiler_params=_params(("parallel", "parallel")),
        name="ada_mod",
    )(c_pad, w_ada, b_ada.reshape(depth, 1, n))


def _rope(y, cos, sin):
    w = y.shape[1]
    reps = w // LANES
    cosw = jnp.concatenate([cos] * reps, axis=1) if reps > 1 else cos
    sinw = jnp.concatenate([sin] * reps, axis=1) if reps > 1 else sin
    lane = lax.broadcasted_iota(jnp.int32, y.shape, 1)
    first = (lane % HEAD_DIM) < (HEAD_DIM // 2)
    partner = jnp.where(first, pltpu.roll(y, w - HEAD_DIM // 2, 1), pltpu.roll(y, HEAD_DIM // 2, 1))
    return y * cosw + partner * sinw


def _inproj_kernel(x_ref, sc_ref, sh_ref, w_ref, cos_ref, sin_ref,
                   q_ref, k_ref, iq_ref, ik_ref, r_ref, m_ref, s_ref, st_ref, vt_ref, *, kb):
    h = (x_ref[0] * (1.0 + sc_ref[0]) + sh_ref[0]).astype(BF16)
    cos = cos_ref[0]
    sin = sin_ref[0]

    def proj(start, width):
        return _dot(h, w_ref[:, start:start + width])

    q_ref[0] = (_rope(proj(OFF_Q, MIX_ATTN), cos, sin) * (HEAD_DIM ** -0.5 * LOG2_E)).astype(BF16)
    k_ref[0] = _rope(proj(OFF_K, MIX_ATTN), cos, sin).astype(BF16)
    iq_ref[0] = _rope(proj(OFF_IQ, IQ_WIDTH), cos, sin).astype(BF16)
    ik_ref[0] = _rope(proj(OFF_IK, LANES), cos, sin).astype(BF16)
    r_ref[0, :, 0:2 * MIX_RET] = _rope(proj(OFF_R, 2 * MIX_RET), cos, sin).astype(BF16)
    r_ref[0, :, 2 * MIX_RET:R_WIDTH] = proj(OFF_R + 2 * MIX_RET, 2 * MIX_RET).astype(BF16)
    m_ref[0, :, 0:2 * MIX_MLSTM] = proj(OFF_M, 2 * MIX_MLSTM).astype(BF16)
    m_ref[0, :, 2 * MIX_MLSTM:M_WIDTH] = proj(OFF_M + 2 * MIX_MLSTM, 2 * MIX_MLSTM).astype(BF16)
    y = proj(OFF_S, S_WIDTH)
    s_ref[0] = y
    st_ref[0] = y.T[0:S_ROWS, :]
    tm = h.shape[0]
    yvt = proj(OFF_V, MIX_ATTN).T.astype(BF16)
    pad_rows = lax.broadcasted_iota(jnp.int32, (V_ROWS - HEAD_DIM, kb), 0)
    ones_rows = jnp.where(pad_rows == 0, 1.0, 0.0).astype(BF16)
    for j in range(tm // kb):
        for hh in range(N_HEADS_ATTN):
            vt_ref[0, j, hh * V_ROWS:hh * V_ROWS + HEAD_DIM, :] = yvt[hh * HEAD_DIM:(hh + 1) * HEAD_DIM,
                                                                      j * kb:(j + 1) * kb]
            vt_ref[0, j, hh * V_ROWS + HEAD_DIM:(hh + 1) * V_ROWS, :] = ones_rows


def _in_proj(x, sc, sh, w, layer, cos_t, sin_t, tm, kb):
    bsz, seq, d = x.shape
    row = lambda b, i: (b, i, 0)
    per_b = lambda b, i: (b, 0, 0)
    widths = (MIX_ATTN, MIX_ATTN, IQ_WIDTH, LANES, R_WIDTH, M_WIDTH)
    return pl.pallas_call(
        functools.partial(_inproj_kernel, kb=kb),
        grid=(bsz, seq // tm),
        in_specs=[pl.BlockSpec((1, tm, d), row),
                  pl.BlockSpec((1, 1, d), per_b),
                  pl.BlockSpec((1, 1, d), per_b),
                  pl.BlockSpec((None, d, W_TOTAL), lambda b, i: (layer, 0, 0)),
                  pl.BlockSpec((1, tm, LANES), row),
                  pl.BlockSpec((1, tm, LANES), row)],
        out_specs=[pl.BlockSpec((1, tm, wd), row) for wd in widths]
                  + [pl.BlockSpec((1, tm, S_WIDTH), row),
                     pl.BlockSpec((1, S_ROWS, tm), lambda b, i: (b, 0, i)),
                     pl.BlockSpec((1, tm // kb, N_HEADS_ATTN * V_ROWS, kb), lambda b, i: (b, i, 0, 0))],
        out_shape=[jax.ShapeDtypeStruct((bsz, seq, wd), BF16) for wd in widths]
                  + [jax.ShapeDtypeStruct((bsz, seq, S_WIDTH), F32),
                     jax.ShapeDtypeStruct((bsz, S_ROWS, seq), F32),
                     jax.ShapeDtypeStruct((bsz, seq // kb, N_HEADS_ATTN * V_ROWS, kb), BF16)],
        compiler_params=_params(("parallel", "parallel")),
        name="in_proj",
    )(x, sc, sh, w, cos_t, sin_t)


def _dsa_kernel(q_ref, k_ref, iq_ref, ik_ref, st_ref, vt_ref, tril_ref, o_ref,
                key_scr, byte_scr, cand_scr, bias_scr, s_scr, p_scr, m_scr, alpha_scr, acc_scr,
                *, tq, kb, topk):
    q0 = pl.program_id(1) * tq
    n_blocks = (q0 + tq + kb - 1) // kb
    qpos = q0 + lax.broadcasted_iota(jnp.int32, (1, tq), 1)
    q_limit = (qpos // CHUNK + 1) * CHUNK
    krow = lax.broadcasted_iota(jnp.int32, (kb, 1), 0)

    def head_of_pair(x, h):
        pair = x[:, (h // 2) * LANES:(h // 2 + 1) * LANES]
        lane = lax.broadcasted_iota(jnp.int32, pair.shape, 1)
        keep = (lane < HEAD_DIM) if h % 2 == 0 else (lane >= HEAD_DIM)
        return jnp.where(keep, pair, jnp.zeros_like(pair))

    iw = st_ref[0][S_IW:S_IW + N_IDX_HEADS, :] * (N_IDX_HEADS ** -0.5 * IDX_DIM ** -0.5)
    iq = iq_ref[0]
    iq_heads = [head_of_pair(iq, h) for h in range(N_IDX_HEADS)]

    def score_block(c):
        k0 = pl.multiple_of(c * kb, kb)
        ik2 = ik_ref[0, pl.ds(k0, kb), :]
        score = jnp.zeros((kb, tq), F32)
        for h in range(N_IDX_HEADS):
            score = score + jnp.maximum(_dot_nt(ik2, iq_heads[h]), 0.0) * iw[h:h + 1, :]
        bits = pltpu.bitcast(score, jnp.int32)
        key = jnp.where(bits >= 0, bits, bits ^ jnp.int32(0x7FFFFFFF))
        key = jnp.where(k0 + krow < q_limit, key, jnp.int32(INT_MIN))
        key_scr[c] = key
        byte_scr[0, c] = ((key >> 24) + 128).astype(F32).astype(BF16)
        for lvl in range(1, 4):
            byte_scr[lvl, c] = ((key >> (24 - 8 * lvl)) & 255).astype(F32).astype(BF16)

    def score_pair(c2, carry):
        score_block(2 * c2)
        score_block(2 * c2 + 1)
        return carry

    lax.fori_loop(0, n_blocks // 2, score_pair, 0)

    @pl.when(n_blocks % 2 == 1)
    def _():
        score_block(n_blocks - 1)

    pack = 16
    one = jnp.ones((kb, tq), BF16)
    zero = jnp.zeros((kb, tq), BF16)

    def count_ge(lvl, cand):
        cand_b = cand.astype(BF16)

        def hits(c):
            plane = byte_scr[0, c] if lvl == 0 else cand_scr[c]
            hit = jnp.where(plane >= cand_b, one, zero)
            parts = [hit[i * pack:(i + 1) * pack, :] for i in range(kb // pack)]
            while len(parts) > 1:
                parts = [parts[i] + parts[i + 1] for i in range(0, len(parts), 2)]
            return parts[0]

        acc = lax.fori_loop(0, n_blocks // 2, lambda c2, a: a + hits(2 * c2) + hits(2 * c2 + 1),
                            jnp.zeros((pack, tq), BF16))
        acc = lax.cond(n_blocks % 2 == 1, lambda a: a + hits(n_blocks - 1), lambda a: a, acc)
        return jnp.sum(acc.astype(F32), axis=0, keepdims=True)

    above = jnp.zeros((1, tq), F32)
    t = jnp.zeros((1, tq), jnp.int32)
    for lvl in range(4):
        def bit_body(i, carry, lvl=lvl, above=above):
            v, rejected = carry
            cand = v + lax.shift_left(jnp.int32(1), 7 - i).astype(F32)
            cnt = count_ge(lvl, cand)
            ok = above + cnt >= topk
            return jnp.where(ok, cand, v), jnp.where(ok, rejected, cnt)

        v, rejected = lax.fori_loop(0, 8, bit_body, (jnp.zeros((1, tq), F32), jnp.zeros((1, tq), F32)))
        above = above + rejected
        t = t | lax.shift_left(v.astype(jnp.int32), 24 - 8 * lvl)
        if lvl < 3:
            v_b = v.astype(BF16)

            def narrow(c, lvl=lvl, v_b=v_b):
                plane = byte_scr[0, c] if lvl == 0 else cand_scr[c]
                cand_scr[c] = jnp.where(plane == v_b, byte_scr[lvl + 1, c], -one)

            def narrow_pair(c2, carry, narrow=narrow):
                narrow(2 * c2)
                narrow(2 * c2 + 1)
                return carry

            lax.fori_loop(0, n_blocks // 2, narrow_pair, 0)

            @pl.when(n_blocks % 2 == 1)
            def _(narrow=narrow):
                narrow(n_blocks - 1)
    thr = jnp.maximum(t ^ jnp.int32(INT_MIN), jnp.int32(INT_MIN + 1))
    need = topk - above

    q = q_ref[0]
    q_heads = [head_of_pair(q, h) for h in range(N_HEADS_ATTN)]
    m_scr[...] = jnp.full(m_scr.shape, NEG_BIG, F32)
    alpha_scr[...] = jnp.ones(alpha_scr.shape, F32)
    acc_scr[...] = jnp.zeros(acc_scr.shape, F32)
    p_scr[...] = jnp.zeros(p_scr.shape, BF16)

    def stage_logits(c, h):
        k0 = pl.multiple_of(c * kb, kb)
        kp = k_ref[0, pl.ds(k0, kb), (h // 2) * LANES:(h // 2 + 1) * LANES]
        s_scr[h] = _dot_nt(kp, q_heads[h])

    def stage_mask(c, ties_before):
        key = key_scr[c]
        tie = key == thr
        rank = _dot(tril_ref[...], jnp.where(tie, 1.0, 0.0).astype(BF16)) + ties_before
        sel = (key > thr) | (tie & (rank <= need))
        bias_scr[...] = jnp.where(sel, 0.0, NEG_BIG)
        return rank[kb - 1:kb, :]

    def stage_softmax(h):
        for half in range(tq // LANES):
            ln = slice(half * LANES, (half + 1) * LANES)
            s = s_scr[h, :, ln] + bias_scr[:, ln]
            m_old = m_scr[h, :, ln]
            m_new = jnp.maximum(m_old, jnp.max(s, axis=0, keepdims=True))
            p_scr[h, :, ln] = jnp.exp2(s - m_new).astype(BF16)
            alpha_scr[h, :, ln] = jnp.exp2(m_old - m_new)
            m_scr[h, :, ln] = m_new

    def stage_values(c, h):
        vt = vt_ref[0, c, h * V_ROWS:(h + 1) * V_ROWS, :]
        acc_scr[h, 0:V_ROWS, :] = alpha_scr[h] * acc_scr[h, 0:V_ROWS, :] + _dot(vt, p_scr[h])

    ties0 = stage_mask(0, jnp.zeros((1, tq), F32))
    for h in range(N_HEADS_ATTN):
        stage_logits(0, h)

    def attn_body(j, ties_before):
        c_old = jnp.maximum(j - 2, 0)
        for h in range(N_HEADS_ATTN):
            stage_values(c_old, h)
            stage_softmax(h)
            stage_logits(j, h)
        return stage_mask(j, ties_before)

    lax.fori_loop(1, n_blocks, attn_body, ties0)
    for h in range(N_HEADS_ATTN):
        stage_values(jnp.maximum(n_blocks - 2, 0), h)
        stage_softmax(h)
    for h in range(N_HEADS_ATTN):
        stage_values(n_blocks - 1, h)
    for h in range(N_HEADS_ATTN):
        acc = acc_scr[h]
        out = acc * (1.0 / acc[HEAD_DIM:HEAD_DIM + 1, :])
        o_ref[0, :, h * HEAD_DIM:(h + 1) * HEAD_DIM] = out.T[:, 0:HEAD_DIM].astype(o_ref.dtype)


def _dsa(q, k, iq, ik2, small_t, v_t, tril, tq, kb):
    bsz, seq, _ = q.shape
    topk = min(TOPK_MAX, seq // 4)
    kern = functools.partial(_dsa_kernel, tq=tq, kb=kb, topk=topk)
    return pl.pallas_call(
        kern,
        grid=(bsz, seq // tq),
        in_specs=[pl.BlockSpec((1, tq, MIX_ATTN), lambda b, i: (b, i, 0)),
                  pl.BlockSpec((1, seq, MIX_ATTN), lambda b, i: (b, 0, 0)),
                  pl.BlockSpec((1, tq, IQ_WIDTH), lambda b, i: (b, i, 0)),
                  pl.BlockSpec((1, seq, LANES), lambda b, i: (b, 0, 0)),
                  pl.BlockSpec((1, S_ROWS, tq), lambda b, i: (b, 0, i)),
                  pl.BlockSpec((1, seq // kb, N_HEADS_ATTN * V_ROWS, kb), lambda b, i: (b, 0, 0, 0)),
                  pl.BlockSpec((kb, kb), lambda b, i: (0, 0))],
        out_specs=pl.BlockSpec((1, tq, MIX_ATTN), lambda b, i: (b, i, 0)),
        out_shape=jax.ShapeDtypeStruct((bsz, seq, MIX_ATTN), BF16),
        scratch_shapes=[pltpu.VMEM((seq // kb, kb, tq), jnp.int32),
                        pltpu.VMEM((4, seq // kb, kb, tq), BF16),
                        pltpu.VMEM((seq // kb, kb, tq), BF16),
                        pltpu.VMEM((kb, tq), F32),
                        pltpu.VMEM((N_HEADS_ATTN, kb, tq), F32),
                        pltpu.VMEM((N_HEADS_ATTN, kb, tq), BF16),
                        pltpu.VMEM((N_HEADS_ATTN, 1, tq), F32),
                        pltpu.VMEM((N_HEADS_ATTN, 1, tq), F32),
                        pltpu.VMEM((N_HEADS_ATTN, LANES, tq), F32)],
        compiler_params=_params(("parallel", "arbitrary")),
        name="dsa",
    )(q, k, iq, ik2, small_t, v_t, tril)


def _head_norm(y):
    mean_mat = jnp.full((HEAD_DIM, HEAD_DIM), 1.0 / HEAD_DIM, BF16)

    def mean_bcast(x):
        hi = x.astype(BF16)
        lo = (x - hi.astype(F32)).astype(BF16)
        return _dot(hi, mean_mat) + _dot(lo, mean_mat)

    yc = y - mean_bcast(y)
    return yc * lax.rsqrt(mean_bcast(yc * yc) + LN_EPS)


def _ret_kernel(r_ref, o_ref, state_scr, *, cr, grp):
    @pl.when(pl.program_id(1) == 0)
    def _():
        state_scr[...] = jnp.zeros_like(state_scr)

    ri = lax.broadcasted_iota(jnp.int32, (cr, cr), 0)
    ci = lax.broadcasted_iota(jnp.int32, (cr, cr), 1)
    diff = (ri - ci).astype(F32)
    pos = lax.broadcasted_iota(jnp.int32, (cr, 1), 0).astype(F32)
    items = [(h, g) for h in range(N_HEADS_RET) for g in range(grp)]
    sl = lambda part, h: slice(part * MIX_RET + h * HEAD_DIM, part * MIX_RET + (h + 1) * HEAD_DIM)
    log_gamma = [jnp.log1p(jnp.full((1, 1), -(2.0 ** (-5.0 - h)), F32)) for h in range(N_HEADS_RET)]
    decay_in = [jnp.where(diff >= 0, jnp.exp(diff * lg), 0.0) * (HEAD_DIM ** -0.5) for lg in log_gamma]
    q = {(h, g): r_ref[g, :, sl(0, h)] for h, g in items}
    k = {(h, g): r_ref[g, :, sl(1, h)] for h, g in items}
    v = {(h, g): r_ref[g, :, sl(2, h)] for h, g in items}
    state = {(h, g): state_scr[g, h] for h, g in items}
    scores = {it: _dot_nt(q[it], k[it]) * decay_in[it[0]] for it in items}
    cross = {it: jnp.exp((pos + 1.0) * log_gamma[it[0]]) * _dot(q[it], state[it].astype(BF16)) for it in items}
    inner = {it: _dot(scores[it].astype(BF16), v[it]) for it in items}
    for it in items:
        h, g = it
        gate = r_ref[g, :, sl(3, h)].astype(F32)
        y = _head_norm(inner[it] + cross[it])
        o_ref[g, :, h * HEAD_DIM:(h + 1) * HEAD_DIM] = (y * (gate * jax.nn.sigmoid(gate))).astype(o_ref.dtype)
    for it in items:
        h, g = it
        k_decay = (HEAD_DIM ** -0.5) * jnp.exp((cr - 1.0 - pos) * log_gamma[h])
        k_dec = (k[it].astype(F32) * k_decay).astype(BF16)
        state_scr[g, h] = state[it] * jnp.exp(cr * log_gamma[h]) + _dot_tn(k_dec, v[it])


def _retention(r_proj, cr, grp):
    bsz, seq, _ = r_proj.shape
    return pl.pallas_call(
        functools.partial(_ret_kernel, cr=cr, grp=grp),
        grid=(bsz // grp, seq // cr),
        in_specs=[pl.BlockSpec((grp, cr, R_WIDTH), lambda b, i: (b, i, 0))],
        out_specs=pl.BlockSpec((grp, cr, MIX_RET), lambda b, i: (b, i, 0)),
        out_shape=jax.ShapeDtypeStruct((bsz, seq, MIX_RET), BF16),
        scratch_shapes=[pltpu.VMEM((grp, N_HEADS_RET, HEAD_DIM, HEAD_DIM), F32)],
        compiler_params=_params(("parallel", "arbitrary")),
        name="retention",
    )(r_proj)


def _mlstm_kernel(m_ref, sm_ref, bias_ref, cw_ref, cb_ref, tril_ref, o_ref,
                  xbuf, a_scr, m_scr, *, cm, grp):
    halo = 8
    gate_shift = S_MF - S_MI

    @pl.when(pl.program_id(1) == 0)
    def _():
        xbuf[:, 0:halo, :] = jnp.zeros((grp, halo, 2 * MIX_MLSTM), F32)
        a_scr[...] = jnp.zeros_like(a_scr)
        m_scr[...] = jnp.zeros_like(m_scr)

    ri = lax.broadcasted_iota(jnp.int32, (cm, cm), 0)
    ci = lax.broadcasted_iota(jnp.int32, (cm, cm), 1)
    causal = ri >= ci
    row = lax.broadcasted_iota(jnp.int32, (cm, LANES), 0)
    lane = lax.broadcasted_iota(jnp.int32, (cm, LANES), 1)
    gate_lanes = (lane >= S_MI) & (lane < S_MI + N_HEADS_MLSTM)
    lane64 = lax.broadcasted_iota(jnp.int32, (cm, HEAD_DIM), 1)
    ones_col = jnp.where(lane64 == 0, 1.0, 0.0).astype(BF16)
    scale = HEAD_DIM ** -0.5

    def head_cols(x, h):
        return x[:, h * HEAD_DIM:(h + 1) * HEAD_DIM]

    items = [(g, h) for g in range(grp) for h in range(N_HEADS_MLSTM)]
    mm, qk, u_row, m_col, inter, e_inv, kw_col, decay = {}, {}, {}, {}, {}, {}, {}, {}
    for g in range(grp):
        mm[g] = m_ref[g]
        xbuf[g, halo:halo + cm, :] = mm[g][:, 0:2 * MIX_MLSTM].astype(F32)
        conv = cb_ref[...]
        for j in range(CONV_WIDTH):
            off = halo - (CONV_WIDTH - 1) + j
            conv = conv + xbuf[g, off:off + cm, :] * cw_ref[j:j + 1, :]
        xbuf[g, 0:halo, :] = xbuf[g, cm:cm + halo, :]
        qk[g] = conv * jax.nn.sigmoid(conv)

        gates = sm_ref[g] + bias_ref[...]
        b_all = jnp.dot(tril_ref[...], jax.nn.log_sigmoid(gates), preferred_element_type=F32,
                        precision=lax.Precision.HIGHEST)
        b_i = jnp.where(gate_lanes, pltpu.roll(b_all, LANES - gate_shift, 1), 0.0)
        u = jnp.where(gate_lanes, gates, 0.0) - b_i
        run = u
        step = 1
        while step < cm:
            run = jnp.maximum(run, jnp.where(row >= step, pltpu.roll(run, step, 0), -jnp.inf))
            step *= 2
        m_prev = m_scr[g]
        m_c = jnp.maximum(m_prev, run)
        m_last = m_c[cm - 1:cm, :]
        u_row[g] = u.T
        m_col[g] = m_c
        inter[g] = jnp.exp(m_prev - m_c)
        e_inv[g] = jnp.exp(-(b_i + m_c))
        kw_col[g] = scale * jnp.exp(u - m_last)
        decay[g] = jnp.exp(m_prev - m_last)
        m_scr[g] = b_i[cm - 1:cm, :] + m_last

    q, k, v_aug, a_mem = {}, {}, {}, {}
    for it in items:
        g, h = it
        q[it] = head_cols(qk[g], h).astype(BF16)
        k[it] = head_cols(qk[g][:, MIX_MLSTM:], h)
        v_aug[it] = jnp.concatenate([head_cols(mm[g][:, 2 * MIX_MLSTM:], h), ones_col], axis=1)
        a_mem[it] = a_scr[g, h]

    def col(x, h):
        return x[:, S_MI + h:S_MI + h + 1]

    w = {}
    for it in items:
        g, h = it
        u_r = u_row[g][S_MI + h:S_MI + h + 1, :]
        w[it] = jnp.exp(jnp.where(causal, u_r - col(m_col[g], h), -jnp.inf))

    s = {it: _dot_nt(q[it], k[it].astype(BF16)) * scale * w[it] for it in items}
    cross = {it: _dot(q[it], a_mem[it].astype(BF16)) for it in items}
    both = {it: _dot(s[it].astype(BF16), v_aug[it]) + col(inter[it[0]], it[1]) * cross[it] for it in items}
    for it in items:
        g, h = it
        den = both[it][:, HEAD_DIM:HEAD_DIM + 1]
        h_tilde = both[it][:, 0:HEAD_DIM] * (1.0 / jnp.maximum(jnp.abs(den), col(e_inv[g], h)))
        og = head_cols(mm[g][:, 3 * MIX_MLSTM:], h).astype(F32)
        o_ref[g, :, h * HEAD_DIM:(h + 1) * HEAD_DIM] = _head_norm(jax.nn.sigmoid(og) * h_tilde).astype(o_ref.dtype)

    for it in items:
        g, h = it
        kw = k[it] * col(kw_col[g], h)
        a_scr[g, h] = col(decay[g], h) * a_mem[it] + _dot_tn(kw.astype(BF16), v_aug[it])


def _mlstm(m_proj, small, gate_bias, conv_w, conv_b, tril, cm, grp):
    bsz, seq, _ = m_proj.shape
    const = lambda b, i: (0, 0)
    return pl.pallas_call(
        functools.partial(_mlstm_kernel, cm=cm, grp=grp),
        grid=(bsz // grp, seq // cm),
        in_specs=[pl.BlockSpec((grp, cm, M_WIDTH), lambda b, i: (b, i, 0)),
                  pl.BlockSpec((grp, cm, S_WIDTH), lambda b, i: (b, i, 0)),
                  pl.BlockSpec((1, S_WIDTH), const),
                  pl.BlockSpec((CONV_WIDTH, 2 * MIX_MLSTM), const),
                  pl.BlockSpec((1, 2 * MIX_MLSTM), const),
                  pl.BlockSpec((cm, cm), const)],
        out_specs=pl.BlockSpec((grp, cm, MIX_MLSTM), lambda b, i: (b, i, 0)),
        out_shape=jax.ShapeDtypeStruct((bsz, seq, MIX_MLSTM), BF16),
        scratch_shapes=[pltpu.VMEM((grp, cm + 8, 2 * MIX_MLSTM), F32),
                        pltpu.VMEM((grp, N_HEADS_MLSTM, HEAD_DIM, LANES), F32),
                        pltpu.VMEM((grp, 1, LANES), F32)],
        compiler_params=_params(("parallel", "arbitrary")),
        name="mlstm",
    )(m_proj, small, gate_bias, conv_w, conv_b, tril)


def _layer_norm(z, g, b):
    mu = jnp.mean(z, axis=-1, keepdims=True)
    var = jnp.mean(jnp.square(z - mu), axis=-1, keepdims=True)
    return (z - mu) * lax.rsqrt(var + LN_EPS) * g + b


def _outproj_kernel(oa_ref, ob_ref, oc_ref, w_ref, x_ref, gm_ref, g_ref, b_ref, o_ref, *, alpha):
    mix = _dot(oa_ref[0], w_ref[0:MIX_ATTN, :])
    mix = mix + _dot(ob_ref[0], w_ref[MIX_ATTN:MIX_ATTN + MIX_RET, :])
    mix = mix + _dot(oc_ref[0], w_ref[MIX_ATTN + MIX_RET:, :])
    z = alpha * x_ref[0] + (1.0 + gm_ref[0]) * mix
    o_ref[0] = _layer_norm(z, g_ref[...], b_ref[...])


def _out_proj(o_a, o_b, o_c, w_out, layer, x, g_m, ln_g, ln_b, tm, alpha):
    bsz, seq, d = x.shape
    row = lambda b, i: (b, i, 0)
    const = lambda b, i: (0, 0)
    return pl.pallas_call(
        functools.partial(_outproj_kernel, alpha=alpha),
        grid=(bsz, seq // tm),
        in_specs=[pl.BlockSpec((1, tm, MIX_ATTN), row),
                  pl.BlockSpec((1, tm, MIX_RET), row),
                  pl.BlockSpec((1, tm, MIX_MLSTM), row),
                  pl.BlockSpec((None,) + w_out.shape[1:], lambda b, i: (layer, 0, 0)),
                  pl.BlockSpec((1, tm, d), row),
                  pl.BlockSpec((1, 1, d), lambda b, i: (b, 0, 0)),
                  pl.BlockSpec((1, d), const),
                  pl.BlockSpec((1, d), const)],
        out_specs=pl.BlockSpec((1, tm, d), row),
        out_shape=jax.ShapeDtypeStruct((bsz, seq, d), F32),
        compiler_params=_params(("parallel", "parallel")),
        name="out_proj",
    )(o_a, o_b, o_c, w_out, x, g_m, ln_g, ln_b)


def _route(scores, biased):
    col = lambda a, e: a[e:e + 1, :]
    epg = EXPERTS_PER_GROUP
    group_scores = []
    for g in range(N_GROUPS):
        vals = [col(biased, g * epg + j) for j in range(epg)]
        best = None
        for a in range(epg):
            for b in range(a + 1, epg):
                pair = vals[a] + vals[b]
                best = pair if best is None else jnp.maximum(best, pair)
        group_scores.append(best)
    best_g = jnp.zeros_like(group_scores[0], dtype=jnp.int32)
    best_v = group_scores[0]
    for g in range(1, N_GROUPS):
        better = group_scores[g] > best_v
        best_g = jnp.where(better, g, best_g)
        best_v = jnp.where(better, group_scores[g], best_v)
    cand_b = [sum(jnp.where(best_g == g, col(biased, g * epg + j), 0.0) for g in range(N_GROUPS))
              for j in range(epg)]
    cand_s = [sum(jnp.where(best_g == g, col(scores, g * epg + j), 0.0) for g in range(N_GROUPS))
              for j in range(epg)]

    def argmax_first(vals, skip=None):
        idx = None
        val = None
        for j, vj in enumerate(vals):
            if skip is not None:
                vj = jnp.where(skip == j, -jnp.inf, vj)
            if idx is None:
                idx, val = jnp.zeros_like(best_g), vj
            else:
                better = vj > val
                idx = jnp.where(better, j, idx)
                val = jnp.where(better, vj, val)
        return idx

    first = argmax_first(cand_b)
    second = argmax_first(cand_b, skip=first)
    w1 = sum(jnp.where(first == j, cand_s[j], 0.0) for j in range(epg))
    w2 = sum(jnp.where(second == j, cand_s[j], 0.0) for j in range(epg))
    total = w1 + w2
    e1 = best_g * epg + first
    e2 = best_g * epg + second
    expert = lax.broadcasted_iota(jnp.int32, scores.shape, 0)
    gate = jnp.where(expert == e1, w1 / total, 0.0) + jnp.where(expert == e2, w2 / total, 0.0)
    return gate, best_g


def _split_bf16(x):
    hi = x.astype(BF16)
    return hi, (x - hi.astype(F32)).astype(BF16)


def _moe_kernel(x_ref, sc_ref, sh_ref, gf_ref, wr_ref, br_ref, wg_ref, wu_ref, wd_ref, tri_ref, g_ref, b_ref,
                o_ref, hid_scr, *, alpha, cap):
    x = x_ref[0]
    tm = x.shape[0]
    h = x * (1.0 + sc_ref[0]) + sh_ref[0]
    hb = h.astype(BF16)
    n_exp, _, d_ff = wg_ref.shape
    scores_t = jax.nn.sigmoid(_dot(h, wr_ref[...]).T[0:n_exp, :])
    gate_t, best_g = _route(scores_t, scores_t + br_ref[...])
    sub = lax.broadcasted_iota(jnp.int32, (2 * SUBLANES, tm), 0)
    member_t = jnp.where(sub == best_g, 1.0, 0.0)
    before_t = _dot(member_t.astype(BF16), tri_ref[...])
    rank = jnp.sum(member_t * before_t, axis=0, keepdims=True)
    count = jnp.max(jnp.sum(member_t, axis=1, keepdims=True))
    grp_f = best_g.astype(F32)
    sub8 = lax.broadcasted_iota(jnp.int32, (SUBLANES, tm), 0)
    extra = jnp.where(sub8 == 0, grp_f, jnp.where(sub8 == 1, rank, 0.0))
    info = jnp.concatenate([gate_t, extra, jnp.zeros((LANES - n_exp - SUBLANES, tm), F32)], axis=0).T
    gate = info
    epg = EXPERTS_PER_GROUP

    def expert_hidden(rows_b, gate_rows, e, dst):
        gate_pre = _dot(rows_b, wg_ref[e])
        up = _dot(rows_b, wu_ref[e])
        hid = gate_pre * jax.nn.sigmoid(gate_pre) * up * gate_rows[:, e:e + 1]
        hid_scr[0:rows_b.shape[0], dst * d_ff:(dst + 1) * d_ff] = hid.astype(BF16)

    def dense(_):
        for e in range(n_exp):
            expert_hidden(hb, gate, e, e)
        return _dot(hid_scr[...], wd_ref[...])


    def grouped(_):
        grp_c = info[:, n_exp:n_exp + 1]
        rank_c = info[:, n_exp + 1:n_exp + 2]
        slot_c = lax.broadcasted_iota(jnp.int32, (cap, 1), 0).astype(F32)
        slot_r = lax.broadcasted_iota(jnp.int32, (1, cap), 1).astype(F32)
        gate_hi, gate_lo = _split_bf16(gate)
        y = jnp.zeros((tm, x.shape[1]), F32)
        for g in range(N_GROUPS):
            take = jnp.where(jnp.where(grp_f == g, rank, -1.0) == slot_c, 1.0, 0.0).astype(BF16)
            give = jnp.where(jnp.where(grp_c == g, rank_c, -1.0) == slot_r, 1.0, 0.0).astype(BF16)
            rows_b = _dot(take, hb).astype(BF16)
            gate_rows = _dot(take, gate_hi) + _dot(take, gate_lo)
            for j in range(epg):
                expert_hidden(rows_b, gate_rows, g * epg + j, j)
            out = _dot(hid_scr[0:cap, 0:epg * d_ff], wd_ref[g * epg * d_ff:(g + 1) * epg * d_ff, :])
            y = y + _dot(give, out.astype(BF16))
        return y

    y = lax.cond(count <= cap, grouped, dense, 0)
    z = alpha * x + (1.0 + gf_ref[0]) * y
    o_ref[0] = _layer_norm(z, g_ref[...], b_ref[...])


def _moe(x, sc, sh, g_f, w_router, b_router, w_gate, w_up, w_down, layer, tri, ln_g, ln_b, tm, cap, alpha):
    bsz, seq, d = x.shape
    _, n_exp, _, d_ff = w_gate.shape
    row = lambda b, i: (b, i, 0)
    per_b = lambda b, i: (b, 0, 0)
    const = lambda b, i: (0, 0)
    of_layer4 = lambda b, i: (layer, 0, 0, 0)
    of_layer3 = lambda b, i: (layer, 0, 0)
    resident = pl.Buffered(1)
    return pl.pallas_call(
        functools.partial(_moe_kernel, alpha=alpha, cap=cap),
        grid=(bsz, seq // tm),
        in_specs=[pl.BlockSpec((1, tm, d), row),
                  pl.BlockSpec((1, 1, d), per_b),
                  pl.BlockSpec((1, 1, d), per_b),
                  pl.BlockSpec((1, 1, d), per_b),
                  pl.BlockSpec((d, LANES), const),
                  pl.BlockSpec((n_exp, 1), const),
                  pl.BlockSpec((None, n_exp, d, d_ff), of_layer4, pipeline_mode=resident),
                  pl.BlockSpec((None, n_exp, d, d_ff), of_layer4, pipeline_mode=resident),
                  pl.BlockSpec((None, n_exp * d_ff, d), of_layer3, pipeline_mode=resident),
                  pl.BlockSpec((tm, tm), const, pipeline_mode=resident),
                  pl.BlockSpec((1, d), const),
                  pl.BlockSpec((1, d), const)],
        out_specs=pl.BlockSpec((1, tm, d), row),
        out_shape=jax.ShapeDtypeStruct((bsz, seq, d), F32),
        scratch_shapes=[pltpu.VMEM((tm, n_exp * d_ff), BF16)],
        compiler_params=_params(("parallel", "parallel")),
        name="moe",
    )(x, sc, sh, g_f, w_router, b_router, w_gate, w_up, w_down, tri, ln_g, ln_b)


def _pick(n, pref):
    t = min(pref, n)
    while n % t:
        t //= 2
    return t


def _rope_tables(positions):
    half = HEAD_DIM // 2
    inv_freq = ROPE_THETA ** (-jnp.arange(half, dtype=F32) / half)
    ang = positions.astype(F32)[..., None] * inv_freq
    cos, sin = jnp.cos(ang), jnp.sin(ang)
    reps = LANES // HEAD_DIM
    return (jnp.concatenate([cos, cos] * reps, axis=-1),
            jnp.concatenate([-sin, sin] * reps, axis=-1))


def _prep_w_kernel(w_ref, o_ref):
    o_v = 2 * MIX_ATTN
    o_iq = o_v + MIX_ATTN
    o_ik = o_iq + IQ_WIDTH
    o_iw = o_ik + IDX_DIM
    o_r = o_iw + N_IDX_HEADS
    o_g = o_r + R_WIDTH + M_WIDTH
    n_gate = 2 * N_HEADS_MLSTM

    def put(dst, src, width):
        o_ref[0, :, dst:dst + width] = w_ref[0, :, src:src + width].astype(o_ref.dtype)

    o_ref[0] = jnp.zeros(o_ref.shape[1:], o_ref.dtype)
    put(OFF_Q, 0, 2 * MIX_ATTN)
    put(OFF_IQ, o_iq, IQ_WIDTH)
    put(OFF_R, o_r, R_WIDTH + M_WIDTH)
    put(OFF_IK, o_ik, IDX_DIM)
    put(OFF_IK + IDX_DIM, o_ik, IDX_DIM)
    put(OFF_S + S_IW, o_iw, N_IDX_HEADS)
    put(OFF_S + S_MI, o_g, n_gate)
    put(OFF_V, o_v, MIX_ATTN)


def _prep_w_in(w_in, tr):
    depth, d, n = w_in.shape
    return pl.pallas_call(
        _prep_w_kernel,
        grid=(depth, d // tr),
        in_specs=[pl.BlockSpec((1, tr, n), lambda l, i: (l, i, 0))],
        out_specs=pl.BlockSpec((1, tr, W_TOTAL), lambda l, i: (l, i, 0)),
        out_shape=jax.ShapeDtypeStruct((depth, d, W_TOTAL), F32),
        compiler_params=_params(("parallel", "parallel")),
        name="prep_w_in",
    )(w_in)


def kernel(x, c, positions, w_ada, b_ada, w_in, i_bias, f_bias, conv_w, conv_b, w_out, ln_mix_g, ln_mix_b,
           w_router, b_router, w_gate, w_up, w_down, ln_ffn_g, ln_ffn_b):
    bsz, seq, d = x.shape
    depth = w_ada.shape[0]
    alpha = (2.0 * depth) ** 0.25

    tm = _pick(seq, 512)
    tq = _pick(seq, 256)
    kb = _pick(seq, 256)
    assert seq // 16 <= 256, "packed bf16 partial counts in the DSA threshold search must stay exact"
    cr = _pick(seq, 256)
    cm = _pick(seq, 256)
    tmoe = _pick(seq, 512)
    moe_cap = min(tmoe, (3 * tmoe // (2 * N_GROUPS) + 15) // 16 * 16)

    cos_t, sin_t = _rope_tables(positions)
    c_pad = jnp.zeros((8, d), F32).at[:bsz].set(c)
    mod = _ada_mod(c_pad, w_ada, b_ada, _pick(6 * d, 1536))
    w_in_p = _prep_w_in(w_in, _pick(d, 256)).astype(BF16)
    w_out_b = w_out.astype(BF16)
    w_gate_b, w_up_b = w_gate.astype(BF16), w_up.astype(BF16)
    w_down_b = w_down.astype(BF16).reshape(depth, -1, d)
    grp = 2 if bsz % 2 == 0 else 1
    grp_ret = 4 if bsz % 4 == 0 else grp

    tril_kb = (jnp.arange(kb)[:, None] >= jnp.arange(kb)[None, :]).astype(BF16)
    tril = (jnp.arange(cm)[:, None] >= jnp.arange(cm)[None, :]).astype(F32)
    w_router_p = jnp.zeros((d, LANES), F32).at[:, :N_EXPERTS].set(w_router)
    b_router_p = b_router.reshape(N_EXPERTS, 1)
    tri_moe = (jnp.arange(tmoe)[:, None] < jnp.arange(tmoe)[None, :]).astype(BF16)

    for l in range(depth):
        parts = [mod[l, :bsz, j * d:(j + 1) * d].reshape(bsz, 1, d) for j in range(6)]
        sh_m, sc_m, g_m, sh_f, sc_f, g_f = parts
        q, k, iq, ik2, r_proj, m_proj, small, small_t, v_t = _in_proj(
            x, sc_m, sh_m, w_in_p, l, cos_t, sin_t, tm, kb)
        o_a = _dsa(q, k, iq, ik2, small_t, v_t, tril_kb, tq, kb)
        o_b = _retention(r_proj, cr, grp_ret)
        gate_bias = (jnp.zeros((1, S_WIDTH), F32).at[0, S_MI:S_MI + N_HEADS_MLSTM].set(i_bias[l])
                     .at[0, S_MF:S_MF + N_HEADS_MLSTM].set(f_bias[l]))
        o_c = _mlstm(m_proj, small, gate_bias, conv_w[l], conv_b[l].reshape(1, -1), tril, cm, grp)
        x = _out_proj(o_a, o_b, o_c, w_out_b, l, x, g_m,
                      ln_mix_g[l].reshape(1, d), ln_mix_b[l].reshape(1, d), tm, alpha)
        x = _moe(x, sc_f, sh_f, g_f, w_router_p, b_router_p,
                 w_gate_b, w_up_b, w_down_b, l, tri_moe,
                 ln_ffn_g[l].reshape(1, d), ln_ffn_b[l].reshape(1, d), tmoe, moe_cap, alpha)
    return x
```

```python
import functools

import numpy as np
import jax
import jax.numpy as jnp
from jax import lax
from jax.experimental import pallas as pl
from jax.experimental.pallas import tpu as pltpu

F32 = jnp.float32
BF16 = jnp.bfloat16

HEAD_DIM = 64
CHUNK = 64
N_HEADS_ATTN = 8
N_IDX_HEADS = 4
IDX_DIM = 64
TOPK_MAX = 256
N_HEADS_RET = 4
N_HEADS_MLSTM = 4
CONV_WIDTH = 4
ROPE_THETA = 10000.0
N_EXPERTS = 16
N_GROUPS = 4
EXPERTS_PER_GROUP = N_EXPERTS // N_GROUPS
D_FF_EXPERT = 256
LN_EPS = 1e-5

MIX_ATTN = N_HEADS_ATTN * HEAD_DIM
MIX_RET = N_HEADS_RET * HEAD_DIM
MIX_MLSTM = N_HEADS_MLSTM * HEAD_DIM

LANES = 128
SUBLANES = 8
VMEM_LIMIT = 56 * 1024 * 1024

IQ_WIDTH = N_IDX_HEADS * IDX_DIM
R_WIDTH = 4 * MIX_RET
M_WIDTH = 4 * MIX_MLSTM
S_WIDTH = LANES
S_IW = 0
S_MI = S_IW + N_IDX_HEADS
S_MF = S_MI + N_HEADS_MLSTM
S_ROWS = 16
V_ROWS = HEAD_DIM + 16
OFF_Q = 0
OFF_K = OFF_Q + MIX_ATTN
OFF_IQ = OFF_K + MIX_ATTN
OFF_R = OFF_IQ + IQ_WIDTH
OFF_M = OFF_R + R_WIDTH
OFF_IK = OFF_M + M_WIDTH
OFF_S = OFF_IK + LANES
OFF_V = OFF_S + S_WIDTH
W_TOTAL = OFF_V + MIX_ATTN

INT_MIN = -2 ** 31
NEG_BIG = -1e30
LOG2_E = 1.4426950408889634


def _dot(a, b):
    return jnp.dot(a, b, preferred_element_type=F32)


def _dot_nt(a, b):
    return lax.dot_general(a, b, (((1,), (1,)), ((), ())), preferred_element_type=F32)


def _dot_tn(a, b):
    return lax.dot_general(a, b, (((0,), (0,)), ((), ())), preferred_element_type=F32)


def _params(sem):
    return pltpu.CompilerParams(dimension_semantics=sem, vmem_limit_bytes=VMEM_LIMIT)


def _ada_kernel(c_ref, w_ref, b_ref, o_ref):
    c = c_ref[...]
    c_act = c * jax.nn.sigmoid(c)
    o_ref[0] = _dot(c_act, w_ref[0]) + b_ref[0]


def _ada_mod(c_pad, w_ada, b_ada, tn):
    depth, d, n = w_ada.shape
    rows = c_pad.shape[0]
    return pl.pallas_call(
        _ada_kernel,
        grid=(depth, n // tn),
        in_specs=[pl.BlockSpec((rows, d), lambda l, j: (0, 0)),
                  pl.BlockSpec((1, d, tn), lambda l, j: (l, 0, j)),
                  pl.BlockSpec((1, 1, tn), lambda l, j: (l, 0, j))],
        out_specs=pl.BlockSpec((1, rows, tn), lambda l, j: (l, 0, j)),
        out_shape=jax.ShapeDtypeStruct((depth, rows, n), F32),
        compiler_params=_params(("parallel", "parallel")),
        name="ada_mod",
    )(c_pad, w_ada, b_ada.reshape(depth, 1, n))


def _rope(y, cos, sin):
    w = y.shape[1]
    reps = w // LANES
    cosw = jnp.concatenate([cos] * reps, axis=1) if reps > 1 else cos
    sinw = jnp.concatenate([sin] * reps, axis=1) if reps > 1 else sin
    lane = lax.broadcasted_iota(jnp.int32, y.shape, 1)
    first = (lane % HEAD_DIM) < (HEAD_DIM // 2)
    partner = jnp.where(first, pltpu.roll(y, w - HEAD_DIM // 2, 1), pltpu.roll(y, HEAD_DIM // 2, 1))
    return y * cosw + partner * sinw


def _inproj_kernel(x_ref, sc_ref, sh_ref, w_ref, cos_ref, sin_ref,
                   q_ref, k_ref, iq_ref, ik_ref, r_ref, m_ref, s_ref, st_ref, vt_ref, *, kb):
    h = (x_ref[0] * (1.0 + sc_ref[0]) + sh_ref[0]).astype(BF16)
    cos = cos_ref[0]
    sin = sin_ref[0]

    def proj(start, width):
        return _dot(h, w_ref[:, start:start + width])

    q_ref[0] = (_rope(proj(OFF_Q, MIX_ATTN), cos, sin) * (HEAD_DIM ** -0.5 * LOG2_E)).astype(BF16)
    k_ref[0] = _rope(proj(OFF_K, MIX_ATTN), cos, sin).astype(BF16)
    iq_ref[0] = _rope(proj(OFF_IQ, IQ_WIDTH), cos, sin).astype(BF16)
    ik_ref[0] = _rope(proj(OFF_IK, LANES), cos, sin).astype(BF16)
    r_ref[0, :, 0:2 * MIX_RET] = _rope(proj(OFF_R, 2 * MIX_RET), cos, sin).astype(BF16)
    r_ref[0, :, 2 * MIX_RET:R_WIDTH] = proj(OFF_R + 2 * MIX_RET, 2 * MIX_RET).astype(BF16)
    m_ref[0, :, 0:2 * MIX_MLSTM] = proj(OFF_M, 2 * MIX_MLSTM).astype(BF16)
    m_ref[0, :, 2 * MIX_MLSTM:M_WIDTH] = proj(OFF_M + 2 * MIX_MLSTM, 2 * MIX_MLSTM).astype(BF16)
    y = proj(OFF_S, S_WIDTH)
    s_ref[0] = y
    st_ref[0] = y.T[0:S_ROWS, :]
    tm = h.shape[0]
    yvt = proj(OFF_V, MIX_ATTN).T.astype(BF16)
    pad_rows = lax.broadcasted_iota(jnp.int32, (V_ROWS - HEAD_DIM, kb), 0)
    ones_rows = jnp.where(pad_rows == 0, 1.0, 0.0).astype(BF16)
    for j in range(tm // kb):
        for hh in range(N_HEADS_ATTN):
            vt_ref[0, j, hh * V_ROWS:hh * V_ROWS + HEAD_DIM, :] = yvt[hh * HEAD_DIM:(hh + 1) * HEAD_DIM,
                                                                      j * kb:(j + 1) * kb]
            vt_ref[0, j, hh * V_ROWS + HEAD_DIM:(hh + 1) * V_ROWS, :] = ones_rows


def _in_proj(x, sc, sh, w, layer, cos_t, sin_t, tm, kb):
    bsz, seq, d = x.shape
    row = lambda b, i: (b, i, 0)
    per_b = lambda b, i: (b, 0, 0)
    widths = (MIX_ATTN, MIX_ATTN, IQ_WIDTH, LANES, R_WIDTH, M_WIDTH)
    return pl.pallas_call(
        functools.partial(_inproj_kernel, kb=kb),
        grid=(bsz, seq // tm),
        in_specs=[pl.BlockSpec((1, tm, d), row),
                  pl.BlockSpec((1, 1, d), per_b),
                  pl.BlockSpec((1, 1, d), per_b),
                  pl.BlockSpec((None, d, W_TOTAL), lambda b, i: (layer, 0, 0)),
                  pl.BlockSpec((1, tm, LANES), row),
                  pl.BlockSpec((1, tm, LANES), row)],
        out_specs=[pl.BlockSpec((1, tm, wd), row) for wd in widths]
                  + [pl.BlockSpec((1, tm, S_WIDTH), row),
                     pl.BlockSpec((1, S_ROWS, tm), lambda b, i: (b, 0, i)),
                     pl.BlockSpec((1, tm // kb, N_HEADS_ATTN * V_ROWS, kb), lambda b, i: (b, i, 0, 0))],
        out_shape=[jax.ShapeDtypeStruct((bsz, seq, wd), BF16) for wd in widths]
                  + [jax.ShapeDtypeStruct((bsz, seq, S_WIDTH), F32),
                     jax.ShapeDtypeStruct((bsz, S_ROWS, seq), F32),
                     jax.ShapeDtypeStruct((bsz, seq // kb, N_HEADS_ATTN * V_ROWS, kb), BF16)],
        compiler_params=_params(("parallel", "parallel")),
        name="in_proj",
    )(x, sc, sh, w, cos_t, sin_t)


def _dsa_kernel(q_ref, k_ref, iq_ref, ik_ref, st_ref, vt_ref, tril_ref, o_ref,
                key_scr, byte_scr, cand_scr, bias_scr, s_scr, p_scr, m_scr, alpha_scr, acc_scr,
                *, tq, kb, topk):
    q0 = pl.program_id(1) * tq
    n_blocks = (q0 + tq + kb - 1) // kb
    qpos = q0 + lax.broadcasted_iota(jnp.int32, (1, tq), 1)
    q_limit = (qpos // CHUNK + 1) * CHUNK
    krow = lax.broadcasted_iota(jnp.int32, (kb, 1), 0)

    def head_of_pair(x, h):
        pair = x[:, (h // 2) * LANES:(h // 2 + 1) * LANES]
        lane = lax.broadcasted_iota(jnp.int32, pair.shape, 1)
        keep = (lane < HEAD_DIM) if h % 2 == 0 else (lane >= HEAD_DIM)
        return jnp.where(keep, pair, jnp.zeros_like(pair))

    iw = st_ref[0][S_IW:S_IW + N_IDX_HEADS, :] * (N_IDX_HEADS ** -0.5 * IDX_DIM ** -0.5)
    iq = iq_ref[0]
    iq_heads = [head_of_pair(iq, h) for h in range(N_IDX_HEADS)]

    def score_block(c):
        k0 = pl.multiple_of(c * kb, kb)
        ik2 = ik_ref[0, pl.ds(k0, kb), :]
        score = jnp.zeros((kb, tq), F32)
        for h in range(N_IDX_HEADS):
            score = score + jnp.maximum(_dot_nt(ik2, iq_heads[h]), 0.0) * iw[h:h + 1, :]
        bits = pltpu.bitcast(score, jnp.int32)
        key = jnp.where(bits >= 0, bits, bits ^ jnp.int32(0x7FFFFFFF))
        key = jnp.where(k0 + krow < q_limit, key, jnp.int32(INT_MIN))
        key_scr[c] = key
        byte_scr[0, c] = ((key >> 24) + 128).astype(F32).astype(BF16)
        for lvl in range(1, 4):
            byte_scr[lvl, c] = ((key >> (24 - 8 * lvl)) & 255).astype(F32).astype(BF16)

    def score_pair(c2, carry):
        score_block(2 * c2)
        score_block(2 * c2 + 1)
        return carry

    lax.fori_loop(0, n_blocks // 2, score_pair, 0)

    @pl.when(n_blocks % 2 == 1)
    def _():
        score_block(n_blocks - 1)

    pack = 16
    one = jnp.ones((kb, tq), BF16)
    zero = jnp.zeros((kb, tq), BF16)

    def count_ge(lvl, cand):
        cand_b = cand.astype(BF16)

        def hits(c):
            plane = byte_scr[0, c] if lvl == 0 else cand_scr[c]
            hit = jnp.where(plane >= cand_b, one, zero)
            parts = [hit[i * pack:(i + 1) * pack, :] for i in range(kb // pack)]
            while len(parts) > 1:
                parts = [parts[i] + parts[i + 1] for i in range(0, len(parts), 2)]
            return parts[0]

        acc = lax.fori_loop(0, n_blocks // 2, lambda c2, a: a + hits(2 * c2) + hits(2 * c2 + 1),
                            jnp.zeros((pack, tq), BF16))
        acc = lax.cond(n_blocks % 2 == 1, lambda a: a + hits(n_blocks - 1), lambda a: a, acc)
        return jnp.sum(acc.astype(F32), axis=0, keepdims=True)

    above = jnp.zeros((1, tq), F32)
    t = jnp.zeros((1, tq), jnp.int32)
    for lvl in range(4):
        def bit_body(i, carry, lvl=lvl, above=above):
            v, rejected = carry
            cand = v + lax.shift_left(jnp.int32(1), 7 - i).astype(F32)
            cnt = count_ge(lvl, cand)
            ok = above + cnt >= topk
            return jnp.where(ok, cand, v), jnp.where(ok, rejected, cnt)

        v, rejected = lax.fori_loop(0, 8, bit_body, (jnp.zeros((1, tq), F32), jnp.zeros((1, tq), F32)))
        above = above + rejected
        t = t | lax.shift_left(v.astype(jnp.int32), 24 - 8 * lvl)
        if lvl < 3:
            v_b = v.astype(BF16)

            def narrow(c, lvl=lvl, v_b=v_b):
                plane = byte_scr[0, c] if lvl == 0 else cand_scr[c]
                cand_scr[c] = jnp.where(plane == v_b, byte_scr[lvl + 1, c], -one)

            def narrow_pair(c2, carry, narrow=narrow):
                narrow(2 * c2)
                narrow(2 * c2 + 1)
                return carry

            lax.fori_loop(0, n_blocks // 2, narrow_pair, 0)

            @pl.when(n_blocks % 2 == 1)
            def _(narrow=narrow):
                narrow(n_blocks - 1)
    thr = jnp.maximum(t ^ jnp.int32(INT_MIN), jnp.int32(INT_MIN + 1))
    need = topk - above

    q = q_ref[0]
    q_heads = [head_of_pair(q, h) for h in range(N_HEADS_ATTN)]
    m_scr[...] = jnp.full(m_scr.shape, NEG_BIG, F32)
    alpha_scr[...] = jnp.ones(alpha_scr.shape, F32)
    acc_scr[...] = jnp.zeros(acc_scr.shape, F32)
    p_scr[...] = jnp.zeros(p_scr.shape, BF16)

    def stage_logits(c, h):
        k0 = pl.multiple_of(c * kb, kb)
        kp = k_ref[0, pl.ds(k0, kb), (h // 2) * LANES:(h // 2 + 1) * LANES]
        s_scr[h] = _dot_nt(kp, q_heads[h])

    def stage_mask(c, ties_before):
        key = key_scr[c]
        tie = key == thr
        rank = _dot(tril_ref[...], jnp.where(tie, 1.0, 0.0).astype(BF16)) + ties_before
        sel = (key > thr) | (tie & (rank <= need))
        bias_scr[...] = jnp.where(sel, 0.0, NEG_BIG)
        return rank[kb - 1:kb, :]

    def stage_softmax(h):
        for half in range(tq // LANES):
            ln = slice(half * LANES, (half + 1) * LANES)
            s = s_scr[h, :, ln] + bias_scr[:, ln]
            m_old = m_scr[h, :, ln]
            m_new = jnp.maximum(m_old, jnp.max(s, axis=0, keepdims=True))
            p_scr[h, :, ln] = jnp.exp2(s - m_new).astype(BF16)
            alpha_scr[h, :, ln] = jnp.exp2(m_old - m_new)
            m_scr[h, :, ln] = m_new

    def stage_values(c, h):
        vt = vt_ref[0, c, h * V_ROWS:(h + 1) * V_ROWS, :]
        acc_scr[h, 0:V_ROWS, :] = alpha_scr[h] * acc_scr[h, 0:V_ROWS, :] + _dot(vt, p_scr[h])

    ties0 = stage_mask(0, jnp.zeros((1, tq), F32))
    for h in range(N_HEADS_ATTN):
        stage_logits(0, h)

    def attn_body(j, ties_before):
        c_old = jnp.maximum(j - 2, 0)
        for h in range(N_HEADS_ATTN):
            stage_values(c_old, h)
            stage_softmax(h)
            stage_logits(j, h)
        return stage_mask(j, ties_before)

    lax.fori_loop(1, n_blocks, attn_body, ties0)
    for h in range(N_HEADS_ATTN):
        stage_values(jnp.maximum(n_blocks - 2, 0), h)
        stage_softmax(h)
    for h in range(N_HEADS_ATTN):
        stage_values(n_blocks - 1, h)
    for h in range(N_HEADS_ATTN):
        acc = acc_scr[h]
        out = acc * (1.0 / acc[HEAD_DIM:HEAD_DIM + 1, :])
        o_ref[0, :, h * HEAD_DIM:(h + 1) * HEAD_DIM] = out.T[:, 0:HEAD_DIM].astype(o_ref.dtype)


def _dsa(q, k, iq, ik2, small_t, v_t, tril, tq, kb):
    bsz, seq, _ = q.shape
    topk = min(TOPK_MAX, seq // 4)
    kern = functools.partial(_dsa_kernel, tq=tq, kb=kb, topk=topk)
    return pl.pallas_call(
        kern,
        grid=(bsz, seq // tq),
        in_specs=[pl.BlockSpec((1, tq, MIX_ATTN), lambda b, i: (b, i, 0)),
                  pl.BlockSpec((1, seq, MIX_ATTN), lambda b, i: (b, 0, 0)),
                  pl.BlockSpec((1, tq, IQ_WIDTH), lambda b, i: (b, i, 0)),
                  pl.BlockSpec((1, seq, LANES), lambda b, i: (b, 0, 0)),
                  pl.BlockSpec((1, S_ROWS, tq), lambda b, i: (b, 0, i)),
                  pl.BlockSpec((1, seq // kb, N_HEADS_ATTN * V_ROWS, kb), lambda b, i: (b, 0, 0, 0)),
                  pl.BlockSpec((kb, kb), lambda b, i: (0, 0))],
        out_specs=pl.BlockSpec((1, tq, MIX_ATTN), lambda b, i: (b, i, 0)),
        out_shape=jax.ShapeDtypeStruct((bsz, seq, MIX_ATTN), BF16),
        scratch_shapes=[pltpu.VMEM((seq // kb, kb, tq), jnp.int32),
                        pltpu.VMEM((4, seq // kb, kb, tq), BF16),
                        pltpu.VMEM((seq // kb, kb, tq), BF16),
                        pltpu.VMEM((kb, tq), F32),
                        pltpu.VMEM((N_HEADS_ATTN, kb, tq), F32),
                        pltpu.VMEM((N_HEADS_ATTN, kb, tq), BF16),
                        pltpu.VMEM((N_HEADS_ATTN, 1, tq), F32),
                        pltpu.VMEM((N_HEADS_ATTN, 1, tq), F32),
                        pltpu.VMEM((N_HEADS_ATTN, LANES, tq), F32)],
        compiler_params=_params(("parallel", "arbitrary")),
        name="dsa",
    )(q, k, iq, ik2, small_t, v_t, tril)


def _head_norm(y):
    mean_mat = jnp.full((HEAD_DIM, HEAD_DIM), 1.0 / HEAD_DIM, BF16)

    def mean_bcast(x):
        hi = x.astype(BF16)
        lo = (x - hi.astype(F32)).astype(BF16)
        return _dot(hi, mean_mat) + _dot(lo, mean_mat)

    yc = y - mean_bcast(y)
    return yc * lax.rsqrt(mean_bcast(yc * yc) + LN_EPS)


def _ret_kernel(r_ref, o_ref, state_scr, *, cr, grp):
    @pl.when(pl.program_id(1) == 0)
    def _():
        state_scr[...] = jnp.zeros_like(state_scr)

    ri = lax.broadcasted_iota(jnp.int32, (cr, cr), 0)
    ci = lax.broadcasted_iota(jnp.int32, (cr, cr), 1)
    diff = (ri - ci).astype(F32)
    pos = lax.broadcasted_iota(jnp.int32, (cr, 1), 0).astype(F32)
    items = [(h, g) for h in range(N_HEADS_RET) for g in range(grp)]
    sl = lambda part, h: slice(part * MIX_RET + h * HEAD_DIM, part * MIX_RET + (h + 1) * HEAD_DIM)
    log_gamma = [jnp.log1p(jnp.full((1, 1), -(2.0 ** (-5.0 - h)), F32)) for h in range(N_HEADS_RET)]
    decay_in = [jnp.where(diff >= 0, jnp.exp(diff * lg), 0.0) * (HEAD_DIM ** -0.5) for lg in log_gamma]
    q = {(h, g): r_ref[g, :, sl(0, h)] for h, g in items}
    k = {(h, g): r_ref[g, :, sl(1, h)] for h, g in items}
    v = {(h, g): r_ref[g, :, sl(2, h)] for h, g in items}
    state = {(h, g): state_scr[g, h] for h, g in items}
    scores = {it: _dot_nt(q[it], k[it]) * decay_in[it[0]] for it in items}
    cross = {it: jnp.exp((pos + 1.0) * log_gamma[it[0]]) * _dot(q[it], state[it].astype(BF16)) for it in items}
    inner = {it: _dot(scores[it].astype(BF16), v[it]) for it in items}
    for it in items:
        h, g = it
        gate = r_ref[g, :, sl(3, h)].astype(F32)
        y = _head_norm(inner[it] + cross[it])
        o_ref[g, :, h * HEAD_DIM:(h + 1) * HEAD_DIM] = (y * (gate * jax.nn.sigmoid(gate))).astype(o_ref.dtype)
    for it in items:
        h, g = it
        k_decay = (HEAD_DIM ** -0.5) * jnp.exp((cr - 1.0 - pos) * log_gamma[h])
        k_dec = (k[it].astype(F32) * k_decay).astype(BF16)
        state_scr[g, h] = state[it] * jnp.exp(cr * log_gamma[h]) + _dot_tn(k_dec, v[it])


def _retention(r_proj, cr, grp):
    bsz, seq, _ = r_proj.shape
    return pl.pallas_call(
        functools.partial(_ret_kernel, cr=cr, grp=grp),
        grid=(bsz // grp, seq // cr),
        in_specs=[pl.BlockSpec((grp, cr, R_WIDTH), lambda b, i: (b, i, 0))],
        out_specs=pl.BlockSpec((grp, cr, MIX_RET), lambda b, i: (b, i, 0)),
        out_shape=jax.ShapeDtypeStruct((bsz, seq, MIX_RET), BF16),
        scratch_shapes=[pltpu.VMEM((grp, N_HEADS_RET, HEAD_DIM, HEAD_DIM), F32)],
        compiler_params=_params(("parallel", "arbitrary")),
        name="retention",
    )(r_proj)


def _mlstm_kernel(m_ref, sm_ref, bias_ref, cw_ref, cb_ref, tril_ref, o_ref,
                  xbuf, a_scr, m_scr, *, cm, grp):
    halo = 8
    gate_shift = S_MF - S_MI

    @pl.when(pl.program_id(1) == 0)
    def _():
        xbuf[:, 0:halo, :] = jnp.zeros((grp, halo, 2 * MIX_MLSTM), F32)
        a_scr[...] = jnp.zeros_like(a_scr)
        m_scr[...] = jnp.zeros_like(m_scr)

    ri = lax.broadcasted_iota(jnp.int32, (cm, cm), 0)
    ci = lax.broadcasted_iota(jnp.int32, (cm, cm), 1)
    causal = ri >= ci
    row = lax.broadcasted_iota(jnp.int32, (cm, LANES), 0)
    lane = lax.broadcasted_iota(jnp.int32, (cm, LANES), 1)
    gate_lanes = (lane >= S_MI) & (lane < S_MI + N_HEADS_MLSTM)
    lane64 = lax.broadcasted_iota(jnp.int32, (cm, HEAD_DIM), 1)
    ones_col = jnp.where(lane64 == 0, 1.0, 0.0).astype(BF16)
    scale = HEAD_DIM ** -0.5

    def head_cols(x, h):
        return x[:, h * HEAD_DIM:(h + 1) * HEAD_DIM]

    items = [(g, h) for g in range(grp) for h in range(N_HEADS_MLSTM)]
    mm, qk, u_row, m_col, inter, e_inv, kw_col, decay = {}, {}, {}, {}, {}, {}, {}, {}
    for g in range(grp):
        mm[g] = m_ref[g]
        xbuf[g, halo:halo + cm, :] = mm[g][:, 0:2 * MIX_MLSTM].astype(F32)
        conv = cb_ref[...]
        for j in range(CONV_WIDTH):
            off = halo - (CONV_WIDTH - 1) + j
            conv = conv + xbuf[g, off:off + cm, :] * cw_ref[j:j + 1, :]
        xbuf[g, 0:halo, :] = xbuf[g, cm:cm + halo, :]
        qk[g] = conv * jax.nn.sigmoid(conv)

        gates = sm_ref[g] + bias_ref[...]
        b_all = jnp.dot(tril_ref[...], jax.nn.log_sigmoid(gates), preferred_element_type=F32,
                        precision=lax.Precision.HIGHEST)
        b_i = jnp.where(gate_lanes, pltpu.roll(b_all, LANES - gate_shift, 1), 0.0)
        u = jnp.where(gate_lanes, gates, 0.0) - b_i
        run = u
        step = 1
        while step < cm:
            run = jnp.maximum(run, jnp.where(row >= step, pltpu.roll(run, step, 0), -jnp.inf))
            step *= 2
        m_prev = m_scr[g]
        m_c = jnp.maximum(m_prev, run)
        m_last = m_c[cm - 1:cm, :]
        u_row[g] = u.T
        m_col[g] = m_c
        inter[g] = jnp.exp(m_prev - m_c)
        e_inv[g] = jnp.exp(-(b_i + m_c))
        kw_col[g] = scale * jnp.exp(u - m_last)
        decay[g] = jnp.exp(m_prev - m_last)
        m_scr[g] = b_i[cm - 1:cm, :] + m_last

    q, k, v_aug, a_mem = {}, {}, {}, {}
    for it in items:
        g, h = it
        q[it] = head_cols(qk[g], h).astype(BF16)
        k[it] = head_cols(qk[g][:, MIX_MLSTM:], h)
        v_aug[it] = jnp.concatenate([head_cols(mm[g][:, 2 * MIX_MLSTM:], h), ones_col], axis=1)
        a_mem[it] = a_scr[g, h]

    def col(x, h):
        return x[:, S_MI + h:S_MI + h + 1]

    w = {}
    for it in items:
        g, h = it
        u_r = u_row[g][S_MI + h:S_MI + h + 1, :]
        w[it] = jnp.exp(jnp.where(causal, u_r - col(m_col[g], h), -jnp.inf))

    s = {it: _dot_nt(q[it], k[it].astype(BF16)) * scale * w[it] for it in items}
    cross = {it: _dot(q[it], a_mem[it].astype(BF16)) for it in items}
    both = {it: _dot(s[it].astype(BF16), v_aug[it]) + col(inter[it[0]], it[1]) * cross[it] for it in items}
    for it in items:
        g, h = it
        den = both[it][:, HEAD_DIM:HEAD_DIM + 1]
        h_tilde = both[it][:, 0:HEAD_DIM] * (1.0 / jnp.maximum(jnp.abs(den), col(e_inv[g], h)))
        og = head_cols(mm[g][:, 3 * MIX_MLSTM:], h).astype(F32)
        o_ref[g, :, h * HEAD_DIM:(h + 1) * HEAD_DIM] = _head_norm(jax.nn.sigmoid(og) * h_tilde).astype(o_ref.dtype)

    for it in items:
        g, h = it
        kw = k[it] * col(kw_col[g], h)
        a_scr[g, h] = col(decay[g], h) * a_mem[it] + _dot_tn(kw.astype(BF16), v_aug[it])


def _mlstm(m_proj, small, gate_bias, conv_w, conv_b, tril, cm, grp):
    bsz, seq, _ = m_proj.shape
    const = lambda b, i: (0, 0)
    return pl.pallas_call(
        functools.partial(_mlstm_kernel, cm=cm, grp=grp),
        grid=(bsz // grp, seq // cm),
        in_specs=[pl.BlockSpec((grp, cm, M_WIDTH), lambda b, i: (b, i, 0)),
                  pl.BlockSpec((grp, cm, S_WIDTH), lambda b, i: (b, i, 0)),
                  pl.BlockSpec((1, S_WIDTH), const),
                  pl.BlockSpec((CONV_WIDTH, 2 * MIX_MLSTM), const),
                  pl.BlockSpec((1, 2 * MIX_MLSTM), const),
                  pl.BlockSpec((cm, cm), const)],
        out_specs=pl.BlockSpec((grp, cm, MIX_MLSTM), lambda b, i: (b, i, 0)),
        out_shape=jax.ShapeDtypeStruct((bsz, seq, MIX_MLSTM), BF16),
        scratch_shapes=[pltpu.VMEM((grp, cm + 8, 2 * MIX_MLSTM), F32),
                        pltpu.VMEM((grp, N_HEADS_MLSTM, HEAD_DIM, LANES), F32),
                        pltpu.VMEM((grp, 1, LANES), F32)],
        compiler_params=_params(("parallel", "arbitrary")),
        name="mlstm",
    )(m_proj, small, gate_bias, conv_w, conv_b, tril)


def _layer_norm(z, g, b):
    mu = jnp.mean(z, axis=-1, keepdims=True)
    var = jnp.mean(jnp.square(z - mu), axis=-1, keepdims=True)
    return (z - mu) * lax.rsqrt(var + LN_EPS) * g + b


def _route(scores, biased):
    col = lambda a, e: a[e:e + 1, :]
    epg = EXPERTS_PER_GROUP
    group_scores = []
    for g in range(N_GROUPS):
        vals = [col(biased, g * epg + j) for j in range(epg)]
        best = None
        for a in range(epg):
            for b in range(a + 1, epg):
                pair = vals[a] + vals[b]
                best = pair if best is None else jnp.maximum(best, pair)
        group_scores.append(best)
    best_g = jnp.zeros_like(group_scores[0], dtype=jnp.int32)
    best_v = group_scores[0]
    for g in range(1, N_GROUPS):
        better = group_scores[g] > best_v
        best_g = jnp.where(better, g, best_g)
        best_v = jnp.where(better, group_scores[g], best_v)
    cand_b = [sum(jnp.where(best_g == g, col(biased, g * epg + j), 0.0) for g in range(N_GROUPS))
              for j in range(epg)]
    cand_s = [sum(jnp.where(best_g == g, col(scores, g * epg + j), 0.0) for g in range(N_GROUPS))
              for j in range(epg)]

    def argmax_first(vals, skip=None):
        idx = None
        val = None
        for j, vj in enumerate(vals):
            if skip is not None:
                vj = jnp.where(skip == j, -jnp.inf, vj)
            if idx is None:
                idx, val = jnp.zeros_like(best_g), vj
            else:
                better = vj > val
                idx = jnp.where(better, j, idx)
                val = jnp.where(better, vj, val)
        return idx

    first = argmax_first(cand_b)
    second = argmax_first(cand_b, skip=first)
    w1 = sum(jnp.where(first == j, cand_s[j], 0.0) for j in range(epg))
    w2 = sum(jnp.where(second == j, cand_s[j], 0.0) for j in range(epg))
    total = w1 + w2
    e1 = best_g * epg + first
    e2 = best_g * epg + second
    expert = lax.broadcasted_iota(jnp.int32, scores.shape, 0)
    gate = jnp.where(expert == e1, w1 / total, 0.0) + jnp.where(expert == e2, w2 / total, 0.0)
    return gate, best_g


def _split_bf16(x):
    hi = x.astype(BF16)
    return hi, (x - hi.astype(F32)).astype(BF16)


def _moe_kernel(oa_ref, ob_ref, oc_ref, wo_ref, x_ref, gm_ref, gmix_ref, bmix_ref,
                sc_ref, sh_ref, gf_ref, wr_ref, br_ref, wg_ref, wu_ref, wd_ref, tri_ref, g_ref, b_ref,
                o_ref, hid_scr, *, alpha, cap):
    mix = _dot(oa_ref[0], wo_ref[0:MIX_ATTN, :])
    mix = mix + _dot(ob_ref[0], wo_ref[MIX_ATTN:MIX_ATTN + MIX_RET, :])
    mix = mix + _dot(oc_ref[0], wo_ref[MIX_ATTN + MIX_RET:, :])
    x = _layer_norm(alpha * x_ref[0] + (1.0 + gm_ref[0]) * mix, gmix_ref[...], bmix_ref[...])
    tm = x.shape[0]
    h = x * (1.0 + sc_ref[0]) + sh_ref[0]
    hb = h.astype(BF16)
    n_exp, _, d_ff = wg_ref.shape
    scores_t = jax.nn.sigmoid(_dot(h, wr_ref[...]).T[0:n_exp, :])
    gate_t, best_g = _route(scores_t, scores_t + br_ref[...])
    sub = lax.broadcasted_iota(jnp.int32, (2 * SUBLANES, tm), 0)
    member_t = jnp.where(sub == best_g, 1.0, 0.0)
    before_t = _dot(member_t.astype(BF16), tri_ref[...])
    rank = jnp.sum(member_t * before_t, axis=0, keepdims=True)
    count = jnp.max(jnp.sum(member_t, axis=1, keepdims=True))
    grp_f = best_g.astype(F32)
    sub8 = lax.broadcasted_iota(jnp.int32, (SUBLANES, tm), 0)
    extra = jnp.where(sub8 == 0, grp_f, jnp.where(sub8 == 1, rank, 0.0))
    info = jnp.concatenate([gate_t, extra, jnp.zeros((LANES - n_exp - SUBLANES, tm), F32)], axis=0).T
    gate = info
    epg = EXPERTS_PER_GROUP

    def expert_hidden(rows_b, gate_rows, e, dst):
        gate_pre = _dot(rows_b, wg_ref[e])
        up = _dot(rows_b, wu_ref[e])
        hid = gate_pre * jax.nn.sigmoid(gate_pre) * up * gate_rows[:, e:e + 1]
        hid_scr[0:rows_b.shape[0], dst * d_ff:(dst + 1) * d_ff] = hid.astype(BF16)

    def dense(_):
        for e in range(n_exp):
            expert_hidden(hb, gate, e, e)
        return _dot(hid_scr[...], wd_ref[...])


    def grouped(_):
        grp_c = info[:, n_exp:n_exp + 1]
        rank_c = info[:, n_exp + 1:n_exp + 2]
        slot_c = lax.broadcasted_iota(jnp.int32, (cap, 1), 0).astype(F32)
        slot_r = lax.broadcasted_iota(jnp.int32, (1, cap), 1).astype(F32)
        gate_hi, gate_lo = _split_bf16(gate)
        y = jnp.zeros((tm, x.shape[1]), F32)
        for g in range(N_GROUPS):
            take = jnp.where(jnp.where(grp_f == g, rank, -1.0) == slot_c, 1.0, 0.0).astype(BF16)
            give = jnp.where(jnp.where(grp_c == g, rank_c, -1.0) == slot_r, 1.0, 0.0).astype(BF16)
            rows_b = _dot(take, hb).astype(BF16)
            gate_rows = _dot(take, gate_hi) + _dot(take, gate_lo)
            for j in range(epg):
                expert_hidden(rows_b, gate_rows, g * epg + j, j)
            out = _dot(hid_scr[0:cap, 0:epg * d_ff], wd_ref[g * epg * d_ff:(g + 1) * epg * d_ff, :])
            y = y + _dot(give, out.astype(BF16))
        return y

    y = lax.cond(count <= cap, grouped, dense, 0)
    z = alpha * x + (1.0 + gf_ref[0]) * y
    o_ref[0] = _layer_norm(z, g_ref[...], b_ref[...])


def _moe(o_a, o_b, o_c, w_out, x, g_m, ln_mix_g, ln_mix_b,
         sc, sh, g_f, w_router, b_router, w_gate, w_up, w_down, layer, tri, ln_g, ln_b, tm, cap, alpha):
    bsz, seq, d = x.shape
    _, n_exp, _, d_ff = w_gate.shape
    row = lambda b, i: (b, i, 0)
    per_b = lambda b, i: (b, 0, 0)
    const = lambda b, i: (0, 0)
    of_layer4 = lambda b, i: (layer, 0, 0, 0)
    of_layer3 = lambda b, i: (layer, 0, 0)
    resident = pl.Buffered(1)
    return pl.pallas_call(
        functools.partial(_moe_kernel, alpha=alpha, cap=cap),
        grid=(bsz, seq // tm),
        in_specs=[pl.BlockSpec((1, tm, MIX_ATTN), row),
                  pl.BlockSpec((1, tm, MIX_RET), row),
                  pl.BlockSpec((1, tm, MIX_MLSTM), row),
                  pl.BlockSpec((None,) + w_out.shape[1:], of_layer3, pipeline_mode=resident),
                  pl.BlockSpec((1, tm, d), row),
                  pl.BlockSpec((1, 1, d), per_b),
                  pl.BlockSpec((1, d), const),
                  pl.BlockSpec((1, d), const),
                  pl.BlockSpec((1, 1, d), per_b),
                  pl.BlockSpec((1, 1, d), per_b),
                  pl.BlockSpec((1, 1, d), per_b),
                  pl.BlockSpec((d, LANES), const),
                  pl.BlockSpec((n_exp, 1), const),
                  pl.BlockSpec((None, n_exp, d, d_ff), of_layer4, pipeline_mode=resident),
                  pl.BlockSpec((None, n_exp, d, d_ff), of_layer4, pipeline_mode=resident),
                  pl.BlockSpec((None, n_exp * d_ff, d), of_layer3, pipeline_mode=resident),
                  pl.BlockSpec((tm, tm), const, pipeline_mode=resident),
                  pl.BlockSpec((1, d), const),
                  pl.BlockSpec((1, d), const)],
        out_specs=pl.BlockSpec((1, tm, d), row),
        out_shape=jax.ShapeDtypeStruct((bsz, seq, d), F32),
        scratch_shapes=[pltpu.VMEM((tm, n_exp * d_ff), BF16)],
        compiler_params=_params(("parallel", "parallel")),
        name="moe",
    )(o_a, o_b, o_c, w_out, x, g_m, ln_mix_g, ln_mix_b,
      sc, sh, g_f, w_router, b_router, w_gate, w_up, w_down, tri, ln_g, ln_b)


def _pick(n, pref):
    t = min(pref, n)
    while n % t:
        t //= 2
    return t


def _rope_tables(positions):
    half = HEAD_DIM // 2
    inv_freq = ROPE_THETA ** (-jnp.arange(half, dtype=F32) / half)
    ang = positions.astype(F32)[..., None] * inv_freq
    cos, sin = jnp.cos(ang), jnp.sin(ang)
    reps = LANES // HEAD_DIM
    return (jnp.concatenate([cos, cos] * reps, axis=-1),
            jnp.concatenate([-sin, sin] * reps, axis=-1))


def _prep_w_kernel(w_ref, o_ref):
    o_v = 2 * MIX_ATTN
    o_iq = o_v + MIX_ATTN
    o_ik = o_iq + IQ_WIDTH
    o_iw = o_ik + IDX_DIM
    o_r = o_iw + N_IDX_HEADS
    o_g = o_r + R_WIDTH + M_WIDTH
    n_gate = 2 * N_HEADS_MLSTM

    def put(dst, src, width):
        o_ref[0, :, dst:dst + width] = w_ref[0, :, src:src + width].astype(o_ref.dtype)

    o_ref[0] = jnp.zeros(o_ref.shape[1:], o_ref.dtype)
    put(OFF_Q, 0, 2 * MIX_ATTN)
    put(OFF_IQ, o_iq, IQ_WIDTH)
    put(OFF_R, o_r, R_WIDTH + M_WIDTH)
    put(OFF_IK, o_ik, IDX_DIM)
    put(OFF_IK + IDX_DIM, o_ik, IDX_DIM)
    put(OFF_S + S_IW, o_iw, N_IDX_HEADS)
    put(OFF_S + S_MI, o_g, n_gate)
    put(OFF_V, o_v, MIX_ATTN)


def _prep_w_in(w_in, tr):
    depth, d, n = w_in.shape
    return pl.pallas_call(
        _prep_w_kernel,
        grid=(depth, d // tr),
        in_specs=[pl.BlockSpec((1, tr, n), lambda l, i: (l, i, 0))],
        out_specs=pl.BlockSpec((1, tr, W_TOTAL), lambda l, i: (l, i, 0)),
        out_shape=jax.ShapeDtypeStruct((depth, d, W_TOTAL), F32),
        compiler_params=_params(("parallel", "parallel")),
        name="prep_w_in",
    )(w_in)


def kernel(x, c, positions, w_ada, b_ada, w_in, i_bias, f_bias, conv_w, conv_b, w_out, ln_mix_g, ln_mix_b,
           w_router, b_router, w_gate, w_up, w_down, ln_ffn_g, ln_ffn_b):
    bsz, seq, d = x.shape
    depth = w_ada.shape[0]
    alpha = (2.0 * depth) ** 0.25

    tm = _pick(seq, 512)
    tq = _pick(seq, 256)
    kb = _pick(seq, 256)
    assert seq // 16 <= 256, "packed bf16 partial counts in the DSA threshold search must stay exact"
    cr = _pick(seq, 256)
    cm = _pick(seq, 256)
    tmoe = _pick(seq, 512)
    moe_cap = min(tmoe, (3 * tmoe // (2 * N_GROUPS) + 15) // 16 * 16)

    cos_t, sin_t = _rope_tables(positions)
    c_pad = jnp.zeros((8, d), F32).at[:bsz].set(c)
    mod = _ada_mod(c_pad, w_ada, b_ada, _pick(6 * d, 1536))
    w_in_p = _prep_w_in(w_in, _pick(d, 256)).astype(BF16)
    w_out_b = w_out.astype(BF16)
    w_gate_b, w_up_b = w_gate.astype(BF16), w_up.astype(BF16)
    w_down_b = w_down.astype(BF16).reshape(depth, -1, d)
    grp = 2 if bsz % 2 == 0 else 1
    grp_ret = 4 if bsz % 4 == 0 else grp

    tril_kb = (jnp.arange(kb)[:, None] >= jnp.arange(kb)[None, :]).astype(BF16)
    tril = (jnp.arange(cm)[:, None] >= jnp.arange(cm)[None, :]).astype(F32)
    w_router_p = jnp.zeros((d, LANES), F32).at[:, :N_EXPERTS].set(w_router)
    b_router_p = b_router.reshape(N_EXPERTS, 1)
    tri_moe = (jnp.arange(tmoe)[:, None] < jnp.arange(tmoe)[None, :]).astype(BF16)

    for l in range(depth):
        parts = [mod[l, :bsz, j * d:(j + 1) * d].reshape(bsz, 1, d) for j in range(6)]
        sh_m, sc_m, g_m, sh_f, sc_f, g_f = parts
        q, k, iq, ik2, r_proj, m_proj, small, small_t, v_t = _in_proj(
            x, sc_m, sh_m, w_in_p, l, cos_t, sin_t, tm, kb)
        o_a = _dsa(q, k, iq, ik2, small_t, v_t, tril_kb, tq, kb)
        o_b = _retention(r_proj, cr, grp_ret)
        gate_bias = (jnp.zeros((1, S_WIDTH), F32).at[0, S_MI:S_MI + N_HEADS_MLSTM].set(i_bias[l])
                     .at[0, S_MF:S_MF + N_HEADS_MLSTM].set(f_bias[l]))
        o_c = _mlstm(m_proj, small, gate_bias, conv_w[l], conv_b[l].reshape(1, -1), tril, cm, grp)
        x = _moe(o_a, o_b, o_c, w_out_b, x, g_m, ln_mix_g[l].reshape(1, d), ln_mix_b[l].reshape(1, d),
                 sc_f, sh_f, g_f, w_router_p, b_router_p,
                 w_gate_b, w_up_b, w_down_b, l, tri_moe,
                 ln_ffn_g[l].reshape(1, d), ln_ffn_b[l].reshape(1, d), tmoe, moe_cap, alpha)
    return x
```

```python
import functools

import numpy as np
import jax
import jax.numpy as jnp
from jax import lax
from jax.experimental import pallas as pl
from jax.experimental.pallas import tpu as pltpu

F32 = jnp.float32
BF16 = jnp.bfloat16

HEAD_DIM = 64
CHUNK = 64
N_HEADS_ATTN = 8
N_IDX_HEADS = 4
IDX_DIM = 64
TOPK_MAX = 256
N_HEADS_RET = 4
N_HEADS_MLSTM = 4
CONV_WIDTH = 4
ROPE_THETA = 10000.0
N_EXPERTS = 16
N_GROUPS = 4
EXPERTS_PER_GROUP = N_EXPERTS // N_GROUPS
D_FF_EXPERT = 256
LN_EPS = 1e-5

MIX_ATTN = N_HEADS_ATTN * HEAD_DIM
MIX_RET = N_HEADS_RET * HEAD_DIM
MIX_MLSTM = N_HEADS_MLSTM * HEAD_DIM

LANES = 128
SUBLANES = 8
VMEM_LIMIT = 56 * 1024 * 1024

IQ_WIDTH = N_IDX_HEADS * IDX_DIM
R_WIDTH = 4 * MIX_RET
M_WIDTH = 4 * MIX_MLSTM
S_WIDTH = LANES
S_IW = 0
S_MI = S_IW + N_IDX_HEADS
S_MF = S_MI + N_HEADS_MLSTM
S_ROWS = 16
V_ROWS = HEAD_DIM + 16
OFF_Q = 0
OFF_K = OFF_Q + MIX_ATTN
OFF_IQ = OFF_K + MIX_ATTN
OFF_R = OFF_IQ + IQ_WIDTH
OFF_M = OFF_R + R_WIDTH
OFF_IK = OFF_M + M_WIDTH
OFF_S = OFF_IK + LANES
OFF_V = OFF_S + S_WIDTH
W_TOTAL = OFF_V + MIX_ATTN

INT_MIN = -2 ** 31
NEG_BIG = -1e30
LOG2_E = 1.4426950408889634


def _dot(a, b):
    return jnp.dot(a, b, preferred_element_type=F32)


def _dot_nt(a, b):
    return lax.dot_general(a, b, (((1,), (1,)), ((), ())), preferred_element_type=F32)


def _dot_tn(a, b):
    return lax.dot_general(a, b, (((0,), (0,)), ((), ())), preferred_element_type=F32)


def _params(sem):
    return pltpu.CompilerParams(dimension_semantics=sem, vmem_limit_bytes=VMEM_LIMIT)


def _ada_kernel(c_ref, w_ref, b_ref, o_ref):
    c = c_ref[...]
    c_act = c * jax.nn.sigmoid(c)
    o_ref[0] = _dot(c_act, w_ref[0]) + b_ref[0]


def _ada_mod(c_pad, w_ada, b_ada, tn):
    depth, d, n = w_ada.shape
    rows = c_pad.shape[0]
    return pl.pallas_call(
        _ada_kernel,
        grid=(depth, n // tn),
        in_specs=[pl.BlockSpec((rows, d), lambda l, j: (0, 0)),
                  pl.BlockSpec((1, d, tn), lambda l, j: (l, 0, j)),
                  pl.BlockSpec((1, 1, tn), lambda l, j: (l, 0, j))],
        out_specs=pl.BlockSpec((1, rows, tn), lambda l, j: (l, 0, j)),
        out_shape=jax.ShapeDtypeStruct((depth, rows, n), F32),
        compiler_params=_params(("parallel", "parallel")),
        name="ada_mod",
    )(c_pad, w_ada, b_ada.reshape(depth, 1, n))


def _rope(y, cos, sin):
    w = y.shape[1]
    reps = w // LANES
    cosw = jnp.concatenate([cos] * reps, axis=1) if reps > 1 else cos
    sinw = jnp.concatenate([sin] * reps, axis=1) if reps > 1 else sin
    lane = lax.broadcasted_iota(jnp.int32, y.shape, 1)
    first = (lane % HEAD_DIM) < (HEAD_DIM // 2)
    partner = jnp.where(first, pltpu.roll(y, w - HEAD_DIM // 2, 1), pltpu.roll(y, HEAD_DIM // 2, 1))
    return y * cosw + partner * sinw


def _inproj_kernel(x_ref, sc_ref, sh_ref, w_ref, cos_ref, sin_ref,
                   q_ref, k_ref, iq_ref, ik_ref, r_ref, m_ref, s_ref, st_ref, vt_ref, *, kb):
    h = (x_ref[0] * (1.0 + sc_ref[0]) + sh_ref[0]).astype(BF16)
    cos = cos_ref[0]
    sin = sin_ref[0]

    def proj(start, width):
        return _dot(h, w_ref[:, start:start + width])

    q_ref[0] = (_rope(proj(OFF_Q, MIX_ATTN), cos, sin) * (HEAD_DIM ** -0.5 * LOG2_E)).astype(BF16)
    k_ref[0] = _rope(proj(OFF_K, MIX_ATTN), cos, sin).astype(BF16)
    iq_ref[0] = _rope(proj(OFF_IQ, IQ_WIDTH), cos, sin).astype(BF16)
    ik_ref[0] = _rope(proj(OFF_IK, LANES), cos, sin).astype(BF16)
    r_ref[0, :, 0:2 * MIX_RET] = _rope(proj(OFF_R, 2 * MIX_RET), cos, sin).astype(BF16)
    r_ref[0, :, 2 * MIX_RET:R_WIDTH] = proj(OFF_R + 2 * MIX_RET, 2 * MIX_RET).astype(BF16)
    m_ref[0, :, 0:2 * MIX_MLSTM] = proj(OFF_M, 2 * MIX_MLSTM).astype(BF16)
    m_ref[0, :, 2 * MIX_MLSTM:M_WIDTH] = proj(OFF_M + 2 * MIX_MLSTM, 2 * MIX_MLSTM).astype(BF16)
    y = proj(OFF_S, S_WIDTH)
    s_ref[0] = y
    st_ref[0] = y.T[0:S_ROWS, :]
    tm = h.shape[0]
    yvt = proj(OFF_V, MIX_ATTN).T.astype(BF16)
    pad_rows = lax.broadcasted_iota(jnp.int32, (V_ROWS - HEAD_DIM, kb), 0)
    ones_rows = jnp.where(pad_rows == 0, 1.0, 0.0).astype(BF16)
    for j in range(tm // kb):
        for hh in range(N_HEADS_ATTN):
            vt_ref[0, j, hh * V_ROWS:hh * V_ROWS + HEAD_DIM, :] = yvt[hh * HEAD_DIM:(hh + 1) * HEAD_DIM,
                                                                      j * kb:(j + 1) * kb]
            vt_ref[0, j, hh * V_ROWS + HEAD_DIM:(hh + 1) * V_ROWS, :] = ones_rows


def _in_proj(x, sc, sh, w, layer, cos_t, sin_t, tm, kb):
    bsz, seq, d = x.shape
    row = lambda b, i: (b, i, 0)
    per_b = lambda b, i: (b, 0, 0)
    widths = (MIX_ATTN, MIX_ATTN, IQ_WIDTH, LANES, R_WIDTH, M_WIDTH)
    return pl.pallas_call(
        functools.partial(_inproj_kernel, kb=kb),
        grid=(bsz, seq // tm),
        in_specs=[pl.BlockSpec((1, tm, d), row),
                  pl.BlockSpec((1, 1, d), per_b),
                  pl.BlockSpec((1, 1, d), per_b),
                  pl.BlockSpec((None, d, W_TOTAL), lambda b, i: (layer, 0, 0)),
                  pl.BlockSpec((1, tm, LANES), row),
                  pl.BlockSpec((1, tm, LANES), row)],
        out_specs=[pl.BlockSpec((1, tm, wd), row) for wd in widths]
                  + [pl.BlockSpec((1, tm, S_WIDTH), row),
                     pl.BlockSpec((1, S_ROWS, tm), lambda b, i: (b, 0, i)),
                     pl.BlockSpec((1, tm // kb, N_HEADS_ATTN * V_ROWS, kb), lambda b, i: (b, i, 0, 0))],
        out_shape=[jax.ShapeDtypeStruct((bsz, seq, wd), BF16) for wd in widths]
                  + [jax.ShapeDtypeStruct((bsz, seq, S_WIDTH), F32),
                     jax.ShapeDtypeStruct((bsz, S_ROWS, seq), F32),
                     jax.ShapeDtypeStruct((bsz, seq // kb, N_HEADS_ATTN * V_ROWS, kb), BF16)],
        compiler_params=_params(("parallel", "parallel")),
        name="in_proj",
    )(x, sc, sh, w, cos_t, sin_t)


def _dsa_kernel(q_ref, k_ref, iq_ref, ik_ref, st_ref, vt_ref, tril_ref, o_ref,
                key_scr, byte_scr, cand_scr, bias_scr, s_scr, p_scr, m_scr, alpha_scr, acc_scr,
                *, tq, kb, topk):
    q0 = pl.program_id(1) * tq
    n_blocks = (q0 + tq + kb - 1) // kb
    qpos = q0 + lax.broadcasted_iota(jnp.int32, (1, tq), 1)
    q_limit = (qpos // CHUNK + 1) * CHUNK
    krow = lax.broadcasted_iota(jnp.int32, (kb, 1), 0)

    def head_of_pair(x, h):
        pair = x[:, (h // 2) * LANES:(h // 2 + 1) * LANES]
        lane = lax.broadcasted_iota(jnp.int32, pair.shape, 1)
        keep = (lane < HEAD_DIM) if h % 2 == 0 else (lane >= HEAD_DIM)
        return jnp.where(keep, pair, jnp.zeros_like(pair))

    iw = st_ref[0][S_IW:S_IW + N_IDX_HEADS, :] * (N_IDX_HEADS ** -0.5 * IDX_DIM ** -0.5)
    iq = iq_ref[0]
    iq_heads = [head_of_pair(iq, h) for h in range(N_IDX_HEADS)]

    def score_block(c):
        k0 = pl.multiple_of(c * kb, kb)
        ik2 = ik_ref[0, pl.ds(k0, kb), :]
        score = jnp.zeros((kb, tq), F32)
        for h in range(N_IDX_HEADS):
            score = score + jnp.maximum(_dot_nt(ik2, iq_heads[h]), 0.0) * iw[h:h + 1, :]
        bits = pltpu.bitcast(score, jnp.int32)
        key = jnp.where(bits >= 0, bits, bits ^ jnp.int32(0x7FFFFFFF))
        key = jnp.where(k0 + krow < q_limit, key, jnp.int32(INT_MIN))
        key_scr[c] = key
        byte_scr[0, c] = ((key >> 24) + 128).astype(F32).astype(BF16)
        for lvl in range(1, 4):
            byte_scr[lvl, c] = ((key >> (24 - 8 * lvl)) & 255).astype(F32).astype(BF16)

    def score_pair(c2, carry):
        score_block(2 * c2)
        score_block(2 * c2 + 1)
        return carry

    lax.fori_loop(0, n_blocks // 2, score_pair, 0)

    @pl.when(n_blocks % 2 == 1)
    def _():
        score_block(n_blocks - 1)

    pack = 16
    one = jnp.ones((kb, tq), BF16)
    zero = jnp.zeros((kb, tq), BF16)

    def count_ge(lvl, cand):
        cand_b = cand.astype(BF16)

        def hits(c):
            plane = byte_scr[0, c] if lvl == 0 else cand_scr[c]
            hit = jnp.where(plane >= cand_b, one, zero)
            parts = [hit[i * pack:(i + 1) * pack, :] for i in range(kb // pack)]
            while len(parts) > 1:
                parts = [parts[i] + parts[i + 1] for i in range(0, len(parts), 2)]
            return parts[0]

        acc = lax.fori_loop(0, n_blocks // 2, lambda c2, a: a + hits(2 * c2) + hits(2 * c2 + 1),
                            jnp.zeros((pack, tq), BF16))
        acc = lax.cond(n_blocks % 2 == 1, lambda a: a + hits(n_blocks - 1), lambda a: a, acc)
        return jnp.sum(acc.astype(F32), axis=0, keepdims=True)

    above = jnp.zeros((1, tq), F32)
    t = jnp.zeros((1, tq), jnp.int32)
    for lvl in range(4):
        def bit_body(i, carry, lvl=lvl, above=above):
            v, rejected = carry
            cand = v + lax.shift_left(jnp.int32(1), 7 - i).astype(F32)
            cnt = count_ge(lvl, cand)
            ok = above + cnt >= topk
            return jnp.where(ok, cand, v), jnp.where(ok, rejected, cnt)

        v, rejected = lax.fori_loop(0, 8, bit_body, (jnp.zeros((1, tq), F32), jnp.zeros((1, tq), F32)))
        above = above + rejected
        t = t | lax.shift_left(v.astype(jnp.int32), 24 - 8 * lvl)
        if lvl < 3:
            v_b = v.astype(BF16)

            def narrow(c, lvl=lvl, v_b=v_b):
                plane = byte_scr[0, c] if lvl == 0 else cand_scr[c]
                cand_scr[c] = jnp.where(plane == v_b, byte_scr[lvl + 1, c], -one)

            def narrow_pair(c2, carry, narrow=narrow):
                narrow(2 * c2)
                narrow(2 * c2 + 1)
                return carry

            lax.fori_loop(0, n_blocks // 2, narrow_pair, 0)

            @pl.when(n_blocks % 2 == 1)
            def _(narrow=narrow):
                narrow(n_blocks - 1)
    thr = jnp.maximum(t ^ jnp.int32(INT_MIN), jnp.int32(INT_MIN + 1))
    need = topk - above

    q = q_ref[0]
    q_heads = [head_of_pair(q, h) for h in range(N_HEADS_ATTN)]
    m_scr[...] = jnp.full(m_scr.shape, NEG_BIG, F32)
    alpha_scr[...] = jnp.ones(alpha_scr.shape, F32)
    acc_scr[...] = jnp.zeros(acc_scr.shape, F32)
    p_scr[...] = jnp.zeros(p_scr.shape, BF16)

    def stage_logits(c, h):
        k0 = pl.multiple_of(c * kb, kb)
        kp = k_ref[0, pl.ds(k0, kb), (h // 2) * LANES:(h // 2 + 1) * LANES]
        s_scr[h] = _dot_nt(kp, q_heads[h])

    def stage_mask(c, ties_before):
        key = key_scr[c]
        tie = key == thr
        rank = _dot(tril_ref[...], jnp.where(tie, 1.0, 0.0).astype(BF16)) + ties_before
        sel = (key > thr) | (tie & (rank <= need))
        bias_scr[...] = jnp.where(sel, 0.0, NEG_BIG)
        return rank[kb - 1:kb, :]

    def stage_softmax(h):
        for half in range(tq // LANES):
            ln = slice(half * LANES, (half + 1) * LANES)
            s = s_scr[h, :, ln] + bias_scr[:, ln]
            m_old = m_scr[h, :, ln]
            m_new = jnp.maximum(m_old, jnp.max(s, axis=0, keepdims=True))
            p_scr[h, :, ln] = jnp.exp2(s - m_new).astype(BF16)
            alpha_scr[h, :, ln] = jnp.exp2(m_old - m_new)
            m_scr[h, :, ln] = m_new

    def stage_values(c, h):
        vt = vt_ref[0, c, h * V_ROWS:(h + 1) * V_ROWS, :]
        acc_scr[h, 0:V_ROWS, :] = alpha_scr[h] * acc_scr[h, 0:V_ROWS, :] + _dot(vt, p_scr[h])

    ties0 = stage_mask(0, jnp.zeros((1, tq), F32))
    for h in range(N_HEADS_ATTN):
        stage_logits(0, h)

    def attn_body(j, ties_before):
        c_old = jnp.maximum(j - 2, 0)
        for h in range(N_HEADS_ATTN):
            stage_values(c_old, h)
            stage_softmax(h)
            stage_logits(j, h)
        return stage_mask(j, ties_before)

    lax.fori_loop(1, n_blocks, attn_body, ties0)
    for h in range(N_HEADS_ATTN):
        stage_values(jnp.maximum(n_blocks - 2, 0), h)
        stage_softmax(h)
    for h in range(N_HEADS_ATTN):
        stage_values(n_blocks - 1, h)
    for h in range(N_HEADS_ATTN):
        acc = acc_scr[h]
        out = acc * (1.0 / acc[HEAD_DIM:HEAD_DIM + 1, :])
        o_ref[0, :, h * HEAD_DIM:(h + 1) * HEAD_DIM] = out.T[:, 0:HEAD_DIM].astype(o_ref.dtype)


def _dsa(q, k, iq, ik2, small_t, v_t, tril, tq, kb):
    bsz, seq, _ = q.shape
    topk = min(TOPK_MAX, seq // 4)
    kern = functools.partial(_dsa_kernel, tq=tq, kb=kb, topk=topk)
    return pl.pallas_call(
        kern,
        grid=(bsz, seq // tq),
        in_specs=[pl.BlockSpec((1, tq, MIX_ATTN), lambda b, i: (b, i, 0)),
                  pl.BlockSpec((1, seq, MIX_ATTN), lambda b, i: (b, 0, 0)),
                  pl.BlockSpec((1, tq, IQ_WIDTH), lambda b, i: (b, i, 0)),
                  pl.BlockSpec((1, seq, LANES), lambda b, i: (b, 0, 0)),
                  pl.BlockSpec((1, S_ROWS, tq), lambda b, i: (b, 0, i)),
                  pl.BlockSpec((1, seq // kb, N_HEADS_ATTN * V_ROWS, kb), lambda b, i: (b, 0, 0, 0)),
                  pl.BlockSpec((kb, kb), lambda b, i: (0, 0))],
        out_specs=pl.BlockSpec((1, tq, MIX_ATTN), lambda b, i: (b, i, 0)),
        out_shape=jax.ShapeDtypeStruct((bsz, seq, MIX_ATTN), BF16),
        scratch_shapes=[pltpu.VMEM((seq // kb, kb, tq), jnp.int32),
                        pltpu.VMEM((4, seq // kb, kb, tq), BF16),
                        pltpu.VMEM((seq // kb, kb, tq), BF16),
                        pltpu.VMEM((kb, tq), F32),
                        pltpu.VMEM((N_HEADS_ATTN, kb, tq), F32),
                        pltpu.VMEM((N_HEADS_ATTN, kb, tq), BF16),
                        pltpu.VMEM((N_HEADS_ATTN, 1, tq), F32),
                        pltpu.VMEM((N_HEADS_ATTN, 1, tq), F32),
                        pltpu.VMEM((N_HEADS_ATTN, LANES, tq), F32)],
        compiler_params=_params(("parallel", "arbitrary")),
        name="dsa",
    )(q, k, iq, ik2, small_t, v_t, tril)


def _head_norm(y):
    mean_mat = jnp.full((HEAD_DIM, HEAD_DIM), 1.0 / HEAD_DIM, BF16)

    def mean_bcast(x):
        hi = x.astype(BF16)
        lo = (x - hi.astype(F32)).astype(BF16)
        return _dot(hi, mean_mat) + _dot(lo, mean_mat)

    yc = y - mean_bcast(y)
    return yc * lax.rsqrt(mean_bcast(yc * yc) + LN_EPS)


def _ret_kernel(r_ref, o_ref, state_scr, *, cr, grp):
    @pl.when(pl.program_id(1) == 0)
    def _():
        state_scr[...] = jnp.zeros_like(state_scr)

    ri = lax.broadcasted_iota(jnp.int32, (cr, cr), 0)
    ci = lax.broadcasted_iota(jnp.int32, (cr, cr), 1)
    diff = (ri - ci).astype(F32)
    pos = lax.broadcasted_iota(jnp.int32, (cr, 1), 0).astype(F32)
    items = [(h, g) for h in range(N_HEADS_RET) for g in range(grp)]
    sl = lambda part, h: slice(part * MIX_RET + h * HEAD_DIM, part * MIX_RET + (h + 1) * HEAD_DIM)
    log_gamma = [jnp.log1p(jnp.full((1, 1), -(2.0 ** (-5.0 - h)), F32)) for h in range(N_HEADS_RET)]
    decay_in = [jnp.where(diff >= 0, jnp.exp(diff * lg), 0.0) * (HEAD_DIM ** -0.5) for lg in log_gamma]
    q = {(h, g): r_ref[g, :, sl(0, h)] for h, g in items}
    k = {(h, g): r_ref[g, :, sl(1, h)] for h, g in items}
    v = {(h, g): r_ref[g, :, sl(2, h)] for h, g in items}
    state = {(h, g): state_scr[g, h] for h, g in items}
    scores = {it: _dot_nt(q[it], k[it]) * decay_in[it[0]] for it in items}
    cross = {it: jnp.exp((pos + 1.0) * log_gamma[it[0]]) * _dot(q[it], state[it].astype(BF16)) for it in items}
    inner = {it: _dot(scores[it].astype(BF16), v[it]) for it in items}
    for it in items:
        h, g = it
        gate = r_ref[g, :, sl(3, h)].astype(F32)
        y = _head_norm(inner[it] + cross[it])
        o_ref[g, :, h * HEAD_DIM:(h + 1) * HEAD_DIM] = (y * (gate * jax.nn.sigmoid(gate))).astype(o_ref.dtype)
    for it in items:
        h, g = it
        k_decay = (HEAD_DIM ** -0.5) * jnp.exp((cr - 1.0 - pos) * log_gamma[h])
        k_dec = (k[it].astype(F32) * k_decay).astype(BF16)
        state_scr[g, h] = state[it] * jnp.exp(cr * log_gamma[h]) + _dot_tn(k_dec, v[it])


def _retention(r_proj, cr, grp):
    bsz, seq, _ = r_proj.shape
    return pl.pallas_call(
        functools.partial(_ret_kernel, cr=cr, grp=grp),
        grid=(bsz // grp, seq // cr),
        in_specs=[pl.BlockSpec((grp, cr, R_WIDTH), lambda b, i: (b, i, 0))],
        out_specs=pl.BlockSpec((grp, cr, MIX_RET), lambda b, i: (b, i, 0)),
        out_shape=jax.ShapeDtypeStruct((bsz, seq, MIX_RET), BF16),
        scratch_shapes=[pltpu.VMEM((grp, N_HEADS_RET, HEAD_DIM, HEAD_DIM), F32)],
        compiler_params=_params(("parallel", "arbitrary")),
        name="retention",
    )(r_proj)


def _mlstm_kernel(m_ref, sm_ref, bias_ref, cw_ref, cb_ref, tril_ref, o_ref,
                  xbuf, a_scr, m_scr, *, cm, grp):
    halo = 8
    gate_shift = S_MF - S_MI

    @pl.when(pl.program_id(1) == 0)
    def _():
        xbuf[:, 0:halo, :] = jnp.zeros((grp, halo, 2 * MIX_MLSTM), F32)
        a_scr[...] = jnp.zeros_like(a_scr)
        m_scr[...] = jnp.zeros_like(m_scr)

    ri = lax.broadcasted_iota(jnp.int32, (cm, cm), 0)
    ci = lax.broadcasted_iota(jnp.int32, (cm, cm), 1)
    causal = ri >= ci
    row = lax.broadcasted_iota(jnp.int32, (cm, LANES), 0)
    lane = lax.broadcasted_iota(jnp.int32, (cm, LANES), 1)
    gate_lanes = (lane >= S_MI) & (lane < S_MI + N_HEADS_MLSTM)
    lane64 = lax.broadcasted_iota(jnp.int32, (cm, HEAD_DIM), 1)
    ones_col = jnp.where(lane64 == 0, 1.0, 0.0).astype(BF16)
    scale = HEAD_DIM ** -0.5

    def head_cols(x, h):
        return x[:, h * HEAD_DIM:(h + 1) * HEAD_DIM]

    items = [(g, h) for g in range(grp) for h in range(N_HEADS_MLSTM)]
    mm, qk, u_row, m_col, inter, e_inv, kw_col, decay = {}, {}, {}, {}, {}, {}, {}, {}
    for g in range(grp):
        mm[g] = m_ref[g]
        xbuf[g, halo:halo + cm, :] = mm[g][:, 0:2 * MIX_MLSTM].astype(F32)
        conv = cb_ref[...]
        for j in range(CONV_WIDTH):
            off = halo - (CONV_WIDTH - 1) + j
            conv = conv + xbuf[g, off:off + cm, :] * cw_ref[j:j + 1, :]
        xbuf[g, 0:halo, :] = xbuf[g, cm:cm + halo, :]
        qk[g] = conv * jax.nn.sigmoid(conv)

        gates = sm_ref[g] + bias_ref[...]
        b_all = jnp.dot(tril_ref[...], jax.nn.log_sigmoid(gates), preferred_element_type=F32,
                        precision=lax.Precision.HIGHEST)
        b_i = jnp.where(gate_lanes, pltpu.roll(b_all, LANES - gate_shift, 1), 0.0)
        u = jnp.where(gate_lanes, gates, 0.0) - b_i
        run = u
        step = 1
        while step < cm:
            run = jnp.maximum(run, jnp.where(row >= step, pltpu.roll(run, step, 0), -jnp.inf))
            step *= 2
        m_prev = m_scr[g]
        m_c = jnp.maximum(m_prev, run)
        m_last = m_c[cm - 1:cm, :]
        u_row[g] = u.T
        m_col[g] = m_c
        inter[g] = jnp.exp(m_prev - m_c)
        e_inv[g] = jnp.exp(-(b_i + m_c))
        kw_col[g] = scale * jnp.exp(u - m_last)
        decay[g] = jnp.exp(m_prev - m_last)
        m_scr[g] = b_i[cm - 1:cm, :] + m_last

    q, k, v_aug, a_mem = {}, {}, {}, {}
    for it in items:
        g, h = it
        q[it] = head_cols(qk[g], h).astype(BF16)
        k[it] = head_cols(qk[g][:, MIX_MLSTM:], h)
        v_aug[it] = jnp.concatenate([head_cols(mm[g][:, 2 * MIX_MLSTM:], h), ones_col], axis=1)
        a_mem[it] = a_scr[g, h]

    def col(x, h):
        return x[:, S_MI + h:S_MI + h + 1]

    w = {}
    for it in items:
        g, h = it
        u_r = u_row[g][S_MI + h:S_MI + h + 1, :]
        w[it] = jnp.exp(jnp.where(causal, u_r - col(m_col[g], h), -jnp.inf))

    s = {it: _dot_nt(q[it], k[it].astype(BF16)) * scale * w[it] for it in items}
    cross = {it: _dot(q[it], a_mem[it].astype(BF16)) for it in items}
    both = {it: _dot(s[it].astype(BF16), v_aug[it]) + col(inter[it[0]], it[1]) * cross[it] for it in items}
    for it in items:
        g, h = it
        den = both[it][:, HEAD_DIM:HEAD_DIM + 1]
        h_tilde = both[it][:, 0:HEAD_DIM] * (1.0 / jnp.maximum(jnp.abs(den), col(e_inv[g], h)))
        og = head_cols(mm[g][:, 3 * MIX_MLSTM:], h).astype(F32)
        o_ref[g, :, h * HEAD_DIM:(h + 1) * HEAD_DIM] = _head_norm(jax.nn.sigmoid(og) * h_tilde).astype(o_ref.dtype)

    for it in items:
        g, h = it
        kw = k[it] * col(kw_col[g], h)
        a_scr[g, h] = col(decay[g], h) * a_mem[it] + _dot_tn(kw.astype(BF16), v_aug[it])


def _mlstm(m_proj, small, gate_bias, conv_w, conv_b, tril, cm, grp):
    bsz, seq, _ = m_proj.shape
    const = lambda b, i: (0, 0)
    return pl.pallas_call(
        functools.partial(_mlstm_kernel, cm=cm, grp=grp),
        grid=(bsz // grp, seq // cm),
        in_specs=[pl.BlockSpec((grp, cm, M_WIDTH), lambda b, i: (b, i, 0)),
                  pl.BlockSpec((grp, cm, S_WIDTH), lambda b, i: (b, i, 0)),
                  pl.BlockSpec((1, S_WIDTH), const),
                  pl.BlockSpec((CONV_WIDTH, 2 * MIX_MLSTM), const),
                  pl.BlockSpec((1, 2 * MIX_MLSTM), const),
                  pl.BlockSpec((cm, cm), const)],
        out_specs=pl.BlockSpec((grp, cm, MIX_MLSTM), lambda b, i: (b, i, 0)),
        out_shape=jax.ShapeDtypeStruct((bsz, seq, MIX_MLSTM), BF16),
        scratch_shapes=[pltpu.VMEM((grp, cm + 8, 2 * MIX_MLSTM), F32),
                        pltpu.VMEM((grp, N_HEADS_MLSTM, HEAD_DIM, LANES), F32),
                        pltpu.VMEM((grp, 1, LANES), F32)],
        compiler_params=_params(("parallel", "arbitrary")),
        name="mlstm",
    )(m_proj, small, gate_bias, conv_w, conv_b, tril)


def _layer_norm(z, g, b):
    mu = jnp.mean(z, axis=-1, keepdims=True)
    var = jnp.mean(jnp.square(z - mu), axis=-1, keepdims=True)
    return (z - mu) * lax.rsqrt(var + LN_EPS) * g + b


def _route(scores, biased):
    col = lambda a, e: a[e:e + 1, :]
    epg = EXPERTS_PER_GROUP
    group_scores = []
    for g in range(N_GROUPS):
        vals = [col(biased, g * epg + j) for j in range(epg)]
        best = None
        for a in range(epg):
            for b in range(a + 1, epg):
                pair = vals[a] + vals[b]
                best = pair if best is None else jnp.maximum(best, pair)
        group_scores.append(best)
    best_g = jnp.zeros_like(group_scores[0], dtype=jnp.int32)
    best_v = group_scores[0]
    for g in range(1, N_GROUPS):
        better = group_scores[g] > best_v
        best_g = jnp.where(better, g, best_g)
        best_v = jnp.where(better, group_scores[g], best_v)
    cand_b = [sum(jnp.where(best_g == g, col(biased, g * epg + j), 0.0) for g in range(N_GROUPS))
              for j in range(epg)]
    cand_s = [sum(jnp.where(best_g == g, col(scores, g * epg + j), 0.0) for g in range(N_GROUPS))
              for j in range(epg)]

    def argmax_first(vals, skip=None):
        idx = None
        val = None
        for j, vj in enumerate(vals):
            if skip is not None:
                vj = jnp.where(skip == j, -jnp.inf, vj)
            if idx is None:
                idx, val = jnp.zeros_like(best_g), vj
            else:
                better = vj > val
                idx = jnp.where(better, j, idx)
                val = jnp.where(better, vj, val)
        return idx

    first = argmax_first(cand_b)
    second = argmax_first(cand_b, skip=first)
    w1 = sum(jnp.where(first == j, cand_s[j], 0.0) for j in range(epg))
    w2 = sum(jnp.where(second == j, cand_s[j], 0.0) for j in range(epg))
    total = w1 + w2
    e1 = best_g * epg + first
    e2 = best_g * epg + second
    expert = lax.broadcasted_iota(jnp.int32, scores.shape, 0)
    gate = jnp.where(expert == e1, w1 / total, 0.0) + jnp.where(expert == e2, w2 / total, 0.0)
    return gate, best_g


def _split_bf16(x):
    hi = x.astype(BF16)
    return hi, (x - hi.astype(F32)).astype(BF16)


def _moe_kernel(oa_ref, ob_ref, oc_ref, wo_ref, x_ref, gm_ref, gmix_ref, bmix_ref,
                sc_ref, sh_ref, gf_ref, wr_ref, br_ref, wg_ref, wu_ref, wd_ref, tri_ref, g_ref, b_ref,
                o_ref, hid_scr, *, alpha, cap):
    mix = _dot(oa_ref[0], wo_ref[0:MIX_ATTN, :])
    mix = mix + _dot(ob_ref[0], wo_ref[MIX_ATTN:MIX_ATTN + MIX_RET, :])
    mix = mix + _dot(oc_ref[0], wo_ref[MIX_ATTN + MIX_RET:, :])
    x = _layer_norm(alpha * x_ref[0] + (1.0 + gm_ref[0]) * mix, gmix_ref[...], bmix_ref[...])
    tm = x.shape[0]
    h = x * (1.0 + sc_ref[0]) + sh_ref[0]
    hb = h.astype(BF16)
    n_exp, _, d_ff = wg_ref.shape
    scores_t = jax.nn.sigmoid(_dot(h, wr_ref[...]).T[0:n_exp, :])
    gate_t, best_g = _route(scores_t, scores_t + br_ref[...])
    sub = lax.broadcasted_iota(jnp.int32, (2 * SUBLANES, tm), 0)
    member_t = jnp.where(sub == best_g, 1.0, 0.0)
    before_t = _dot(member_t.astype(BF16), tri_ref[...])
    rank = jnp.sum(member_t * before_t, axis=0, keepdims=True)
    count = jnp.max(jnp.sum(member_t, axis=1, keepdims=True))
    grp_f = best_g.astype(F32)
    sub8 = lax.broadcasted_iota(jnp.int32, (SUBLANES, tm), 0)
    extra = jnp.where(sub8 == 0, grp_f, jnp.where(sub8 == 1, rank, 0.0))
    info = jnp.concatenate([gate_t, extra, jnp.zeros((LANES - n_exp - SUBLANES, tm), F32)], axis=0).T
    gate = info
    epg = EXPERTS_PER_GROUP

    def expert_hidden(rows_b, gate_rows, e, dst):
        gate_pre = _dot(rows_b, wg_ref[e])
        up = _dot(rows_b, wu_ref[e])
        hid = gate_pre * jax.nn.sigmoid(gate_pre) * up * gate_rows[:, e:e + 1]
        hid_scr[0:rows_b.shape[0], dst * d_ff:(dst + 1) * d_ff] = hid.astype(BF16)

    def dense(_):
        for e in range(n_exp):
            expert_hidden(hb, gate, e, e)
        return _dot(hid_scr[...], wd_ref[...])


    def grouped(_):
        grp_c = info[:, n_exp:n_exp + 1]
        rank_c = info[:, n_exp + 1:n_exp + 2]
        slot_c = lax.broadcasted_iota(jnp.int32, (cap, 1), 0).astype(F32)
        slot_r = lax.broadcasted_iota(jnp.int32, (1, cap), 1).astype(F32)
        gate_hi, gate_lo = _split_bf16(gate)
        y = jnp.zeros((tm, x.shape[1]), F32)
        for g in range(N_GROUPS):
            take = jnp.where(jnp.where(grp_f == g, rank, -1.0) == slot_c, 1.0, 0.0).astype(BF16)
            give = jnp.where(jnp.where(grp_c == g, rank_c, -1.0) == slot_r, 1.0, 0.0).astype(BF16)
            rows_b = _dot(take, hb).astype(BF16)
            gate_rows = _dot(take, gate_hi) + _dot(take, gate_lo)
            for j in range(epg):
                expert_hidden(rows_b, gate_rows, g * epg + j, j)
            out = _dot(hid_scr[0:cap, 0:epg * d_ff], wd_ref[g * epg * d_ff:(g + 1) * epg * d_ff, :])
            y = y + _dot(give, out.astype(BF16))
        return y

    y = lax.cond(count <= cap, grouped, dense, 0)
    z = alpha * x + (1.0 + gf_ref[0]) * y
    o_ref[0] = _layer_norm(z, g_ref[...], b_ref[...])


def _moe(o_a, o_b, o_c, w_out, x, g_m, ln_mix_g, ln_mix_b,
         sc, sh, g_f, w_router, b_router, w_gate, w_up, w_down, layer, tri, ln_g, ln_b, tm, cap, alpha):
    bsz, seq, d = x.shape
    _, n_exp, _, d_ff = w_gate.shape
    row = lambda b, i: (b, i, 0)
    per_b = lambda b, i: (b, 0, 0)
    const = lambda b, i: (0, 0)
    of_layer4 = lambda b, i: (layer, 0, 0, 0)
    of_layer3 = lambda b, i: (layer, 0, 0)
    resident = pl.Buffered(1)
    return pl.pallas_call(
        functools.partial(_moe_kernel, alpha=alpha, cap=cap),
        grid=(bsz, seq // tm),
        in_specs=[pl.BlockSpec((1, tm, MIX_ATTN), row),
                  pl.BlockSpec((1, tm, MIX_RET), row),
                  pl.BlockSpec((1, tm, MIX_MLSTM), row),
                  pl.BlockSpec((None,) + w_out.shape[1:], of_layer3, pipeline_mode=resident),
                  pl.BlockSpec((1, tm, d), row),
                  pl.BlockSpec((1, 1, d), per_b),
                  pl.BlockSpec((1, d), const),
                  pl.BlockSpec((1, d), const),
                  pl.BlockSpec((1, 1, d), per_b),
                  pl.BlockSpec((1, 1, d), per_b),
                  pl.BlockSpec((1, 1, d), per_b),
                  pl.BlockSpec((d, LANES), const),
                  pl.BlockSpec((n_exp, 1), const),
                  pl.BlockSpec((None, n_exp, d, d_ff), of_layer4, pipeline_mode=resident),
                  pl.BlockSpec((None, n_exp, d, d_ff), of_layer4, pipeline_mode=resident),
                  pl.BlockSpec((None, n_exp * d_ff, d), of_layer3, pipeline_mode=resident),
                  pl.BlockSpec((tm, tm), const, pipeline_mode=resident),
                  pl.BlockSpec((1, d), const),
                  pl.BlockSpec((1, d), const)],
        out_specs=pl.BlockSpec((1, tm, d), row),
        out_shape=jax.ShapeDtypeStruct((bsz, seq, d), F32),
        scratch_shapes=[pltpu.VMEM((tm, n_exp * d_ff), BF16)],
        compiler_params=_params(("parallel", "parallel")),
        name="moe",
    )(o_a, o_b, o_c, w_out, x, g_m, ln_mix_g, ln_mix_b,
      sc, sh, g_f, w_router, b_router, w_gate, w_up, w_down, tri, ln_g, ln_b)


def _pick(n, pref):
    t = min(pref, n)
    while n % t:
        t //= 2
    return t


def _rope_tables(positions):
    half = HEAD_DIM // 2
    inv_freq = ROPE_THETA ** (-jnp.arange(half, dtype=F32) / half)
    ang = positions.astype(F32)[..., None] * inv_freq
    cos, sin = jnp.cos(ang), jnp.sin(ang)
    reps = LANES // HEAD_DIM
    return (jnp.concatenate([cos, cos] * reps, axis=-1),
            jnp.concatenate([-sin, sin] * reps, axis=-1))


def _prep_w_kernel(w_ref, o_ref):
    o_v = 2 * MIX_ATTN
    o_iq = o_v + MIX_ATTN
    o_ik = o_iq + IQ_WIDTH
    o_iw = o_ik + IDX_DIM
    o_r = o_iw + N_IDX_HEADS
    o_g = o_r + R_WIDTH + M_WIDTH
    n_gate = 2 * N_HEADS_MLSTM

    def put(dst, src, width):
        o_ref[0, :, dst:dst + width] = w_ref[0, :, src:src + width].astype(o_ref.dtype)

    o_ref[0] = jnp.zeros(o_ref.shape[1:], o_ref.dtype)
    put(OFF_Q, 0, 2 * MIX_ATTN)
    put(OFF_IQ, o_iq, IQ_WIDTH)
    put(OFF_R, o_r, R_WIDTH + M_WIDTH)
    put(OFF_IK, o_ik, IDX_DIM)
    put(OFF_IK + IDX_DIM, o_ik, IDX_DIM)
    put(OFF_S + S_IW, o_iw, N_IDX_HEADS)
    put(OFF_S + S_MI, o_g, n_gate)
    put(OFF_V, o_v, MIX_ATTN)


def _prep_w_in(w_in, tr):
    depth, d, n = w_in.shape
    return pl.pallas_call(
        _prep_w_kernel,
        grid=(depth, d // tr),
        in_specs=[pl.BlockSpec((1, tr, n), lambda l, i: (l, i, 0))],
        out_specs=pl.BlockSpec((1, tr, W_TOTAL), lambda l, i: (l, i, 0)),
        out_shape=jax.ShapeDtypeStruct((depth, d, W_TOTAL), BF16),
        compiler_params=_params(("parallel", "parallel")),
        name="prep_w_in",
    )(w_in)


def kernel(x, c, positions, w_ada, b_ada, w_in, i_bias, f_bias, conv_w, conv_b, w_out, ln_mix_g, ln_mix_b,
           w_router, b_router, w_gate, w_up, w_down, ln_ffn_g, ln_ffn_b):
    bsz, seq, d = x.shape
    depth = w_ada.shape[0]
    alpha = (2.0 * depth) ** 0.25

    tm = _pick(seq, 512)
    tq = _pick(seq, 256)
    kb = _pick(seq, 256)
    assert seq // 16 <= 256, "packed bf16 partial counts in the DSA threshold search must stay exact"
    cr = _pick(seq, 256)
    cm = _pick(seq, 256)
    tmoe = _pick(seq, 512)
    moe_cap = min(tmoe, (3 * tmoe // (2 * N_GROUPS) + 15) // 16 * 16)

    cos_t, sin_t = _rope_tables(positions)
    c_pad = jnp.zeros((8, d), F32).at[:bsz].set(c)
    mod = _ada_mod(c_pad, w_ada, b_ada, _pick(6 * d, 1536))
    w_in_p = _prep_w_in(w_in, _pick(d, 256))
    w_out_b = w_out.astype(BF16)
    w_gate_b, w_up_b = w_gate.astype(BF16), w_up.astype(BF16)
    w_down_b = w_down.astype(BF16).reshape(depth, -1, d)
    grp = 2 if bsz % 2 == 0 else 1
    grp_ret = 4 if bsz % 4 == 0 else grp

    tril_kb = (jnp.arange(kb)[:, None] >= jnp.arange(kb)[None, :]).astype(BF16)
    tril = (jnp.arange(cm)[:, None] >= jnp.arange(cm)[None, :]).astype(F32)
    w_router_p = jnp.zeros((d, LANES), F32).at[:, :N_EXPERTS].set(w_router)
    b_router_p = b_router.reshape(N_EXPERTS, 1)
    tri_moe = (jnp.arange(tmoe)[:, None] < jnp.arange(tmoe)[None, :]).astype(BF16)

    for l in range(depth):
        parts = [mod[l, :bsz, j * d:(j + 1) * d].reshape(bsz, 1, d) for j in range(6)]
        sh_m, sc_m, g_m, sh_f, sc_f, g_f = parts
        q, k, iq, ik2, r_proj, m_proj, small, small_t, v_t = _in_proj(
            x, sc_m, sh_m, w_in_p, l, cos_t, sin_t, tm, kb)
        o_a = _dsa(q, k, iq, ik2, small_t, v_t, tril_kb, tq, kb)
        o_b = _retention(r_proj, cr, grp_ret)
        gate_bias = (jnp.zeros((1, S_WIDTH), F32).at[0, S_MI:S_MI + N_HEADS_MLSTM].set(i_bias[l])
                     .at[0, S_MF:S_MF + N_HEADS_MLSTM].set(f_bias[l]))
        o_c = _mlstm(m_proj, small, gate_bias, conv_w[l], conv_b[l].reshape(1, -1), tril, cm, grp)
        x = _moe(o_a, o_b, o_c, w_out_b, x, g_m, ln_mix_g[l].reshape(1, d), ln_mix_b[l].reshape(1, d),
                 sc_f, sh_f, g_f, w_router_p, b_router_p,
                 w_gate_b, w_up_b, w_down_b, l, tri_moe,
                 ln_ffn_g[l].reshape(1, d), ln_ffn_b[l].reshape(1, d), tmoe, moe_cap, alpha)
    return x
```

```python
import functools

import numpy as np
import jax
import jax.numpy as jnp
from jax import lax
from jax.experimental import pallas as pl
from jax.experimental.pallas import tpu as pltpu

F32 = jnp.float32
BF16 = jnp.bfloat16

HEAD_DIM = 64
CHUNK = 64
N_HEADS_ATTN = 8
N_IDX_HEADS = 4
IDX_DIM = 64
TOPK_MAX = 256
N_HEADS_RET = 4
N_HEADS_MLSTM = 4
CONV_WIDTH = 4
ROPE_THETA = 10000.0
N_EXPERTS = 16
N_GROUPS = 4
EXPERTS_PER_GROUP = N_EXPERTS // N_GROUPS
D_FF_EXPERT = 256
LN_EPS = 1e-5

MIX_ATTN = N_HEADS_ATTN * HEAD_DIM
MIX_RET = N_HEADS_RET * HEAD_DIM
MIX_MLSTM = N_HEADS_MLSTM * HEAD_DIM

LANES = 128
SUBLANES = 8
VMEM_LIMIT = 56 * 1024 * 1024

IQ_WIDTH = N_IDX_HEADS * IDX_DIM
R_WIDTH = 4 * MIX_RET
M_WIDTH = 4 * MIX_MLSTM
S_WIDTH = LANES
S_IW = 0
S_MI = S_IW + N_IDX_HEADS
S_MF = S_MI + N_HEADS_MLSTM
S_ROWS = 16
V_ROWS = HEAD_DIM + 16
OFF_Q = 0
OFF_K = OFF_Q + MIX_ATTN
OFF_IQ = OFF_K + MIX_ATTN
OFF_R = OFF_IQ + IQ_WIDTH
OFF_M = OFF_R + R_WIDTH
OFF_IK = OFF_M + M_WIDTH
OFF_S = OFF_IK + LANES
OFF_V = OFF_S + S_WIDTH
W_TOTAL = OFF_V + MIX_ATTN

INT_MIN = -2 ** 31
NEG_BIG = -1e30
LOG2_E = 1.4426950408889634


def _dot(a, b):
    return jnp.dot(a, b, preferred_element_type=F32)


def _dot_nt(a, b):
    return lax.dot_general(a, b, (((1,), (1,)), ((), ())), preferred_element_type=F32)


def _dot_tn(a, b):
    return lax.dot_general(a, b, (((0,), (0,)), ((), ())), preferred_element_type=F32)


def _params(sem):
    return pltpu.CompilerParams(dimension_semantics=sem, vmem_limit_bytes=VMEM_LIMIT)


def _ada_kernel(c_ref, w_ref, b_ref, o_ref):
    c = c_ref[...]
    c_act = c * jax.nn.sigmoid(c)
    o_ref[0] = _dot(c_act, w_ref[0]) + b_ref[0]


def _ada_mod(c_pad, w_ada, b_ada, tn):
    depth, d, n = w_ada.shape
    rows = c_pad.shape[0]
    return pl.pallas_call(
        _ada_kernel,
        grid=(depth, n // tn),
        in_specs=[pl.BlockSpec((rows, d), lambda l, j: (0, 0)),
                  pl.BlockSpec((1, d, tn), lambda l, j: (l, 0, j)),
                  pl.BlockSpec((1, 1, tn), lambda l, j: (l, 0, j))],
        out_specs=pl.BlockSpec((1, rows, tn), lambda l, j: (l, 0, j)),
        out_shape=jax.ShapeDtypeStruct((depth, rows, n), F32),
        compiler_params=_params(("parallel", "parallel")),
        name="ada_mod",
    )(c_pad, w_ada, b_ada.reshape(depth, 1, n))


def _rope(y, cos, sin):
    w = y.shape[1]
    reps = w // LANES
    cosw = jnp.concatenate([cos] * reps, axis=1) if reps > 1 else cos
    sinw = jnp.concatenate([sin] * reps, axis=1) if reps > 1 else sin
    lane = lax.broadcasted_iota(jnp.int32, y.shape, 1)
    first = (lane % HEAD_DIM) < (HEAD_DIM // 2)
    partner = jnp.where(first, pltpu.roll(y, w - HEAD_DIM // 2, 1), pltpu.roll(y, HEAD_DIM // 2, 1))
    return y * cosw + partner * sinw


def _inproj_kernel(x_ref, sc_ref, sh_ref, w_ref, cos_ref, sin_ref,
                   q_ref, k_ref, iq_ref, ik_ref, r_ref, m_ref, s_ref, st_ref, vt_ref, *, kb):
    h = (x_ref[0] * (1.0 + sc_ref[0]) + sh_ref[0]).astype(BF16)
    cos = cos_ref[0]
    sin = sin_ref[0]

    def proj(start, width):
        return _dot(h, w_ref[:, start:start + width])

    q_ref[0] = (_rope(proj(OFF_Q, MIX_ATTN), cos, sin) * (HEAD_DIM ** -0.5 * LOG2_E)).astype(BF16)
    k_ref[0] = _rope(proj(OFF_K, MIX_ATTN), cos, sin).astype(BF16)
    iq_ref[0] = _rope(proj(OFF_IQ, IQ_WIDTH), cos, sin).astype(BF16)
    ik_ref[0] = _rope(proj(OFF_IK, LANES), cos, sin).astype(BF16)
    r_ref[0, :, 0:2 * MIX_RET] = _rope(proj(OFF_R, 2 * MIX_RET), cos, sin).astype(BF16)
    r_ref[0, :, 2 * MIX_RET:R_WIDTH] = proj(OFF_R + 2 * MIX_RET, 2 * MIX_RET).astype(BF16)
    m_ref[0, :, 0:2 * MIX_MLSTM] = proj(OFF_M, 2 * MIX_MLSTM).astype(BF16)
    m_ref[0, :, 2 * MIX_MLSTM:M_WIDTH] = proj(OFF_M + 2 * MIX_MLSTM, 2 * MIX_MLSTM).astype(BF16)
    y = proj(OFF_S, S_WIDTH)
    s_ref[0] = y
    st_ref[0] = y.T[0:S_ROWS, :]
    tm = h.shape[0]
    yvt = proj(OFF_V, MIX_ATTN).T.astype(BF16)
    pad_rows = lax.broadcasted_iota(jnp.int32, (V_ROWS - HEAD_DIM, kb), 0)
    ones_rows = jnp.where(pad_rows == 0, 1.0, 0.0).astype(BF16)
    for j in range(tm // kb):
        for hh in range(N_HEADS_ATTN):
            vt_ref[0, j, hh * V_ROWS:hh * V_ROWS + HEAD_DIM, :] = yvt[hh * HEAD_DIM:(hh + 1) * HEAD_DIM,
                                                                      j * kb:(j + 1) * kb]
            vt_ref[0, j, hh * V_ROWS + HEAD_DIM:(hh + 1) * V_ROWS, :] = ones_rows


def _in_proj(x, sc, sh, w, layer, cos_t, sin_t, tm, kb):
    bsz, seq, d = x.shape
    row = lambda b, i: (b, i, 0)
    per_b = lambda b, i: (b, 0, 0)
    widths = (MIX_ATTN, MIX_ATTN, IQ_WIDTH, LANES, R_WIDTH, M_WIDTH)
    return pl.pallas_call(
        functools.partial(_inproj_kernel, kb=kb),
        grid=(bsz, seq // tm),
        in_specs=[pl.BlockSpec((1, tm, d), row),
                  pl.BlockSpec((1, 1, d), per_b),
                  pl.BlockSpec((1, 1, d), per_b),
                  pl.BlockSpec((None, d, W_TOTAL), lambda b, i: (layer, 0, 0)),
                  pl.BlockSpec((1, tm, LANES), row),
                  pl.BlockSpec((1, tm, LANES), row)],
        out_specs=[pl.BlockSpec((1, tm, wd), row) for wd in widths]
                  + [pl.BlockSpec((1, tm, S_WIDTH), row),
                     pl.BlockSpec((1, S_ROWS, tm), lambda b, i: (b, 0, i)),
                     pl.BlockSpec((1, tm // kb, N_HEADS_ATTN * V_ROWS, kb), lambda b, i: (b, i, 0, 0))],
        out_shape=[jax.ShapeDtypeStruct((bsz, seq, wd), BF16) for wd in widths]
                  + [jax.ShapeDtypeStruct((bsz, seq, S_WIDTH), F32),
                     jax.ShapeDtypeStruct((bsz, S_ROWS, seq), F32),
                     jax.ShapeDtypeStruct((bsz, seq // kb, N_HEADS_ATTN * V_ROWS, kb), BF16)],
        compiler_params=_params(("parallel", "parallel")),
        name="in_proj",
    )(x, sc, sh, w, cos_t, sin_t)


def _dsa_kernel(q_ref, k_ref, iq_ref, ik_ref, st_ref, vt_ref, tril_ref, o_ref,
                key_scr, byte_scr, cand_scr, bias_scr, s_scr, p_scr, m_scr, alpha_scr, acc_scr,
                *, tq, kb, topk):
    q0 = pl.program_id(1) * tq
    n_blocks = (q0 + tq + kb - 1) // kb
    qpos = q0 + lax.broadcasted_iota(jnp.int32, (1, tq), 1)
    q_limit = (qpos // CHUNK + 1) * CHUNK
    krow = lax.broadcasted_iota(jnp.int32, (kb, 1), 0)

    def head_of_pair(x, h):
        pair = x[:, (h // 2) * LANES:(h // 2 + 1) * LANES]
        lane = lax.broadcasted_iota(jnp.int32, pair.shape, 1)
        keep = (lane < HEAD_DIM) if h % 2 == 0 else (lane >= HEAD_DIM)
        return jnp.where(keep, pair, jnp.zeros_like(pair))

    iw = st_ref[0][S_IW:S_IW + N_IDX_HEADS, :] * (N_IDX_HEADS ** -0.5 * IDX_DIM ** -0.5)
    iq = iq_ref[0]
    iq_heads = [head_of_pair(iq, h) for h in range(N_IDX_HEADS)]

    def score_block(c):
        k0 = pl.multiple_of(c * kb, kb)
        ik2 = ik_ref[0, pl.ds(k0, kb), :]
        score = jnp.zeros((kb, tq), F32)
        for h in range(N_IDX_HEADS):
            score = score + jnp.maximum(_dot_nt(ik2, iq_heads[h]), 0.0) * iw[h:h + 1, :]
        bits = pltpu.bitcast(score, jnp.int32)
        key = jnp.where(bits >= 0, bits, bits ^ jnp.int32(0x7FFFFFFF))
        key = jnp.where(k0 + krow < q_limit, key, jnp.int32(INT_MIN))
        key_scr[c] = key
        byte_scr[0, c] = ((key >> 24) + 128).astype(F32).astype(BF16)
        for lvl in range(1, 4):
            byte_scr[lvl, c] = ((key >> (24 - 8 * lvl)) & 255).astype(F32).astype(BF16)

    def score_pair(c2, carry):
        score_block(2 * c2)
        score_block(2 * c2 + 1)
        return carry

    lax.fori_loop(0, n_blocks // 2, score_pair, 0)

    @pl.when(n_blocks % 2 == 1)
    def _():
        score_block(n_blocks - 1)

    pack = 16
    one = jnp.ones((kb, tq), BF16)
    zero = jnp.zeros((kb, tq), BF16)

    def count_ge(lvl, cand):
        cand_b = cand.astype(BF16)

        def hits(c):
            plane = byte_scr[0, c] if lvl == 0 else cand_scr[c]
            hit = jnp.where(plane >= cand_b, one, zero)
            parts = [hit[i * pack:(i + 1) * pack, :] for i in range(kb // pack)]
            while len(parts) > 1:
                parts = [parts[i] + parts[i + 1] for i in range(0, len(parts), 2)]
            return parts[0]

        acc = lax.fori_loop(0, n_blocks // 2, lambda c2, a: a + hits(2 * c2) + hits(2 * c2 + 1),
                            jnp.zeros((pack, tq), BF16))
        acc = lax.cond(n_blocks % 2 == 1, lambda a: a + hits(n_blocks - 1), lambda a: a, acc)
        return jnp.sum(acc.astype(F32), axis=0, keepdims=True)

    above = jnp.zeros((1, tq), F32)
    t = jnp.zeros((1, tq), jnp.int32)
    for lvl in range(4):
        def bit_body(i, carry, lvl=lvl, above=above):
            v, rejected = carry
            cand = v + lax.shift_left(jnp.int32(1), 7 - i).astype(F32)
            cnt = count_ge(lvl, cand)
            ok = above + cnt >= topk
            return jnp.where(ok, cand, v), jnp.where(ok, rejected, cnt)

        v, rejected = lax.fori_loop(0, 8, bit_body, (jnp.zeros((1, tq), F32), jnp.zeros((1, tq), F32)))
        above = above + rejected
        t = t | lax.shift_left(v.astype(jnp.int32), 24 - 8 * lvl)
        if lvl < 3:
            v_b = v.astype(BF16)

            def narrow(c, lvl=lvl, v_b=v_b):
                plane = byte_scr[0, c] if lvl == 0 else cand_scr[c]
                cand_scr[c] = jnp.where(plane == v_b, byte_scr[lvl + 1, c], -one)

            def narrow_pair(c2, carry, narrow=narrow):
                narrow(2 * c2)
                narrow(2 * c2 + 1)
                return carry

            lax.fori_loop(0, n_blocks // 2, narrow_pair, 0)

            @pl.when(n_blocks % 2 == 1)
            def _(narrow=narrow):
                narrow(n_blocks - 1)
    thr = jnp.maximum(t ^ jnp.int32(INT_MIN), jnp.int32(INT_MIN + 1))
    need = topk - above

    q = q_ref[0]
    q_heads = [head_of_pair(q, h) for h in range(N_HEADS_ATTN)]
    m_scr[...] = jnp.full(m_scr.shape, NEG_BIG, F32)
    alpha_scr[...] = jnp.ones(alpha_scr.shape, F32)
    acc_scr[...] = jnp.zeros(acc_scr.shape, F32)
    p_scr[...] = jnp.zeros(p_scr.shape, BF16)

    def stage_logits(c, h):
        k0 = pl.multiple_of(c * kb, kb)
        kp = k_ref[0, pl.ds(k0, kb), (h // 2) * LANES:(h // 2 + 1) * LANES]
        s_scr[h] = _dot_nt(kp, q_heads[h])

    def stage_mask(c, ties_before):
        key = key_scr[c]
        tie = key == thr
        rank = _dot(tril_ref[...], jnp.where(tie, 1.0, 0.0).astype(BF16)) + ties_before
        sel = (key > thr) | (tie & (rank <= need))
        bias_scr[...] = jnp.where(sel, 0.0, NEG_BIG)
        return rank[kb - 1:kb, :]

    def stage_softmax(h):
        for half in range(tq // LANES):
            ln = slice(half * LANES, (half + 1) * LANES)
            s = s_scr[h, :, ln] + bias_scr[:, ln]
            m_old = m_scr[h, :, ln]
            m_new = jnp.maximum(m_old, jnp.max(s, axis=0, keepdims=True))
            p_scr[h, :, ln] = jnp.exp2(s - m_new).astype(BF16)
            alpha_scr[h, :, ln] = jnp.exp2(m_old - m_new)
            m_scr[h, :, ln] = m_new

    def stage_values(c, h):
        vt = vt_ref[0, c, h * V_ROWS:(h + 1) * V_ROWS, :]
        acc_scr[h, 0:V_ROWS, :] = alpha_scr[h] * acc_scr[h, 0:V_ROWS, :] + _dot(vt, p_scr[h])

    ties0 = stage_mask(0, jnp.zeros((1, tq), F32))
    for h in range(N_HEADS_ATTN):
        stage_logits(0, h)

    def attn_body(j, ties_before):
        c_old = jnp.maximum(j - 2, 0)
        for h in range(N_HEADS_ATTN):
            stage_values(c_old, h)
            stage_softmax(h)
            stage_logits(j, h)
        return stage_mask(j, ties_before)

    lax.fori_loop(1, n_blocks, attn_body, ties0)
    for h in range(N_HEADS_ATTN):
        stage_values(jnp.maximum(n_blocks - 2, 0), h)
        stage_softmax(h)
    for h in range(N_HEADS_ATTN):
        stage_values(n_blocks - 1, h)
    for h in range(N_HEADS_ATTN):
        acc = acc_scr[h]
        out = acc * (1.0 / acc[HEAD_DIM:HEAD_DIM + 1, :])
        o_ref[0, :, h * HEAD_DIM:(h + 1) * HEAD_DIM] = out.T[:, 0:HEAD_DIM].astype(o_ref.dtype)


def _dsa(q, k, iq, ik2, small_t, v_t, tril, tq, kb):
    bsz, seq, _ = q.shape
    topk = min(TOPK_MAX, seq // 4)
    kern = functools.partial(_dsa_kernel, tq=tq, kb=kb, topk=topk)
    per_batch = pl.Buffered(1)
    return pl.pallas_call(
        kern,
        grid=(bsz, seq // tq),
        in_specs=[pl.BlockSpec((1, tq, MIX_ATTN), lambda b, i: (b, i, 0)),
                  pl.BlockSpec((1, seq, MIX_ATTN), lambda b, i: (b, 0, 0), pipeline_mode=per_batch),
                  pl.BlockSpec((1, tq, IQ_WIDTH), lambda b, i: (b, i, 0)),
                  pl.BlockSpec((1, seq, LANES), lambda b, i: (b, 0, 0), pipeline_mode=per_batch),
                  pl.BlockSpec((1, S_ROWS, tq), lambda b, i: (b, 0, i)),
                  pl.BlockSpec((1, seq // kb, N_HEADS_ATTN * V_ROWS, kb), lambda b, i: (b, 0, 0, 0),
                               pipeline_mode=per_batch),
                  pl.BlockSpec((kb, kb), lambda b, i: (0, 0))],
        out_specs=pl.BlockSpec((1, tq, MIX_ATTN), lambda b, i: (b, i, 0)),
        out_shape=jax.ShapeDtypeStruct((bsz, seq, MIX_ATTN), BF16),
        scratch_shapes=[pltpu.VMEM((seq // kb, kb, tq), jnp.int32),
                        pltpu.VMEM((4, seq // kb, kb, tq), BF16),
                        pltpu.VMEM((seq // kb, kb, tq), BF16),
                        pltpu.VMEM((kb, tq), F32),
                        pltpu.VMEM((N_HEADS_ATTN, kb, tq), F32),
                        pltpu.VMEM((N_HEADS_ATTN, kb, tq), BF16),
                        pltpu.VMEM((N_HEADS_ATTN, 1, tq), F32),
                        pltpu.VMEM((N_HEADS_ATTN, 1, tq), F32),
                        pltpu.VMEM((N_HEADS_ATTN, LANES, tq), F32)],
        compiler_params=_params(("parallel", "arbitrary")),
        name="dsa",
    )(q, k, iq, ik2, small_t, v_t, tril)


def _head_norm(y):
    mean_mat = jnp.full((HEAD_DIM, HEAD_DIM), 1.0 / HEAD_DIM, BF16)

    def mean_bcast(x):
        hi = x.astype(BF16)
        lo = (x - hi.astype(F32)).astype(BF16)
        return _dot(hi, mean_mat) + _dot(lo, mean_mat)

    yc = y - mean_bcast(y)
    return yc * lax.rsqrt(mean_bcast(yc * yc) + LN_EPS)


def _ret_kernel(r_ref, o_ref, state_scr, *, cr, grp):
    @pl.when(pl.program_id(1) == 0)
    def _():
        state_scr[...] = jnp.zeros_like(state_scr)

    ri = lax.broadcasted_iota(jnp.int32, (cr, cr), 0)
    ci = lax.broadcasted_iota(jnp.int32, (cr, cr), 1)
    diff = (ri - ci).astype(F32)
    pos = lax.broadcasted_iota(jnp.int32, (cr, 1), 0).astype(F32)
    items = [(h, g) for h in range(N_HEADS_RET) for g in range(grp)]
    sl = lambda part, h: slice(part * MIX_RET + h * HEAD_DIM, part * MIX_RET + (h + 1) * HEAD_DIM)
    log_gamma = [jnp.log1p(jnp.full((1, 1), -(2.0 ** (-5.0 - h)), F32)) for h in range(N_HEADS_RET)]
    decay_in = [jnp.where(diff >= 0, jnp.exp(diff * lg), 0.0) * (HEAD_DIM ** -0.5) for lg in log_gamma]
    q = {(h, g): r_ref[g, :, sl(0, h)] for h, g in items}
    k = {(h, g): r_ref[g, :, sl(1, h)] for h, g in items}
    v = {(h, g): r_ref[g, :, sl(2, h)] for h, g in items}
    state = {(h, g): state_scr[g, h] for h, g in items}
    scores = {it: _dot_nt(q[it], k[it]) * decay_in[it[0]] for it in items}
    cross = {it: jnp.exp((pos + 1.0) * log_gamma[it[0]]) * _dot(q[it], state[it].astype(BF16)) for it in items}
    inner = {it: _dot(scores[it].astype(BF16), v[it]) for it in items}
    for it in items:
        h, g = it
        gate = r_ref[g, :, sl(3, h)].astype(F32)
        y = _head_norm(inner[it] + cross[it])
        o_ref[g, :, h * HEAD_DIM:(h + 1) * HEAD_DIM] = (y * (gate * jax.nn.sigmoid(gate))).astype(o_ref.dtype)
    for it in items:
        h, g = it
        k_decay = (HEAD_DIM ** -0.5) * jnp.exp((cr - 1.0 - pos) * log_gamma[h])
        k_dec = (k[it].astype(F32) * k_decay).astype(BF16)
        state_scr[g, h] = state[it] * jnp.exp(cr * log_gamma[h]) + _dot_tn(k_dec, v[it])


def _retention(r_proj, cr, grp):
    bsz, seq, _ = r_proj.shape
    return pl.pallas_call(
        functools.partial(_ret_kernel, cr=cr, grp=grp),
        grid=(bsz // grp, seq // cr),
        in_specs=[pl.BlockSpec((grp, cr, R_WIDTH), lambda b, i: (b, i, 0))],
        out_specs=pl.BlockSpec((grp, cr, MIX_RET), lambda b, i: (b, i, 0)),
        out_shape=jax.ShapeDtypeStruct((bsz, seq, MIX_RET), BF16),
        scratch_shapes=[pltpu.VMEM((grp, N_HEADS_RET, HEAD_DIM, HEAD_DIM), F32)],
        compiler_params=_params(("parallel", "arbitrary")),
        name="retention",
    )(r_proj)


def _mlstm_kernel(m_ref, sm_ref, bias_ref, cw_ref, cb_ref, tril_ref, o_ref,
                  xbuf, a_scr, m_scr, *, cm, grp):
    halo = 8
    gate_shift = S_MF - S_MI

    @pl.when(pl.program_id(1) == 0)
    def _():
        xbuf[:, 0:halo, :] = jnp.zeros((grp, halo, 2 * MIX_MLSTM), F32)
        a_scr[...] = jnp.zeros_like(a_scr)
        m_scr[...] = jnp.zeros_like(m_scr)

    ri = lax.broadcasted_iota(jnp.int32, (cm, cm), 0)
    ci = lax.broadcasted_iota(jnp.int32, (cm, cm), 1)
    causal = ri >= ci
    row = lax.broadcasted_iota(jnp.int32, (cm, LANES), 0)
    lane = lax.broadcasted_iota(jnp.int32, (cm, LANES), 1)
    gate_lanes = (lane >= S_MI) & (lane < S_MI + N_HEADS_MLSTM)
    lane64 = lax.broadcasted_iota(jnp.int32, (cm, HEAD_DIM), 1)
    ones_col = jnp.where(lane64 == 0, 1.0, 0.0).astype(BF16)
    scale = HEAD_DIM ** -0.5

    def head_cols(x, h):
        return x[:, h * HEAD_DIM:(h + 1) * HEAD_DIM]

    items = [(g, h) for g in range(grp) for h in range(N_HEADS_MLSTM)]
    mm, qk, u_row, m_col, inter, e_inv, kw_col, decay = {}, {}, {}, {}, {}, {}, {}, {}
    for g in range(grp):
        mm[g] = m_ref[g]
        xbuf[g, halo:halo + cm, :] = mm[g][:, 0:2 * MIX_MLSTM].astype(F32)
        conv = cb_ref[...]
        for j in range(CONV_WIDTH):
            off = halo - (CONV_WIDTH - 1) + j
            conv = conv + xbuf[g, off:off + cm, :] * cw_ref[j:j + 1, :]
        xbuf[g, 0:halo, :] = xbuf[g, cm:cm + halo, :]
        qk[g] = conv * jax.nn.sigmoid(conv)

        gates = sm_ref[g] + bias_ref[...]
        b_all = jnp.dot(tril_ref[...], jax.nn.log_sigmoid(gates), preferred_element_type=F32,
                        precision=lax.Precision.HIGHEST)
        b_i = jnp.where(gate_lanes, pltpu.roll(b_all, LANES - gate_shift, 1), 0.0)
        u = jnp.where(gate_lanes, gates, 0.0) - b_i
        run = u
        step = 1
        while step < cm:
            run = jnp.maximum(run, jnp.where(row >= step, pltpu.roll(run, step, 0), -jnp.inf))
            step *= 2
        m_prev = m_scr[g]
        m_c = jnp.maximum(m_prev, run)
        m_last = m_c[cm - 1:cm, :]
        u_row[g] = u.T
        m_col[g] = m_c
        inter[g] = jnp.exp(m_prev - m_c)
        e_inv[g] = jnp.exp(-(b_i + m_c))
        kw_col[g] = scale * jnp.exp(u - m_last)
        decay[g] = jnp.exp(m_prev - m_last)
        m_scr[g] = b_i[cm - 1:cm, :] + m_last

    q, k, v_aug, a_mem = {}, {}, {}, {}
    for it in items:
        g, h = it
        q[it] = head_cols(qk[g], h).astype(BF16)
        k[it] = head_cols(qk[g][:, MIX_MLSTM:], h)
        v_aug[it] = jnp.concatenate([head_cols(mm[g][:, 2 * MIX_MLSTM:], h), ones_col], axis=1)
        a_mem[it] = a_scr[g, h]

    def col(x, h):
        return x[:, S_MI + h:S_MI + h + 1]

    w = {}
    for it in items:
        g, h = it
        u_r = u_row[g][S_MI + h:S_MI + h + 1, :]
        w[it] = jnp.exp(jnp.where(causal, u_r - col(m_col[g], h), -jnp.inf))

    s = {it: _dot_nt(q[it], k[it].astype(BF16)) * scale * w[it] for it in items}
    cross = {it: _dot(q[it], a_mem[it].astype(BF16)) for it in items}
    both = {it: _dot(s[it].astype(BF16), v_aug[it]) + col(inter[it[0]], it[1]) * cross[it] for it in items}
    for it in items:
        g, h = it
        den = both[it][:, HEAD_DIM:HEAD_DIM + 1]
        h_tilde = both[it][:, 0:HEAD_DIM] * (1.0 / jnp.maximum(jnp.abs(den), col(e_inv[g], h)))
        og = head_cols(mm[g][:, 3 * MIX_MLSTM:], h).astype(F32)
        o_ref[g, :, h * HEAD_DIM:(h + 1) * HEAD_DIM] = _head_norm(jax.nn.sigmoid(og) * h_tilde).astype(o_ref.dtype)

    for it in items:
        g, h = it
        kw = k[it] * col(kw_col[g], h)
        a_scr[g, h] = col(decay[g], h) * a_mem[it] + _dot_tn(kw.astype(BF16), v_aug[it])


def _mlstm(m_proj, small, gate_bias, conv_w, conv_b, tril, cm, grp):
    bsz, seq, _ = m_proj.shape
    const = lambda b, i: (0, 0)
    return pl.pallas_call(
        functools.partial(_mlstm_kernel, cm=cm, grp=grp),
        grid=(bsz // grp, seq // cm),
        in_specs=[pl.BlockSpec((grp, cm, M_WIDTH), lambda b, i: (b, i, 0)),
                  pl.BlockSpec((grp, cm, S_WIDTH), lambda b, i: (b, i, 0)),
                  pl.BlockSpec((1, S_WIDTH), const),
                  pl.BlockSpec((CONV_WIDTH, 2 * MIX_MLSTM), const),
                  pl.BlockSpec((1, 2 * MIX_MLSTM), const),
                  pl.BlockSpec((cm, cm), const)],
        out_specs=pl.BlockSpec((grp, cm, MIX_MLSTM), lambda b, i: (b, i, 0)),
        out_shape=jax.ShapeDtypeStruct((bsz, seq, MIX_MLSTM), BF16),
        scratch_shapes=[pltpu.VMEM((grp, cm + 8, 2 * MIX_MLSTM), F32),
                        pltpu.VMEM((grp, N_HEADS_MLSTM, HEAD_DIM, LANES), F32),
                        pltpu.VMEM((grp, 1, LANES), F32)],
        compiler_params=_params(("parallel", "arbitrary")),
        name="mlstm",
    )(m_proj, small, gate_bias, conv_w, conv_b, tril)


def _layer_norm(z, g, b):
    mu = jnp.mean(z, axis=-1, keepdims=True)
    var = jnp.mean(jnp.square(z - mu), axis=-1, keepdims=True)
    return (z - mu) * lax.rsqrt(var + LN_EPS) * g + b


def _route(scores, biased):
    col = lambda a, e: a[e:e + 1, :]
    epg = EXPERTS_PER_GROUP
    group_scores = []
    for g in range(N_GROUPS):
        vals = [col(biased, g * epg + j) for j in range(epg)]
        best = None
        for a in range(epg):
            for b in range(a + 1, epg):
                pair = vals[a] + vals[b]
                best = pair if best is None else jnp.maximum(best, pair)
        group_scores.append(best)
    best_g = jnp.zeros_like(group_scores[0], dtype=jnp.int32)
    best_v = group_scores[0]
    for g in range(1, N_GROUPS):
        better = group_scores[g] > best_v
        best_g = jnp.where(better, g, best_g)
        best_v = jnp.where(better, group_scores[g], best_v)
    cand_b = [sum(jnp.where(best_g == g, col(biased, g * epg + j), 0.0) for g in range(N_GROUPS))
              for j in range(epg)]
    cand_s = [sum(jnp.where(best_g == g, col(scores, g * epg + j), 0.0) for g in range(N_GROUPS))
              for j in range(epg)]

    def argmax_first(vals, skip=None):
        idx = None
        val = None
        for j, vj in enumerate(vals):
            if skip is not None:
                vj = jnp.where(skip == j, -jnp.inf, vj)
            if idx is None:
                idx, val = jnp.zeros_like(best_g), vj
            else:
                better = vj > val
                idx = jnp.where(better, j, idx)
                val = jnp.where(better, vj, val)
        return idx

    first = argmax_first(cand_b)
    second = argmax_first(cand_b, skip=first)
    w1 = sum(jnp.where(first == j, cand_s[j], 0.0) for j in range(epg))
    w2 = sum(jnp.where(second == j, cand_s[j], 0.0) for j in range(epg))
    total = w1 + w2
    e1 = best_g * epg + first
    e2 = best_g * epg + second
    expert = lax.broadcasted_iota(jnp.int32, scores.shape, 0)
    gate = jnp.where(expert == e1, w1 / total, 0.0) + jnp.where(expert == e2, w2 / total, 0.0)
    return gate, best_g


def _split_bf16(x):
    hi = x.astype(BF16)
    return hi, (x - hi.astype(F32)).astype(BF16)


def _moe_kernel(oa_ref, ob_ref, oc_ref, wo_ref, x_ref, gm_ref, gmix_ref, bmix_ref,
                sc_ref, sh_ref, gf_ref, wr_ref, br_ref, wg_ref, wu_ref, wd_ref, tri_ref, g_ref, b_ref,
                o_ref, hid_scr, *, alpha, cap):
    mix = _dot(oa_ref[0], wo_ref[0:MIX_ATTN, :])
    mix = mix + _dot(ob_ref[0], wo_ref[MIX_ATTN:MIX_ATTN + MIX_RET, :])
    mix = mix + _dot(oc_ref[0], wo_ref[MIX_ATTN + MIX_RET:, :])
    x = _layer_norm(alpha * x_ref[0] + (1.0 + gm_ref[0]) * mix, gmix_ref[...], bmix_ref[...])
    tm = x.shape[0]
    h = x * (1.0 + sc_ref[0]) + sh_ref[0]
    hb = h.astype(BF16)
    n_exp, _, d_ff = wg_ref.shape
    scores_t = jax.nn.sigmoid(_dot(h, wr_ref[...]).T[0:n_exp, :])
    gate_t, best_g = _route(scores_t, scores_t + br_ref[...])
    sub = lax.broadcasted_iota(jnp.int32, (2 * SUBLANES, tm), 0)
    member_t = jnp.where(sub == best_g, 1.0, 0.0)
    before_t = _dot(member_t.astype(BF16), tri_ref[...])
    rank = jnp.sum(member_t * before_t, axis=0, keepdims=True)
    count = jnp.max(jnp.sum(member_t, axis=1, keepdims=True))
    grp_f = best_g.astype(F32)
    sub8 = lax.broadcasted_iota(jnp.int32, (SUBLANES, tm), 0)
    extra = jnp.where(sub8 == 0, grp_f, jnp.where(sub8 == 1, rank, 0.0))
    info = jnp.concatenate([gate_t, extra, jnp.zeros((LANES - n_exp - SUBLANES, tm), F32)], axis=0).T
    gate = info
    epg = EXPERTS_PER_GROUP

    def expert_hidden(rows_b, gate_rows, e, dst):
        gate_pre = _dot(rows_b, wg_ref[e])
        up = _dot(rows_b, wu_ref[e])
        hid = gate_pre * jax.nn.sigmoid(gate_pre) * up * gate_rows[:, e:e + 1]
        hid_scr[0:rows_b.shape[0], dst * d_ff:(dst + 1) * d_ff] = hid.astype(BF16)

    def dense(_):
        for e in range(n_exp):
            expert_hidden(hb, gate, e, e)
        return _dot(hid_scr[...], wd_ref[...])


    def grouped(_):
        grp_c = info[:, n_exp:n_exp + 1]
        rank_c = info[:, n_exp + 1:n_exp + 2]
        slot_c = lax.broadcasted_iota(jnp.int32, (cap, 1), 0).astype(F32)
        slot_r = lax.broadcasted_iota(jnp.int32, (1, cap), 1).astype(F32)
        gate_hi, gate_lo = _split_bf16(gate)
        y = jnp.zeros((tm, x.shape[1]), F32)
        for g in range(N_GROUPS):
            take = jnp.where(jnp.where(grp_f == g, rank, -1.0) == slot_c, 1.0, 0.0).astype(BF16)
            give = jnp.where(jnp.where(grp_c == g, rank_c, -1.0) == slot_r, 1.0, 0.0).astype(BF16)
            rows_b = _dot(take, hb).astype(BF16)
            gate_rows = _dot(take, gate_hi) + _dot(take, gate_lo)
            for j in range(epg):
                expert_hidden(rows_b, gate_rows, g * epg + j, j)
            out = _dot(hid_scr[0:cap, 0:epg * d_ff], wd_ref[g * epg * d_ff:(g + 1) * epg * d_ff, :])
            y = y + _dot(give, out.astype(BF16))
        return y

    y = lax.cond(count <= cap, grouped, dense, 0)
    z = alpha * x + (1.0 + gf_ref[0]) * y
    o_ref[0] = _layer_norm(z, g_ref[...], b_ref[...])


def _moe(o_a, o_b, o_c, w_out, x, g_m, ln_mix_g, ln_mix_b,
         sc, sh, g_f, w_router, b_router, w_gate, w_up, w_down, layer, tri, ln_g, ln_b, tm, cap, alpha):
    bsz, seq, d = x.shape
    _, n_exp, _, d_ff = w_gate.shape
    row = lambda b, i: (b, i, 0)
    per_b = lambda b, i: (b, 0, 0)
    const = lambda b, i: (0, 0)
    of_layer4 = lambda b, i: (layer, 0, 0, 0)
    of_layer3 = lambda b, i: (layer, 0, 0)
    resident = pl.Buffered(1)
    return pl.pallas_call(
        functools.partial(_moe_kernel, alpha=alpha, cap=cap),
        grid=(bsz, seq // tm),
        in_specs=[pl.BlockSpec((1, tm, MIX_ATTN), row),
                  pl.BlockSpec((1, tm, MIX_RET), row),
                  pl.BlockSpec((1, tm, MIX_MLSTM), row),
                  pl.BlockSpec((None,) + w_out.shape[1:], of_layer3, pipeline_mode=resident),
                  pl.BlockSpec((1, tm, d), row),
                  pl.BlockSpec((1, 1, d), per_b),
                  pl.BlockSpec((1, d), const),
                  pl.BlockSpec((1, d), const),
                  pl.BlockSpec((1, 1, d), per_b),
                  pl.BlockSpec((1, 1, d), per_b),
                  pl.BlockSpec((1, 1, d), per_b),
                  pl.BlockSpec((d, LANES), const),
                  pl.BlockSpec((n_exp, 1), const),
                  pl.BlockSpec((None, n_exp, d, d_ff), of_layer4, pipeline_mode=resident),
                  pl.BlockSpec((None, n_exp, d, d_ff), of_layer4, pipeline_mode=resident),
                  pl.BlockSpec((None, n_exp * d_ff, d), of_layer3, pipeline_mode=resident),
                  pl.BlockSpec((tm, tm), const, pipeline_mode=resident),
                  pl.BlockSpec((1, d), const),
                  pl.BlockSpec((1, d), const)],
        out_specs=pl.BlockSpec((1, tm, d), row),
        out_shape=jax.ShapeDtypeStruct((bsz, seq, d), F32),
        scratch_shapes=[pltpu.VMEM((tm, n_exp * d_ff), BF16)],
        compiler_params=_params(("parallel", "parallel")),
        name="moe",
    )(o_a, o_b, o_c, w_out, x, g_m, ln_mix_g, ln_mix_b,
      sc, sh, g_f, w_router, b_router, w_gate, w_up, w_down, tri, ln_g, ln_b)


def _pick(n, pref):
    t = min(pref, n)
    while n % t:
        t //= 2
    return t


def _rope_tables(positions):
    half = HEAD_DIM // 2
    inv_freq = ROPE_THETA ** (-jnp.arange(half, dtype=F32) / half)
    ang = positions.astype(F32)[..., None] * inv_freq
    cos, sin = jnp.cos(ang), jnp.sin(ang)
    reps = LANES // HEAD_DIM
    return (jnp.concatenate([cos, cos] * reps, axis=-1),
            jnp.concatenate([-sin, sin] * reps, axis=-1))


def _prep_w_kernel(w_ref, o_ref):
    o_v = 2 * MIX_ATTN
    o_iq = o_v + MIX_ATTN
    o_ik = o_iq + IQ_WIDTH
    o_iw = o_ik + IDX_DIM
    o_r = o_iw + N_IDX_HEADS
    o_g = o_r + R_WIDTH + M_WIDTH
    n_gate = 2 * N_HEADS_MLSTM

    def put(dst, src, width):
        o_ref[0, :, dst:dst + width] = w_ref[0, :, src:src + width].astype(o_ref.dtype)

    o_ref[0] = jnp.zeros(o_ref.shape[1:], o_ref.dtype)
    put(OFF_Q, 0, 2 * MIX_ATTN)
    put(OFF_IQ, o_iq, IQ_WIDTH)
    put(OFF_R, o_r, R_WIDTH + M_WIDTH)
    put(OFF_IK, o_ik, IDX_DIM)
    put(OFF_IK + IDX_DIM, o_ik, IDX_DIM)
    put(OFF_S + S_IW, o_iw, N_IDX_HEADS)
    put(OFF_S + S_MI, o_g, n_gate)
    put(OFF_V, o_v, MIX_ATTN)


def _prep_w_in(w_in, tr):
    depth, d, n = w_in.shape
    return pl.pallas_call(
        _prep_w_kernel,
        grid=(depth, d // tr),
        in_specs=[pl.BlockSpec((1, tr, n), lambda l, i: (l, i, 0))],
        out_specs=pl.BlockSpec((1, tr, W_TOTAL), lambda l, i: (l, i, 0)),
        out_shape=jax.ShapeDtypeStruct((depth, d, W_TOTAL), BF16),
        compiler_params=_params(("parallel", "parallel")),
        name="prep_w_in",
    )(w_in)


def kernel(x, c, positions, w_ada, b_ada, w_in, i_bias, f_bias, conv_w, conv_b, w_out, ln_mix_g, ln_mix_b,
           w_router, b_router, w_gate, w_up, w_down, ln_ffn_g, ln_ffn_b):
    bsz, seq, d = x.shape
    depth = w_ada.shape[0]
    alpha = (2.0 * depth) ** 0.25

    tm = _pick(seq, 512)
    tq = _pick(seq, 256)
    kb = _pick(seq, 256)
    assert seq // 16 <= 256, "packed bf16 partial counts in the DSA threshold search must stay exact"
    cr = _pick(seq, 256)
    cm = _pick(seq, 256)
    tmoe = _pick(seq, 512)
    moe_cap = min(tmoe, (3 * tmoe // (2 * N_GROUPS) + 15) // 16 * 16)

    cos_t, sin_t = _rope_tables(positions)
    c_pad = jnp.zeros((8, d), F32).at[:bsz].set(c)
    mod = _ada_mod(c_pad, w_ada, b_ada, _pick(6 * d, 1536))
    w_in_p = _prep_w_in(w_in, _pick(d, 256))
    w_out_b = w_out.astype(BF16)
    w_gate_b, w_up_b = w_gate.astype(BF16), w_up.astype(BF16)
    w_down_b = w_down.astype(BF16).reshape(depth, -1, d)
    grp = 2 if bsz % 2 == 0 else 1
    grp_ret = 4 if bsz % 4 == 0 else grp

    tril_kb = (jnp.arange(kb)[:, None] >= jnp.arange(kb)[None, :]).astype(BF16)
    tril = (jnp.arange(cm)[:, None] >= jnp.arange(cm)[None, :]).astype(F32)
    w_router_p = jnp.zeros((d, LANES), F32).at[:, :N_EXPERTS].set(w_router)
    b_router_p = b_router.reshape(N_EXPERTS, 1)
    tri_moe = (jnp.arange(tmoe)[:, None] < jnp.arange(tmoe)[None, :]).astype(BF16)

    for l in range(depth):
        parts = [mod[l, :bsz, j * d:(j + 1) * d].reshape(bsz, 1, d) for j in range(6)]
        sh_m, sc_m, g_m, sh_f, sc_f, g_f = parts
        q, k, iq, ik2, r_proj, m_proj, small, small_t, v_t = _in_proj(
            x, sc_m, sh_m, w_in_p, l, cos_t, sin_t, tm, kb)
        o_a = _dsa(q, k, iq, ik2, small_t, v_t, tril_kb, tq, kb)
        o_b = _retention(r_proj, cr, grp_ret)
        gate_bias = (jnp.zeros((1, S_WIDTH), F32).at[0, S_MI:S_MI + N_HEADS_MLSTM].set(i_bias[l])
                     .at[0, S_MF:S_MF + N_HEADS_MLSTM].set(f_bias[l]))
        o_c = _mlstm(m_proj, small, gate_bias, conv_w[l], conv_b[l].reshape(1, -1), tril, cm, grp)
        x = _moe(o_a, o_b, o_c, w_out_b, x, g_m, ln_mix_g[l].reshape(1, d), ln_mix_b[l].reshape(1, d),
                 sc_f, sh_f, g_f, w_router_p, b_router_p,
                 w_gate_b, w_up_b, w_down_b, l, tri_moe,
                 ln_ffn_g[l].reshape(1, d), ln_ffn_b[l].reshape(1, d), tmoe, moe_cap, alpha)
    return x
```

```python
import functools

import numpy as np
import jax
import jax.numpy as jnp
from jax import lax
from jax.experimental import pallas as pl
from jax.experimental.pallas import tpu as pltpu

F32 = jnp.float32
BF16 = jnp.bfloat16

HEAD_DIM = 64
CHUNK = 64
N_HEADS_ATTN = 8
N_IDX_HEADS = 4
IDX_DIM = 64
TOPK_MAX = 256
N_HEADS_RET = 4
N_HEADS_MLSTM = 4
CONV_WIDTH = 4
ROPE_THETA = 10000.0
N_EXPERTS = 16
N_GROUPS = 4
EXPERTS_PER_GROUP = N_EXPERTS // N_GROUPS
D_FF_EXPERT = 256
LN_EPS = 1e-5

MIX_ATTN = N_HEADS_ATTN * HEAD_DIM
MIX_RET = N_HEADS_RET * HEAD_DIM
MIX_MLSTM = N_HEADS_MLSTM * HEAD_DIM

LANES = 128
SUBLANES = 8
VMEM_LIMIT = 56 * 1024 * 1024
SMALL_VMEM_LIMIT = 32 * 1024 * 1024

IQ_WIDTH = N_IDX_HEADS * IDX_DIM
R_WIDTH = 4 * MIX_RET
M_WIDTH = 4 * MIX_MLSTM
S_WIDTH = LANES
S_IW = 0
S_MI = S_IW + N_IDX_HEADS
S_MF = S_MI + N_HEADS_MLSTM
S_ROWS = 16
V_ROWS = HEAD_DIM + 16
OFF_Q = 0
OFF_K = OFF_Q + MIX_ATTN
OFF_IQ = OFF_K + MIX_ATTN
OFF_R = OFF_IQ + IQ_WIDTH
OFF_M = OFF_R + R_WIDTH
OFF_IK = OFF_M + M_WIDTH
OFF_S = OFF_IK + LANES
OFF_V = OFF_S + S_WIDTH
W_TOTAL = OFF_V + MIX_ATTN

INT_MIN = -2 ** 31
NEG_BIG = -1e30
LOG2_E = 1.4426950408889634


def _dot(a, b):
    return jnp.dot(a, b, preferred_element_type=F32)


def _dot_nt(a, b):
    return lax.dot_general(a, b, (((1,), (1,)), ((), ())), preferred_element_type=F32)


def _dot_tn(a, b):
    return lax.dot_general(a, b, (((0,), (0,)), ((), ())), preferred_element_type=F32)


def _params(sem, vmem=VMEM_LIMIT):
    return pltpu.CompilerParams(dimension_semantics=sem, vmem_limit_bytes=vmem)


def _ada_kernel(c_ref, w_ref, b_ref, o_ref):
    c = c_ref[...]
    c_act = c * jax.nn.sigmoid(c)
    o_ref[0] = _dot(c_act, w_ref[0]) + b_ref[0]


def _ada_mod(c_pad, w_ada, b_ada, tn):
    depth, d, n = w_ada.shape
    rows = c_pad.shape[0]
    return pl.pallas_call(
        _ada_kernel,
        grid=(depth, n // tn),
        in_specs=[pl.BlockSpec((rows, d), lambda l, j: (0, 0)),
                  pl.BlockSpec((1, d, tn), lambda l, j: (l, 0, j)),
                  pl.BlockSpec((1, 1, tn), lambda l, j: (l, 0, j))],
        out_specs=pl.BlockSpec((1, rows, tn), lambda l, j: (l, 0, j)),
        out_shape=jax.ShapeDtypeStruct((depth, rows, n), F32),
        compiler_params=_params(("parallel", "parallel")),
        name="ada_mod",
    )(c_pad, w_ada, b_ada.reshape(depth, 1, n))


def _rope(y, cos, sin):
    w = y.shape[1]
    reps = w // LANES
    cosw = jnp.concatenate([cos] * reps, axis=1) if reps > 1 else cos
    sinw = jnp.concatenate([sin] * reps, axis=1) if reps > 1 else sin
    lane = lax.broadcasted_iota(jnp.int32, y.shape, 1)
    first = (lane % HEAD_DIM) < (HEAD_DIM // 2)
    partner = jnp.where(first, pltpu.roll(y, w - HEAD_DIM // 2, 1), pltpu.roll(y, HEAD_DIM // 2, 1))
    return y * cosw + partner * sinw


def _inproj_kernel(x_ref, sc_ref, sh_ref, w_ref, cos_ref, sin_ref,
                   q_ref, k_ref, iq_ref, ik_ref, r_ref, m_ref, s_ref, st_ref, vt_ref, *, kb):
    h = (x_ref[0] * (1.0 + sc_ref[0]) + sh_ref[0]).astype(BF16)
    cos = cos_ref[0]
    sin = sin_ref[0]

    def proj(start, width):
        return _dot(h, w_ref[:, start:start + width])

    q_ref[0] = (_rope(proj(OFF_Q, MIX_ATTN), cos, sin) * (HEAD_DIM ** -0.5 * LOG2_E)).astype(BF16)
    k_ref[0] = _rope(proj(OFF_K, MIX_ATTN), cos, sin).astype(BF16)
    iq_ref[0] = _rope(proj(OFF_IQ, IQ_WIDTH), cos, sin).astype(BF16)
    ik_ref[0] = _rope(proj(OFF_IK, LANES), cos, sin).astype(BF16)
    r_ref[0, :, 0:2 * MIX_RET] = _rope(proj(OFF_R, 2 * MIX_RET), cos, sin).astype(BF16)
    r_ref[0, :, 2 * MIX_RET:R_WIDTH] = proj(OFF_R + 2 * MIX_RET, 2 * MIX_RET).astype(BF16)
    m_ref[0, :, 0:2 * MIX_MLSTM] = proj(OFF_M, 2 * MIX_MLSTM).astype(BF16)
    m_ref[0, :, 2 * MIX_MLSTM:M_WIDTH] = proj(OFF_M + 2 * MIX_MLSTM, 2 * MIX_MLSTM).astype(BF16)
    y = proj(OFF_S, S_WIDTH)
    s_ref[0] = y
    st_ref[0] = y.T[0:S_ROWS, :]
    tm = h.shape[0]
    yvt = proj(OFF_V, MIX_ATTN).T.astype(BF16)
    pad_rows = lax.broadcasted_iota(jnp.int32, (V_ROWS - HEAD_DIM, kb), 0)
    ones_rows = jnp.where(pad_rows == 0, 1.0, 0.0).astype(BF16)
    for j in range(tm // kb):
        for hh in range(N_HEADS_ATTN):
            vt_ref[0, j, hh * V_ROWS:hh * V_ROWS + HEAD_DIM, :] = yvt[hh * HEAD_DIM:(hh + 1) * HEAD_DIM,
                                                                      j * kb:(j + 1) * kb]
            vt_ref[0, j, hh * V_ROWS + HEAD_DIM:(hh + 1) * V_ROWS, :] = ones_rows


def _in_proj(x, sc, sh, w, layer, cos_t, sin_t, tm, kb):
    bsz, seq, d = x.shape
    row = lambda b, i: (b, i, 0)
    per_b = lambda b, i: (b, 0, 0)
    widths = (MIX_ATTN, MIX_ATTN, IQ_WIDTH, LANES, R_WIDTH, M_WIDTH)
    return pl.pallas_call(
        functools.partial(_inproj_kernel, kb=kb),
        grid=(bsz, seq // tm),
        in_specs=[pl.BlockSpec((1, tm, d), row),
                  pl.BlockSpec((1, 1, d), per_b),
                  pl.BlockSpec((1, 1, d), per_b),
                  pl.BlockSpec((None, d, W_TOTAL), lambda b, i: (layer, 0, 0)),
                  pl.BlockSpec((1, tm, LANES), row),
                  pl.BlockSpec((1, tm, LANES), row)],
        out_specs=[pl.BlockSpec((1, tm, wd), row) for wd in widths]
                  + [pl.BlockSpec((1, tm, S_WIDTH), row),
                     pl.BlockSpec((1, S_ROWS, tm), lambda b, i: (b, 0, i)),
                     pl.BlockSpec((1, tm // kb, N_HEADS_ATTN * V_ROWS, kb), lambda b, i: (b, i, 0, 0))],
        out_shape=[jax.ShapeDtypeStruct((bsz, seq, wd), BF16) for wd in widths]
                  + [jax.ShapeDtypeStruct((bsz, seq, S_WIDTH), F32),
                     jax.ShapeDtypeStruct((bsz, S_ROWS, seq), F32),
                     jax.ShapeDtypeStruct((bsz, seq // kb, N_HEADS_ATTN * V_ROWS, kb), BF16)],
        compiler_params=_params(("parallel", "parallel")),
        name="in_proj",
    )(x, sc, sh, w, cos_t, sin_t)


def _dsa_kernel(q_ref, k_ref, iq_ref, ik_ref, st_ref, vt_ref, tril_ref, o_ref,
                key_scr, byte_scr, cand_scr, bias_scr, s_scr, p_scr, m_scr, alpha_scr, acc_scr,
                *, tq, kb, topk):
    q0 = pl.program_id(1) * tq
    n_blocks = (q0 + tq + kb - 1) // kb
    qpos = q0 + lax.broadcasted_iota(jnp.int32, (1, tq), 1)
    q_limit = (qpos // CHUNK + 1) * CHUNK
    krow = lax.broadcasted_iota(jnp.int32, (kb, 1), 0)

    def head_of_pair(x, h):
        pair = x[:, (h // 2) * LANES:(h // 2 + 1) * LANES]
        lane = lax.broadcasted_iota(jnp.int32, pair.shape, 1)
        keep = (lane < HEAD_DIM) if h % 2 == 0 else (lane >= HEAD_DIM)
        return jnp.where(keep, pair, jnp.zeros_like(pair))

    iw = st_ref[0][S_IW:S_IW + N_IDX_HEADS, :] * (N_IDX_HEADS ** -0.5 * IDX_DIM ** -0.5)
    iq = iq_ref[0]
    iq_heads = [head_of_pair(iq, h) for h in range(N_IDX_HEADS)]

    def score_block(c):
        k0 = pl.multiple_of(c * kb, kb)
        ik2 = ik_ref[0, pl.ds(k0, kb), :]
        score = jnp.zeros((kb, tq), F32)
        for h in range(N_IDX_HEADS):
            score = score + jnp.maximum(_dot_nt(ik2, iq_heads[h]), 0.0) * iw[h:h + 1, :]
        bits = pltpu.bitcast(score, jnp.int32)
        key = jnp.where(bits >= 0, bits, bits ^ jnp.int32(0x7FFFFFFF))
        key = jnp.where(k0 + krow < q_limit, key, jnp.int32(INT_MIN))
        key_scr[c] = key
        byte_scr[0, c] = ((key >> 24) + 128).astype(F32).astype(BF16)
        for lvl in range(1, 4):
            byte_scr[lvl, c] = ((key >> (24 - 8 * lvl)) & 255).astype(F32).astype(BF16)

    def score_pair(c2, carry):
        score_block(2 * c2)
        score_block(2 * c2 + 1)
        return carry

    lax.fori_loop(0, n_blocks // 2, score_pair, 0)

    @pl.when(n_blocks % 2 == 1)
    def _():
        score_block(n_blocks - 1)

    pack = 16
    one = jnp.ones((kb, tq), BF16)
    zero = jnp.zeros((kb, tq), BF16)

    def count_ge(lvl, cand):
        cand_b = cand.astype(BF16)

        def hits(c):
            plane = byte_scr[0, c] if lvl == 0 else cand_scr[c]
            hit = jnp.where(plane >= cand_b, one, zero)
            parts = [hit[i * pack:(i + 1) * pack, :] for i in range(kb // pack)]
            while len(parts) > 1:
                parts = [parts[i] + parts[i + 1] for i in range(0, len(parts), 2)]
            return parts[0]

        acc = lax.fori_loop(0, n_blocks // 2, lambda c2, a: a + hits(2 * c2) + hits(2 * c2 + 1),
                            jnp.zeros((pack, tq), BF16))
        acc = lax.cond(n_blocks % 2 == 1, lambda a: a + hits(n_blocks - 1), lambda a: a, acc)
        return jnp.sum(acc.astype(F32), axis=0, keepdims=True)

    above = jnp.zeros((1, tq), F32)
    t = jnp.zeros((1, tq), jnp.int32)
    for lvl in range(4):
        def bit_body(i, carry, lvl=lvl, above=above):
            v, rejected = carry
            cand = v + lax.shift_left(jnp.int32(1), 7 - i).astype(F32)
            cnt = count_ge(lvl, cand)
            ok = above + cnt >= topk
            return jnp.where(ok, cand, v), jnp.where(ok, rejected, cnt)

        v, rejected = lax.fori_loop(0, 8, bit_body, (jnp.zeros((1, tq), F32), jnp.zeros((1, tq), F32)))
        above = above + rejected
        t = t | lax.shift_left(v.astype(jnp.int32), 24 - 8 * lvl)
        if lvl < 3:
            v_b = v.astype(BF16)

            def narrow(c, lvl=lvl, v_b=v_b):
                plane = byte_scr[0, c] if lvl == 0 else cand_scr[c]
                cand_scr[c] = jnp.where(plane == v_b, byte_scr[lvl + 1, c], -one)

            def narrow_pair(c2, carry, narrow=narrow):
                narrow(2 * c2)
                narrow(2 * c2 + 1)
                return carry

            lax.fori_loop(0, n_blocks // 2, narrow_pair, 0)

            @pl.when(n_blocks % 2 == 1)
            def _(narrow=narrow):
                narrow(n_blocks - 1)
    thr = jnp.maximum(t ^ jnp.int32(INT_MIN), jnp.int32(INT_MIN + 1))
    need = topk - above

    q = q_ref[0]
    q_heads = [head_of_pair(q, h) for h in range(N_HEADS_ATTN)]
    m_scr[...] = jnp.full(m_scr.shape, NEG_BIG, F32)
    alpha_scr[...] = jnp.ones(alpha_scr.shape, F32)
    acc_scr[...] = jnp.zeros(acc_scr.shape, F32)
    p_scr[...] = jnp.zeros(p_scr.shape, BF16)

    def stage_logits(c, h):
        k0 = pl.multiple_of(c * kb, kb)
        kp = k_ref[0, pl.ds(k0, kb), (h // 2) * LANES:(h // 2 + 1) * LANES]
        s_scr[h] = _dot_nt(kp, q_heads[h])

    def stage_mask(c, ties_before):
        key = key_scr[c]
        tie = key == thr
        rank = _dot(tril_ref[...], jnp.where(tie, 1.0, 0.0).astype(BF16)) + ties_before
        sel = (key > thr) | (tie & (rank <= need))
        bias_scr[...] = jnp.where(sel, 0.0, NEG_BIG)
        return rank[kb - 1:kb, :]

    def stage_softmax(h):
        for half in range(tq // LANES):
            ln = slice(half * LANES, (half + 1) * LANES)
            s = s_scr[h, :, ln] + bias_scr[:, ln]
            m_old = m_scr[h, :, ln]
            m_new = jnp.maximum(m_old, jnp.max(s, axis=0, keepdims=True))
            p_scr[h, :, ln] = jnp.exp2(s - m_new).astype(BF16)
            alpha_scr[h, :, ln] = jnp.exp2(m_old - m_new)
            m_scr[h, :, ln] = m_new

    def stage_values(c, h):
        vt = vt_ref[0, c, h * V_ROWS:(h + 1) * V_ROWS, :]
        acc_scr[h, 0:V_ROWS, :] = alpha_scr[h] * acc_scr[h, 0:V_ROWS, :] + _dot(vt, p_scr[h])

    ties0 = stage_mask(0, jnp.zeros((1, tq), F32))
    for h in range(N_HEADS_ATTN):
        stage_logits(0, h)

    def attn_body(j, ties_before):
        c_old = jnp.maximum(j - 2, 0)
        for h in range(N_HEADS_ATTN):
            stage_values(c_old, h)
            stage_softmax(h)
            stage_logits(j, h)
        return stage_mask(j, ties_before)

    lax.fori_loop(1, n_blocks, attn_body, ties0)
    for h in range(N_HEADS_ATTN):
        stage_values(jnp.maximum(n_blocks - 2, 0), h)
        stage_softmax(h)
    for h in range(N_HEADS_ATTN):
        stage_values(n_blocks - 1, h)
    for h in range(N_HEADS_ATTN):
        acc = acc_scr[h]
        out = acc * (1.0 / acc[HEAD_DIM:HEAD_DIM + 1, :])
        o_ref[0, :, h * HEAD_DIM:(h + 1) * HEAD_DIM] = out.T[:, 0:HEAD_DIM].astype(o_ref.dtype)


def _dsa(q, k, iq, ik2, small_t, v_t, tril, tq, kb):
    bsz, seq, _ = q.shape
    topk = min(TOPK_MAX, seq // 4)
    kern = functools.partial(_dsa_kernel, tq=tq, kb=kb, topk=topk)
    return pl.pallas_call(
        kern,
        grid=(bsz, seq // tq),
        in_specs=[pl.BlockSpec((1, tq, MIX_ATTN), lambda b, i: (b, i, 0)),
                  pl.BlockSpec((1, seq, MIX_ATTN), lambda b, i: (b, 0, 0)),
                  pl.BlockSpec((1, tq, IQ_WIDTH), lambda b, i: (b, i, 0)),
                  pl.BlockSpec((1, seq, LANES), lambda b, i: (b, 0, 0)),
                  pl.BlockSpec((1, S_ROWS, tq), lambda b, i: (b, 0, i)),
                  pl.BlockSpec((1, seq // kb, N_HEADS_ATTN * V_ROWS, kb), lambda b, i: (b, 0, 0, 0)),
                  pl.BlockSpec((kb, kb), lambda b, i: (0, 0))],
        out_specs=pl.BlockSpec((1, tq, MIX_ATTN), lambda b, i: (b, i, 0)),
        out_shape=jax.ShapeDtypeStruct((bsz, seq, MIX_ATTN), BF16),
        scratch_shapes=[pltpu.VMEM((seq // kb, kb, tq), jnp.int32),
                        pltpu.VMEM((4, seq // kb, kb, tq), BF16),
                        pltpu.VMEM((seq // kb, kb, tq), BF16),
                        pltpu.VMEM((kb, tq), F32),
                        pltpu.VMEM((N_HEADS_ATTN, kb, tq), F32),
                        pltpu.VMEM((N_HEADS_ATTN, kb, tq), BF16),
                        pltpu.VMEM((N_HEADS_ATTN, 1, tq), F32),
                        pltpu.VMEM((N_HEADS_ATTN, 1, tq), F32),
                        pltpu.VMEM((N_HEADS_ATTN, LANES, tq), F32)],
        compiler_params=_params(("parallel", "arbitrary")),
        name="dsa",
    )(q, k, iq, ik2, small_t, v_t, tril)


def _head_norm(y):
    mean_mat = jnp.full((HEAD_DIM, HEAD_DIM), 1.0 / HEAD_DIM, BF16)

    def mean_bcast(x):
        hi = x.astype(BF16)
        lo = (x - hi.astype(F32)).astype(BF16)
        return _dot(hi, mean_mat) + _dot(lo, mean_mat)

    yc = y - mean_bcast(y)
    return yc * lax.rsqrt(mean_bcast(yc * yc) + LN_EPS)


def _ret_kernel(r_ref, o_ref, state_scr, *, cr, grp):
    @pl.when(pl.program_id(1) == 0)
    def _():
        state_scr[...] = jnp.zeros_like(state_scr)

    ri = lax.broadcasted_iota(jnp.int32, (cr, cr), 0)
    ci = lax.broadcasted_iota(jnp.int32, (cr, cr), 1)
    diff = (ri - ci).astype(F32)
    pos = lax.broadcasted_iota(jnp.int32, (cr, 1), 0).astype(F32)
    items = [(h, g) for h in range(N_HEADS_RET) for g in range(grp)]
    sl = lambda part, h: slice(part * MIX_RET + h * HEAD_DIM, part * MIX_RET + (h + 1) * HEAD_DIM)
    log_gamma = [jnp.log1p(jnp.full((1, 1), -(2.0 ** (-5.0 - h)), F32)) for h in range(N_HEADS_RET)]
    decay_in = [jnp.where(diff >= 0, jnp.exp(diff * lg), 0.0) * (HEAD_DIM ** -0.5) for lg in log_gamma]
    q = {(h, g): r_ref[g, :, sl(0, h)] for h, g in items}
    k = {(h, g): r_ref[g, :, sl(1, h)] for h, g in items}
    v = {(h, g): r_ref[g, :, sl(2, h)] for h, g in items}
    state = {(h, g): state_scr[g, h] for h, g in items}
    scores = {it: _dot_nt(q[it], k[it]) * decay_in[it[0]] for it in items}
    cross = {it: jnp.exp((pos + 1.0) * log_gamma[it[0]]) * _dot(q[it], state[it].astype(BF16)) for it in items}
    inner = {it: _dot(scores[it].astype(BF16), v[it]) for it in items}
    for it in items:
        h, g = it
        gate = r_ref[g, :, sl(3, h)].astype(F32)
        y = _head_norm(inner[it] + cross[it])
        o_ref[g, :, h * HEAD_DIM:(h + 1) * HEAD_DIM] = (y * (gate * jax.nn.sigmoid(gate))).astype(o_ref.dtype)
    for it in items:
        h, g = it
        k_decay = (HEAD_DIM ** -0.5) * jnp.exp((cr - 1.0 - pos) * log_gamma[h])
        k_dec = (k[it].astype(F32) * k_decay).astype(BF16)
        state_scr[g, h] = state[it] * jnp.exp(cr * log_gamma[h]) + _dot_tn(k_dec, v[it])


def _retention(r_proj, cr, grp):
    bsz, seq, _ = r_proj.shape
    return pl.pallas_call(
        functools.partial(_ret_kernel, cr=cr, grp=grp),
        grid=(bsz // grp, seq // cr),
        in_specs=[pl.BlockSpec((grp, cr, R_WIDTH), lambda b, i: (b, i, 0))],
        out_specs=pl.BlockSpec((grp, cr, MIX_RET), lambda b, i: (b, i, 0)),
        out_shape=jax.ShapeDtypeStruct((bsz, seq, MIX_RET), BF16),
        scratch_shapes=[pltpu.VMEM((grp, N_HEADS_RET, HEAD_DIM, HEAD_DIM), F32)],
        compiler_params=_params(("parallel", "arbitrary"), SMALL_VMEM_LIMIT),
        name="retention",
    )(r_proj)


def _mlstm_kernel(m_ref, sm_ref, bias_ref, cw_ref, cb_ref, tril_ref, o_ref,
                  xbuf, a_scr, m_scr, *, cm, grp):
    halo = 8
    gate_shift = S_MF - S_MI

    @pl.when(pl.program_id(1) == 0)
    def _():
        xbuf[:, 0:halo, :] = jnp.zeros((grp, halo, 2 * MIX_MLSTM), F32)
        a_scr[...] = jnp.zeros_like(a_scr)
        m_scr[...] = jnp.zeros_like(m_scr)

    ri = lax.broadcasted_iota(jnp.int32, (cm, cm), 0)
    ci = lax.broadcasted_iota(jnp.int32, (cm, cm), 1)
    causal = ri >= ci
    row = lax.broadcasted_iota(jnp.int32, (cm, LANES), 0)
    lane = lax.broadcasted_iota(jnp.int32, (cm, LANES), 1)
    gate_lanes = (lane >= S_MI) & (lane < S_MI + N_HEADS_MLSTM)
    lane64 = lax.broadcasted_iota(jnp.int32, (cm, HEAD_DIM), 1)
    ones_col = jnp.where(lane64 == 0, 1.0, 0.0).astype(BF16)
    scale = HEAD_DIM ** -0.5

    def head_cols(x, h):
        return x[:, h * HEAD_DIM:(h + 1) * HEAD_DIM]

    items = [(g, h) for g in range(grp) for h in range(N_HEADS_MLSTM)]
    mm, qk, u_row, m_col, inter, e_inv, kw_col, decay = {}, {}, {}, {}, {}, {}, {}, {}
    for g in range(grp):
        mm[g] = m_ref[g]
        xbuf[g, halo:halo + cm, :] = mm[g][:, 0:2 * MIX_MLSTM].astype(F32)
        conv = cb_ref[...]
        for j in range(CONV_WIDTH):
            off = halo - (CONV_WIDTH - 1) + j
            conv = conv + xbuf[g, off:off + cm, :] * cw_ref[j:j + 1, :]
        xbuf[g, 0:halo, :] = xbuf[g, cm:cm + halo, :]
        qk[g] = conv * jax.nn.sigmoid(conv)

        gates = sm_ref[g] + bias_ref[...]
        b_all = jnp.dot(tril_ref[...], jax.nn.log_sigmoid(gates), preferred_element_type=F32,
                        precision=lax.Precision.HIGHEST)
        b_i = jnp.where(gate_lanes, pltpu.roll(b_all, LANES - gate_shift, 1), 0.0)
        u = jnp.where(gate_lanes, gates, 0.0) - b_i
        run = u
        step = 1
        while step < cm:
            run = jnp.maximum(run, jnp.where(row >= step, pltpu.roll(run, step, 0), -jnp.inf))
            step *= 2
        m_prev = m_scr[g]
        m_c = jnp.maximum(m_prev, run)
        m_last = m_c[cm - 1:cm, :]
        u_row[g] = u.T
        m_col[g] = m_c
        inter[g] = jnp.exp(m_prev - m_c)
        e_inv[g] = jnp.exp(-(b_i + m_c))
        kw_col[g] = scale * jnp.exp(u - m_last)
        decay[g] = jnp.exp(m_prev - m_last)
        m_scr[g] = b_i[cm - 1:cm, :] + m_last

    q, k, v_aug, a_mem = {}, {}, {}, {}
    for it in items:
        g, h = it
        q[it] = head_cols(qk[g], h).astype(BF16)
        k[it] = head_cols(qk[g][:, MIX_MLSTM:], h)
        v_aug[it] = jnp.concatenate([head_cols(mm[g][:, 2 * MIX_MLSTM:], h), ones_col], axis=1)
        a_mem[it] = a_scr[g, h]

    def col(x, h):
        return x[:, S_MI + h:S_MI + h + 1]

    w = {}
    for it in items:
        g, h = it
        u_r = u_row[g][S_MI + h:S_MI + h + 1, :]
        w[it] = jnp.exp(jnp.where(causal, u_r - col(m_col[g], h), -jnp.inf))

    s = {it: _dot_nt(q[it], k[it].astype(BF16)) * scale * w[it] for it in items}
    cross = {it: _dot(q[it], a_mem[it].astype(BF16)) for it in items}
    both = {it: _dot(s[it].astype(BF16), v_aug[it]) + col(inter[it[0]], it[1]) * cross[it] for it in items}
    for it in items:
        g, h = it
        den = both[it][:, HEAD_DIM:HEAD_DIM + 1]
        h_tilde = both[it][:, 0:HEAD_DIM] * (1.0 / jnp.maximum(jnp.abs(den), col(e_inv[g], h)))
        og = head_cols(mm[g][:, 3 * MIX_MLSTM:], h).astype(F32)
        o_ref[g, :, h * HEAD_DIM:(h + 1) * HEAD_DIM] = _head_norm(jax.nn.sigmoid(og) * h_tilde).astype(o_ref.dtype)

    for it in items:
        g, h = it
        kw = k[it] * col(kw_col[g], h)
        a_scr[g, h] = col(decay[g], h) * a_mem[it] + _dot_tn(kw.astype(BF16), v_aug[it])


def _mlstm(m_proj, small, gate_bias, conv_w, conv_b, tril, cm, grp):
    bsz, seq, _ = m_proj.shape
    const = lambda b, i: (0, 0)
    return pl.pallas_call(
        functools.partial(_mlstm_kernel, cm=cm, grp=grp),
        grid=(bsz // grp, seq // cm),
        in_specs=[pl.BlockSpec((grp, cm, M_WIDTH), lambda b, i: (b, i, 0)),
                  pl.BlockSpec((grp, cm, S_WIDTH), lambda b, i: (b, i, 0)),
                  pl.BlockSpec((1, S_WIDTH), const),
                  pl.BlockSpec((CONV_WIDTH, 2 * MIX_MLSTM), const),
                  pl.BlockSpec((1, 2 * MIX_MLSTM), const),
                  pl.BlockSpec((cm, cm), const)],
        out_specs=pl.BlockSpec((grp, cm, MIX_MLSTM), lambda b, i: (b, i, 0)),
        out_shape=jax.ShapeDtypeStruct((bsz, seq, MIX_MLSTM), BF16),
        scratch_shapes=[pltpu.VMEM((grp, cm + 8, 2 * MIX_MLSTM), F32),
                        pltpu.VMEM((grp, N_HEADS_MLSTM, HEAD_DIM, LANES), F32),
                        pltpu.VMEM((grp, 1, LANES), F32)],
        compiler_params=_params(("parallel", "arbitrary"), SMALL_VMEM_LIMIT),
        name="mlstm",
    )(m_proj, small, gate_bias, conv_w, conv_b, tril)


def _layer_norm(z, g, b):
    mu = jnp.mean(z, axis=-1, keepdims=True)
    var = jnp.mean(jnp.square(z - mu), axis=-1, keepdims=True)
    return (z - mu) * lax.rsqrt(var + LN_EPS) * g + b


def _route(scores, biased):
    col = lambda a, e: a[e:e + 1, :]
    epg = EXPERTS_PER_GROUP
    group_scores = []
    for g in range(N_GROUPS):
        vals = [col(biased, g * epg + j) for j in range(epg)]
        best = None
        for a in range(epg):
            for b in range(a + 1, epg):
                pair = vals[a] + vals[b]
                best = pair if best is None else jnp.maximum(best, pair)
        group_scores.append(best)
    best_g = jnp.zeros_like(group_scores[0], dtype=jnp.int32)
    best_v = group_scores[0]
    for g in range(1, N_GROUPS):
        better = group_scores[g] > best_v
        best_g = jnp.where(better, g, best_g)
        best_v = jnp.where(better, group_scores[g], best_v)
    cand_b = [sum(jnp.where(best_g == g, col(biased, g * epg + j), 0.0) for g in range(N_GROUPS))
              for j in range(epg)]
    cand_s = [sum(jnp.where(best_g == g, col(scores, g * epg + j), 0.0) for g in range(N_GROUPS))
              for j in range(epg)]

    def argmax_first(vals, skip=None):
        idx = None
        val = None
        for j, vj in enumerate(vals):
            if skip is not None:
                vj = jnp.where(skip == j, -jnp.inf, vj)
            if idx is None:
                idx, val = jnp.zeros_like(best_g), vj
            else:
                better = vj > val
                idx = jnp.where(better, j, idx)
                val = jnp.where(better, vj, val)
        return idx

    first = argmax_first(cand_b)
    second = argmax_first(cand_b, skip=first)
    w1 = sum(jnp.where(first == j, cand_s[j], 0.0) for j in range(epg))
    w2 = sum(jnp.where(second == j, cand_s[j], 0.0) for j in range(epg))
    total = w1 + w2
    e1 = best_g * epg + first
    e2 = best_g * epg + second
    expert = lax.broadcasted_iota(jnp.int32, scores.shape, 0)
    gate = jnp.where(expert == e1, w1 / total, 0.0) + jnp.where(expert == e2, w2 / total, 0.0)
    return gate, best_g


def _split_bf16(x):
    hi = x.astype(BF16)
    return hi, (x - hi.astype(F32)).astype(BF16)


def _moe_kernel(oa_ref, ob_ref, oc_ref, wo_ref, x_ref, gm_ref, gmix_ref, bmix_ref,
                sc_ref, sh_ref, gf_ref, wr_ref, br_ref, wg_ref, wu_ref, wd_ref, tri_ref, g_ref, b_ref,
                o_ref, hid_scr, *, alpha, cap):
    mix = _dot(oa_ref[0], wo_ref[0:MIX_ATTN, :])
    mix = mix + _dot(ob_ref[0], wo_ref[MIX_ATTN:MIX_ATTN + MIX_RET, :])
    mix = mix + _dot(oc_ref[0], wo_ref[MIX_ATTN + MIX_RET:, :])
    x = _layer_norm(alpha * x_ref[0] + (1.0 + gm_ref[0]) * mix, gmix_ref[...], bmix_ref[...])
    tm = x.shape[0]
    h = x * (1.0 + sc_ref[0]) + sh_ref[0]
    hb = h.astype(BF16)
    n_exp, _, d_ff = wg_ref.shape
    scores_t = jax.nn.sigmoid(_dot(h, wr_ref[...]).T[0:n_exp, :])
    gate_t, best_g = _route(scores_t, scores_t + br_ref[...])
    sub = lax.broadcasted_iota(jnp.int32, (2 * SUBLANES, tm), 0)
    member_t = jnp.where(sub == best_g, 1.0, 0.0)
    before_t = _dot(member_t.astype(BF16), tri_ref[...])
    rank = jnp.sum(member_t * before_t, axis=0, keepdims=True)
    count = jnp.max(jnp.sum(member_t, axis=1, keepdims=True))
    grp_f = best_g.astype(F32)
    sub8 = lax.broadcasted_iota(jnp.int32, (SUBLANES, tm), 0)
    extra = jnp.where(sub8 == 0, grp_f, jnp.where(sub8 == 1, rank, 0.0))
    info = jnp.concatenate([gate_t, extra, jnp.zeros((LANES - n_exp - SUBLANES, tm), F32)], axis=0).T
    gate = info
    epg = EXPERTS_PER_GROUP

    def expert_hidden(rows_b, gate_rows, e, dst):
        gate_pre = _dot(rows_b, wg_ref[e])
        up = _dot(rows_b, wu_ref[e])
        hid = gate_pre * jax.nn.sigmoid(gate_pre) * up * gate_rows[:, e:e + 1]
        hid_scr[0:rows_b.shape[0], dst * d_ff:(dst + 1) * d_ff] = hid.astype(BF16)

    def dense(_):
        for e in range(n_exp):
            expert_hidden(hb, gate, e, e)
        return _dot(hid_scr[...], wd_ref[...])


    def grouped(_):
        grp_c = info[:, n_exp:n_exp + 1]
        rank_c = info[:, n_exp + 1:n_exp + 2]
        slot_c = lax.broadcasted_iota(jnp.int32, (cap, 1), 0).astype(F32)
        slot_r = lax.broadcasted_iota(jnp.int32, (1, cap), 1).astype(F32)
        gate_hi, gate_lo = _split_bf16(gate)
        y = jnp.zeros((tm, x.shape[1]), F32)
        for g in range(N_GROUPS):
            take = jnp.where(jnp.where(grp_f == g, rank, -1.0) == slot_c, 1.0, 0.0).astype(BF16)
            give = jnp.where(jnp.where(grp_c == g, rank_c, -1.0) == slot_r, 1.0, 0.0).astype(BF16)
            rows_b = _dot(take, hb).astype(BF16)
            gate_rows = _dot(take, gate_hi) + _dot(take, gate_lo)
            for j in range(epg):
                expert_hidden(rows_b, gate_rows, g * epg + j, j)
            out = _dot(hid_scr[0:cap, 0:epg * d_ff], wd_ref[g * epg * d_ff:(g + 1) * epg * d_ff, :])
            y = y + _dot(give, out.astype(BF16))
        return y

    y = lax.cond(count <= cap, grouped, dense, 0)
    z = alpha * x + (1.0 + gf_ref[0]) * y
    o_ref[0] = _layer_norm(z, g_ref[...], b_ref[...])


def _moe(o_a, o_b, o_c, w_out, x, g_m, ln_mix_g, ln_mix_b,
         sc, sh, g_f, w_router, b_router, w_gate, w_up, w_down, layer, tri, ln_g, ln_b, tm, cap, alpha):
    bsz, seq, d = x.shape
    _, n_exp, _, d_ff = w_gate.shape
    row = lambda b, i: (b, i, 0)
    per_b = lambda b, i: (b, 0, 0)
    const = lambda b, i: (0, 0)
    of_layer4 = lambda b, i: (layer, 0, 0, 0)
    of_layer3 = lambda b, i: (layer, 0, 0)
    resident = pl.Buffered(1)
    return pl.pallas_call(
        functools.partial(_moe_kernel, alpha=alpha, cap=cap),
        grid=(bsz, seq // tm),
        in_specs=[pl.BlockSpec((1, tm, MIX_ATTN), row),
                  pl.BlockSpec((1, tm, MIX_RET), row),
                  pl.BlockSpec((1, tm, MIX_MLSTM), row),
                  pl.BlockSpec((None,) + w_out.shape[1:], of_layer3, pipeline_mode=resident),
                  pl.BlockSpec((1, tm, d), row),
                  pl.BlockSpec((1, 1, d), per_b),
                  pl.BlockSpec((1, d), const),
                  pl.BlockSpec((1, d), const),
                  pl.BlockSpec((1, 1, d), per_b),
                  pl.BlockSpec((1, 1, d), per_b),
                  pl.BlockSpec((1, 1, d), per_b),
                  pl.BlockSpec((d, LANES), const),
                  pl.BlockSpec((n_exp, 1), const),
                  pl.BlockSpec((None, n_exp, d, d_ff), of_layer4, pipeline_mode=resident),
                  pl.BlockSpec((None, n_exp, d, d_ff), of_layer4, pipeline_mode=resident),
                  pl.BlockSpec((None, n_exp * d_ff, d), of_layer3, pipeline_mode=resident),
                  pl.BlockSpec((tm, tm), const, pipeline_mode=resident),
                  pl.BlockSpec((1, d), const),
                  pl.BlockSpec((1, d), const)],
        out_specs=pl.BlockSpec((1, tm, d), row),
        out_shape=jax.ShapeDtypeStruct((bsz, seq, d), F32),
        scratch_shapes=[pltpu.VMEM((tm, n_exp * d_ff), BF16)],
        compiler_params=_params(("parallel", "parallel")),
        name="moe",
    )(o_a, o_b, o_c, w_out, x, g_m, ln_mix_g, ln_mix_b,
      sc, sh, g_f, w_router, b_router, w_gate, w_up, w_down, tri, ln_g, ln_b)


def _pick(n, pref):
    t = min(pref, n)
    while n % t:
        t //= 2
    return t


def _rope_tables(positions):
    half = HEAD_DIM // 2
    inv_freq = ROPE_THETA ** (-jnp.arange(half, dtype=F32) / half)
    ang = positions.astype(F32)[..., None] * inv_freq
    cos, sin = jnp.cos(ang), jnp.sin(ang)
    reps = LANES // HEAD_DIM
    return (jnp.concatenate([cos, cos] * reps, axis=-1),
            jnp.concatenate([-sin, sin] * reps, axis=-1))


def _prep_w_kernel(w_ref, o_ref):
    o_v = 2 * MIX_ATTN
    o_iq = o_v + MIX_ATTN
    o_ik = o_iq + IQ_WIDTH
    o_iw = o_ik + IDX_DIM
    o_r = o_iw + N_IDX_HEADS
    o_g = o_r + R_WIDTH + M_WIDTH
    n_gate = 2 * N_HEADS_MLSTM

    def put(dst, src, width):
        o_ref[0, :, dst:dst + width] = w_ref[0, :, src:src + width].astype(o_ref.dtype)

    o_ref[0] = jnp.zeros(o_ref.shape[1:], o_ref.dtype)
    put(OFF_Q, 0, 2 * MIX_ATTN)
    put(OFF_IQ, o_iq, IQ_WIDTH)
    put(OFF_R, o_r, R_WIDTH + M_WIDTH)
    put(OFF_IK, o_ik, IDX_DIM)
    put(OFF_IK + IDX_DIM, o_ik, IDX_DIM)
    put(OFF_S + S_IW, o_iw, N_IDX_HEADS)
    put(OFF_S + S_MI, o_g, n_gate)
    put(OFF_V, o_v, MIX_ATTN)


def _prep_w_in(w_in, tr):
    depth, d, n = w_in.shape
    return pl.pallas_call(
        _prep_w_kernel,
        grid=(depth, d // tr),
        in_specs=[pl.BlockSpec((1, tr, n), lambda l, i: (l, i, 0))],
        out_specs=pl.BlockSpec((1, tr, W_TOTAL), lambda l, i: (l, i, 0)),
        out_shape=jax.ShapeDtypeStruct((depth, d, W_TOTAL), BF16),
        compiler_params=_params(("parallel", "parallel")),
        name="prep_w_in",
    )(w_in)


def kernel(x, c, positions, w_ada, b_ada, w_in, i_bias, f_bias, conv_w, conv_b, w_out, ln_mix_g, ln_mix_b,
           w_router, b_router, w_gate, w_up, w_down, ln_ffn_g, ln_ffn_b):
    bsz, seq, d = x.shape
    depth = w_ada.shape[0]
    alpha = (2.0 * depth) ** 0.25

    tm = _pick(seq, 512)
    tq = _pick(seq, 256)
    kb = _pick(seq, 256)
    assert seq // 16 <= 256, "packed bf16 partial counts in the DSA threshold search must stay exact"
    cr = _pick(seq, 256)
    cm = _pick(seq, 256)
    tmoe = _pick(seq, 512)
    moe_cap = min(tmoe, (3 * tmoe // (2 * N_GROUPS) + 15) // 16 * 16)

    cos_t, sin_t = _rope_tables(positions)
    c_pad = jnp.zeros((8, d), F32).at[:bsz].set(c)
    mod = _ada_mod(c_pad, w_ada, b_ada, _pick(6 * d, 1536))
    w_in_p = _prep_w_in(w_in, _pick(d, 256))
    w_out_b = w_out.astype(BF16)
    w_gate_b, w_up_b = w_gate.astype(BF16), w_up.astype(BF16)
    w_down_b = w_down.astype(BF16).reshape(depth, -1, d)
    grp = 2 if bsz % 2 == 0 else 1
    grp_ret = 4 if bsz % 4 == 0 else grp

    tril_kb = (jnp.arange(kb)[:, None] >= jnp.arange(kb)[None, :]).astype(BF16)
    tril = (jnp.arange(cm)[:, None] >= jnp.arange(cm)[None, :]).astype(F32)
    w_router_p = jnp.zeros((d, LANES), F32).at[:, :N_EXPERTS].set(w_router)
    b_router_p = b_router.reshape(N_EXPERTS, 1)
    tri_moe = (jnp.arange(tmoe)[:, None] < jnp.arange(tmoe)[None, :]).astype(BF16)

    for l in range(depth):
        parts = [mod[l, :bsz, j * d:(j + 1) * d].reshape(bsz, 1, d) for j in range(6)]
        sh_m, sc_m, g_m, sh_f, sc_f, g_f = parts
        q, k, iq, ik2, r_proj, m_proj, small, small_t, v_t = _in_proj(
            x, sc_m, sh_m, w_in_p, l, cos_t, sin_t, tm, kb)
        o_a = _dsa(q, k, iq, ik2, small_t, v_t, tril_kb, tq, kb)
        o_b = _retention(r_proj, cr, grp_ret)
        gate_bias = (jnp.zeros((1, S_WIDTH), F32).at[0, S_MI:S_MI + N_HEADS_MLSTM].set(i_bias[l])
                     .at[0, S_MF:S_MF + N_HEADS_MLSTM].set(f_bias[l]))
        o_c = _mlstm(m_proj, small, gate_bias, conv_w[l], conv_b[l].reshape(1, -1), tril, cm, grp)
        x = _moe(o_a, o_b, o_c, w_out_b, x, g_m, ln_mix_g[l].reshape(1, d), ln_mix_b[l].reshape(1, d),
                 sc_f, sh_f, g_f, w_router_p, b_router_p,
                 w_gate_b, w_up_b, w_down_b, l, tri_moe,
                 ln_ffn_g[l].reshape(1, d), ln_ffn_b[l].reshape(1, d), tmoe, moe_cap, alpha)
    return x
```
